```python
import math
import jax
import jax.numpy as jnp
from jax import lax
import numpy as np

D_MODEL = 1024
BATCH = 4
SEQ = 8192
DEPTH = 2
DEC_BATCH = 128
DEC_SEQ = 1
PAST_LEN = 16384
PAGE_SIZE = 128

N_META = 16
WINDOW = 128
N_HEADS = 8
N_KV_HEADS = 2
HEAD_DIM = 64
Q_PER_KV = N_HEADS // N_KV_HEADS
ATTN_W = N_HEADS * HEAD_DIM
KV_W = N_KV_HEADS * HEAD_DIM
ATTN_SCALE = HEAD_DIM ** -0.5
REL_BUCKETS = 32
REL_EXACT = REL_BUCKETS // 2
REL_MAX_DIST = 128
SSM_GROUP = 16
SSM_GROUPS = 16
SSM_WIDTH = SSM_GROUP * SSM_GROUPS
SSM_STATE = 64
RWKV_HEAD = 64
RWKV_HEADS = 4
RWKV_WIDTH = RWKV_HEAD * RWKV_HEADS
RWKV_W_LORA = 32
RWKV_A_LORA = 32
RWKV_G_LORA = 64
RWKV_COLS = 3 * RWKV_WIDTH + RWKV_W_LORA + RWKV_A_LORA + RWKV_G_LORA
RV_W0, RV_A0, RV_KK, RV_KA, RV_RK, RV_GNW, RV_GNB = 0, 1, 2, 3, 4, 5, 6
N_RWKV_VEC = 7
N_BRANCH = 3
MIX_W = ATTN_W + SSM_WIDTH + RWKV_WIDTH
Q_OFF = 0
K_OFF = Q_OFF + ATTN_W
V_OFF = K_OFF + KV_W
U_OFF = V_OFF + KV_W
C_OFF = U_OFF + SSM_WIDTH
G_OFF = C_OFF + RWKV_COLS
IN_COLS = G_OFF + N_BRANCH * D_MODEL
D_FF = 2816
N_EXPERTS = 8
TOP_K = 2
D_FF_EXPERT = 3584
MOE_BLOCK = 256
LN_EPS = 1e-5
RWKV_GN_EPS = 64e-5
NEG_INF = -1e30
DEEPNORM_ALPHA = (2 * DEPTH) ** 0.25
DEEPNORM_BETA = (8 * DEPTH) ** -0.25

kernel_name = 'hybrid_swa_s5_rwkv7_deepnorm_step'


def layer_norm(x, g, b):
    xf = x.astype(jnp.float32)
    mu = jnp.mean(xf, axis=-1, keepdims=True)
    var = jnp.mean(jnp.square(xf - mu), axis=-1, keepdims=True)
    return ((xf - mu) * lax.rsqrt(var + LN_EPS) * g.astype(jnp.float32) + b.astype(jnp.float32)).astype(x.dtype)


def t5_bucket(dist):
    n = jnp.maximum(dist, 0)
    log_ratio = jnp.log(jnp.maximum(n, 1).astype(jnp.float32) / REL_EXACT) / math.log(REL_MAX_DIST / REL_EXACT)
    large = jnp.minimum(REL_EXACT + (log_ratio * (REL_BUCKETS - REL_EXACT)).astype(jnp.int32), REL_BUCKETS - 1)
    return jnp.where(n < REL_EXACT, n, large)


def sink_attention(q, k, v, dist, valid, rel_bias, sinks):
    nb, nq, nk = dist.shape
    bias = rel_bias.astype(jnp.float32)[t5_bucket(dist)]
    bias = jnp.moveaxis(bias, -1, 1).reshape(nb, N_KV_HEADS, Q_PER_KV, nq, nk)
    s = jnp.einsum('bnqhgd,bnkhd->bnhgqk', q, k).astype(jnp.float32) * ATTN_SCALE + bias
    s = jnp.where(valid[:, None, None], s, NEG_INF)
    sk = sinks.astype(jnp.float32).reshape(N_KV_HEADS, Q_PER_KV, 1)
    m = jnp.maximum(jnp.max(s, axis=-1), sk)
    p = jnp.exp(s - m[..., None])
    p = p / (jnp.sum(p, axis=-1, keepdims=True) + jnp.exp(sk - m)[..., None])
    return jnp.einsum('bnhgqk,bnkhd->bnqhgd', p.astype(v.dtype), v)


def prompt_window_attention(q, k, v, rel_bias, sinks):
    bsz, seq, _ = q.shape
    lead = (-seq) % WINDOW
    nb = (seq + lead) // WINDOW
    q = q.reshape(bsz, seq, N_KV_HEADS, Q_PER_KV, HEAD_DIM)
    k = k.reshape(bsz, seq, N_KV_HEADS, HEAD_DIM)
    v = v.reshape(bsz, seq, N_KV_HEADS, HEAD_DIM)
    qb = jnp.pad(q, ((0, 0), (lead, 0), (0, 0), (0, 0), (0, 0))).reshape(bsz, nb, WINDOW, N_KV_HEADS, Q_PER_KV, HEAD_DIM)

    def band(t):
        tp = jnp.pad(t, ((0, 0), (lead + WINDOW, 0), (0, 0), (0, 0))).reshape(bsz, nb + 1, WINDOW, N_KV_HEADS, HEAD_DIM)
        meta = jnp.broadcast_to(t[:, None, :N_META], (bsz, nb, N_META, N_KV_HEADS, HEAD_DIM))
        return jnp.concatenate([meta, tp[:, :-1], tp[:, 1:]], axis=2)

    blk = np.arange(nb)[:, None]
    q_pos = blk * WINDOW + np.arange(WINDOW)[None, :] - lead
    band_pos = (blk - 1) * WINDOW + np.arange(2 * WINDOW)[None, :] - lead
    k_pos = np.concatenate([np.broadcast_to(np.arange(N_META)[None, :], (nb, N_META)), band_pos], axis=1)
    dist = q_pos[:, :, None] - k_pos[:, None, :]
    is_meta = (np.arange(N_META + 2 * WINDOW) < N_META)[None, None, :]
    valid = (dist >= 0) & (is_meta | ((k_pos[:, None, :] >= N_META) & (dist <= WINDOW)))
    o = sink_attention(qb, band(k), band(v), jnp.asarray(dist, jnp.int32), jnp.asarray(valid), rel_bias, sinks)
    o = o.reshape(bsz, nb * WINDOW, ATTN_W)[:, lead:]
    return o, (k[:, :N_META], v[:, :N_META], k[:, seq - WINDOW:], v[:, seq - WINDOW:])


def sample_window_attention(q, k, v, meta_k, meta_v, win_k, win_v, rel_bias, sinks):
    bsz, ds, _ = q.shape
    wc = win_k.shape[1]
    qb = q.reshape(bsz, 1, ds, N_KV_HEADS, Q_PER_KV, HEAD_DIM)
    k_all = jnp.concatenate([win_k.astype(k.dtype), k.reshape(bsz, ds, N_KV_HEADS, HEAD_DIM)], axis=1)
    v_all = jnp.concatenate([win_v.astype(v.dtype), v.reshape(bsz, ds, N_KV_HEADS, HEAD_DIM)], axis=1)
    keys = jnp.concatenate([meta_k.astype(k.dtype), k_all], axis=1)[:, None]
    vals = jnp.concatenate([meta_v.astype(v.dtype), v_all], axis=1)[:, None]
    q_pos = PAST_LEN + np.arange(ds)
    k_pos = np.concatenate([np.arange(N_META), PAST_LEN - wc + np.arange(wc + ds)])
    dist = q_pos[:, None] - k_pos[None, :]
    is_meta = (np.arange(N_META + wc + ds) < N_META)[None, :]
    valid = (dist >= 0) & (is_meta | ((k_pos[None, :] >= N_META) & (dist <= WINDOW)))
    o = sink_attention(qb, keys, vals, jnp.asarray(dist[None], jnp.int32), jnp.asarray(valid[None]), rel_bias, sinks)
    return o.reshape(bsz, ds, ATTN_W), (k_all[:, ds:], v_all[:, ds:])


def s5_mixer(u, h0_re, h0_im, a_re, a_im, log_dt, b_re, b_im, c_re, c_im, d, w_glu):
    bsz, seq, _ = u.shape
    f32 = jnp.float32
    uf = u.astype(f32).reshape(bsz, seq, SSM_GROUPS, SSM_GROUP)
    a_re = a_re.astype(f32)
    a_im = a_im.astype(f32)
    dt = jnp.exp(log_dt.astype(f32))[:, None]
    mag = jnp.exp(a_re * dt)
    ab_re = mag * jnp.cos(a_im * dt)
    ab_im = mag * jnp.sin(a_im * dt)
    den = a_re * a_re + a_im * a_im
    nr = ab_re - 1.0
    cf_re = (nr * a_re + ab_im * a_im) / den
    cf_im = (ab_im * a_re - nr * a_im) / den
    b_re = b_re.astype(f32)
    b_im = b_im.astype(f32)
    bb_re = cf_re[..., None] * b_re - cf_im[..., None] * b_im
    bb_im = cf_re[..., None] * b_im + cf_im[..., None] * b_re
    bu_re = jnp.einsum('gpc,blgc->blgp', bb_re, uf)
    bu_im = jnp.einsum('gpc,blgc->blgp', bb_im, uf)
    h0_re = h0_re.astype(f32)
    h0_im = h0_im.astype(f32)
    bu_re = bu_re.at[:, 0].add(ab_re * h0_re - ab_im * h0_im)
    bu_im = bu_im.at[:, 0].add(ab_re * h0_im + ab_im * h0_re)
    aa_re = jnp.broadcast_to(ab_re, bu_re.shape)
    aa_im = jnp.broadcast_to(ab_im, bu_im.shape)

    def combine(e1, e2):
        a1r, a1i, b1r, b1i = e1
        a2r, a2i, b2r, b2i = e2
        return (a2r * a1r - a2i * a1i, a2r * a1i + a2i * a1r,
                a2r * b1r - a2i * b1i + b2r, a2r * b1i + a2i * b1r + b2i)

    _, _, hr, hi = lax.associative_scan(combine, (aa_re, aa_im, bu_re, bu_im), axis=1)
    y = (jnp.einsum('gcp,blgp->blgc', c_re.astype(f32), hr)
         - jnp.einsum('gcp,blgp->blgc', c_im.astype(f32), hi)
         + d.astype(f32).reshape(SSM_GROUPS, SSM_GROUP) * uf).reshape(bsz, seq, SSM_WIDTH)
    z = jax.nn.gelu(y)
    out = z * jax.nn.sigmoid(z @ w_glu.astype(f32))
    return out.astype(u.dtype), hr[:, -1], hi[:, -1]


def rwkv7_mixer(pc, shift0, wkv0, mu, vec, w2, a2, g2):
    bsz, seq, _ = pc.shape
    f32 = jnp.float32
    pf = pc.astype(f32)
    prev = jnp.concatenate([shift0.astype(f32)[:, None], pf[:, :-1]], axis=1)
    xm = pf + (prev - pf) * mu.astype(f32)
    rw = RWKV_WIDTH
    xr = xm[..., :rw]
    xk = xm[..., rw:2 * rw]
    xv = xm[..., 2 * rw:3 * rw]
    xw = xm[..., 3 * rw:3 * rw + RWKV_W_LORA]
    xa = xm[..., 3 * rw + RWKV_W_LORA:3 * rw + RWKV_W_LORA + RWKV_A_LORA]
    xg = xm[..., 3 * rw + RWKV_W_LORA + RWKV_A_LORA:]
    vec = vec.astype(f32)
    w_log = -jax.nn.softplus(-(vec[RV_W0] + jnp.tanh(xw) @ w2.astype(f32))) - 0.5
    decay = jnp.exp(-jnp.exp(w_log))
    a = jax.nn.sigmoid(vec[RV_A0] + xa @ a2.astype(f32))
    g = jax.nn.sigmoid(xg) @ g2.astype(f32)

    def heads(t):
        return t.reshape(bsz, seq, RWKV_HEADS, RWKV_HEAD)

    kk = heads(xk * vec[RV_KK])
    kk = kk / jnp.maximum(jnp.sqrt(jnp.sum(kk * kk, axis=-1, keepdims=True)), 1e-12)
    k = xk * (1.0 + (a - 1.0) * vec[RV_KA])
    r, k, v, decay, a = heads(xr), heads(k), heads(xv), heads(decay), heads(a)

    def step(S, inp):
        r_t, d_t, k_t, v_t, kk_t, a_t = inp
        sa = jnp.einsum('bhvk,bhk->bhv', S, -kk_t)
        S = (S * d_t[:, :, None, :] + sa[..., None] * (kk_t * a_t)[:, :, None, :]
             + v_t[..., None] * k_t[:, :, None, :])
        return S, jnp.einsum('bhvk,bhk->bhv', S, r_t)

    xs = tuple(jnp.moveaxis(t, 1, 0) for t in (r, decay, k, v, kk, a))
    S, y = lax.scan(step, wkv0.astype(f32), xs)
    y = jnp.moveaxis(y, 0, 1)
    ym = jnp.mean(y, axis=-1, keepdims=True)
    yv = jnp.mean(jnp.square(y - ym), axis=-1, keepdims=True)
    y = ((y - ym) * lax.rsqrt(yv + RWKV_GN_EPS)).reshape(bsz, seq, rw) * vec[RV_GNW] + vec[RV_GNB]
    bonus = jnp.sum(r * k * vec[RV_RK].reshape(RWKV_HEADS, RWKV_HEAD), axis=-1, keepdims=True) * v
    y = (y + bonus.reshape(bsz, seq, rw)) * g
    return y.astype(pc.dtype), S, pc[:, -1]


def swiglu(x, w_in, w_down):
    f = w_down.shape[-2]
    gu = x @ w_in
    return (jax.nn.silu(gu[..., :f]) * gu[..., f:]) @ w_down


def moe_swiglu(x, router, w_in, w_down):
    bsz, seq, _ = x.shape
    n = bsz * seq
    xf = x.reshape(n, D_MODEL)
    logits = (xf @ router).astype(jnp.float32)
    top_logit, top_e = lax.top_k(logits, TOP_K)
    gate = jax.nn.softmax(top_logit, axis=-1)
    n_assign = n * TOP_K
    blk = int(min(MOE_BLOCK, max(8, 2 ** math.ceil(math.log2(max(1, n_assign // N_EXPERTS))))))
    n_blocks = -(-(n_assign + N_EXPERTS * (blk - 1)) // blk)
    flat_e = top_e.reshape(-1)
    flat_tok = jnp.repeat(jnp.arange(n, dtype=jnp.int32), TOP_K)
    flat_g = gate.reshape(-1)
    order = jnp.argsort(flat_e)
    se, stok, sg = flat_e[order], flat_tok[order], flat_g[order]
    counts = jnp.bincount(flat_e, length=N_EXPERTS)
    padded = (counts + blk - 1) // blk * blk
    pad_end = jnp.cumsum(padded)
    dest = (pad_end - padded)[se] + jnp.arange(n_assign) - (jnp.cumsum(counts) - counts)[se]
    rows_tok = jnp.zeros((n_blocks * blk,), jnp.int32).at[dest].set(stok)
    rows_g = jnp.zeros((n_blocks * blk,), jnp.float32).at[dest].set(sg)
    block_e = jnp.minimum(jnp.searchsorted(pad_end, jnp.arange(n_blocks) * blk, side='right'), N_EXPERTS - 1)

    def expert_block(args):
        tok, e = args
        return swiglu(xf[tok], w_in[e], w_down[e])

    rows = lax.map(expert_block, (rows_tok.reshape(n_blocks, blk), block_e))
    out = jnp.zeros((n, D_MODEL), jnp.float32).at[rows_tok].add(
        rows.reshape(-1, D_MODEL).astype(jnp.float32) * rows_g[:, None])
    return out.astype(x.dtype).reshape(bsz, seq, D_MODEL)


def trunk_layer(h, attn_cache, ssm_re0, ssm_im0, wkv0, shift0, mix_params, ffn_params):
    (w_in, rel_bias, sinks, ssm_params, rwkv_params, w_branch, w_out, ln_g, ln_b) = mix_params
    bsz, seq, _ = h.shape
    proj = h @ w_in
    q = proj[..., Q_OFF:K_OFF]
    k = proj[..., K_OFF:V_OFF]
    v = proj[..., V_OFF:U_OFF]
    u = proj[..., U_OFF:C_OFF]
    pc = proj[..., C_OFF:G_OFF]
    gate_logits = proj[..., G_OFF:]
    if attn_cache is None:
        o_a, attn_state = prompt_window_attention(q, k, v, rel_bias, sinks)
    else:
        o_a, attn_state = sample_window_attention(q, k, v, attn_cache[0], attn_cache[1], attn_cache[2], attn_cache[3], rel_bias, sinks)
    o_b, s_re, s_im = s5_mixer(u, ssm_re0, ssm_im0, *ssm_params)
    o_c, wkv, shift = rwkv7_mixer(pc, shift0, wkv0, *rwkv_params)
    gates = jax.nn.sigmoid(gate_logits.astype(jnp.float32)).reshape(bsz, seq, N_BRANCH, D_MODEL)
    merged = (gates[..., 0, :] * (o_a @ w_branch[:ATTN_W])
              + gates[..., 1, :] * (o_b @ w_branch[ATTN_W:ATTN_W + SSM_WIDTH])
              + gates[..., 2, :] * (o_c @ w_branch[ATTN_W + SSM_WIDTH:]))
    mix = merged.astype(h.dtype) @ w_out
    h = layer_norm(DEEPNORM_ALPHA * h + mix, ln_g[0], ln_b[0])
    if ffn_params[0]:
        f = moe_swiglu(h, ffn_params[1], ffn_params[2], ffn_params[3])
    else:
        f = swiglu(h, ffn_params[1], ffn_params[2])
    h = layer_norm(DEEPNORM_ALPHA * h + f, ln_g[1], ln_b[1])
    return h, attn_state, (s_re, s_im, wkv, shift)


def setup_inputs(seed: int = 0) -> dict:
    key = jax.random.key(seed)
    ks = iter(jax.random.split(key, 64))
    f32 = jnp.float32

    def nrm(shape, scale=1.0):
        return jax.random.normal(next(ks), shape, f32) * scale

    def unif(shape, lo, hi):
        return jax.random.uniform(next(ks), shape, f32, lo, hi)

    n_dense = (DEPTH + 1) // 2
    n_moe = DEPTH // 2
    win = min(WINDOW, PAST_LEN)
    inputs = {
        'x_prompt': nrm((BATCH, SEQ, D_MODEL)),
        'x_sample': nrm((DEC_BATCH, DEC_SEQ, D_MODEL)),
        'cache_meta_k': nrm((DEPTH, DEC_BATCH, N_META, N_KV_HEADS, HEAD_DIM)),
        'cache_meta_v': nrm((DEPTH, DEC_BATCH, N_META, N_KV_HEADS, HEAD_DIM)),
        'cache_win_k': nrm((DEPTH, DEC_BATCH, win, N_KV_HEADS, HEAD_DIM)),
        'cache_win_v': nrm((DEPTH, DEC_BATCH, win, N_KV_HEADS, HEAD_DIM)),
        'state_ssm_re': nrm((DEPTH, DEC_BATCH, SSM_GROUPS, SSM_STATE), 0.1),
        'state_ssm_im': nrm((DEPTH, DEC_BATCH, SSM_GROUPS, SSM_STATE), 0.1),
        'state_wkv': nrm((DEPTH, DEC_BATCH, RWKV_HEADS, RWKV_HEAD, RWKV_HEAD), 0.3),
        'state_shift': nrm((DEPTH, DEC_BATCH, RWKV_COLS)),
        'meta_tokens': nrm((N_META, D_MODEL)),
        'ln_in_g': 1.0 + nrm((D_MODEL,), 0.01),
        'ln_in_b': nrm((D_MODEL,), 0.01),
        'w_in': nrm((DEPTH, D_MODEL, IN_COLS), D_MODEL ** -0.5),
        'rel_bias': nrm((REL_BUCKETS, N_HEADS), 0.5),
        'attn_sinks': nrm((DEPTH, N_HEADS), 0.5),
        'ssm_a_re': -0.5 + nrm((DEPTH, SSM_GROUPS, SSM_STATE), 0.01),
        'ssm_a_im': jnp.pi * jnp.arange(SSM_STATE, dtype=f32) + nrm((DEPTH, SSM_GROUPS, SSM_STATE), 0.01),
        'ssm_log_dt': unif((DEPTH, SSM_GROUPS), math.log(1e-3), math.log(1e-1)),
        'ssm_b_re': nrm((DEPTH, SSM_GROUPS, SSM_STATE, SSM_GROUP), (2 * SSM_GROUP) ** -0.5),
        'ssm_b_im': nrm((DEPTH, SSM_GROUPS, SSM_STATE, SSM_GROUP), (2 * SSM_GROUP) ** -0.5),
        'ssm_c_re': nrm((DEPTH, SSM_GROUPS, SSM_GROUP, SSM_STATE), SSM_STATE ** -0.5),
        'ssm_c_im': nrm((DEPTH, SSM_GROUPS, SSM_GROUP, SSM_STATE), SSM_STATE ** -0.5),
        'ssm_d': nrm((DEPTH, SSM_WIDTH)),
        'ssm_w_glu': nrm((DEPTH, SSM_WIDTH, SSM_WIDTH), SSM_WIDTH ** -0.5),
        'rwkv_mu': unif((DEPTH, RWKV_COLS), 0.0, 1.0),
        'rwkv_vec': jnp.stack([
            unif((DEPTH, RWKV_WIDTH), -6.0, -1.0),
            nrm((DEPTH, RWKV_WIDTH), 0.1),
            0.85 + nrm((DEPTH, RWKV_WIDTH), 0.05),
            1.0 + nrm((DEPTH, RWKV_WIDTH), 0.05),
            nrm((DEPTH, RWKV_WIDTH), 0.1),
            1.0 + nrm((DEPTH, RWKV_WIDTH), 0.01),
            nrm((DEPTH, RWKV_WIDTH), 0.01)], axis=1),
        'rwkv_w2': nrm((DEPTH, RWKV_W_LORA, RWKV_WIDTH), 0.5 * RWKV_W_LORA ** -0.5),
        'rwkv_a2': nrm((DEPTH, RWKV_A_LORA, RWKV_WIDTH), 0.5 * RWKV_A_LORA ** -0.5),
        'rwkv_g2': nrm((DEPTH, RWKV_G_LORA, RWKV_WIDTH), RWKV_G_LORA ** -0.5),
        'w_branch': DEEPNORM_BETA * jnp.concatenate([
            nrm((DEPTH, ATTN_W, D_MODEL), ATTN_W ** -0.5),
            nrm((DEPTH, SSM_WIDTH, D_MODEL), SSM_WIDTH ** -0.5),
            nrm((DEPTH, RWKV_WIDTH, D_MODEL), RWKV_WIDTH ** -0.5)], axis=1),
        'w_out': nrm((DEPTH, D_MODEL, D_MODEL), DEEPNORM_BETA * D_MODEL ** -0.5),
        'ln_g': 1.0 + nrm((DEPTH, 2, D_MODEL), 0.01),
        'ln_b': nrm((DEPTH, 2, D_MODEL), 0.01),
        'ffn_w_in': nrm((n_dense, D_MODEL, 2 * D_FF), D_MODEL ** -0.5),
        'ffn_w_down': nrm((n_dense, D_FF, D_MODEL), DEEPNORM_BETA * D_FF ** -0.5),
        'moe_router': nrm((n_moe, D_MODEL, N_EXPERTS), D_MODEL ** -0.5),
        'moe_w_in': nrm((n_moe, N_EXPERTS, D_MODEL, 2 * D_FF_EXPERT), D_MODEL ** -0.5),
        'moe_w_down': nrm((n_moe, N_EXPERTS, D_FF_EXPERT, D_MODEL), DEEPNORM_BETA * D_FF_EXPERT ** -0.5),
    }
    return inputs


def reference(x_prompt, x_sample, cache_meta_k, cache_meta_v, cache_win_k, cache_win_v,
              state_ssm_re, state_ssm_im, state_wkv, state_shift,
              meta_tokens, ln_in_g, ln_in_b, w_in, rel_bias, attn_sinks,
              ssm_a_re, ssm_a_im, ssm_log_dt, ssm_b_re, ssm_b_im, ssm_c_re, ssm_c_im, ssm_d, ssm_w_glu,
              rwkv_mu, rwkv_vec, rwkv_w2, rwkv_a2, rwkv_g2,
              w_branch, w_out, ln_g, ln_b, ffn_w_in, ffn_w_down, moe_router, moe_w_in, moe_w_down):
    bsz = x_prompt.shape[0]
    meta = jnp.broadcast_to(meta_tokens.astype(x_prompt.dtype)[None], (bsz, N_META, D_MODEL))
    hp = layer_norm(jnp.concatenate([meta, x_prompt], axis=1), ln_in_g, ln_in_b)
    hs = layer_norm(x_sample, ln_in_g, ln_in_b)
    zero_ssm = jnp.zeros((bsz, SSM_GROUPS, SSM_STATE), jnp.float32)
    zero_wkv = jnp.zeros((bsz, RWKV_HEADS, RWKV_HEAD, RWKV_HEAD), jnp.float32)
    zero_shift = jnp.zeros((bsz, RWKV_COLS), x_prompt.dtype)
    mk_p, mv_p, wk_p, wv_p, wk_s, wv_s = [], [], [], [], [], []
    sre_p, sim_p, sre_s, sim_s, wkv_p, wkv_s, sh_p, sh_s = [], [], [], [], [], [], [], []
    for l in range(DEPTH):
        mix_params = (w_in[l], rel_bias, attn_sinks[l],
                      (ssm_a_re[l], ssm_a_im[l], ssm_log_dt[l], ssm_b_re[l], ssm_b_im[l],
                       ssm_c_re[l], ssm_c_im[l], ssm_d[l], ssm_w_glu[l]),
                      (rwkv_mu[l], rwkv_vec[l], rwkv_w2[l], rwkv_a2[l], rwkv_g2[l]),
                      w_branch[l], w_out[l], ln_g[l], ln_b[l])
        if l % 2 == 0:
            ffn_params = (False, ffn_w_in[l // 2], ffn_w_down[l // 2])
        else:
            ffn_params = (True, moe_router[l // 2], moe_w_in[l // 2], moe_w_down[l // 2])
        hp, a_p, r_p = trunk_layer(hp, None, zero_ssm, zero_ssm, zero_wkv, zero_shift, mix_params, ffn_params)
        hs, a_s, r_s = trunk_layer(hs, (cache_meta_k[l], cache_meta_v[l], cache_win_k[l], cache_win_v[l]),
                                   state_ssm_re[l], state_ssm_im[l], state_wkv[l], state_shift[l],
                                   mix_params, ffn_params)
        mk_p.append(a_p[0]); mv_p.append(a_p[1]); wk_p.append(a_p[2]); wv_p.append(a_p[3])
        wk_s.append(a_s[0]); wv_s.append(a_s[1])
        sre_p.append(r_p[0]); sim_p.append(r_p[1]); wkv_p.append(r_p[2]); sh_p.append(r_p[3])
        sre_s.append(r_s[0]); sim_s.append(r_s[1]); wkv_s.append(r_s[2]); sh_s.append(r_s[3])
    y_prompt = hp[:, N_META:]
    y_sample = hs
    return (y_prompt, y_sample,
            jnp.stack(mk_p), jnp.stack(mv_p), jnp.stack(wk_p), jnp.stack(wv_p),
            jnp.stack(wk_s), jnp.stack(wv_s),
            jnp.stack(sre_p), jnp.stack(sim_p), jnp.stack(sre_s), jnp.stack(sim_s),
            jnp.stack(wkv_p), jnp.stack(wkv_s), jnp.stack(sh_p), jnp.stack(sh_s))
```

```python
import functools
import math

import numpy as np
import jax
import jax.numpy as jnp
from jax import lax
from jax.experimental import pallas as pl
from jax.experimental.pallas import tpu as pltpu

F32 = jnp.float32
BF16 = jnp.bfloat16

D_MODEL = 1024
DEPTH = 2
PAST_LEN = 16384
N_META = 16
WINDOW = 128
N_HEADS = 8
N_KV_HEADS = 2
HEAD_DIM = 64
Q_PER_KV = N_HEADS // N_KV_HEADS
ATTN_W = N_HEADS * HEAD_DIM
KV_W = N_KV_HEADS * HEAD_DIM
ATTN_SCALE = HEAD_DIM ** -0.5
REL_BUCKETS = 32
REL_EXACT = REL_BUCKETS // 2
REL_MAX_DIST = 128
SSM_GROUP = 16
SSM_GROUPS = 16
SSM_WIDTH = SSM_GROUP * SSM_GROUPS
SSM_STATE = 64
SSM_N = SSM_GROUPS * SSM_STATE
RWKV_HEAD = 64
RWKV_HEADS = 4
RWKV_WIDTH = RWKV_HEAD * RWKV_HEADS
RWKV_W_LORA = 32
RWKV_A_LORA = 32
RWKV_G_LORA = 64
RWKV_LORA = RWKV_W_LORA + RWKV_A_LORA + RWKV_G_LORA
RWKV_COLS = 3 * RWKV_WIDTH + RWKV_LORA
RV_W0, RV_A0, RV_KK, RV_KA, RV_RK, RV_GNW, RV_GNB = 0, 1, 2, 3, 4, 5, 6
N_BRANCH = 3
MIX_COLS = ATTN_W + 2 * KV_W + SSM_WIDTH + RWKV_COLS
N_EXPERTS = 8
TOP_K = 2
LN_EPS = 1e-5
RWKV_GN_EPS = 64e-5
NEG_INF = -1e30
DEEPNORM_ALPHA = (2 * DEPTH) ** 0.25

LANES = 128
VMEM_LIMIT = 48 * 1024 * 1024
RWKV_CHUNK = 64


def _params(sem):
    return pltpu.CompilerParams(dimension_semantics=sem, vmem_limit_bytes=VMEM_LIMIT)


def _dot(a, b):
    return jnp.dot(a.astype(BF16), b.astype(BF16), preferred_element_type=F32)


def _dot_nt(a, b):
    return lax.dot_general(a.astype(BF16), b.astype(BF16), (((1,), (1,)), ((), ())),
                           preferred_element_type=F32)


def _split3(x):
    h1 = x.astype(BF16)
    r1 = x - h1.astype(F32)
    h2 = r1.astype(BF16)
    h3 = (r1 - h2.astype(F32)).astype(BF16)
    return h1, h2, h3


def _dot_exact_rhs(x, m):
    h1, h2, h3 = _split3(x)
    dot = functools.partial(jnp.dot, preferred_element_type=F32)
    return dot(h1, m) + dot(h2, m) + dot(h3, m)


def _dot_exact_lhs(m, x):
    h1, h2, h3 = _split3(x)
    dot = functools.partial(jnp.dot, preferred_element_type=F32)
    return dot(m, h1) + dot(m, h2) + dot(m, h3)


def _layer_norm(x, g, b):
    mu = jnp.mean(x, axis=-1, keepdims=True)
    xc = x - mu
    var = jnp.mean(xc * xc, axis=-1, keepdims=True)
    return xc * lax.rsqrt(var + LN_EPS) * g + b


def _sigmoid(x):
    return 1.0 / (1.0 + jnp.exp(-x))


def _rows_call(body, row_inputs, const_inputs, out_widths, tm, out_dtypes=None, name=None):
    n = row_inputs[0].shape[0]
    tm = min(tm, n)
    assert n % tm == 0, (n, tm)
    out_dtypes = out_dtypes or [F32] * len(out_widths)
    in_specs = [pl.BlockSpec((tm, a.shape[1]), lambda i: (i, 0)) for a in row_inputs]
    in_specs += [pl.BlockSpec(c.shape, lambda i, nd=c.ndim: (0,) * nd) for c in const_inputs]
    out_specs = [pl.BlockSpec((tm, w), lambda i: (i, 0)) for w in out_widths]
    out_shape = [jax.ShapeDtypeStruct((n, w), dt) for w, dt in zip(out_widths, out_dtypes)]
    return pl.pallas_call(
        body, grid=(n // tm,), in_specs=in_specs, out_specs=out_specs, out_shape=out_shape,
        compiler_params=_params(("parallel",)), name=name,
    )(*row_inputs, *const_inputs)


PROJ_WIDTHS = (ATTN_W, KV_W, KV_W, SSM_WIDTH, RWKV_COLS)


def _proj_kernel(x_ref, g_ref, b_ref, w_ref, *out_refs, pre_ln):
    x = x_ref[...]
    if pre_ln:
        x = _layer_norm(x, g_ref[...], b_ref[...])
        out_refs[0][...] = x
        out_refs = out_refs[1:]
    xb = x.astype(BF16)
    col = 0
    for o_ref in out_refs:
        n = o_ref.shape[-1]
        o_ref[...] = jnp.dot(xb, w_ref[:, col:col + n], preferred_element_type=F32)
        col += n


def _proj(x, ln_g, ln_b, w_mix, pre_ln, tm):
    widths = ((D_MODEL,) if pre_ln else ()) + PROJ_WIDTHS
    outs = _rows_call(functools.partial(_proj_kernel, pre_ln=pre_ln), [x], [ln_g, ln_b, w_mix],
                      widths, tm, name="proj")
    if pre_ln:
        return outs[0], outs[1:]
    return x, outs


def _attn_kernel(tab_ref, q_ref, k_ref, v_ref, bias_ref, sink_ref, o_ref):
    del tab_ref
    for bb in range(q_ref.shape[0]):
        for h in range(N_KV_HEADS):
            q = q_ref[bb, h].astype(BF16)
            k = k_ref[bb, h].astype(BF16)
            v = v_ref[bb, h].astype(BF16)
            s = lax.dot_general(q, k, (((1,), (1,)), ((), ())), preferred_element_type=F32)
            s = s * ATTN_SCALE + bias_ref[0, h]
            sk = sink_ref[h]
            m = jnp.maximum(jnp.max(s, axis=-1, keepdims=True), sk)
            p = jnp.exp(s - m)
            den = jnp.sum(p, axis=-1, keepdims=True) + jnp.exp(sk - m)
            o = jnp.dot(p.astype(BF16), v, preferred_element_type=F32)
            o_ref[bb, h] = o / den


def _attention(q, k, v, bias, sinks, tab_idx, bblk):
    p, _, mq, _ = q.shape
    nk = k.shape[2]
    assert p % bblk == 0
    grid_spec = pltpu.PrefetchScalarGridSpec(
        num_scalar_prefetch=1, grid=(p // bblk,),
        in_specs=[
            pl.BlockSpec((bblk, N_KV_HEADS, mq, HEAD_DIM), lambda i, t: (i, 0, 0, 0)),
            pl.BlockSpec((bblk, N_KV_HEADS, nk, HEAD_DIM), lambda i, t: (i, 0, 0, 0)),
            pl.BlockSpec((bblk, N_KV_HEADS, nk, HEAD_DIM), lambda i, t: (i, 0, 0, 0)),
            pl.BlockSpec((1, N_KV_HEADS, mq, nk), lambda i, t: (t[i], 0, 0, 0)),
            pl.BlockSpec((N_KV_HEADS, mq, 1), lambda i, t: (0, 0, 0)),
        ],
        out_specs=pl.BlockSpec((bblk, N_KV_HEADS, mq, HEAD_DIM), lambda i, t: (i, 0, 0, 0)),
    )
    return pl.pallas_call(
        _attn_kernel, grid_spec=grid_spec,
        out_shape=jax.ShapeDtypeStruct(q.shape, F32),
        compiler_params=_params(("arbitrary",)), name="attention",
    )(tab_idx, q, k, v, bias, sinks)


def _t5_bucket(dist):
    n = jnp.maximum(dist, 0)
    log_ratio = jnp.log(jnp.maximum(n, 1).astype(F32) / REL_EXACT) / math.log(REL_MAX_DIST / REL_EXACT)
    large = jnp.minimum(REL_EXACT + (log_ratio * (REL_BUCKETS - REL_EXACT)).astype(jnp.int32), REL_BUCKETS - 1)
    return jnp.where(n < REL_EXACT, n, large)


def _bias_table(rel_bias, dist, valid, mq_pad=None, nk_pad=None):
    tq, nk = dist.shape
    bias = rel_bias.astype(F32)[_t5_bucket(jnp.asarray(dist, jnp.int32))]
    bias = jnp.where(jnp.asarray(valid)[..., None], bias, NEG_INF)
    bias = jnp.moveaxis(bias, -1, 0).reshape(N_KV_HEADS, Q_PER_KV * tq, nk)
    mq_pad = mq_pad or Q_PER_KV * tq
    nk_pad = nk_pad or nk
    bias = jnp.pad(bias, ((0, 0), (0, mq_pad - Q_PER_KV * tq), (0, 0)))
    return jnp.pad(bias, ((0, 0), (0, 0), (0, nk_pad - nk)), constant_values=NEG_INF)


def _sink_rows(sinks, tq, mq_pad=None):
    s = jnp.repeat(sinks.astype(F32).reshape(N_KV_HEADS, Q_PER_KV, 1), tq, axis=2)
    s = s.reshape(N_KV_HEADS, Q_PER_KV * tq, 1)
    mq_pad = mq_pad or Q_PER_KV * tq
    return jnp.pad(s, ((0, 0), (0, mq_pad - Q_PER_KV * tq), (0, 0)))


def _heads_q(q, nb, tq):
    q = q.reshape(nb, tq, N_KV_HEADS, Q_PER_KV, HEAD_DIM)
    return jnp.transpose(q, (0, 2, 3, 1, 4)).reshape(nb, N_KV_HEADS, Q_PER_KV * tq, HEAD_DIM)


def _unheads_o(o, nb, tq):
    o = o[:, :, :Q_PER_KV * tq].reshape(nb, N_KV_HEADS, Q_PER_KV, tq, HEAD_DIM)
    return jnp.transpose(o, (0, 3, 1, 2, 4)).reshape(nb * tq, ATTN_W)


def _body_attention(q, k, v, k_meta, v_meta, rel_bias, sinks, bsz, seq):
    nblk = seq // WINDOW
    i = np.arange(WINDOW)[:, None]
    c = np.arange(WINDOW)[None, :]
    tabs = []
    for m in (0, 1):
        q_pos = N_META + WINDOW * m + i
        meta_pos = np.arange(N_META)[None, :]
        dist = np.concatenate([q_pos - meta_pos, WINDOW + i - c, i - c], axis=1)
        valid = np.concatenate([np.ones((WINDOW, N_META), bool),
                                (c >= i) & (m > 0), c <= i], axis=1)
        tabs.append(_bias_table(rel_bias, dist, valid))
    bias = jnp.stack(tabs)
    tab_idx = jnp.asarray(np.tile(np.minimum(np.arange(nblk), 1), bsz), jnp.int32)

    def band(t, t_meta):
        t = t.reshape(bsz, nblk, WINDOW, N_KV_HEADS, HEAD_DIM)
        prev = jnp.concatenate([jnp.zeros_like(t[:, :1]), t[:, :-1]], axis=1)
        meta = jnp.broadcast_to(t_meta.reshape(bsz, 1, N_META, N_KV_HEADS, HEAD_DIM),
                                (bsz, nblk, N_META, N_KV_HEADS, HEAD_DIM))
        full = jnp.concatenate([meta, prev, t], axis=2)
        return jnp.transpose(full, (0, 1, 3, 2, 4)).reshape(bsz * nblk, N_KV_HEADS, N_META + 2 * WINDOW, HEAD_DIM)

    o = _attention(_heads_q(q, bsz * nblk, WINDOW), band(k, k_meta), band(v, v_meta),
                   bias, _sink_rows(sinks, WINDOW), tab_idx, 1)
    return _unheads_o(o, bsz * nblk, WINDOW)


def _meta_attention(q, k, v, rel_bias, sinks, bsz):
    i = np.arange(N_META)
    dist = i[:, None] - i[None, :]
    bias = _bias_table(rel_bias, dist, dist >= 0)[None]
    kv = lambda t: jnp.transpose(t.reshape(bsz, N_META, N_KV_HEADS, HEAD_DIM), (0, 2, 1, 3))
    o = _attention(_heads_q(q, bsz, N_META), kv(k), kv(v), bias, _sink_rows(sinks, N_META),
                   jnp.zeros((bsz,), jnp.int32), bsz)
    return _unheads_o(o, bsz, N_META)


def _sample_attention(q, k_all, v_all, meta_k, meta_v, rel_bias, sinks):
    bsz = q.shape[0]
    wc = k_all.shape[1] - 1
    nk = N_META + wc + 1
    nk_pad = -(-nk // LANES) * LANES
    mq_pad = 8
    k_pos = np.concatenate([np.arange(N_META), PAST_LEN - wc + np.arange(wc + 1)])
    dist = (PAST_LEN - k_pos)[None, :]
    is_meta = (np.arange(nk) < N_META)[None, :]
    valid = (dist >= 0) & (is_meta | ((k_pos[None, :] >= N_META) & (dist <= WINDOW)))
    bias = _bias_table(rel_bias, dist, valid, mq_pad, nk_pad)[None]

    def kv(meta, t):
        full = jnp.concatenate([meta.astype(F32), t], axis=1)
        full = jnp.pad(full, ((0, 0), (0, nk_pad - nk), (0, 0), (0, 0)))
        return jnp.transpose(full, (0, 2, 1, 3))

    qh = jnp.pad(_heads_q(q, bsz, 1), ((0, 0), (0, 0), (0, mq_pad - Q_PER_KV), (0, 0)))
    o = _attention(qh, kv(meta_k, k_all), kv(meta_v, v_all), bias, _sink_rows(sinks, 1, mq_pad),
                   jnp.zeros((bsz // 8,), jnp.int32), 8)
    return _unheads_o(o, bsz, 1)


def _s5_bu_kernel(u_ref, w_ref, re_ref, im_ref):
    r = jnp.dot(u_ref[...].astype(BF16), w_ref[...], preferred_element_type=F32)
    re_ref[...] = r[:, :SSM_N]
    im_ref[...] = r[:, SSM_N:]


def _s5_scan_kernel(bur_ref, bui_ref, h0r_ref, h0i_ref, ar_ref, ai_ref, hr_ref, hi_ref, cr_ref, ci_ref):
    @pl.when(pl.program_id(1) == 0)
    def _():
        cr_ref[...] = h0r_ref[0]
        ci_ref[...] = h0i_ref[0]

    ar = ar_ref[...]
    ai = ai_ref[...]

    def step(t, carry):
        hr, hi = carry
        nr = ar * hr - ai * hi + bur_ref[pl.ds(t, 1), :]
        ni = ar * hi + ai * hr + bui_ref[pl.ds(t, 1), :]
        hr_ref[pl.ds(t, 1), :] = nr
        hi_ref[pl.ds(t, 1), :] = ni
        return nr, ni

    hr, hi = lax.fori_loop(0, bur_ref.shape[0], step, (cr_ref[...], ci_ref[...]), unroll=8)
    cr_ref[...] = hr
    ci_ref[...] = hi


def _s5_scan(bur, bui, h0r, h0i, ar, ai, bsz, seq, tt):
    tt = min(tt, seq)
    nt = seq // tt
    row = pl.BlockSpec((tt, SSM_N), lambda b, t: (b * nt + t, 0))
    st = pl.BlockSpec((1, 1, SSM_N), lambda b, t: (b, 0, 0))
    cst = pl.BlockSpec((1, SSM_N), lambda b, t: (0, 0))
    return pl.pallas_call(
        _s5_scan_kernel, grid=(bsz, nt), in_specs=[row, row, st, st, cst, cst], out_specs=[row, row],
        out_shape=[jax.ShapeDtypeStruct(bur.shape, F32)] * 2,
        scratch_shapes=[pltpu.VMEM((1, SSM_N), F32)] * 2,
        compiler_params=_params(("arbitrary", "arbitrary")), name="s5_scan",
    )(bur, bui, h0r, h0i, ar, ai)


def _s5_step_kernel(bur_ref, bui_ref, h0r_ref, h0i_ref, ar_ref, ai_ref, hr_ref, hi_ref):
    ar, ai, hr, hi = ar_ref[...], ai_ref[...], h0r_ref[...], h0i_ref[...]
    hr_ref[...] = ar * hr - ai * hi + bur_ref[...]
    hi_ref[...] = ar * hi + ai * hr + bui_ref[...]


def _s5_out_kernel(hr_ref, hi_ref, u_ref, wc_ref, d_ref, wg_ref, o_ref):
    y = (_dot(hr_ref[...], wc_ref[:SSM_N]) + _dot(hi_ref[...], wc_ref[SSM_N:])
         + d_ref[...] * u_ref[...])
    z = jax.nn.gelu(y)
    o_ref[...] = z * _sigmoid(_dot(z, wg_ref[...]))


def _block_diag(blocks):
    g, a, b = blocks.shape
    eye = jnp.eye(g, dtype=blocks.dtype)
    return (eye[:, None, :, None] * blocks[:, :, None, :]).reshape(g * a, g * b)


def _s5_weights(a_re, a_im, log_dt, b_re, b_im, c_re, c_im):
    a_re = a_re.astype(F32)
    a_im = a_im.astype(F32)
    dt = jnp.exp(log_dt.astype(F32))[:, None]
    mag = jnp.exp(a_re * dt)
    ab_re = mag * jnp.cos(a_im * dt)
    ab_im = mag * jnp.sin(a_im * dt)
    den = a_re * a_re + a_im * a_im
    nr = ab_re - 1.0
    cf_re = (nr * a_re + ab_im * a_im) / den
    cf_im = (ab_im * a_re - nr * a_im) / den
    b_re = b_re.astype(F32)
    b_im = b_im.astype(F32)
    bb_re = cf_re[..., None] * b_re - cf_im[..., None] * b_im
    bb_im = cf_re[..., None] * b_im + cf_im[..., None] * b_re
    w_b = jnp.concatenate([_block_diag(jnp.swapaxes(bb_re, 1, 2)),
                           _block_diag(jnp.swapaxes(bb_im, 1, 2))], axis=1)
    w_c = jnp.concatenate([_block_diag(jnp.swapaxes(c_re.astype(F32), 1, 2)),
                           -_block_diag(jnp.swapaxes(c_im.astype(F32), 1, 2))], axis=0)
    return ab_re.reshape(1, SSM_N), ab_im.reshape(1, SSM_N), w_b.astype(BF16), w_c.astype(BF16)


def _seg_ones():
    r = lax.broadcasted_iota(jnp.int32, (RWKV_WIDTH, RWKV_WIDTH), 0) // RWKV_HEAD
    c = lax.broadcasted_iota(jnp.int32, (RWKV_WIDTH, RWKV_WIDTH), 1) // RWKV_HEAD
    return (r == c).astype(BF16)


def _rwkv_prep_kernel(pc_ref, prev_ref, mu_ref, vec_ref, w2_ref, a2_ref, g2_ref,
                      r_ref, k_ref, v_ref, lw_ref, kk_ref, bb_ref, g_ref):
    pc = pc_ref[...]
    xm = pc + (prev_ref[...] - pc) * mu_ref[...]
    rw = RWKV_WIDTH
    xr, xk, xv, xl = xm[:, :rw], xm[:, rw:2 * rw], xm[:, 2 * rw:3 * rw], xm[:, 3 * rw:]
    vec = vec_ref[...]
    wpre = -(vec[RV_W0:RV_W0 + 1] + _dot(jnp.tanh(xl), w2_ref[...]))
    softplus = jnp.maximum(wpre, 0.0) + jnp.log(1.0 + jnp.exp(-jnp.abs(wpre)))
    lw_ref[...] = -jnp.exp(-softplus - 0.5)
    a = _sigmoid(vec[RV_A0:RV_A0 + 1] + _dot(xl, a2_ref[...]))
    g_ref[...] = _dot(_sigmoid(xl), g2_ref[...])
    kk = xk * vec[RV_KK:RV_KK + 1]
    norm = jnp.sqrt(_dot_exact_rhs(kk * kk, _seg_ones()))
    kk = kk / jnp.maximum(norm, 1e-12)
    r_ref[...] = xr
    k_ref[...] = xk * (1.0 + (a - 1.0) * vec[RV_KA:RV_KA + 1])
    v_ref[...] = xv
    kk_ref[...] = kk
    bb_ref[...] = kk * a


def _rwkv_post_kernel(y_ref, r_ref, k_ref, v_ref, g_ref, vec_ref, o_ref):
    ones = _seg_ones()
    vec = vec_ref[...]
    y = y_ref[...]
    yc = y - _dot_exact_rhs(y, ones) * (1.0 / RWKV_HEAD)
    yv = _dot_exact_rhs(yc * yc, ones) * (1.0 / RWKV_HEAD)
    yn = yc * lax.rsqrt(yv + RWKV_GN_EPS) * vec[RV_GNW:RV_GNW + 1] + vec[RV_GNB:RV_GNB + 1]
    bonus = _dot_exact_rhs(r_ref[...] * k_ref[...] * vec[RV_RK:RV_RK + 1], ones) * v_ref[...]
    o_ref[...] = (yn + bonus) * g_ref[...]


def _rwkv_chunk_kernel(r_ref, k_ref, lw_ref, kk_ref, bb_ref, vt_ref, s0_ref, yt_ref, so_ref, s_ref, *, chunk):
    t = pl.program_id(1)

    @pl.when(t == 0)
    def _():
        s_ref[...] = s0_ref[0]

    c = chunk
    row = lax.broadcasted_iota(jnp.int32, (c, c), 0)
    col = lax.broadcasted_iota(jnp.int32, (c, c), 1)
    incl = (row >= col).astype(F32)
    strict = (row > col).astype(F32)
    eye = (row == col).astype(F32)
    for sub in range(r_ref.shape[1] // c):
        sl = slice(sub * c, (sub + 1) * c)
        lw = lw_ref[0, sl, :]
        cum = _dot_exact_lhs(incl.astype(BF16), lw)
        ecum = jnp.exp(cum)
        einv = jnp.exp(-cum)
        kt_all = kk_ref[0, sl, :] * jnp.exp(cum - lw)
        bt_all = bb_ref[0, sl, :] * einv
        kkt_all = k_ref[0, sl, :] * einv
        rt_all = r_ref[0, sl, :] * ecum
        for h in range(RWKV_HEADS):
            hs = slice(h * RWKV_HEAD, (h + 1) * RWKV_HEAD)
            kt, bt, kkt, rt = kt_all[:, hs], bt_all[:, hs], kkt_all[:, hs], rt_all[:, hs]
            vt = vt_ref[0, hs, sl]
            s = s_ref[h]
            a_b = strict * _dot_nt(kt, bt)
            a_k = strict * _dot_nt(kt, kkt)
            r_b = incl * _dot_nt(rt, bt)
            r_k = incl * _dot_nt(rt, kkt)
            pw = -a_b
            tinv = eye + pw
            n = 1
            while 2 * n < c:
                pw = _dot(pw, pw)
                tinv = tinv + _dot(tinv, pw)
                n *= 2
            w1t = _dot_nt(_dot_nt(vt, a_k), tinv)
            w2 = _dot(tinv, kt)
            ut = -(_dot_nt(s, w2) + w1t)
            yt_ref[0, hs, sl] = _dot_nt(s, rt) + _dot_nt(ut, r_b) + _dot_nt(vt, r_k)
            s_ref[h] = (s + _dot(ut, bt) + _dot(vt, kkt)) * ecum[c - 1:c, hs]

    @pl.when(t == pl.num_programs(1) - 1)
    def _():
        so_ref[0] = s_ref[...]


def _rwkv_chunk(r, k, v, lw, kk, bb, s0, bsz, seq):
    tb = min(seq, 2 * RWKV_CHUNK)
    chunk = min(seq, RWKV_CHUNK)
    sh = lambda a: a.reshape(bsz, seq, RWKV_WIDTH)
    vt = jnp.swapaxes(sh(v), 1, 2)
    row = pl.BlockSpec((1, tb, RWKV_WIDTH), lambda b, t: (b, t, 0))
    colb = pl.BlockSpec((1, RWKV_WIDTH, tb), lambda b, t: (b, 0, t))
    st = pl.BlockSpec((1, RWKV_HEADS, RWKV_HEAD, RWKV_HEAD), lambda b, t: (b, 0, 0, 0))
    yt, s_out = pl.pallas_call(
        functools.partial(_rwkv_chunk_kernel, chunk=chunk), grid=(bsz, seq // tb),
        in_specs=[row, row, row, row, row, colb, st], out_specs=[colb, st],
        out_shape=[jax.ShapeDtypeStruct((bsz, RWKV_WIDTH, seq), F32), jax.ShapeDtypeStruct(s0.shape, F32)],
        scratch_shapes=[pltpu.VMEM((RWKV_HEADS, RWKV_HEAD, RWKV_HEAD), F32)],
        compiler_params=_params(("arbitrary", "arbitrary")), name="rwkv_chunk",
    )(sh(r), sh(k), sh(lw), sh(kk), sh(bb), vt, s0)
    return jnp.swapaxes(yt, 1, 2).reshape(bsz * seq, RWKV_WIDTH), s_out


def _rwkv_step_kernel(s_ref, r_ref, k_ref, lw_ref, kk_ref, bb_ref, v_ref, so_ref, y_ref):
    s = s_ref[...]
    sa = jnp.sum(s * (-kk_ref[...]), axis=-1, keepdims=True)
    s = s * jnp.exp(lw_ref[...]) + sa * bb_ref[...] + v_ref[...] * k_ref[...]
    so_ref[...] = s
    y_ref[...] = jnp.sum(s * r_ref[...], axis=-1, keepdims=True)


def _rwkv_step(r, k, v, lw, kk, bb, s0):
    bsz = r.shape[0]
    p = bsz * RWKV_HEADS
    nb = min(64, p)
    rowv = lambda a: a.reshape(p, 1, RWKV_HEAD)
    rs = pl.BlockSpec((nb, 1, RWKV_HEAD), lambda i: (i, 0, 0))
    cs = pl.BlockSpec((nb, RWKV_HEAD, 1), lambda i: (i, 0, 0))
    ss = pl.BlockSpec((nb, RWKV_HEAD, RWKV_HEAD), lambda i: (i, 0, 0))
    s_out, y = pl.pallas_call(
        _rwkv_step_kernel, grid=(p // nb,), in_specs=[ss, rs, rs, rs, rs, rs, cs], out_specs=[ss, cs],
        out_shape=[jax.ShapeDtypeStruct((p, RWKV_HEAD, RWKV_HEAD), F32),
                   jax.ShapeDtypeStruct((p, RWKV_HEAD, 1), F32)],
        compiler_params=_params(("parallel",)), name="rwkv_step",
    )(s0.reshape(p, RWKV_HEAD, RWKV_HEAD), rowv(r), rowv(k), rowv(lw), rowv(kk), rowv(bb),
      v.reshape(p, RWKV_HEAD, 1))
    return y.reshape(bsz, RWKV_WIDTH), s_out.reshape(s0.shape)


def _merge_kernel(h_ref, oa_ref, ob_ref, oc_ref, wg_ref, wb_ref, wo_ref, g_ref, b_ref, o_ref):
    h = h_ref[...]
    gates = _sigmoid(jnp.dot(h.astype(BF16), wg_ref[...], preferred_element_type=F32))
    d = D_MODEL
    merged = (gates[:, :d] * _dot(oa_ref[...], wb_ref[:ATTN_W])
              + gates[:, d:2 * d] * _dot(ob_ref[...], wb_ref[ATTN_W:ATTN_W + SSM_WIDTH])
              + gates[:, 2 * d:] * _dot(oc_ref[...], wb_ref[ATTN_W + SSM_WIDTH:]))
    mix = _dot(merged, wo_ref[...])
    o_ref[...] = _layer_norm(DEEPNORM_ALPHA * h + mix, g_ref[...], b_ref[...])


def _ffn_kernel(be_ref, x_ref, wg_ref, wu_ref, wd_ref, g_ref, b_ref, o_ref, xb_ref, acc_ref, *, post_ln):
    del be_ref
    j = pl.program_id(1)

    @pl.when(j == 0)
    def _():
        xb_ref[...] = x_ref[...].astype(BF16)
        acc_ref[...] = jnp.zeros_like(acc_ref)

    xb = xb_ref[...]
    gate = jnp.dot(xb, wg_ref[0], preferred_element_type=F32)
    up = jnp.dot(xb, wu_ref[0], preferred_element_type=F32)
    act = gate * _sigmoid(gate) * up
    acc_ref[...] += jnp.dot(act.astype(BF16), wd_ref[0], preferred_element_type=F32)

    @pl.when(j == pl.num_programs(1) - 1)
    def _():
        if post_ln:
            o_ref[...] = _layer_norm(DEEPNORM_ALPHA * x_ref[...] + acc_ref[...], g_ref[...], b_ref[...])
        else:
            o_ref[...] = acc_ref[...]


def _ffn(x, block_e, w_in, w_down, ln_g, ln_b, blk, tf, post_ln):
    rows = x.shape[0]
    assert rows % blk == 0
    f = w_down.shape[1]
    nf = f // tf
    grid_spec = pltpu.PrefetchScalarGridSpec(
        num_scalar_prefetch=1, grid=(rows // blk, nf),
        in_specs=[
            pl.BlockSpec((blk, D_MODEL), lambda i, j, be: (i, 0)),
            pl.BlockSpec((1, D_MODEL, tf), lambda i, j, be: (be[i], 0, j)),
            pl.BlockSpec((1, D_MODEL, tf), lambda i, j, be: (be[i], 0, nf + j)),
            pl.BlockSpec((1, tf, D_MODEL), lambda i, j, be: (be[i], j, 0)),
            pl.BlockSpec((1, D_MODEL), lambda i, j, be: (0, 0)),
            pl.BlockSpec((1, D_MODEL), lambda i, j, be: (0, 0)),
        ],
        out_specs=pl.BlockSpec((blk, D_MODEL), lambda i, j, be: (i, 0)),
        scratch_shapes=[pltpu.VMEM((blk, D_MODEL), BF16), pltpu.VMEM((blk, D_MODEL), F32)],
    )
    return pl.pallas_call(
        functools.partial(_ffn_kernel, post_ln=post_ln), grid_spec=grid_spec,
        out_shape=jax.ShapeDtypeStruct((rows, D_MODEL), F32),
        compiler_params=_params(("arbitrary", "arbitrary")), name="ffn",
    )(block_e, x, w_in, w_in, w_down, ln_g, ln_b)


def _router_kernel(h_ref, w_ref, e_ref, g_ref):
    logits = jnp.dot(h_ref[...], w_ref[...], preferred_element_type=F32, precision=lax.Precision.HIGHEST)
    lane = lax.broadcasted_iota(jnp.int32, logits.shape, 1)
    lg = jnp.where(lane < N_EXPERTS, logits, -jnp.inf)
    m1 = jnp.max(lg, axis=-1, keepdims=True)
    i1 = jnp.min(jnp.where(lg == m1, lane, LANES), axis=-1, keepdims=True)
    lg2 = jnp.where(lane == i1, -jnp.inf, lg)
    m2 = jnp.max(lg2, axis=-1, keepdims=True)
    i2 = jnp.min(jnp.where(lg2 == m2, lane, LANES), axis=-1, keepdims=True)
    e2 = jnp.exp(m2 - m1)
    den = 1.0 + e2
    e_ref[...] = jnp.where(lane == 0, i1, jnp.where(lane == 1, i2, 0))
    g_ref[...] = jnp.where(lane == 0, 1.0 / den, jnp.where(lane == 1, e2 / den, 0.0))


def _gather_kernel(idx_ref, src_ref, o_ref, sem):
    rows = o_ref.shape[0]

    def row_copy(r, tok):
        return pltpu.make_async_copy(src_ref.at[pl.ds(tok, 1)], o_ref.at[pl.ds(r, 1)], sem)

    def start(r, c):
        row_copy(r, idx_ref[0, 0, r]).start()
        return c

    def wait(r, c):
        row_copy(r, 0).wait()
        return c

    lax.fori_loop(0, rows, start, 0)
    lax.fori_loop(0, rows, wait, 0)


def _gather_rows(src, idx, blk):
    n = idx.shape[0]
    assert n % blk == 0
    d = src.shape[1]
    return pl.pallas_call(
        _gather_kernel, grid=(n // blk,),
        in_specs=[pl.BlockSpec((1, 1, blk), lambda i: (i, 0, 0), memory_space=pltpu.SMEM),
                  pl.BlockSpec(memory_space=pl.ANY)],
        out_specs=pl.BlockSpec((blk, d), lambda i: (i, 0)),
        out_shape=jax.ShapeDtypeStruct((n, d), src.dtype),
        scratch_shapes=[pltpu.SemaphoreType.DMA(())],
        compiler_params=_params(("arbitrary",)), name="gather_rows",
    )(idx.reshape(n // blk, 1, blk), src)


def _combine_kernel(h_ref, a_ref, b_ref, gate_ref, g_ref, bias_ref, o_ref):
    gate = gate_ref[...]
    f = a_ref[...] * gate[:, 0:1] + b_ref[...] * gate[:, 1:2]
    o_ref[...] = _layer_norm(DEEPNORM_ALPHA * h_ref[...] + f, g_ref[...], bias_ref[...])


def _moe(h, router_pad, w_in, w_down, ln_g, ln_b, tm, blk, tf, gblk):
    n = h.shape[0]
    e_pad, gate = _rows_call(_router_kernel, [h], [router_pad], (LANES, LANES), tm,
                             out_dtypes=[jnp.int32, F32], name="router")
    flat_e = e_pad[:, :TOP_K].reshape(-1)
    n_assign = n * TOP_K
    n_blocks = -(-(n_assign + N_EXPERTS * (blk - 1)) // blk)
    onehot = (flat_e[:, None] == jnp.arange(N_EXPERTS, dtype=jnp.int32)[None, :]).astype(jnp.int32)
    csum = jnp.cumsum(onehot, axis=0)
    rank = jnp.sum(csum * onehot, axis=1) - 1
    counts = csum[-1]
    padded = (counts + blk - 1) // blk * blk
    pad_end = jnp.cumsum(padded)
    dest = (pad_end - padded)[flat_e] + rank
    rows_tok = jnp.zeros((n_blocks * blk,), jnp.int32).at[dest].set(
        jnp.arange(n_assign, dtype=jnp.int32) // TOP_K)
    block_e = jnp.minimum(jnp.searchsorted(pad_end, jnp.arange(n_blocks, dtype=jnp.int32) * blk, side='right'),
                          N_EXPERTS - 1).astype(jnp.int32)
    x_rows = _gather_rows(h, rows_tok, gblk)
    y_rows = _ffn(x_rows, block_e, w_in, w_down, ln_g, ln_b, blk, tf, post_ln=False)
    dest2 = dest.reshape(n, TOP_K)
    back = _gather_rows(y_rows, jnp.concatenate([dest2[:, 0], dest2[:, 1]]), gblk)
    nt = n // tm
    row = lambda off: pl.BlockSpec((tm, D_MODEL), lambda i: (i + off, 0))
    cst = pl.BlockSpec((1, D_MODEL), lambda i: (0, 0))
    return pl.pallas_call(
        _combine_kernel, grid=(nt,),
        in_specs=[row(0), row(0), row(nt), pl.BlockSpec((tm, LANES), lambda i: (i, 0)), cst, cst],
        out_specs=row(0), out_shape=jax.ShapeDtypeStruct((n, D_MODEL), F32),
        compiler_params=_params(("parallel",)), name="moe_combine",
    )(h, back, back, gate, ln_g, ln_b)


def _layer(l, hb, hs, pre_ln, bsz, seq, dec, cache, states, prm):
    (cache_meta_k, cache_meta_v, cache_win_k, cache_win_v) = cache
    (state_ssm_re, state_ssm_im, state_wkv, state_shift) = states
    n_meta = bsz * N_META
    tm_b, tm_s = 256, hs.shape[0]
    w_in = prm['w_in'][l]
    w_mix = w_in[:, :MIX_COLS].astype(BF16)
    w_gate = w_in[:, MIX_COLS:].astype(BF16)
    ln_in_g, ln_in_b = prm['ln_in_g'].reshape(1, -1), prm['ln_in_b'].reshape(1, -1)

    hb, (q_b, k_b, v_b, u_b, pc_b) = _proj(hb, ln_in_g, ln_in_b, w_mix, pre_ln, tm_b)
    hs, (q_s, k_s, v_s, u_s, pc_s) = _proj(hs, ln_in_g, ln_in_b, w_mix, pre_ln, tm_s)

    rel_bias, sinks = prm['rel_bias'], prm['attn_sinks'][l]
    k_meta, v_meta = k_s[:n_meta], v_s[:n_meta]
    oa_b = _body_attention(q_b, k_b, v_b, k_meta, v_meta, rel_bias, sinks, bsz, seq)
    oa_m = _meta_attention(q_s[:n_meta], k_meta, v_meta, rel_bias, sinks, bsz)
    kd = lambda t: t[n_meta:].reshape(dec, 1, N_KV_HEADS, HEAD_DIM)
    k_all = jnp.concatenate([cache_win_k[l].astype(F32), kd(k_s)], axis=1)
    v_all = jnp.concatenate([cache_win_v[l].astype(F32), kd(v_s)], axis=1)
    oa_d = _sample_attention(q_s[n_meta:], k_all, v_all, cache_meta_k[l], cache_meta_v[l], rel_bias, sinks)
    oa_s = jnp.concatenate([oa_m, oa_d], axis=0)
    kv4 = lambda t, b: t.reshape(b, -1, N_KV_HEADS, HEAD_DIM)
    attn_out = (kv4(k_meta, bsz), kv4(v_meta, bsz), kv4(k_b, bsz)[:, seq - WINDOW:], kv4(v_b, bsz)[:, seq - WINDOW:],
                k_all[:, 1:], v_all[:, 1:])

    ar, ai, w_b, w_c = _s5_weights(prm['ssm_a_re'][l], prm['ssm_a_im'][l], prm['ssm_log_dt'][l],
                                   prm['ssm_b_re'][l], prm['ssm_b_im'][l], prm['ssm_c_re'][l], prm['ssm_c_im'][l])
    bur_b, bui_b = _rows_call(_s5_bu_kernel, [u_b], [w_b], (SSM_N, SSM_N), 512, name="s5_bu")
    bur_s, bui_s = _rows_call(_s5_bu_kernel, [u_s], [w_b], (SSM_N, SSM_N), tm_s, name="s5_bu")
    zero_h = jnp.zeros((bsz, 1, SSM_N), F32)
    hr_m, hi_m = _s5_scan(bur_s[:n_meta], bui_s[:n_meta], zero_h, zero_h, ar, ai, bsz, N_META, N_META)
    last = lambda t: t.reshape(bsz, -1, SSM_N)[:, -1:]
    hr_b, hi_b = _s5_scan(bur_b, bui_b, last(hr_m), last(hi_m), ar, ai, bsz, seq, 512)
    hr_d, hi_d = _rows_call(_s5_step_kernel,
                            [bur_s[n_meta:], bui_s[n_meta:], state_ssm_re[l].reshape(dec, SSM_N).astype(F32),
                             state_ssm_im[l].reshape(dec, SSM_N).astype(F32)], [ar, ai], (SSM_N, SSM_N), dec,
                            name="s5_step")
    hr_s = jnp.concatenate([hr_m, hr_d], axis=0)
    hi_s = jnp.concatenate([hi_m, hi_d], axis=0)
    s5_consts = [w_c, prm['ssm_d'][l].reshape(1, -1).astype(F32), prm['ssm_w_glu'][l].astype(BF16)]
    (ob_b,) = _rows_call(_s5_out_kernel, [hr_b, hi_b, u_b], s5_consts, (SSM_WIDTH,), 512, name="s5_out")
    (ob_s,) = _rows_call(_s5_out_kernel, [hr_s, hi_s, u_s], s5_consts, (SSM_WIDTH,), tm_s, name="s5_out")
    st4 = lambda t, b: t.reshape(b, SSM_GROUPS, SSM_STATE)
    ssm_out = (st4(last(hr_b), bsz), st4(last(hi_b), bsz), st4(hr_d, dec), st4(hi_d, dec))

    pc_m = pc_s[:n_meta].reshape(bsz, N_META, RWKV_COLS)
    pc_b3 = pc_b.reshape(bsz, seq, RWKV_COLS)
    prev_m = jnp.concatenate([jnp.zeros((bsz, 1, RWKV_COLS), F32), pc_m[:, :-1]], axis=1)
    prev_b = jnp.concatenate([pc_m[:, -1:], pc_b3[:, :-1]], axis=1).reshape(bsz * seq, RWKV_COLS)
    prev_s = jnp.concatenate([prev_m.reshape(n_meta, RWKV_COLS), state_shift[l].astype(F32)], axis=0)
    pad_rows = lambda w, lo: jnp.pad(w.astype(F32), ((lo, RWKV_LORA - lo - w.shape[0]), (0, 0))).astype(BF16)
    vec = jnp.pad(prm['rwkv_vec'][l].astype(F32), ((0, 1), (0, 0)))
    prep_consts = [prm['rwkv_mu'][l].reshape(1, -1).astype(F32), vec,
                   pad_rows(prm['rwkv_w2'][l], 0), pad_rows(prm['rwkv_a2'][l], RWKV_W_LORA),
                   pad_rows(prm['rwkv_g2'][l], RWKV_W_LORA + RWKV_A_LORA)]
    w7 = (RWKV_WIDTH,) * 7
    r_b, kx_b, vx_b, lw_b, kk_b, bb_b, g_b = _rows_call(_rwkv_prep_kernel, [pc_b, prev_b], prep_consts, w7, 512,
                                                        name="rwkv_prep")
    r_s, kx_s, vx_s, lw_s, kk_s, bb_s, g_s = _rows_call(_rwkv_prep_kernel, [pc_s, prev_s], prep_consts, w7, tm_s,
                                                        name="rwkv_prep")
    mrows = lambda t: t[:n_meta]
    drows = lambda t: t[n_meta:]
    zero_s = jnp.zeros((bsz, RWKV_HEADS, RWKV_HEAD, RWKV_HEAD), F32)
    y_m, s_m = _rwkv_chunk(mrows(r_s), mrows(kx_s), mrows(vx_s), mrows(lw_s), mrows(kk_s), mrows(bb_s),
                           zero_s, bsz, N_META)
    y_b, s_b = _rwkv_chunk(r_b, kx_b, vx_b, lw_b, kk_b, bb_b, s_m, bsz, seq)
    y_d, s_d = _rwkv_step(drows(r_s), drows(kx_s), drows(vx_s), drows(lw_s), drows(kk_s), drows(bb_s),
                          state_wkv[l].astype(F32))
    y_s = jnp.concatenate([y_m, y_d], axis=0)
    (oc_b,) = _rows_call(_rwkv_post_kernel, [y_b, r_b, kx_b, vx_b, g_b], [vec], (RWKV_WIDTH,), 512,
                         name="rwkv_post")
    (oc_s,) = _rows_call(_rwkv_post_kernel, [y_s, r_s, kx_s, vx_s, g_s], [vec], (RWKV_WIDTH,), tm_s,
                         name="rwkv_post")
    rwkv_out = (s_b, s_d, pc_b3[:, -1], pc_s[n_meta:])

    ln_g, ln_b = prm['ln_g'][l].astype(F32), prm['ln_b'][l].astype(F32)
    merge_consts = [w_gate, prm['w_branch'][l].astype(BF16), prm['w_out'][l].astype(BF16), ln_g[0:1], ln_b[0:1]]
    (hb,) = _rows_call(_merge_kernel, [hb, oa_b, ob_b, oc_b], merge_consts, (D_MODEL,), tm_b, name="merge")
    (hs,) = _rows_call(_merge_kernel, [hs, oa_s, ob_s, oc_s], merge_consts, (D_MODEL,), tm_s, name="merge")

    if l % 2 == 0:
        w_ffn_in = prm['ffn_w_in'][l // 2].astype(BF16)[None]
        w_ffn_down = prm['ffn_w_down'][l // 2].astype(BF16)[None]
        blk_b = min(1024, hb.shape[0])
        hb = _ffn(hb, jnp.zeros((hb.shape[0] // blk_b,), jnp.int32), w_ffn_in, w_ffn_down,
                  ln_g[1:2], ln_b[1:2], blk_b, 256, post_ln=True)
        hs = _ffn(hs, jnp.zeros((1,), jnp.int32), w_ffn_in, w_ffn_down, ln_g[1:2], ln_b[1:2], tm_s, 256,
                  post_ln=True)
    else:
        router_pad = jnp.pad(prm['moe_router'][l // 2].astype(F32), ((0, 0), (0, LANES - N_EXPERTS)))
        w_moe_in = prm['moe_w_in'][l // 2].astype(BF16)
        w_moe_down = prm['moe_w_down'][l // 2].astype(BF16)
        hb = _moe(hb, router_pad, w_moe_in, w_moe_down, ln_g[1:2], ln_b[1:2], 512, 1024, 512, 512)
        hs = _moe(hs, router_pad, w_moe_in, w_moe_down, ln_g[1:2], ln_b[1:2], tm_s, 64, 512, 64)
    return hb, hs, attn_out, ssm_out, rwkv_out


def kernel(x_prompt, x_sample, cache_meta_k, cache_meta_v, cache_win_k, cache_win_v, state_ssm_re, state_ssm_im, state_wkv, state_shift, meta_tokens, ln_in_g, ln_in_b, w_in, rel_bias, attn_sinks, ssm_a_re, ssm_a_im, ssm_log_dt, ssm_b_re, ssm_b_im, ssm_c_re, ssm_c_im, ssm_d, ssm_w_glu, rwkv_mu, rwkv_vec, rwkv_w2, rwkv_a2, rwkv_g2, w_branch, w_out, ln_g, ln_b, ffn_w_in, ffn_w_down, moe_router, moe_w_in, moe_w_down):
    bsz, seq, _ = x_prompt.shape
    dec = x_sample.shape[0]
    assert x_sample.shape[1] == 1 and seq % (2 * RWKV_CHUNK) == 0
    prm = dict(ln_in_g=ln_in_g.astype(F32), ln_in_b=ln_in_b.astype(F32), w_in=w_in, rel_bias=rel_bias,
               attn_sinks=attn_sinks, ssm_a_re=ssm_a_re, ssm_a_im=ssm_a_im, ssm_log_dt=ssm_log_dt,
               ssm_b_re=ssm_b_re, ssm_b_im=ssm_b_im, ssm_c_re=ssm_c_re, ssm_c_im=ssm_c_im, ssm_d=ssm_d,
               ssm_w_glu=ssm_w_glu, rwkv_mu=rwkv_mu, rwkv_vec=rwkv_vec, rwkv_w2=rwkv_w2, rwkv_a2=rwkv_a2,
               rwkv_g2=rwkv_g2, w_branch=w_branch, w_out=w_out, ln_g=ln_g, ln_b=ln_b, ffn_w_in=ffn_w_in,
               ffn_w_down=ffn_w_down, moe_router=moe_router, moe_w_in=moe_w_in, moe_w_down=moe_w_down)
    hb = x_prompt.reshape(bsz * seq, D_MODEL).astype(F32)
    meta = jnp.broadcast_to(meta_tokens.astype(F32)[None], (bsz, N_META, D_MODEL)).reshape(bsz * N_META, D_MODEL)
    hs = jnp.concatenate([meta, x_sample.reshape(dec, D_MODEL).astype(F32)], axis=0)
    cache = (cache_meta_k, cache_meta_v, cache_win_k, cache_win_v)
    states = (state_ssm_re, state_ssm_im, state_wkv, state_shift)
    attn_outs, ssm_outs, rwkv_outs = [], [], []
    for l in range(DEPTH):
        hb, hs, a_o, s_o, r_o = _layer(l, hb, hs, l == 0, bsz, seq, dec, cache, states, prm)
        attn_outs.append(a_o)
        ssm_outs.append(s_o)
        rwkv_outs.append(r_o)
    stack = lambda outs, i: jnp.stack([o[i] for o in outs])
    y_prompt = hb.reshape(bsz, seq, D_MODEL)
    y_sample = hs[bsz * N_META:].reshape(dec, 1, D_MODEL)
    return (y_prompt, y_sample,
            stack(attn_outs, 0), stack(attn_outs, 1), stack(attn_outs, 2), stack(attn_outs, 3),
            stack(attn_outs, 4), stack(attn_outs, 5),
            stack(ssm_outs, 0), stack(ssm_outs, 1), stack(ssm_outs, 2), stack(ssm_outs, 3),
            stack(rwkv_outs, 0), stack(rwkv_outs, 1), stack(rwkv_outs, 2), stack(rwkv_outs, 3))
```

```python
import functools
import math

import numpy as np
import jax
import jax.numpy as jnp
from jax import lax
from jax.experimental import pallas as pl
from jax.experimental.pallas import tpu as pltpu

F32 = jnp.float32
BF16 = jnp.bfloat16

D_MODEL = 1024
DEPTH = 2
PAST_LEN = 16384
N_META = 16
WINDOW = 128
N_HEADS = 8
N_KV_HEADS = 2
HEAD_DIM = 64
Q_PER_KV = N_HEADS // N_KV_HEADS
ATTN_W = N_HEADS * HEAD_DIM
KV_W = N_KV_HEADS * HEAD_DIM
ATTN_SCALE = HEAD_DIM ** -0.5
REL_BUCKETS = 32
REL_EXACT = REL_BUCKETS // 2
REL_MAX_DIST = 128
SSM_GROUP = 16
SSM_GROUPS = 16
SSM_WIDTH = SSM_GROUP * SSM_GROUPS
SSM_STATE = 64
SSM_N = SSM_GROUPS * SSM_STATE
RWKV_HEAD = 64
RWKV_HEADS = 4
RWKV_WIDTH = RWKV_HEAD * RWKV_HEADS
RWKV_W_LORA = 32
RWKV_A_LORA = 32
RWKV_G_LORA = 64
RWKV_LORA = RWKV_W_LORA + RWKV_A_LORA + RWKV_G_LORA
RWKV_COLS = 3 * RWKV_WIDTH + RWKV_LORA
RV_W0, RV_A0, RV_KK, RV_KA, RV_RK, RV_GNW, RV_GNB = 0, 1, 2, 3, 4, 5, 6
N_BRANCH = 3
MIX_COLS = ATTN_W + 2 * KV_W + SSM_WIDTH + RWKV_COLS
N_EXPERTS = 8
TOP_K = 2
LN_EPS = 1e-5
RWKV_GN_EPS = 64e-5
NEG_INF = -1e30
DEEPNORM_ALPHA = (2 * DEPTH) ** 0.25

LANES = 128
SUBLANES = 8
VMEM_LIMIT = 48 * 1024 * 1024
RWKV_CHUNK = 64


def _params(sem):
    return pltpu.CompilerParams(dimension_semantics=sem, vmem_limit_bytes=VMEM_LIMIT)


def _dot(a, b):
    return jnp.dot(a.astype(BF16), b.astype(BF16), preferred_element_type=F32)


def _dot_nt(a, b):
    return lax.dot_general(a.astype(BF16), b.astype(BF16), (((1,), (1,)), ((), ())),
                           preferred_element_type=F32)


def _split3(x):
    h1 = x.astype(BF16)
    r1 = x - h1.astype(F32)
    h2 = r1.astype(BF16)
    h3 = (r1 - h2.astype(F32)).astype(BF16)
    return h1, h2, h3


def _dot_exact_rhs(x, m):
    h1, h2, h3 = _split3(x)
    dot = functools.partial(jnp.dot, preferred_element_type=F32)
    return dot(h1, m) + dot(h2, m) + dot(h3, m)


def _dot_exact_lhs(m, x):
    h1, h2, h3 = _split3(x)
    dot = functools.partial(jnp.dot, preferred_element_type=F32)
    return dot(m, h1) + dot(m, h2) + dot(m, h3)


def _layer_norm(x, g, b):
    mu = jnp.mean(x, axis=-1, keepdims=True)
    xc = x - mu
    var = jnp.mean(xc * xc, axis=-1, keepdims=True)
    return xc * lax.rsqrt(var + LN_EPS) * g + b


def _sigmoid(x):
    return 1.0 / (1.0 + jnp.exp(-x))


def _rows_call(body, row_inputs, const_inputs, out_widths, tm, out_dtypes=None, name=None):
    n = row_inputs[0].shape[0]
    tm = min(tm, n)
    assert n % tm == 0, (n, tm)
    out_dtypes = out_dtypes or [F32] * len(out_widths)
    in_specs = [pl.BlockSpec((tm, a.shape[1]), lambda i: (i, 0)) for a in row_inputs]
    in_specs += [pl.BlockSpec(c.shape, lambda i, nd=c.ndim: (0,) * nd) for c in const_inputs]
    tails = [w if isinstance(w, tuple) else (w,) for w in out_widths]
    out_specs = [pl.BlockSpec((tm,) + w, lambda i, nd=len(w): (i,) + (0,) * nd) for w in tails]
    out_shape = [jax.ShapeDtypeStruct((n,) + w, dt) for w, dt in zip(tails, out_dtypes)]
    return pl.pallas_call(
        body, grid=(n // tm,), in_specs=in_specs, out_specs=out_specs, out_shape=out_shape,
        compiler_params=_params(("parallel",)), name=name,
    )(*row_inputs, *const_inputs)


PROJ_WIDTHS = (ATTN_W, KV_W, KV_W, SSM_WIDTH, RWKV_COLS)


def _proj_kernel(x_ref, g_ref, b_ref, w_ref, *out_refs, pre_ln):
    x = x_ref[...]
    if pre_ln:
        x = _layer_norm(x, g_ref[...], b_ref[...])
        out_refs[0][...] = x
        out_refs = out_refs[1:]
    xb = x.astype(BF16)
    col = 0
    for o_ref in out_refs:
        n = o_ref.shape[-1]
        o_ref[...] = jnp.dot(xb, w_ref[:, col:col + n], preferred_element_type=F32)
        col += n


def _proj(x, ln_g, ln_b, w_mix, pre_ln, tm):
    widths = ((D_MODEL,) if pre_ln else ()) + PROJ_WIDTHS
    outs = _rows_call(functools.partial(_proj_kernel, pre_ln=pre_ln), [x], [ln_g, ln_b, w_mix],
                      widths, tm, name="proj")
    if pre_ln:
        return outs[0], outs[1:]
    return x, outs


def _attn_kernel(tab_ref, q_ref, k_ref, v_ref, bias_ref, sink_ref, o_ref):
    del tab_ref
    for bb in range(q_ref.shape[0]):
        for h in range(N_KV_HEADS):
            q = q_ref[bb, h].astype(BF16)
            k = k_ref[bb, h].astype(BF16)
            v = v_ref[bb, h].astype(BF16)
            s = lax.dot_general(q, k, (((1,), (1,)), ((), ())), preferred_element_type=F32)
            s = s * ATTN_SCALE + bias_ref[0, h]
            sk = sink_ref[h]
            m = jnp.maximum(jnp.max(s, axis=-1, keepdims=True), sk)
            p = jnp.exp(s - m)
            den = jnp.sum(p, axis=-1, keepdims=True) + jnp.exp(sk - m)
            o = jnp.dot(p.astype(BF16), v, preferred_element_type=F32)
            o_ref[bb, h] = o / den


def _attention(q, k, v, bias, sinks, tab_idx, bblk):
    p, _, mq, _ = q.shape
    nk = k.shape[2]
    assert p % bblk == 0
    grid_spec = pltpu.PrefetchScalarGridSpec(
        num_scalar_prefetch=1, grid=(p // bblk,),
        in_specs=[
            pl.BlockSpec((bblk, N_KV_HEADS, mq, HEAD_DIM), lambda i, t: (i, 0, 0, 0)),
            pl.BlockSpec((bblk, N_KV_HEADS, nk, HEAD_DIM), lambda i, t: (i, 0, 0, 0)),
            pl.BlockSpec((bblk, N_KV_HEADS, nk, HEAD_DIM), lambda i, t: (i, 0, 0, 0)),
            pl.BlockSpec((1, N_KV_HEADS, mq, nk), lambda i, t: (t[i], 0, 0, 0)),
            pl.BlockSpec((N_KV_HEADS, mq, 1), lambda i, t: (0, 0, 0)),
        ],
        out_specs=pl.BlockSpec((bblk, N_KV_HEADS, mq, HEAD_DIM), lambda i, t: (i, 0, 0, 0)),
    )
    return pl.pallas_call(
        _attn_kernel, grid_spec=grid_spec,
        out_shape=jax.ShapeDtypeStruct(q.shape, F32),
        compiler_params=_params(("arbitrary",)), name="attention",
    )(tab_idx, q, k, v, bias, sinks)


def _t5_bucket(dist):
    n = jnp.maximum(dist, 0)
    log_ratio = jnp.log(jnp.maximum(n, 1).astype(F32) / REL_EXACT) / math.log(REL_MAX_DIST / REL_EXACT)
    large = jnp.minimum(REL_EXACT + (log_ratio * (REL_BUCKETS - REL_EXACT)).astype(jnp.int32), REL_BUCKETS - 1)
    return jnp.where(n < REL_EXACT, n, large)


def _bias_table(rel_bias, dist, valid, mq_pad=None, nk_pad=None):
    tq, nk = dist.shape
    bias = rel_bias.astype(F32)[_t5_bucket(jnp.asarray(dist, jnp.int32))]
    bias = jnp.where(jnp.asarray(valid)[..., None], bias, NEG_INF)
    bias = jnp.moveaxis(bias, -1, 0).reshape(N_KV_HEADS, Q_PER_KV * tq, nk)
    mq_pad = mq_pad or Q_PER_KV * tq
    nk_pad = nk_pad or nk
    bias = jnp.pad(bias, ((0, 0), (0, mq_pad - Q_PER_KV * tq), (0, 0)))
    return jnp.pad(bias, ((0, 0), (0, 0), (0, nk_pad - nk)), constant_values=NEG_INF)


def _sink_rows(sinks, tq, mq_pad=None):
    s = jnp.repeat(sinks.astype(F32).reshape(N_KV_HEADS, Q_PER_KV, 1), tq, axis=2)
    s = s.reshape(N_KV_HEADS, Q_PER_KV * tq, 1)
    mq_pad = mq_pad or Q_PER_KV * tq
    return jnp.pad(s, ((0, 0), (0, mq_pad - Q_PER_KV * tq), (0, 0)))


def _heads_q(q, nb, tq):
    q = q.reshape(nb, tq, N_KV_HEADS, Q_PER_KV, HEAD_DIM)
    return jnp.transpose(q, (0, 2, 3, 1, 4)).reshape(nb, N_KV_HEADS, Q_PER_KV * tq, HEAD_DIM)


def _unheads_o(o, nb, tq):
    o = o[:, :, :Q_PER_KV * tq].reshape(nb, N_KV_HEADS, Q_PER_KV, tq, HEAD_DIM)
    return jnp.transpose(o, (0, 3, 1, 2, 4)).reshape(nb * tq, ATTN_W)


BODY_KEYS = N_META + 2 * WINDOW + 16


def _body_attn_kernel(q_ref, ko_ref, kp_ref, km_ref, vo_ref, vp_ref, vm_ref, bias_ref, o_ref):
    kv_heads = range(N_KV_HEADS)
    hs = [slice(h * HEAD_DIM, (h + 1) * HEAD_DIM) for h in kv_heads]
    heads = [[h * Q_PER_KV + g for g in range(Q_PER_KV)] for h in kv_heads]
    pad = jnp.zeros((BODY_KEYS - N_META - 2 * WINDOW, HEAD_DIM), F32)
    ones = jnp.ones((BODY_KEYS, HEAD_DIM), BF16)
    k = [jnp.concatenate([km_ref[:, hs[h]], kp_ref[:, hs[h]], ko_ref[:, hs[h]], pad], axis=0).astype(BF16)
         for h in kv_heads]
    v = [jnp.concatenate([vm_ref[:, hs[h]], vp_ref[:, hs[h]], vo_ref[:, hs[h]], pad], axis=0).astype(BF16)
         for h in kv_heads]
    q = [jnp.concatenate([q_ref[:, qh * HEAD_DIM:(qh + 1) * HEAD_DIM] for qh in heads[h]], axis=0).astype(BF16)
         for h in kv_heads]
    s = [lax.dot_general(q[h], k[h], (((1,), (1,)), ((), ())), preferred_element_type=F32) for h in kv_heads]
    s = [s[h] * ATTN_SCALE + bias_ref[0, h] for h in kv_heads]
    p = [jnp.exp(s[h] - jnp.max(s[h], axis=-1, keepdims=True)).astype(BF16) for h in kv_heads]
    o = [jnp.dot(p[h], v[h], preferred_element_type=F32) / jnp.dot(p[h], ones, preferred_element_type=F32)
         for h in kv_heads]
    for h in kv_heads:
        for g, qh in enumerate(heads[h]):
            o_ref[:, qh * HEAD_DIM:(qh + 1) * HEAD_DIM] = o[h][g * WINDOW:(g + 1) * WINDOW]


def _body_attention(q, k, v, k_meta, v_meta, rel_bias, sinks, bsz, seq):
    nblk = seq // WINDOW
    i = np.arange(WINDOW)[:, None]
    c = np.arange(WINDOW)[None, :]
    sink_col = _sink_rows(sinks, WINDOW)
    tabs = []
    for m in (0, 1):
        q_pos = N_META + WINDOW * m + i
        meta_pos = np.arange(N_META)[None, :]
        dist = np.concatenate([q_pos - meta_pos, WINDOW + i - c, i - c], axis=1)
        valid = np.concatenate([np.ones((WINDOW, N_META), bool),
                                (c >= i) & (m > 0), c <= i], axis=1)
        tab = _bias_table(rel_bias, dist, valid, nk_pad=BODY_KEYS)
        tabs.append(tab.at[:, :, N_META + 2 * WINDOW].set(sink_col[:, :, 0]))
    bias = jnp.stack(tabs)
    nk = BODY_KEYS
    mq = Q_PER_KV * WINDOW
    own = lambda w: pl.BlockSpec((WINDOW, w), lambda b, m: (b * nblk + m, 0))
    prev = lambda w: pl.BlockSpec((WINDOW, w), lambda b, m: (b * nblk + jnp.maximum(m - 1, 0), 0))
    meta = pl.BlockSpec((N_META, KV_W), lambda b, m: (b, 0))
    return pl.pallas_call(
        _body_attn_kernel, grid=(bsz, nblk),
        in_specs=[own(ATTN_W), own(KV_W), prev(KV_W), meta, own(KV_W), prev(KV_W), meta,
                  pl.BlockSpec((1, N_KV_HEADS, mq, nk), lambda b, m: (jnp.minimum(m, 1), 0, 0, 0))],
        out_specs=own(ATTN_W), out_shape=jax.ShapeDtypeStruct(q.shape, F32),
        compiler_params=_params(("parallel", "arbitrary")), name="body_attention",
    )(q, k, k, k_meta, v, v, v_meta, bias)


def _meta_attention(q, k, v, rel_bias, sinks, bsz):
    i = np.arange(N_META)
    dist = i[:, None] - i[None, :]
    bias = _bias_table(rel_bias, dist, dist >= 0)[None]
    kv = lambda t: jnp.transpose(t.reshape(bsz, N_META, N_KV_HEADS, HEAD_DIM), (0, 2, 1, 3))
    o = _attention(_heads_q(q, bsz, N_META), kv(k), kv(v), bias, _sink_rows(sinks, N_META),
                   jnp.zeros((bsz,), jnp.int32), bsz)
    return _unheads_o(o, bsz, N_META)


def _sample_attention(q, k_all, v_all, meta_k, meta_v, rel_bias, sinks):
    bsz = q.shape[0]
    wc = k_all.shape[1] - 1
    nk = N_META + wc + 1
    nk_pad = -(-nk // LANES) * LANES
    mq_pad = 8
    k_pos = np.concatenate([np.arange(N_META), PAST_LEN - wc + np.arange(wc + 1)])
    dist = (PAST_LEN - k_pos)[None, :]
    is_meta = (np.arange(nk) < N_META)[None, :]
    valid = (dist >= 0) & (is_meta | ((k_pos[None, :] >= N_META) & (dist <= WINDOW)))
    bias = _bias_table(rel_bias, dist, valid, mq_pad, nk_pad)[None]

    def kv(meta, t):
        full = jnp.concatenate([meta.astype(F32), t], axis=1)
        full = jnp.pad(full, ((0, 0), (0, nk_pad - nk), (0, 0), (0, 0)))
        return jnp.transpose(full, (0, 2, 1, 3))

    qh = jnp.pad(_heads_q(q, bsz, 1), ((0, 0), (0, 0), (0, mq_pad - Q_PER_KV), (0, 0)))
    o = _attention(qh, kv(meta_k, k_all), kv(meta_v, v_all), bias, _sink_rows(sinks, 1, mq_pad),
                   jnp.zeros((bsz // 8,), jnp.int32), 8)
    return _unheads_o(o, bsz, 1)


def _s5_bu_kernel(u_ref, w_ref, re_ref, im_ref):
    r = jnp.dot(u_ref[...].astype(BF16), w_ref[...], preferred_element_type=F32)
    re_ref[...] = r[:, :SSM_N]
    im_ref[...] = r[:, SSM_N:]


def _s5_scan_kernel(bur_ref, bui_ref, h0r_ref, h0i_ref, ar_ref, ai_ref, hr_ref, hi_ref, cr_ref, ci_ref):
    @pl.when(pl.program_id(1) == 0)
    def _():
        cr_ref[...] = h0r_ref[0]
        ci_ref[...] = h0i_ref[0]

    ar = ar_ref[...]
    ai = ai_ref[...]

    def step(t, carry):
        hr, hi = carry
        nr = ar * hr - ai * hi + bur_ref[pl.ds(t, 1), :]
        ni = ar * hi + ai * hr + bui_ref[pl.ds(t, 1), :]
        hr_ref[pl.ds(t, 1), :] = nr
        hi_ref[pl.ds(t, 1), :] = ni
        return nr, ni

    hr, hi = lax.fori_loop(0, bur_ref.shape[0], step, (cr_ref[...], ci_ref[...]), unroll=8)
    cr_ref[...] = hr
    ci_ref[...] = hi


def _s5_scan(bur, bui, h0r, h0i, ar, ai, bsz, seq, tt):
    tt = min(tt, seq)
    nt = seq // tt
    row = pl.BlockSpec((tt, SSM_N), lambda b, t: (b * nt + t, 0))
    st = pl.BlockSpec((1, 1, SSM_N), lambda b, t: (b, 0, 0))
    cst = pl.BlockSpec((1, SSM_N), lambda b, t: (0, 0))
    return pl.pallas_call(
        _s5_scan_kernel, grid=(bsz, nt), in_specs=[row, row, st, st, cst, cst], out_specs=[row, row],
        out_shape=[jax.ShapeDtypeStruct(bur.shape, F32)] * 2,
        scratch_shapes=[pltpu.VMEM((1, SSM_N), F32)] * 2,
        compiler_params=_params(("arbitrary", "arbitrary")), name="s5_scan",
    )(bur, bui, h0r, h0i, ar, ai)


def _s5_step_kernel(bur_ref, bui_ref, h0r_ref, h0i_ref, ar_ref, ai_ref, hr_ref, hi_ref):
    ar, ai, hr, hi = ar_ref[...], ai_ref[...], h0r_ref[...], h0i_ref[...]
    hr_ref[...] = ar * hr - ai * hi + bur_ref[...]
    hi_ref[...] = ar * hi + ai * hr + bui_ref[...]


def _s5_out_kernel(hr_ref, hi_ref, u_ref, wc_ref, d_ref, wg_ref, o_ref):
    y = (_dot(hr_ref[...], wc_ref[:SSM_N]) + _dot(hi_ref[...], wc_ref[SSM_N:])
         + d_ref[...] * u_ref[...])
    z = jax.nn.gelu(y)
    o_ref[...] = z * _sigmoid(_dot(z, wg_ref[...]))


def _block_diag(blocks):
    g, a, b = blocks.shape
    eye = jnp.eye(g, dtype=blocks.dtype)
    return (eye[:, None, :, None] * blocks[:, :, None, :]).reshape(g * a, g * b)


def _s5_weights(a_re, a_im, log_dt, b_re, b_im, c_re, c_im):
    a_re = a_re.astype(F32)
    a_im = a_im.astype(F32)
    dt = jnp.exp(log_dt.astype(F32))[:, None]
    mag = jnp.exp(a_re * dt)
    ab_re = mag * jnp.cos(a_im * dt)
    ab_im = mag * jnp.sin(a_im * dt)
    den = a_re * a_re + a_im * a_im
    nr = ab_re - 1.0
    cf_re = (nr * a_re + ab_im * a_im) / den
    cf_im = (ab_im * a_re - nr * a_im) / den
    b_re = b_re.astype(F32)
    b_im = b_im.astype(F32)
    bb_re = cf_re[..., None] * b_re - cf_im[..., None] * b_im
    bb_im = cf_re[..., None] * b_im + cf_im[..., None] * b_re
    w_b = jnp.concatenate([_block_diag(jnp.swapaxes(bb_re, 1, 2)),
                           _block_diag(jnp.swapaxes(bb_im, 1, 2))], axis=1)
    w_c = jnp.concatenate([_block_diag(jnp.swapaxes(c_re.astype(F32), 1, 2)),
                           -_block_diag(jnp.swapaxes(c_im.astype(F32), 1, 2))], axis=0)
    return ab_re.reshape(1, SSM_N), ab_im.reshape(1, SSM_N), w_b.astype(BF16), w_c.astype(BF16)


def _seg_ones():
    r = lax.broadcasted_iota(jnp.int32, (RWKV_WIDTH, RWKV_WIDTH), 0) // RWKV_HEAD
    c = lax.broadcasted_iota(jnp.int32, (RWKV_WIDTH, RWKV_WIDTH), 1) // RWKV_HEAD
    return (r == c).astype(BF16)


def _rwkv_prep_kernel(pc_ref, prev_ref, mu_ref, vec_ref, w2_ref, a2_ref, g2_ref,
                      r_ref, k_ref, v_ref, lw_ref, kk_ref, bb_ref, g_ref):
    pc = pc_ref[...]
    xm = pc + (prev_ref[...] - pc) * mu_ref[...]
    rw = RWKV_WIDTH
    xr, xk, xv, xl = xm[:, :rw], xm[:, rw:2 * rw], xm[:, 2 * rw:3 * rw], xm[:, 3 * rw:]
    vec = vec_ref[...]
    wpre = -(vec[RV_W0:RV_W0 + 1] + _dot(jnp.tanh(xl), w2_ref[...]))
    softplus = jnp.maximum(wpre, 0.0) + jnp.log(1.0 + jnp.exp(-jnp.abs(wpre)))
    lw_ref[...] = -jnp.exp(-softplus - 0.5)
    a = _sigmoid(vec[RV_A0:RV_A0 + 1] + _dot(xl, a2_ref[...]))
    g_ref[...] = _dot(_sigmoid(xl), g2_ref[...])
    kk = xk * vec[RV_KK:RV_KK + 1]
    norm = jnp.sqrt(_dot_exact_rhs(kk * kk, _seg_ones()))
    kk = kk / jnp.maximum(norm, 1e-12)
    r_ref[...] = xr
    k_ref[...] = xk * (1.0 + (a - 1.0) * vec[RV_KA:RV_KA + 1])
    v_ref[...] = xv
    kk_ref[...] = kk
    bb_ref[...] = kk * a


def _rwkv_post_kernel(y_ref, r_ref, k_ref, v_ref, g_ref, vec_ref, o_ref):
    ones = _seg_ones()
    vec = vec_ref[...]
    y = y_ref[...]
    yc = y - _dot_exact_rhs(y, ones) * (1.0 / RWKV_HEAD)
    yv = _dot_exact_rhs(yc * yc, ones) * (1.0 / RWKV_HEAD)
    yn = yc * lax.rsqrt(yv + RWKV_GN_EPS) * vec[RV_GNW:RV_GNW + 1] + vec[RV_GNB:RV_GNB + 1]
    bonus = _dot_exact_rhs(r_ref[...] * k_ref[...] * vec[RV_RK:RV_RK + 1], ones) * v_ref[...]
    o_ref[...] = (yn + bonus) * g_ref[...]


def _rwkv_chunk_kernel(r_ref, k_ref, lw_ref, kk_ref, bb_ref, vt_ref, s0_ref, yt_ref, so_ref, s_ref, *, chunk):
    t = pl.program_id(1)

    @pl.when(t == 0)
    def _():
        s_ref[...] = s0_ref[...]

    c = chunk
    row = lax.broadcasted_iota(jnp.int32, (c, c), 0)
    col = lax.broadcasted_iota(jnp.int32, (c, c), 1)
    incl = (row >= col).astype(F32)
    strict = (row > col).astype(F32)
    eye = (row == col).astype(F32)
    nb = r_ref.shape[0]
    nsub = r_ref.shape[1] // c
    heads = range(RWKV_HEADS)
    hsl = [slice(h * RWKV_HEAD, (h + 1) * RWKV_HEAD) for h in heads]
    tsl = [slice(sub * c, (sub + 1) * c) for sub in range(nsub)]
    seqs = [(b, h) for b in range(nb) for h in heads]
    units = [(b, sub, h) for b in range(nb) for sub in range(nsub) for h in heads]

    scaled = {}
    for b in range(nb):
        for sub in range(nsub):
            lw = lw_ref[b, tsl[sub], :]
            cum = _dot_exact_lhs(incl.astype(BF16), lw)
            ecum = jnp.exp(cum)
            einv = jnp.exp(-cum)
            scaled[b, sub] = (kk_ref[b, tsl[sub], :] * jnp.exp(cum - lw), bb_ref[b, tsl[sub], :] * einv,
                              k_ref[b, tsl[sub], :] * einv, r_ref[b, tsl[sub], :] * ecum, ecum[c - 1:c, :])
    kt = {u: scaled[u[0], u[1]][0][:, hsl[u[2]]] for u in units}
    bt = {u: scaled[u[0], u[1]][1][:, hsl[u[2]]] for u in units}
    kkt = {u: scaled[u[0], u[1]][2][:, hsl[u[2]]] for u in units}
    rt = {u: scaled[u[0], u[1]][3][:, hsl[u[2]]] for u in units}
    vt = {u: vt_ref[u[0], hsl[u[2]], tsl[u[1]]] for u in units}
    a_b = {u: strict * _dot_nt(kt[u], bt[u]) for u in units}
    a_k = {u: strict * _dot_nt(kt[u], kkt[u]) for u in units}
    r_b = {u: incl * _dot_nt(rt[u], bt[u]) for u in units}
    r_k = {u: incl * _dot_nt(rt[u], kkt[u]) for u in units}
    pw = {u: -a_b[u] for u in units}
    tinv = {u: eye + pw[u] for u in units}
    n = 1
    while 2 * n < c:
        pw = {u: _dot(pw[u], pw[u]) for u in units}
        tinv = {u: tinv[u] + _dot(tinv[u], pw[u]) for u in units}
        n *= 2
    x = {u: _dot_nt(vt[u], a_k[u]) for u in units}
    w1t = {u: _dot_nt(x[u], tinv[u]) for u in units}
    w2 = {u: _dot(tinv[u], kt[u]) for u in units}
    yt_local = {u: _dot_nt(vt[u], r_k[u]) for u in units}
    s_local = {u: _dot(vt[u], kkt[u]) for u in units}

    s = {q: s_ref[q[0], q[1]] for q in seqs}
    for sub in range(nsub):
        ut = {(b, h): -(_dot_nt(s[b, h], w2[b, sub, h]) + w1t[b, sub, h]) for b, h in seqs}
        for b, h in seqs:
            yt_ref[b, hsl[h], tsl[sub]] = (_dot_nt(s[b, h], rt[b, sub, h]) + _dot_nt(ut[b, h], r_b[b, sub, h])
                                           + yt_local[b, sub, h])
        s = {(b, h): (s[b, h] + _dot(ut[b, h], bt[b, sub, h]) + s_local[b, sub, h]) * scaled[b, sub][4][:, hsl[h]]
             for b, h in seqs}
    for b, h in seqs:
        s_ref[b, h] = s[b, h]

    @pl.when(t == pl.num_programs(1) - 1)
    def _():
        so_ref[...] = s_ref[...]


def _rwkv_chunk(r, k, v, lw, kk, bb, s0, bsz, seq, nb):
    tb = min(seq, 2 * RWKV_CHUNK)
    chunk = min(seq, RWKV_CHUNK)
    assert bsz % nb == 0
    sh = lambda a: a.reshape(bsz, seq, RWKV_WIDTH)
    vt = jnp.swapaxes(sh(v), 1, 2)
    row = pl.BlockSpec((nb, tb, RWKV_WIDTH), lambda b, t: (b, t, 0))
    colb = pl.BlockSpec((nb, RWKV_WIDTH, tb), lambda b, t: (b, 0, t))
    st = pl.BlockSpec((nb, RWKV_HEADS, RWKV_HEAD, RWKV_HEAD), lambda b, t: (b, 0, 0, 0))
    yt, s_out = pl.pallas_call(
        functools.partial(_rwkv_chunk_kernel, chunk=chunk), grid=(bsz // nb, seq // tb),
        in_specs=[row, row, row, row, row, colb, st], out_specs=[colb, st],
        out_shape=[jax.ShapeDtypeStruct((bsz, RWKV_WIDTH, seq), F32), jax.ShapeDtypeStruct(s0.shape, F32)],
        scratch_shapes=[pltpu.VMEM((nb, RWKV_HEADS, RWKV_HEAD, RWKV_HEAD), F32)],
        compiler_params=_params(("arbitrary", "arbitrary")), name="rwkv_chunk",
    )(sh(r), sh(k), sh(lw), sh(kk), sh(bb), vt, s0)
    return jnp.swapaxes(yt, 1, 2).reshape(bsz * seq, RWKV_WIDTH), s_out


def _rwkv_step_kernel(s_ref, r_ref, k_ref, lw_ref, kk_ref, bb_ref, v_ref, so_ref, y_ref):
    s = s_ref[...]
    sa = jnp.sum(s * (-kk_ref[...]), axis=-1, keepdims=True)
    s = s * jnp.exp(lw_ref[...]) + sa * bb_ref[...] + v_ref[...] * k_ref[...]
    so_ref[...] = s
    y_ref[...] = jnp.sum(s * r_ref[...], axis=-1, keepdims=True)


def _rwkv_step(r, k, v, lw, kk, bb, s0):
    bsz = r.shape[0]
    p = bsz * RWKV_HEADS
    nb = min(64, p)
    rowv = lambda a: a.reshape(p, 1, RWKV_HEAD)
    rs = pl.BlockSpec((nb, 1, RWKV_HEAD), lambda i: (i, 0, 0))
    cs = pl.BlockSpec((nb, RWKV_HEAD, 1), lambda i: (i, 0, 0))
    ss = pl.BlockSpec((nb, RWKV_HEAD, RWKV_HEAD), lambda i: (i, 0, 0))
    s_out, y = pl.pallas_call(
        _rwkv_step_kernel, grid=(p // nb,), in_specs=[ss, rs, rs, rs, rs, rs, cs], out_specs=[ss, cs],
        out_shape=[jax.ShapeDtypeStruct((p, RWKV_HEAD, RWKV_HEAD), F32),
                   jax.ShapeDtypeStruct((p, RWKV_HEAD, 1), F32)],
        compiler_params=_params(("parallel",)), name="rwkv_step",
    )(s0.reshape(p, RWKV_HEAD, RWKV_HEAD), rowv(r), rowv(k), rowv(lw), rowv(kk), rowv(bb),
      v.reshape(p, RWKV_HEAD, 1))
    return y.reshape(bsz, RWKV_WIDTH), s_out.reshape(s0.shape)


def _merge_kernel(h_ref, oa_ref, ob_ref, oc_ref, wg_ref, wb_ref, wo_ref, g_ref, b_ref, o_ref, *tile_refs):
    h = h_ref[...]
    gates = _sigmoid(jnp.dot(h.astype(BF16), wg_ref[...], preferred_element_type=F32))
    d = D_MODEL
    merged = (gates[:, :d] * _dot(oa_ref[...], wb_ref[:ATTN_W])
              + gates[:, d:2 * d] * _dot(ob_ref[...], wb_ref[ATTN_W:ATTN_W + SSM_WIDTH])
              + gates[:, 2 * d:] * _dot(oc_ref[...], wb_ref[ATTN_W + SSM_WIDTH:]))
    mix = _dot(merged, wo_ref[...])
    out = _layer_norm(DEEPNORM_ALPHA * h + mix, g_ref[...], b_ref[...])
    o_ref[...] = out
    for t_ref in tile_refs:
        for t in range(SUBLANES):
            t_ref[:, t, :] = out[:, t * LANES:(t + 1) * LANES]


def _ffn_kernel(be_ref, x_ref, wg_ref, wu_ref, wd_ref, g_ref, b_ref, o_ref, xb_ref, acc_ref, *, post_ln):
    del be_ref
    j = pl.program_id(1)

    @pl.when(j == 0)
    def _():
        xb_ref[...] = x_ref[...].astype(BF16)
        acc_ref[...] = jnp.zeros_like(acc_ref)

    xb = xb_ref[...]
    gate = jnp.dot(xb, wg_ref[0], preferred_element_type=F32)
    up = jnp.dot(xb, wu_ref[0], preferred_element_type=F32)
    act = gate * _sigmoid(gate) * up
    acc_ref[...] += jnp.dot(act.astype(BF16), wd_ref[0], preferred_element_type=F32)

    @pl.when(j == pl.num_programs(1) - 1)
    def _():
        if post_ln:
            o_ref[...] = _layer_norm(DEEPNORM_ALPHA * x_ref[...] + acc_ref[...], g_ref[...], b_ref[...])
        else:
            o_ref[...] = acc_ref[...]


def _ffn(x, block_e, w_in, w_down, ln_g, ln_b, blk, tf, post_ln):
    rows = x.shape[0]
    assert rows % blk == 0
    f = w_down.shape[1]
    nf = f // tf
    grid_spec = pltpu.PrefetchScalarGridSpec(
        num_scalar_prefetch=1, grid=(rows // blk, nf),
        in_specs=[
            pl.BlockSpec((blk, D_MODEL), lambda i, j, be: (i, 0)),
            pl.BlockSpec((1, D_MODEL, tf), lambda i, j, be: (be[i], 0, j)),
            pl.BlockSpec((1, D_MODEL, tf), lambda i, j, be: (be[i], 0, nf + j)),
            pl.BlockSpec((1, tf, D_MODEL), lambda i, j, be: (be[i], j, 0)),
            pl.BlockSpec((1, D_MODEL), lambda i, j, be: (0, 0)),
            pl.BlockSpec((1, D_MODEL), lambda i, j, be: (0, 0)),
        ],
        out_specs=pl.BlockSpec((blk, D_MODEL), lambda i, j, be: (i, 0)),
        scratch_shapes=[pltpu.VMEM((blk, D_MODEL), BF16), pltpu.VMEM((blk, D_MODEL), F32)],
    )
    return pl.pallas_call(
        functools.partial(_ffn_kernel, post_ln=post_ln), grid_spec=grid_spec,
        out_shape=jax.ShapeDtypeStruct((rows, D_MODEL), F32),
        compiler_params=_params(("arbitrary", "arbitrary")), name="ffn",
    )(block_e, x, w_in, w_in, w_down, ln_g, ln_b)


def _router_kernel(h_ref, w_ref, e_ref, g_ref):
    logits = jnp.dot(h_ref[...], w_ref[...], preferred_element_type=F32, precision=lax.Precision.HIGHEST)
    lane = lax.broadcasted_iota(jnp.int32, logits.shape, 1)
    lg = jnp.where(lane < N_EXPERTS, logits, -jnp.inf)
    m1 = jnp.max(lg, axis=-1, keepdims=True)
    i1 = jnp.min(jnp.where(lg == m1, lane, LANES), axis=-1, keepdims=True)
    lg2 = jnp.where(lane == i1, -jnp.inf, lg)
    m2 = jnp.max(lg2, axis=-1, keepdims=True)
    i2 = jnp.min(jnp.where(lg2 == m2, lane, LANES), axis=-1, keepdims=True)
    e2 = jnp.exp(m2 - m1)
    den = 1.0 + e2
    e_ref[...] = jnp.where(lane == 0, i1, jnp.where(lane == 1, i2, 0))
    g_ref[...] = jnp.where(lane == 0, 1.0 / den, jnp.where(lane == 1, e2 / den, 0.0))


def _moe_ffn_kernel(be_ref, nv_ref, tok_ref, nxt_ref, dst_ref, h_hbm, wg_ref, wu_ref, wd_ref, out_hbm,
                    xbuf, xb_ref, acc_ref, stage, gsem, ssem, cnt_ref):
    del be_ref
    i, j = pl.program_id(0), pl.program_id(1)
    nblk, nf = pl.num_programs(0), pl.num_programs(1)
    blk = xb_ref.shape[0]
    n_valid = nv_ref[0]
    valid = i < n_valid
    slot = lax.rem(i, 2)

    def start_gather(idx_ref, s):
        def body(r, c):
            pltpu.make_async_copy(h_hbm.at[idx_ref[0, 0, r]], xbuf.at[s, r], gsem.at[s]).start()
            return c
        lax.fori_loop(0, blk, body, 0, unroll=8)

    def drain_scatter():
        def body(r, c):
            pltpu.make_async_copy(stage.at[0], out_hbm.at[0], ssem).wait()
            return c
        lax.fori_loop(0, cnt_ref[0], body, 0)
        cnt_ref[0] = 0

    @pl.when((i == 0) & (j == 0))
    def _():
        cnt_ref[0] = 0
        start_gather(tok_ref, 0)

    @pl.when(valid & (j == 0))
    def _():
        @pl.when(i + 1 < n_valid)
        def _():
            start_gather(nxt_ref, 1 - slot)

        def wait_row(r, c):
            pltpu.make_async_copy(h_hbm.at[0], xbuf.at[slot, r], gsem.at[slot]).wait()
            return c
        lax.fori_loop(0, blk, wait_row, 0, unroll=8)
        for t in range(SUBLANES):
            xb_ref[:, t * LANES:(t + 1) * LANES] = xbuf[slot, :, t, :].astype(BF16)
        acc_ref[...] = jnp.zeros_like(acc_ref)

    @pl.when(valid)
    def _():
        xb = xb_ref[...]
        gate = jnp.dot(xb, wg_ref[0], preferred_element_type=F32)
        up = jnp.dot(xb, wu_ref[0], preferred_element_type=F32)
        act = gate * _sigmoid(gate) * up
        acc_ref[...] += jnp.dot(act.astype(BF16), wd_ref[0], preferred_element_type=F32)

    @pl.when(valid & (j == nf - 1))
    def _():
        drain_scatter()
        for t in range(SUBLANES):
            stage[:, t, :] = acc_ref[:, t * LANES:(t + 1) * LANES]

        def body(r, c):
            d = dst_ref[0, 0, r]

            @pl.when(d >= 0)
            def _():
                pltpu.make_async_copy(stage.at[r], out_hbm.at[d], ssem).start()
            return c + (d >= 0).astype(jnp.int32)
        cnt_ref[0] = lax.fori_loop(0, blk, body, 0, unroll=8)

    @pl.when((i == nblk - 1) & (j == nf - 1))
    def _():
        drain_scatter()


def _moe_ffn(h_tiles, rows_tok, rows_dst, block_e, n_valid, w_in, w_down, blk, tf):
    n = h_tiles.shape[0]
    rows = rows_tok.shape[0]
    nblk = rows // blk
    nf = w_down.shape[1] // tf
    idx3 = lambda a: a.reshape(nblk, 1, blk)
    smem = lambda fn: pl.BlockSpec((1, 1, blk), fn, memory_space=pltpu.SMEM)
    ftile = lambda i, j, nv: jnp.where(i < nv[0], j, nf - 1)
    grid_spec = pltpu.PrefetchScalarGridSpec(
        num_scalar_prefetch=2, grid=(nblk, nf),
        in_specs=[
            smem(lambda i, j, be, nv: (i, 0, 0)),
            smem(lambda i, j, be, nv: (jnp.minimum(i + 1, nblk - 1), 0, 0)),
            smem(lambda i, j, be, nv: (i, 0, 0)),
            pl.BlockSpec(memory_space=pl.ANY),
            pl.BlockSpec((1, D_MODEL, tf), lambda i, j, be, nv: (be[i], 0, ftile(i, j, nv))),
            pl.BlockSpec((1, D_MODEL, tf), lambda i, j, be, nv: (be[i], 0, nf + ftile(i, j, nv))),
            pl.BlockSpec((1, tf, D_MODEL), lambda i, j, be, nv: (be[i], ftile(i, j, nv), 0)),
        ],
        out_specs=pl.BlockSpec(memory_space=pl.ANY),
        scratch_shapes=[pltpu.VMEM((2, blk, SUBLANES, LANES), F32), pltpu.VMEM((blk, D_MODEL), BF16),
                        pltpu.VMEM((blk, D_MODEL), F32), pltpu.VMEM((blk, SUBLANES, LANES), F32),
                        pltpu.SemaphoreType.DMA((2,)), pltpu.SemaphoreType.DMA(()), pltpu.SMEM((1,), jnp.int32)],
    )
    return pl.pallas_call(
        _moe_ffn_kernel, grid_spec=grid_spec,
        out_shape=jax.ShapeDtypeStruct((TOP_K * n, SUBLANES, LANES), F32),
        compiler_params=_params(("arbitrary", "arbitrary")), name="moe_ffn",
    )(block_e, n_valid, idx3(rows_tok), idx3(rows_tok), idx3(rows_dst), h_tiles, w_in, w_in, w_down)


def _combine_kernel(h_ref, y_ref, gate_ref, g_ref, bias_ref, o_ref):
    gate = gate_ref[...]
    rows = lambda s: jnp.concatenate([y_ref[:, s, t, :] for t in range(SUBLANES)], axis=-1)
    f = rows(0) * gate[:, 0:1] + rows(1) * gate[:, 1:2]
    o_ref[...] = _layer_norm(DEEPNORM_ALPHA * h_ref[...] + f, g_ref[...], bias_ref[...])


def _moe(h, h_tiles, router_pad, w_in, w_down, ln_g, ln_b, tm, blk, tf):
    n = h.shape[0]
    e_pad, gate = _rows_call(_router_kernel, [h], [router_pad], (LANES, LANES), tm,
                             out_dtypes=[jnp.int32, F32], name="router")
    flat_e = e_pad[:, :TOP_K].reshape(-1)
    n_assign = n * TOP_K
    n_blocks = -(-(n_assign + N_EXPERTS * (blk - 1)) // blk)
    onehot = (flat_e[:, None] == jnp.arange(N_EXPERTS, dtype=jnp.int32)[None, :]).astype(jnp.int32)
    csum = jnp.cumsum(onehot, axis=0)
    rank = jnp.sum(csum * onehot, axis=1) - 1
    counts = csum[-1]
    padded = (counts + blk - 1) // blk * blk
    pad_end = jnp.cumsum(padded)
    dest = (pad_end - padded)[flat_e] + rank
    assign = jnp.arange(n_assign, dtype=jnp.int32)
    rows_dst = jnp.full((n_blocks * blk,), -1, jnp.int32).at[dest].set(assign, unique_indices=True)
    rows_tok = jnp.maximum(rows_dst, 0) // TOP_K
    block_e = jnp.minimum(jnp.searchsorted(pad_end, jnp.arange(n_blocks, dtype=jnp.int32) * blk, side='right'),
                          N_EXPERTS - 1).astype(jnp.int32)
    n_valid = (pad_end[-1:] // blk).astype(jnp.int32)
    y = _moe_ffn(h_tiles, rows_tok, rows_dst, block_e, n_valid, w_in, w_down, blk, tf)
    tm = min(tm, n)
    cst = pl.BlockSpec((1, D_MODEL), lambda i: (0, 0))
    return pl.pallas_call(
        _combine_kernel, grid=(n // tm,),
        in_specs=[pl.BlockSpec((tm, D_MODEL), lambda i: (i, 0)),
                  pl.BlockSpec((tm, TOP_K, SUBLANES, LANES), lambda i: (i, 0, 0, 0)),
                  pl.BlockSpec((tm, LANES), lambda i: (i, 0)), cst, cst],
        out_specs=pl.BlockSpec((tm, D_MODEL), lambda i: (i, 0)),
        out_shape=jax.ShapeDtypeStruct((n, D_MODEL), F32),
        compiler_params=_params(("parallel",)), name="moe_combine",
    )(h, y.reshape(n, TOP_K, SUBLANES, LANES), gate, ln_g, ln_b)


def _layer(l, hb, hs, pre_ln, bsz, seq, dec, cache, states, prm):
    (cache_meta_k, cache_meta_v, cache_win_k, cache_win_v) = cache
    (state_ssm_re, state_ssm_im, state_wkv, state_shift) = states
    n_meta = bsz * N_META
    tm_b, tm_s = 256, hs.shape[0]
    w_in = prm['w_in'][l]
    w_mix = w_in[:, :MIX_COLS].astype(BF16)
    w_gate = w_in[:, MIX_COLS:].astype(BF16)
    ln_in_g, ln_in_b = prm['ln_in_g'].reshape(1, -1), prm['ln_in_b'].reshape(1, -1)

    hb, (q_b, k_b, v_b, u_b, pc_b) = _proj(hb, ln_in_g, ln_in_b, w_mix, pre_ln, tm_b)
    hs, (q_s, k_s, v_s, u_s, pc_s) = _proj(hs, ln_in_g, ln_in_b, w_mix, pre_ln, tm_s)

    rel_bias, sinks = prm['rel_bias'], prm['attn_sinks'][l]
    k_meta, v_meta = k_s[:n_meta], v_s[:n_meta]
    oa_b = _body_attention(q_b, k_b, v_b, k_meta, v_meta, rel_bias, sinks, bsz, seq)
    oa_m = _meta_attention(q_s[:n_meta], k_meta, v_meta, rel_bias, sinks, bsz)
    kd = lambda t: t[n_meta:].reshape(dec, 1, N_KV_HEADS, HEAD_DIM)
    k_all = jnp.concatenate([cache_win_k[l].astype(F32), kd(k_s)], axis=1)
    v_all = jnp.concatenate([cache_win_v[l].astype(F32), kd(v_s)], axis=1)
    oa_d = _sample_attention(q_s[n_meta:], k_all, v_all, cache_meta_k[l], cache_meta_v[l], rel_bias, sinks)
    oa_s = jnp.concatenate([oa_m, oa_d], axis=0)
    kv4 = lambda t, b: t.reshape(b, -1, N_KV_HEADS, HEAD_DIM)
    attn_out = (kv4(k_meta, bsz), kv4(v_meta, bsz), kv4(k_b, bsz)[:, seq - WINDOW:], kv4(v_b, bsz)[:, seq - WINDOW:],
                k_all[:, 1:], v_all[:, 1:])

    ar, ai, w_b, w_c = _s5_weights(prm['ssm_a_re'][l], prm['ssm_a_im'][l], prm['ssm_log_dt'][l],
                                   prm['ssm_b_re'][l], prm['ssm_b_im'][l], prm['ssm_c_re'][l], prm['ssm_c_im'][l])
    bur_b, bui_b = _rows_call(_s5_bu_kernel, [u_b], [w_b], (SSM_N, SSM_N), 512, name="s5_bu")
    bur_s, bui_s = _rows_call(_s5_bu_kernel, [u_s], [w_b], (SSM_N, SSM_N), tm_s, name="s5_bu")
    zero_h = jnp.zeros((bsz, 1, SSM_N), F32)
    hr_m, hi_m = _s5_scan(bur_s[:n_meta], bui_s[:n_meta], zero_h, zero_h, ar, ai, bsz, N_META, N_META)
    last = lambda t: t.reshape(bsz, -1, SSM_N)[:, -1:]
    hr_b, hi_b = _s5_scan(bur_b, bui_b, last(hr_m), last(hi_m), ar, ai, bsz, seq, 512)
    hr_d, hi_d = _rows_call(_s5_step_kernel,
                            [bur_s[n_meta:], bui_s[n_meta:], state_ssm_re[l].reshape(dec, SSM_N).astype(F32),
                             state_ssm_im[l].reshape(dec, SSM_N).astype(F32)], [ar, ai], (SSM_N, SSM_N), dec,
                            name="s5_step")
    hr_s = jnp.concatenate([hr_m, hr_d], axis=0)
    hi_s = jnp.concatenate([hi_m, hi_d], axis=0)
    s5_consts = [w_c, prm['ssm_d'][l].reshape(1, -1).astype(F32), prm['ssm_w_glu'][l].astype(BF16)]
    (ob_b,) = _rows_call(_s5_out_kernel, [hr_b, hi_b, u_b], s5_consts, (SSM_WIDTH,), 512, name="s5_out")
    (ob_s,) = _rows_call(_s5_out_kernel, [hr_s, hi_s, u_s], s5_consts, (SSM_WIDTH,), tm_s, name="s5_out")
    st4 = lambda t, b: t.reshape(b, SSM_GROUPS, SSM_STATE)
    ssm_out = (st4(last(hr_b), bsz), st4(last(hi_b), bsz), st4(hr_d, dec), st4(hi_d, dec))

    pc_m = pc_s[:n_meta].reshape(bsz, N_META, RWKV_COLS)
    pc_b3 = pc_b.reshape(bsz, seq, RWKV_COLS)
    prev_m = jnp.concatenate([jnp.zeros((bsz, 1, RWKV_COLS), F32), pc_m[:, :-1]], axis=1)
    prev_b = jnp.concatenate([pc_m[:, -1:], pc_b3[:, :-1]], axis=1).reshape(bsz * seq, RWKV_COLS)
    prev_s = jnp.concatenate([prev_m.reshape(n_meta, RWKV_COLS), state_shift[l].astype(F32)], axis=0)
    pad_rows = lambda w, lo: jnp.pad(w.astype(F32), ((lo, RWKV_LORA - lo - w.shape[0]), (0, 0))).astype(BF16)
    vec = jnp.pad(prm['rwkv_vec'][l].astype(F32), ((0, 1), (0, 0)))
    prep_consts = [prm['rwkv_mu'][l].reshape(1, -1).astype(F32), vec,
                   pad_rows(prm['rwkv_w2'][l], 0), pad_rows(prm['rwkv_a2'][l], RWKV_W_LORA),
                   pad_rows(prm['rwkv_g2'][l], RWKV_W_LORA + RWKV_A_LORA)]
    w7 = (RWKV_WIDTH,) * 7
    r_b, kx_b, vx_b, lw_b, kk_b, bb_b, g_b = _rows_call(_rwkv_prep_kernel, [pc_b, prev_b], prep_consts, w7, 512,
                                                        name="rwkv_prep")
    r_s, kx_s, vx_s, lw_s, kk_s, bb_s, g_s = _rows_call(_rwkv_prep_kernel, [pc_s, prev_s], prep_consts, w7, tm_s,
                                                        name="rwkv_prep")
    mrows = lambda t: t[:n_meta]
    drows = lambda t: t[n_meta:]
    zero_s = jnp.zeros((bsz, RWKV_HEADS, RWKV_HEAD, RWKV_HEAD), F32)
    y_m, s_m = _rwkv_chunk(mrows(r_s), mrows(kx_s), mrows(vx_s), mrows(lw_s), mrows(kk_s), mrows(bb_s),
                           zero_s, bsz, N_META, bsz)
    y_b, s_b = _rwkv_chunk(r_b, kx_b, vx_b, lw_b, kk_b, bb_b, s_m, bsz, seq, bsz)
    y_d, s_d = _rwkv_step(drows(r_s), drows(kx_s), drows(vx_s), drows(lw_s), drows(kk_s), drows(bb_s),
                          state_wkv[l].astype(F32))
    y_s = jnp.concatenate([y_m, y_d], axis=0)
    (oc_b,) = _rows_call(_rwkv_post_kernel, [y_b, r_b, kx_b, vx_b, g_b], [vec], (RWKV_WIDTH,), 512,
                         name="rwkv_post")
    (oc_s,) = _rows_call(_rwkv_post_kernel, [y_s, r_s, kx_s, vx_s, g_s], [vec], (RWKV_WIDTH,), tm_s,
                         name="rwkv_post")
    rwkv_out = (s_b, s_d, pc_b3[:, -1], pc_s[n_meta:])

    ln_g, ln_b = prm['ln_g'][l].astype(F32), prm['ln_b'][l].astype(F32)
    merge_consts = [w_gate, prm['w_branch'][l].astype(BF16), prm['w_out'][l].astype(BF16), ln_g[0:1], ln_b[0:1]]
    moe_layer = l % 2 == 1
    merge_outs = (D_MODEL,) + (((SUBLANES, LANES),) if moe_layer else ())
    hb, *hb_tiles = _rows_call(_merge_kernel, [hb, oa_b, ob_b, oc_b], merge_consts, merge_outs, tm_b, name="merge")
    hs, *hs_tiles = _rows_call(_merge_kernel, [hs, oa_s, ob_s, oc_s], merge_consts, merge_outs, tm_s, name="merge")

    if not moe_layer:
        w_ffn_in = prm['ffn_w_in'][l // 2].astype(BF16)[None]
        w_ffn_down = prm['ffn_w_down'][l // 2].astype(BF16)[None]
        blk_b = min(1024, hb.shape[0])
        hb = _ffn(hb, jnp.zeros((hb.shape[0] // blk_b,), jnp.int32), w_ffn_in, w_ffn_down,
                  ln_g[1:2], ln_b[1:2], blk_b, 256, post_ln=True)
        hs = _ffn(hs, jnp.zeros((1,), jnp.int32), w_ffn_in, w_ffn_down, ln_g[1:2], ln_b[1:2], tm_s, 256,
                  post_ln=True)
    else:
        router_pad = jnp.pad(prm['moe_router'][l // 2].astype(F32), ((0, 0), (0, LANES - N_EXPERTS)))
        w_moe_in = prm['moe_w_in'][l // 2].astype(BF16)
        w_moe_down = prm['moe_w_down'][l // 2].astype(BF16)
        hb = _moe(hb, hb_tiles[0], router_pad, w_moe_in, w_moe_down, ln_g[1:2], ln_b[1:2], 512, 1024, 512)
        hs = _moe(hs, hs_tiles[0], router_pad, w_moe_in, w_moe_down, ln_g[1:2], ln_b[1:2], tm_s, 64, 512)
    return hb, hs, attn_out, ssm_out, rwkv_out


def kernel(x_prompt, x_sample, cache_meta_k, cache_meta_v, cache_win_k, cache_win_v, state_ssm_re, state_ssm_im, state_wkv, state_shift, meta_tokens, ln_in_g, ln_in_b, w_in, rel_bias, attn_sinks, ssm_a_re, ssm_a_im, ssm_log_dt, ssm_b_re, ssm_b_im, ssm_c_re, ssm_c_im, ssm_d, ssm_w_glu, rwkv_mu, rwkv_vec, rwkv_w2, rwkv_a2, rwkv_g2, w_branch, w_out, ln_g, ln_b, ffn_w_in, ffn_w_down, moe_router, moe_w_in, moe_w_down):
    bsz, seq, _ = x_prompt.shape
    dec = x_sample.shape[0]
    assert x_sample.shape[1] == 1 and seq % (2 * RWKV_CHUNK) == 0
    prm = dict(ln_in_g=ln_in_g.astype(F32), ln_in_b=ln_in_b.astype(F32), w_in=w_in, rel_bias=rel_bias,
               attn_sinks=attn_sinks, ssm_a_re=ssm_a_re, ssm_a_im=ssm_a_im, ssm_log_dt=ssm_log_dt,
               ssm_b_re=ssm_b_re, ssm_b_im=ssm_b_im, ssm_c_re=ssm_c_re, ssm_c_im=ssm_c_im, ssm_d=ssm_d,
               ssm_w_glu=ssm_w_glu, rwkv_mu=rwkv_mu, rwkv_vec=rwkv_vec, rwkv_w2=rwkv_w2, rwkv_a2=rwkv_a2,
               rwkv_g2=rwkv_g2, w_branch=w_branch, w_out=w_out, ln_g=ln_g, ln_b=ln_b, ffn_w_in=ffn_w_in,
               ffn_w_down=ffn_w_down, moe_router=moe_router, moe_w_in=moe_w_in, moe_w_down=moe_w_down)
    hb = x_prompt.reshape(bsz * seq, D_MODEL).astype(F32)
    meta = jnp.broadcast_to(meta_tokens.astype(F32)[None], (bsz, N_META, D_MODEL)).reshape(bsz * N_META, D_MODEL)
    hs = jnp.concatenate([meta, x_sample.reshape(dec, D_MODEL).astype(F32)], axis=0)
    cache = (cache_meta_k, cache_meta_v, cache_win_k, cache_win_v)
    states = (state_ssm_re, state_ssm_im, state_wkv, state_shift)
    attn_outs, ssm_outs, rwkv_outs = [], [], []
    for l in range(DEPTH):
        hb, hs, a_o, s_o, r_o = _layer(l, hb, hs, l == 0, bsz, seq, dec, cache, states, prm)
        attn_outs.append(a_o)
        ssm_outs.append(s_o)
        rwkv_outs.append(r_o)
    stack = lambda outs, i: jnp.stack([o[i] for o in outs])
    y_prompt = hb.reshape(bsz, seq, D_MODEL)
    y_sample = hs[bsz * N_META:].reshape(dec, 1, D_MODEL)
    return (y_prompt, y_sample,
            stack(attn_outs, 0), stack(attn_outs, 1), stack(attn_outs, 2), stack(attn_outs, 3),
            stack(attn_outs, 4), stack(attn_outs, 5),
            stack(ssm_outs, 0), stack(ssm_outs, 1), stack(ssm_outs, 2), stack(ssm_outs, 3),
            stack(rwkv_outs, 0), stack(rwkv_outs, 1), stack(rwkv_outs, 2), stack(rwkv_outs, 3))
```

```python
import functools
import math

import numpy as np
import jax
import jax.numpy as jnp
from jax import lax
from jax.experimental import pallas as pl
from jax.experimental.pallas import tpu as pltpu

F32 = jnp.float32
BF16 = jnp.bfloat16

D_MODEL = 1024
DEPTH = 2
PAST_LEN = 16384
N_META = 16
WINDOW = 128
N_HEADS = 8
N_KV_HEADS = 2
HEAD_DIM = 64
Q_PER_KV = N_HEADS // N_KV_HEADS
ATTN_W = N_HEADS * HEAD_DIM
KV_W = N_KV_HEADS * HEAD_DIM
ATTN_SCALE = HEAD_DIM ** -0.5
REL_BUCKETS = 32
REL_EXACT = REL_BUCKETS // 2
REL_MAX_DIST = 128
SSM_GROUP = 16
SSM_GROUPS = 16
SSM_WIDTH = SSM_GROUP * SSM_GROUPS
SSM_STATE = 64
SSM_N = SSM_GROUPS * SSM_STATE
RWKV_HEAD = 64
RWKV_HEADS = 4
RWKV_WIDTH = RWKV_HEAD * RWKV_HEADS
RWKV_W_LORA = 32
RWKV_A_LORA = 32
RWKV_G_LORA = 64
RWKV_LORA = RWKV_W_LORA + RWKV_A_LORA + RWKV_G_LORA
RWKV_COLS = 3 * RWKV_WIDTH + RWKV_LORA
RV_W0, RV_A0, RV_KK, RV_KA, RV_RK, RV_GNW, RV_GNB = 0, 1, 2, 3, 4, 5, 6
N_BRANCH = 3
MIX_COLS = ATTN_W + 2 * KV_W + SSM_WIDTH + RWKV_COLS
N_EXPERTS = 8
TOP_K = 2
LN_EPS = 1e-5
RWKV_GN_EPS = 64e-5
NEG_INF = -1e30
DEEPNORM_ALPHA = (2 * DEPTH) ** 0.25

LANES = 128
SUBLANES = 8
VMEM_LIMIT = 48 * 1024 * 1024
RWKV_CHUNK = 64


def _params(sem):
    return pltpu.CompilerParams(dimension_semantics=sem, vmem_limit_bytes=VMEM_LIMIT)


def _dot(a, b):
    return jnp.dot(a.astype(BF16), b.astype(BF16), preferred_element_type=F32)


def _dot_nt(a, b):
    return lax.dot_general(a.astype(BF16), b.astype(BF16), (((1,), (1,)), ((), ())),
                           preferred_element_type=F32)


def _split3(x):
    h1 = x.astype(BF16)
    r1 = x - h1.astype(F32)
    h2 = r1.astype(BF16)
    h3 = (r1 - h2.astype(F32)).astype(BF16)
    return h1, h2, h3


def _dot_exact_rhs(x, m):
    h1, h2, h3 = _split3(x)
    dot = functools.partial(jnp.dot, preferred_element_type=F32)
    return dot(h1, m) + dot(h2, m) + dot(h3, m)


def _dot_exact_lhs(m, x):
    h1, h2, h3 = _split3(x)
    dot = functools.partial(jnp.dot, preferred_element_type=F32)
    return dot(m, h1) + dot(m, h2) + dot(m, h3)


def _layer_norm(x, g, b):
    mu = jnp.mean(x, axis=-1, keepdims=True)
    xc = x - mu
    var = jnp.mean(xc * xc, axis=-1, keepdims=True)
    return xc * lax.rsqrt(var + LN_EPS) * g + b


def _sigmoid(x):
    return 1.0 / (1.0 + jnp.exp(-x))


def _rows_to_tiles(x):
    slabs = [x[:, t * LANES:(t + 1) * LANES] for t in range(SUBLANES)]
    return jnp.swapaxes(jnp.stack(slabs, axis=0), 0, 1)


def _tiles_to_cols(x):
    xt = jnp.swapaxes(x, 0, 1)
    return [xt[t] for t in range(SUBLANES)]


def _rows_call(body, row_inputs, const_inputs, out_widths, tm, out_dtypes=None, name=None, extra_inputs=()):
    n = row_inputs[0].shape[0]
    tm = min(tm, n)
    assert n % tm == 0, (n, tm)
    out_dtypes = out_dtypes or [F32] * len(out_widths)
    in_specs = [pl.BlockSpec((tm, a.shape[1]), lambda i: (i, 0)) for a in row_inputs]
    in_specs += [pl.BlockSpec(shape, fn) for _, shape, fn in extra_inputs]
    row_inputs = list(row_inputs) + [a for a, _, _ in extra_inputs]
    in_specs += [pl.BlockSpec(c.shape, lambda i, nd=c.ndim: (0,) * nd) for c in const_inputs]
    tails = [w if isinstance(w, tuple) else (w,) for w in out_widths]
    out_specs = [pl.BlockSpec((tm,) + w, lambda i, nd=len(w): (i,) + (0,) * nd) for w in tails]
    out_shape = [jax.ShapeDtypeStruct((n,) + w, dt) for w, dt in zip(tails, out_dtypes)]
    return pl.pallas_call(
        body, grid=(n // tm,), in_specs=in_specs, out_specs=out_specs, out_shape=out_shape,
        compiler_params=_params(("parallel",)), name=name,
    )(*row_inputs, *const_inputs)


PROJ_WIDTHS = (ATTN_W, KV_W, KV_W, SSM_WIDTH, RWKV_COLS)


def _proj_kernel(x_ref, g_ref, b_ref, w_ref, *out_refs, pre_ln):
    x = x_ref[...]
    if pre_ln:
        x = _layer_norm(x, g_ref[...], b_ref[...])
        out_refs[0][...] = x
        out_refs = out_refs[1:]
    xb = x.astype(BF16)
    col = 0
    for o_ref in out_refs:
        n = o_ref.shape[-1]
        o_ref[...] = jnp.dot(xb, w_ref[:, col:col + n], preferred_element_type=F32)
        col += n


def _proj(x, ln_g, ln_b, w_mix, pre_ln, tm):
    widths = ((D_MODEL,) if pre_ln else ()) + PROJ_WIDTHS
    outs = _rows_call(functools.partial(_proj_kernel, pre_ln=pre_ln), [x], [ln_g, ln_b, w_mix],
                      widths, tm, name="proj")
    if pre_ln:
        return outs[0], outs[1:]
    return x, outs


def _attn_kernel(tab_ref, q_ref, k_ref, v_ref, bias_ref, sink_ref, o_ref):
    del tab_ref
    for bb in range(q_ref.shape[0]):
        for h in range(N_KV_HEADS):
            q = q_ref[bb, h].astype(BF16)
            k = k_ref[bb, h].astype(BF16)
            v = v_ref[bb, h].astype(BF16)
            s = lax.dot_general(q, k, (((1,), (1,)), ((), ())), preferred_element_type=F32)
            s = s * ATTN_SCALE + bias_ref[0, h]
            sk = sink_ref[h]
            m = jnp.maximum(jnp.max(s, axis=-1, keepdims=True), sk)
            p = jnp.exp(s - m)
            den = jnp.sum(p, axis=-1, keepdims=True) + jnp.exp(sk - m)
            o = jnp.dot(p.astype(BF16), v, preferred_element_type=F32)
            o_ref[bb, h] = o / den


def _attention(q, k, v, bias, sinks, tab_idx, bblk):
    p, _, mq, _ = q.shape
    nk = k.shape[2]
    assert p % bblk == 0
    grid_spec = pltpu.PrefetchScalarGridSpec(
        num_scalar_prefetch=1, grid=(p // bblk,),
        in_specs=[
            pl.BlockSpec((bblk, N_KV_HEADS, mq, HEAD_DIM), lambda i, t: (i, 0, 0, 0)),
            pl.BlockSpec((bblk, N_KV_HEADS, nk, HEAD_DIM), lambda i, t: (i, 0, 0, 0)),
            pl.BlockSpec((bblk, N_KV_HEADS, nk, HEAD_DIM), lambda i, t: (i, 0, 0, 0)),
            pl.BlockSpec((1, N_KV_HEADS, mq, nk), lambda i, t: (t[i], 0, 0, 0)),
            pl.BlockSpec((N_KV_HEADS, mq, 1), lambda i, t: (0, 0, 0)),
        ],
        out_specs=pl.BlockSpec((bblk, N_KV_HEADS, mq, HEAD_DIM), lambda i, t: (i, 0, 0, 0)),
    )
    return pl.pallas_call(
        _attn_kernel, grid_spec=grid_spec,
        out_shape=jax.ShapeDtypeStruct(q.shape, F32),
        compiler_params=_params(("arbitrary",)), name="attention",
    )(tab_idx, q, k, v, bias, sinks)


def _t5_bucket(dist):
    n = np.maximum(dist, 0)
    scaled = (np.log(np.maximum(n, 1).astype(np.float32) / np.float32(REL_EXACT))
              / np.float32(math.log(REL_MAX_DIST / REL_EXACT)) * np.float32(REL_BUCKETS - REL_EXACT))
    frac = np.abs(scaled - np.round(scaled))
    assert np.all((n <= REL_EXACT) | (n >= REL_MAX_DIST) | (frac > 1e-3))
    large = np.minimum(REL_EXACT + scaled.astype(np.int32), REL_BUCKETS - 1)
    return np.where(n < REL_EXACT, n, large)


def _bias_table(rel_bias, dist, valid, mq_pad=None, nk_pad=None):
    tq, nk = dist.shape
    onehot = np.eye(REL_BUCKETS, dtype=np.float32)[_t5_bucket(dist).reshape(-1)]
    bias = jnp.dot(jnp.asarray(onehot), rel_bias.astype(F32), precision=lax.Precision.HIGHEST)
    bias = bias.reshape(tq, nk, N_HEADS)
    bias = jnp.where(jnp.asarray(valid)[..., None], bias, NEG_INF)
    bias = jnp.moveaxis(bias, -1, 0).reshape(N_KV_HEADS, Q_PER_KV * tq, nk)
    mq_pad = mq_pad or Q_PER_KV * tq
    nk_pad = nk_pad or nk
    bias = jnp.pad(bias, ((0, 0), (0, mq_pad - Q_PER_KV * tq), (0, 0)))
    return jnp.pad(bias, ((0, 0), (0, 0), (0, nk_pad - nk)), constant_values=NEG_INF)


def _sink_rows(sinks, tq, mq_pad=None):
    s = jnp.repeat(sinks.astype(F32).reshape(N_KV_HEADS, Q_PER_KV, 1), tq, axis=2)
    s = s.reshape(N_KV_HEADS, Q_PER_KV * tq, 1)
    mq_pad = mq_pad or Q_PER_KV * tq
    return jnp.pad(s, ((0, 0), (0, mq_pad - Q_PER_KV * tq), (0, 0)))


def _heads_q(q, nb, tq):
    q = q.reshape(nb, tq, N_KV_HEADS, Q_PER_KV, HEAD_DIM)
    return jnp.transpose(q, (0, 2, 3, 1, 4)).reshape(nb, N_KV_HEADS, Q_PER_KV * tq, HEAD_DIM)


def _unheads_o(o, nb, tq):
    o = o[:, :, :Q_PER_KV * tq].reshape(nb, N_KV_HEADS, Q_PER_KV, tq, HEAD_DIM)
    return jnp.transpose(o, (0, 3, 1, 2, 4)).reshape(nb * tq, ATTN_W)


BODY_KEYS = N_META + 2 * WINDOW + 16


def _body_attn_kernel(q_ref, ko_ref, kp_ref, km_ref, vo_ref, vp_ref, vm_ref, bias_ref, o_ref):
    kv_heads = range(N_KV_HEADS)
    hs = [slice(h * HEAD_DIM, (h + 1) * HEAD_DIM) for h in kv_heads]
    heads = [[h * Q_PER_KV + g for g in range(Q_PER_KV)] for h in kv_heads]
    pad = jnp.zeros((BODY_KEYS - N_META - 2 * WINDOW, HEAD_DIM), F32)
    ones = jnp.ones((BODY_KEYS, HEAD_DIM), BF16)
    k = [jnp.concatenate([km_ref[:, hs[h]], kp_ref[:, hs[h]], ko_ref[:, hs[h]], pad], axis=0).astype(BF16)
         for h in kv_heads]
    v = [jnp.concatenate([vm_ref[:, hs[h]], vp_ref[:, hs[h]], vo_ref[:, hs[h]], pad], axis=0).astype(BF16)
         for h in kv_heads]
    q = [jnp.concatenate([q_ref[:, qh * HEAD_DIM:(qh + 1) * HEAD_DIM] for qh in heads[h]], axis=0).astype(BF16)
         for h in kv_heads]
    s = [lax.dot_general(q[h], k[h], (((1,), (1,)), ((), ())), preferred_element_type=F32) for h in kv_heads]
    s = [s[h] * ATTN_SCALE + bias_ref[0, h] for h in kv_heads]
    p = [jnp.exp(s[h] - jnp.max(s[h], axis=-1, keepdims=True)).astype(BF16) for h in kv_heads]
    o = [jnp.dot(p[h], v[h], preferred_element_type=F32) / jnp.dot(p[h], ones, preferred_element_type=F32)
         for h in kv_heads]
    for h in kv_heads:
        for g, qh in enumerate(heads[h]):
            o_ref[:, qh * HEAD_DIM:(qh + 1) * HEAD_DIM] = o[h][g * WINDOW:(g + 1) * WINDOW]


def _body_attention(q, k, v, k_meta, v_meta, rel_bias, sinks, bsz, seq):
    nblk = seq // WINDOW
    i = np.arange(WINDOW)[:, None]
    c = np.arange(WINDOW)[None, :]
    sink_col = _sink_rows(sinks, WINDOW)
    tabs = []
    for m in (0, 1):
        q_pos = N_META + WINDOW * m + i
        meta_pos = np.arange(N_META)[None, :]
        dist = np.concatenate([q_pos - meta_pos, WINDOW + i - c, i - c], axis=1)
        valid = np.concatenate([np.ones((WINDOW, N_META), bool),
                                (c >= i) & (m > 0), c <= i], axis=1)
        tab = _bias_table(rel_bias, dist, valid, nk_pad=BODY_KEYS)
        tabs.append(tab.at[:, :, N_META + 2 * WINDOW].set(sink_col[:, :, 0]))
    bias = jnp.stack(tabs)
    nk = BODY_KEYS
    mq = Q_PER_KV * WINDOW
    own = lambda w: pl.BlockSpec((WINDOW, w), lambda b, m: (b * nblk + m, 0))
    prev = lambda w: pl.BlockSpec((WINDOW, w), lambda b, m: (b * nblk + jnp.maximum(m - 1, 0), 0))
    meta = pl.BlockSpec((N_META, KV_W), lambda b, m: (b, 0))
    return pl.pallas_call(
        _body_attn_kernel, grid=(bsz, nblk),
        in_specs=[own(ATTN_W), own(KV_W), prev(KV_W), meta, own(KV_W), prev(KV_W), meta,
                  pl.BlockSpec((1, N_KV_HEADS, mq, nk), lambda b, m: (jnp.minimum(m, 1), 0, 0, 0))],
        out_specs=own(ATTN_W), out_shape=jax.ShapeDtypeStruct(q.shape, F32),
        compiler_params=_params(("parallel", "arbitrary")), name="body_attention",
    )(q, k, k, k_meta, v, v, v_meta, bias)


def _meta_attention(q, k, v, rel_bias, sinks, bsz):
    i = np.arange(N_META)
    dist = i[:, None] - i[None, :]
    bias = _bias_table(rel_bias, dist, dist >= 0)[None]
    kv = lambda t: jnp.transpose(t.reshape(bsz, N_META, N_KV_HEADS, HEAD_DIM), (0, 2, 1, 3))
    o = _attention(_heads_q(q, bsz, N_META), kv(k), kv(v), bias, _sink_rows(sinks, N_META),
                   jnp.zeros((bsz,), jnp.int32), bsz)
    return _unheads_o(o, bsz, N_META)


def _sample_attention(q, k_all, v_all, meta_k, meta_v, rel_bias, sinks):
    bsz = q.shape[0]
    wc = k_all.shape[1] - 1
    nk = N_META + wc + 1
    nk_pad = -(-nk // LANES) * LANES
    mq_pad = 8
    k_pos = np.concatenate([np.arange(N_META), PAST_LEN - wc + np.arange(wc + 1)])
    dist = (PAST_LEN - k_pos)[None, :]
    is_meta = (np.arange(nk) < N_META)[None, :]
    valid = (dist >= 0) & (is_meta | ((k_pos[None, :] >= N_META) & (dist <= WINDOW)))
    bias = _bias_table(rel_bias, dist, valid, mq_pad, nk_pad)[None]

    def kv(meta, t):
        full = jnp.concatenate([meta.astype(F32), t], axis=1)
        full = jnp.pad(full, ((0, 0), (0, nk_pad - nk), (0, 0), (0, 0)))
        return jnp.transpose(full, (0, 2, 1, 3))

    qh = jnp.pad(_heads_q(q, bsz, 1), ((0, 0), (0, 0), (0, mq_pad - Q_PER_KV), (0, 0)))
    o = _attention(qh, kv(meta_k, k_all), kv(meta_v, v_all), bias, _sink_rows(sinks, 1, mq_pad),
                   jnp.zeros((bsz // 8,), jnp.int32), 8)
    return _unheads_o(o, bsz, 1)


def _s5_bu_kernel(u_ref, w_ref, re_ref, im_ref):
    r = jnp.dot(u_ref[...].astype(BF16), w_ref[...], preferred_element_type=F32)
    re_ref[...] = r[:, :SSM_N]
    im_ref[...] = r[:, SSM_N:]


def _s5_scan_kernel(bur_ref, bui_ref, h0r_ref, h0i_ref, ar_ref, ai_ref, hr_ref, hi_ref, cr_ref, ci_ref):
    @pl.when(pl.program_id(1) == 0)
    def _():
        cr_ref[...] = h0r_ref[0]
        ci_ref[...] = h0i_ref[0]

    ar = ar_ref[...]
    ai = ai_ref[...]

    def step(t, carry):
        hr, hi = carry
        nr = ar * hr - ai * hi + bur_ref[pl.ds(t, 1), :]
        ni = ar * hi + ai * hr + bui_ref[pl.ds(t, 1), :]
        hr_ref[pl.ds(t, 1), :] = nr
        hi_ref[pl.ds(t, 1), :] = ni
        return nr, ni

    hr, hi = lax.fori_loop(0, bur_ref.shape[0], step, (cr_ref[...], ci_ref[...]), unroll=8)
    cr_ref[...] = hr
    ci_ref[...] = hi


def _s5_scan(bur, bui, h0r, h0i, ar, ai, bsz, seq, tt):
    tt = min(tt, seq)
    nt = seq // tt
    row = pl.BlockSpec((tt, SSM_N), lambda b, t: (b * nt + t, 0))
    st = pl.BlockSpec((1, 1, SSM_N), lambda b, t: (b, 0, 0))
    cst = pl.BlockSpec((1, SSM_N), lambda b, t: (0, 0))
    return pl.pallas_call(
        _s5_scan_kernel, grid=(bsz, nt), in_specs=[row, row, st, st, cst, cst], out_specs=[row, row],
        out_shape=[jax.ShapeDtypeStruct(bur.shape, F32)] * 2,
        scratch_shapes=[pltpu.VMEM((1, SSM_N), F32)] * 2,
        compiler_params=_params(("arbitrary", "arbitrary")), name="s5_scan",
    )(bur, bui, h0r, h0i, ar, ai)


def _s5_step_kernel(bur_ref, bui_ref, h0r_ref, h0i_ref, ar_ref, ai_ref, hr_ref, hi_ref):
    ar, ai, hr, hi = ar_ref[...], ai_ref[...], h0r_ref[...], h0i_ref[...]
    hr_ref[...] = ar * hr - ai * hi + bur_ref[...]
    hi_ref[...] = ar * hi + ai * hr + bui_ref[...]


def _s5_out_kernel(hr_ref, hi_ref, u_ref, wc_ref, d_ref, wg_ref, o_ref):
    y = (_dot(hr_ref[...], wc_ref[:SSM_N]) + _dot(hi_ref[...], wc_ref[SSM_N:])
         + d_ref[...] * u_ref[...])
    z = jax.nn.gelu(y)
    o_ref[...] = z * _sigmoid(_dot(z, wg_ref[...]))


def _block_diag(blocks):
    g, a, b = blocks.shape
    eye = jnp.eye(g, dtype=blocks.dtype)
    return (eye[:, None, :, None] * blocks[:, :, None, :]).reshape(g * a, g * b)


def _s5_weights(a_re, a_im, log_dt, b_re, b_im, c_re, c_im):
    a_re = a_re.astype(F32)
    a_im = a_im.astype(F32)
    dt = jnp.exp(log_dt.astype(F32))[:, None]
    mag = jnp.exp(a_re * dt)
    ab_re = mag * jnp.cos(a_im * dt)
    ab_im = mag * jnp.sin(a_im * dt)
    den = a_re * a_re + a_im * a_im
    nr = ab_re - 1.0
    cf_re = (nr * a_re + ab_im * a_im) / den
    cf_im = (ab_im * a_re - nr * a_im) / den
    b_re = b_re.astype(F32)
    b_im = b_im.astype(F32)
    bb_re = cf_re[..., None] * b_re - cf_im[..., None] * b_im
    bb_im = cf_re[..., None] * b_im + cf_im[..., None] * b_re
    w_b = jnp.concatenate([_block_diag(jnp.swapaxes(bb_re, 1, 2)),
                           _block_diag(jnp.swapaxes(bb_im, 1, 2))], axis=1)
    w_c = jnp.concatenate([_block_diag(jnp.swapaxes(c_re.astype(F32), 1, 2)),
                           -_block_diag(jnp.swapaxes(c_im.astype(F32), 1, 2))], axis=0)
    return ab_re.reshape(1, SSM_N), ab_im.reshape(1, SSM_N), w_b.astype(BF16), w_c.astype(BF16)


def _seg_ones():
    r = lax.broadcasted_iota(jnp.int32, (RWKV_WIDTH, RWKV_WIDTH), 0) // RWKV_HEAD
    c = lax.broadcasted_iota(jnp.int32, (RWKV_WIDTH, RWKV_WIDTH), 1) // RWKV_HEAD
    return (r == c).astype(BF16)


def _rwkv_prep_kernel(pc_ref, prev_ref, *rest):
    _rwkv_prep(pc_ref[...], prev_ref[...], *rest)


def _rwkv_prep_shift_kernel(pc_ref, tail_ref, first_ref, *rest, tiles_per_seq):
    pc = pc_ref[...]
    seq_start = pl.program_id(0) % tiles_per_seq == 0
    above = jnp.where(seq_start, first_ref[0], tail_ref[SUBLANES - 1:SUBLANES, :])
    row = lax.broadcasted_iota(jnp.int32, pc.shape, 0)
    _rwkv_prep(pc, jnp.where(row == 0, above, pltpu.roll(pc, 1, axis=0)), *rest)


def _rwkv_prep(pc, prev, mu_ref, vec_ref, w2_ref, a2_ref, g2_ref,
               r_ref, k_ref, v_ref, lw_ref, kk_ref, bb_ref, g_ref):
    xm = pc + (prev - pc) * mu_ref[...]
    rw = RWKV_WIDTH
    xr, xk, xv, xl = xm[:, :rw], xm[:, rw:2 * rw], xm[:, 2 * rw:3 * rw], xm[:, 3 * rw:]
    vec = vec_ref[...]
    wpre = -(vec[RV_W0:RV_W0 + 1] + _dot(jnp.tanh(xl), w2_ref[...]))
    softplus = jnp.maximum(wpre, 0.0) + jnp.log(1.0 + jnp.exp(-jnp.abs(wpre)))
    lw_ref[...] = -jnp.exp(-softplus - 0.5)
    a = _sigmoid(vec[RV_A0:RV_A0 + 1] + _dot(xl, a2_ref[...]))
    g_ref[...] = _dot(_sigmoid(xl), g2_ref[...])
    kk = xk * vec[RV_KK:RV_KK + 1]
    norm = jnp.sqrt(_dot_exact_rhs(kk * kk, _seg_ones()))
    kk = kk / jnp.maximum(norm, 1e-12)
    r_ref[...] = xr
    k_ref[...] = xk * (1.0 + (a - 1.0) * vec[RV_KA:RV_KA + 1])
    v_ref[...] = xv
    kk_ref[...] = kk
    bb_ref[...] = kk * a


def _rwkv_post_kernel(y_ref, r_ref, k_ref, v_ref, g_ref, vec_ref, o_ref):
    ones = _seg_ones()
    vec = vec_ref[...]
    y = y_ref[...]
    yc = y - _dot_exact_rhs(y, ones) * (1.0 / RWKV_HEAD)
    yv = _dot_exact_rhs(yc * yc, ones) * (1.0 / RWKV_HEAD)
    yn = yc * lax.rsqrt(yv + RWKV_GN_EPS) * vec[RV_GNW:RV_GNW + 1] + vec[RV_GNB:RV_GNB + 1]
    bonus = _dot_exact_rhs(r_ref[...] * k_ref[...] * vec[RV_RK:RV_RK + 1], ones) * v_ref[...]
    o_ref[...] = (yn + bonus) * g_ref[...]


def _rwkv_chunk_kernel(*refs, chunk, nb):
    r_refs, k_refs, v_refs, lw_refs, kk_refs, bb_refs = (refs[i * nb:(i + 1) * nb] for i in range(6))
    s0_ref = refs[6 * nb]
    y_refs = refs[6 * nb + 1:7 * nb + 1]
    so_ref, s_ref = refs[7 * nb + 1:]
    t = pl.program_id(0)

    @pl.when(t == 0)
    def _():
        s_ref[...] = s0_ref[...]

    c = chunk
    row = lax.broadcasted_iota(jnp.int32, (c, c), 0)
    col = lax.broadcasted_iota(jnp.int32, (c, c), 1)
    incl = (row >= col).astype(F32)
    strict = (row > col).astype(F32)
    eye = (row == col).astype(F32)
    nsub = r_refs[0].shape[0] // c
    heads = range(RWKV_HEADS)
    hsl = [slice(h * RWKV_HEAD, (h + 1) * RWKV_HEAD) for h in heads]
    tsl = [slice(sub * c, (sub + 1) * c) for sub in range(nsub)]
    seqs = [(b, h) for b in range(nb) for h in heads]
    units = [(b, sub, h) for b in range(nb) for sub in range(nsub) for h in heads]

    scaled = {}
    for b in range(nb):
        for sub in range(nsub):
            lw = lw_refs[b][tsl[sub], :]
            cum = _dot_exact_lhs(incl.astype(BF16), lw)
            ecum = jnp.exp(cum)
            einv = jnp.exp(-cum)
            scaled[b, sub] = (kk_refs[b][tsl[sub], :] * jnp.exp(cum - lw), bb_refs[b][tsl[sub], :] * einv,
                              k_refs[b][tsl[sub], :] * einv, r_refs[b][tsl[sub], :] * ecum, ecum[c - 1:c, :])
    kt = {u: scaled[u[0], u[1]][0][:, hsl[u[2]]] for u in units}
    bt = {u: scaled[u[0], u[1]][1][:, hsl[u[2]]] for u in units}
    kkt = {u: scaled[u[0], u[1]][2][:, hsl[u[2]]] for u in units}
    rt = {u: scaled[u[0], u[1]][3][:, hsl[u[2]]] for u in units}
    v_t = [v_refs[b][...].T for b in range(nb)]
    vt = {u: v_t[u[0]][hsl[u[2]], tsl[u[1]]] for u in units}
    a_b = {u: strict * _dot_nt(kt[u], bt[u]) for u in units}
    a_k = {u: strict * _dot_nt(kt[u], kkt[u]) for u in units}
    r_b = {u: incl * _dot_nt(rt[u], bt[u]) for u in units}
    r_k = {u: incl * _dot_nt(rt[u], kkt[u]) for u in units}
    pw = {u: -a_b[u] for u in units}
    tinv = {u: eye + pw[u] for u in units}
    n = 1
    while 2 * n < c:
        pw = {u: _dot(pw[u], pw[u]) for u in units}
        tinv = {u: tinv[u] + _dot(tinv[u], pw[u]) for u in units}
        n *= 2
    x = {u: _dot_nt(vt[u], a_k[u]) for u in units}
    w1t = {u: _dot_nt(x[u], tinv[u]) for u in units}
    w2 = {u: _dot(tinv[u], kt[u]) for u in units}
    yt_local = {u: _dot_nt(vt[u], r_k[u]) for u in units}
    s_local = {u: _dot(vt[u], kkt[u]) for u in units}

    s = {q: s_ref[q[0], q[1]] for q in seqs}
    yt = {}
    for sub in range(nsub):
        ut = {(b, h): -(_dot_nt(s[b, h], w2[b, sub, h]) + w1t[b, sub, h]) for b, h in seqs}
        for b, h in seqs:
            yt[b, sub, h] = (_dot_nt(s[b, h], rt[b, sub, h]) + _dot_nt(ut[b, h], r_b[b, sub, h])
                             + yt_local[b, sub, h])
        s = {(b, h): (s[b, h] + _dot(ut[b, h], bt[b, sub, h]) + s_local[b, sub, h]) * scaled[b, sub][4][:, hsl[h]]
             for b, h in seqs}
    for b, h in seqs:
        s_ref[b, h] = s[b, h]
    for b in range(nb):
        y_t = jnp.concatenate([jnp.concatenate([yt[b, sub, h] for sub in range(nsub)], axis=1) for h in heads],
                              axis=0)
        y_refs[b][...] = y_t.T

    @pl.when(t == pl.num_programs(0) - 1)
    def _():
        so_ref[...] = s_ref[...]


def _rwkv_chunk(r, k, v, lw, kk, bb, s0, bsz, seq):
    tb = 2 * RWKV_CHUNK
    assert seq % tb == 0
    nt = seq // tb
    tiles = [pl.BlockSpec((tb, RWKV_WIDTH), lambda t, b=b: (b * nt + t, 0)) for b in range(bsz)]
    st = pl.BlockSpec(s0.shape, lambda t: (0, 0, 0, 0))
    *ys, s_out = pl.pallas_call(
        functools.partial(_rwkv_chunk_kernel, chunk=RWKV_CHUNK, nb=bsz), grid=(nt,),
        in_specs=tiles * 6 + [st], out_specs=[pl.BlockSpec((tb, RWKV_WIDTH), lambda t: (t, 0))] * bsz + [st],
        out_shape=[jax.ShapeDtypeStruct((seq, RWKV_WIDTH), F32)] * bsz + [jax.ShapeDtypeStruct(s0.shape, F32)],
        scratch_shapes=[pltpu.VMEM(s0.shape, F32)],
        compiler_params=_params(("arbitrary",)), name="rwkv_chunk",
    )(*([r] * bsz + [k] * bsz + [v] * bsz + [lw] * bsz + [kk] * bsz + [bb] * bsz), s0)
    return jnp.concatenate(ys, axis=0), s_out


def _rwkv_step_kernel(s_ref, r_ref, k_ref, lw_ref, kk_ref, bb_ref, v_ref, so_ref, y_ref):
    s = s_ref[...]
    sa = jnp.sum(s * (-kk_ref[...]), axis=-1, keepdims=True)
    s = s * jnp.exp(lw_ref[...]) + sa * bb_ref[...] + v_ref[...] * k_ref[...]
    so_ref[...] = s
    y_ref[...] = jnp.sum(s * r_ref[...], axis=-1, keepdims=True)


def _rwkv_step(r, k, v, lw, kk, bb, s0):
    bsz = r.shape[0]
    p = bsz * RWKV_HEADS
    nb = min(64, p)
    rowv = lambda a: a.reshape(p, 1, RWKV_HEAD)
    rs = pl.BlockSpec((nb, 1, RWKV_HEAD), lambda i: (i, 0, 0))
    cs = pl.BlockSpec((nb, RWKV_HEAD, 1), lambda i: (i, 0, 0))
    ss = pl.BlockSpec((nb, RWKV_HEAD, RWKV_HEAD), lambda i: (i, 0, 0))
    s_out, y = pl.pallas_call(
        _rwkv_step_kernel, grid=(p // nb,), in_specs=[ss, rs, rs, rs, rs, rs, cs], out_specs=[ss, cs],
        out_shape=[jax.ShapeDtypeStruct((p, RWKV_HEAD, RWKV_HEAD), F32),
                   jax.ShapeDtypeStruct((p, RWKV_HEAD, 1), F32)],
        compiler_params=_params(("parallel",)), name="rwkv_step",
    )(s0.reshape(p, RWKV_HEAD, RWKV_HEAD), rowv(r), rowv(k), rowv(lw), rowv(kk), rowv(bb),
      v.reshape(p, RWKV_HEAD, 1))
    return y.reshape(bsz, RWKV_WIDTH), s_out.reshape(s0.shape)


def _merge_kernel(h_ref, oa_ref, ob_ref, oc_ref, wg_ref, wb_ref, wo_ref, g_ref, b_ref, o_ref, *tile_refs):
    h = h_ref[...]
    gates = _sigmoid(jnp.dot(h.astype(BF16), wg_ref[...], preferred_element_type=F32))
    d = D_MODEL
    merged = (gates[:, :d] * _dot(oa_ref[...], wb_ref[:ATTN_W])
              + gates[:, d:2 * d] * _dot(ob_ref[...], wb_ref[ATTN_W:ATTN_W + SSM_WIDTH])
              + gates[:, 2 * d:] * _dot(oc_ref[...], wb_ref[ATTN_W + SSM_WIDTH:]))
    mix = _dot(merged, wo_ref[...])
    out = _layer_norm(DEEPNORM_ALPHA * h + mix, g_ref[...], b_ref[...])
    o_ref[...] = out
    for t_ref in tile_refs:
        t_ref[...] = _rows_to_tiles(out)


def _ffn_kernel(be_ref, x_ref, wg_ref, wu_ref, wd_ref, g_ref, b_ref, o_ref, xb_ref, acc_ref, *, post_ln):
    del be_ref
    j = pl.program_id(1)

    @pl.when(j == 0)
    def _():
        xb_ref[...] = x_ref[...].astype(BF16)
        acc_ref[...] = jnp.zeros_like(acc_ref)

    xb = xb_ref[...]
    gate = jnp.dot(xb, wg_ref[0], preferred_element_type=F32)
    up = jnp.dot(xb, wu_ref[0], preferred_element_type=F32)
    act = gate * _sigmoid(gate) * up
    acc_ref[...] += jnp.dot(act.astype(BF16), wd_ref[0], preferred_element_type=F32)

    @pl.when(j == pl.num_programs(1) - 1)
    def _():
        if post_ln:
            o_ref[...] = _layer_norm(DEEPNORM_ALPHA * x_ref[...] + acc_ref[...], g_ref[...], b_ref[...])
        else:
            o_ref[...] = acc_ref[...]


def _ffn(x, block_e, w_in, w_down, ln_g, ln_b, blk, tf, post_ln):
    rows = x.shape[0]
    assert rows % blk == 0
    f = w_down.shape[1]
    nf = f // tf
    grid_spec = pltpu.PrefetchScalarGridSpec(
        num_scalar_prefetch=1, grid=(rows // blk, nf),
        in_specs=[
            pl.BlockSpec((blk, D_MODEL), lambda i, j, be: (i, 0)),
            pl.BlockSpec((1, D_MODEL, tf), lambda i, j, be: (be[i], 0, j)),
            pl.BlockSpec((1, D_MODEL, tf), lambda i, j, be: (be[i], 0, nf + j)),
            pl.BlockSpec((1, tf, D_MODEL), lambda i, j, be: (be[i], j, 0)),
            pl.BlockSpec((1, D_MODEL), lambda i, j, be: (0, 0)),
            pl.BlockSpec((1, D_MODEL), lambda i, j, be: (0, 0)),
        ],
        out_specs=pl.BlockSpec((blk, D_MODEL), lambda i, j, be: (i, 0)),
        scratch_shapes=[pltpu.VMEM((blk, D_MODEL), BF16), pltpu.VMEM((blk, D_MODEL), F32)],
    )
    return pl.pallas_call(
        functools.partial(_ffn_kernel, post_ln=post_ln), grid_spec=grid_spec,
        out_shape=jax.ShapeDtypeStruct((rows, D_MODEL), F32),
        compiler_params=_params(("arbitrary", "arbitrary")), name="ffn",
    )(block_e, x, w_in, w_in, w_down, ln_g, ln_b)


def _router_kernel(h_ref, w_ref, e_ref, g_ref):
    logits = jnp.dot(h_ref[...], w_ref[...], preferred_element_type=F32, precision=lax.Precision.HIGHEST)
    lane = lax.broadcasted_iota(jnp.int32, logits.shape, 1)
    lg = jnp.where(lane < N_EXPERTS, logits, -jnp.inf)
    m1 = jnp.max(lg, axis=-1, keepdims=True)
    i1 = jnp.min(jnp.where(lg == m1, lane, LANES), axis=-1, keepdims=True)
    lg2 = jnp.where(lane == i1, -jnp.inf, lg)
    m2 = jnp.max(lg2, axis=-1, keepdims=True)
    i2 = jnp.min(jnp.where(lg2 == m2, lane, LANES), axis=-1, keepdims=True)
    e2 = jnp.exp(m2 - m1)
    den = 1.0 + e2
    e_ref[...] = jnp.where(lane == 0, i1, jnp.where(lane == 1, i2, 0))
    g_ref[...] = jnp.where(lane == 0, 1.0 / den, jnp.where(lane == 1, e2 / den, 0.0))


def _moe_ffn_kernel(be_ref, nv_ref, tok_ref, nxt_ref, dst_ref, h_hbm, wg_ref, wu_ref, wd_ref, out_hbm,
                    xbuf, xb_ref, acc_ref, stage, gsem, ssem, pending_ref):
    del be_ref
    i, j = pl.program_id(0), pl.program_id(1)
    nblk, nf = pl.num_programs(0), pl.num_programs(1)
    blk = xb_ref.shape[0]
    n_valid = nv_ref[0]
    valid = i < n_valid
    slot = lax.rem(i, 2)

    def start_gather(idx_ref, s):
        def body(r, c):
            pltpu.make_async_copy(h_hbm.at[idx_ref[0, 0, r]], xbuf.at[s, r], gsem.at[s]).start(priority=1)
            return c
        lax.fori_loop(0, blk, body, 0, unroll=8)

    def drain_scatter():
        @pl.when(pending_ref[0] == 1)
        def _():
            pltpu.make_async_copy(stage, out_hbm.at[pl.ds(0, blk)], ssem).wait()
            pending_ref[0] = 0

    @pl.when((i == 0) & (j == 0))
    def _():
        pending_ref[0] = 0
        start_gather(tok_ref, 0)
        stage[...] = jnp.zeros_like(stage)
        spare = pltpu.make_async_copy(stage, out_hbm.at[pl.ds(out_hbm.shape[0] - blk, blk)], ssem)
        spare.start()
        spare.wait()

    @pl.when(valid & (j == 0))
    def _():
        @pl.when(i + 1 < n_valid)
        def _():
            start_gather(nxt_ref, 1 - slot)

        pltpu.make_async_copy(h_hbm.at[pl.ds(0, blk)], xbuf.at[slot], gsem.at[slot]).wait()
        for t, cols in enumerate(_tiles_to_cols(xbuf[slot])):
            xb_ref[:, t * LANES:(t + 1) * LANES] = cols.astype(BF16)
        acc_ref[...] = jnp.zeros_like(acc_ref)

    @pl.when(valid)
    def _():
        xb = xb_ref[...]
        gate = jnp.dot(xb, wg_ref[0].astype(BF16), preferred_element_type=F32)
        up = jnp.dot(xb, wu_ref[0].astype(BF16), preferred_element_type=F32)
        act = gate * _sigmoid(gate) * up
        acc_ref[...] += jnp.dot(act.astype(BF16), wd_ref[0].astype(BF16), preferred_element_type=F32)

    @pl.when(valid & (j == nf - 1))
    def _():
        drain_scatter()
        stage[...] = _rows_to_tiles(acc_ref[...])

        def body(r, c):
            pltpu.make_async_copy(stage.at[r], out_hbm.at[dst_ref[0, 0, r]], ssem).start(priority=1)
            return c
        lax.fori_loop(0, blk, body, 0, unroll=8)
        pending_ref[0] = 1

    @pl.when((i == nblk - 1) & (j == nf - 1))
    def _():
        drain_scatter()


def _moe_ffn(h_tiles, rows_tok, rows_dst, block_e, n_valid, w_in, w_down, blk, tf):
    n = h_tiles.shape[0]
    rows = rows_tok.shape[0]
    nblk = rows // blk
    nf = w_down.shape[1] // tf
    idx3 = lambda a: a.reshape(nblk, 1, blk)
    smem = lambda fn: pl.BlockSpec((1, 1, blk), fn, memory_space=pltpu.SMEM)
    ftile = lambda i, j, nv: jnp.where(i < nv[0], j, nf - 1)
    grid_spec = pltpu.PrefetchScalarGridSpec(
        num_scalar_prefetch=2, grid=(nblk, nf),
        in_specs=[
            smem(lambda i, j, be, nv: (i, 0, 0)),
            smem(lambda i, j, be, nv: (jnp.minimum(i + 1, nblk - 1), 0, 0)),
            smem(lambda i, j, be, nv: (i, 0, 0)),
            pl.BlockSpec(memory_space=pl.ANY),
            pl.BlockSpec((1, D_MODEL, tf), lambda i, j, be, nv: (be[i], 0, ftile(i, j, nv))),
            pl.BlockSpec((1, D_MODEL, tf), lambda i, j, be, nv: (be[i], 0, nf + ftile(i, j, nv))),
            pl.BlockSpec((1, tf, D_MODEL), lambda i, j, be, nv: (be[i], ftile(i, j, nv), 0)),
        ],
        out_specs=pl.BlockSpec(memory_space=pl.ANY),
        scratch_shapes=[pltpu.VMEM((2, blk, SUBLANES, LANES), F32), pltpu.VMEM((blk, D_MODEL), BF16),
                        pltpu.VMEM((blk, D_MODEL), F32), pltpu.VMEM((blk, SUBLANES, LANES), F32),
                        pltpu.SemaphoreType.DMA((2,)), pltpu.SemaphoreType.DMA(()), pltpu.SMEM((1,), jnp.int32)],
    )
    return pl.pallas_call(
        _moe_ffn_kernel, grid_spec=grid_spec,
        out_shape=jax.ShapeDtypeStruct((TOP_K * n + blk, SUBLANES, LANES), F32),
        compiler_params=_params(("arbitrary", "arbitrary")), name="moe_ffn",
    )(block_e, n_valid, idx3(rows_tok), idx3(rows_tok), idx3(rows_dst), h_tiles, w_in, w_in, w_down)


def _combine_kernel(h_ref, y_ref, gate_ref, g_ref, bias_ref, o_ref):
    gate = gate_ref[...]
    rows = lambda s: jnp.concatenate(_tiles_to_cols(y_ref[:, s]), axis=-1)
    f = rows(0) * gate[:, 0:1] + rows(1) * gate[:, 1:2]
    o_ref[...] = _layer_norm(DEEPNORM_ALPHA * h_ref[...] + f, g_ref[...], bias_ref[...])


def _moe(h, h_tiles, router_pad, w_in, w_down, ln_g, ln_b, tm, blk, tf):
    n = h.shape[0]
    e_pad, gate = _rows_call(_router_kernel, [h], [router_pad], (LANES, LANES), tm,
                             out_dtypes=[jnp.int32, F32], name="router")
    flat_e = e_pad[:, :TOP_K].reshape(-1)
    n_assign = n * TOP_K
    n_blocks = -(-(n_assign + N_EXPERTS * (blk - 1)) // blk)
    onehot = (flat_e[:, None] == jnp.arange(N_EXPERTS, dtype=jnp.int32)[None, :]).astype(jnp.int32)
    csum = jnp.cumsum(onehot, axis=0)
    rank = jnp.sum(csum * onehot, axis=1) - 1
    counts = csum[-1]
    padded = (counts + blk - 1) // blk * blk
    pad_end = jnp.cumsum(padded)
    dest = (pad_end - padded)[flat_e] + rank
    assign = jnp.arange(n_assign, dtype=jnp.int32)
    rows_dst = jnp.full((n_blocks * blk,), -1, jnp.int32).at[dest].set(assign, unique_indices=True)
    rows_tok = jnp.maximum(rows_dst, 0) // TOP_K
    spare = n_assign + jnp.arange(n_blocks * blk, dtype=jnp.int32) % blk
    rows_dst = jnp.where(rows_dst < 0, spare, rows_dst)
    block_e = jnp.minimum(jnp.searchsorted(pad_end, jnp.arange(n_blocks, dtype=jnp.int32) * blk, side='right'),
                          N_EXPERTS - 1).astype(jnp.int32)
    n_valid = (pad_end[-1:] // blk).astype(jnp.int32)
    y = _moe_ffn(h_tiles, rows_tok, rows_dst, block_e, n_valid, w_in, w_down, blk, tf)
    tm = min(tm, n)
    cst = pl.BlockSpec((1, D_MODEL), lambda i: (0, 0))
    return pl.pallas_call(
        _combine_kernel, grid=(n // tm,),
        in_specs=[pl.BlockSpec((tm, D_MODEL), lambda i: (i, 0)),
                  pl.BlockSpec((tm, TOP_K, SUBLANES, LANES), lambda i: (i, 0, 0, 0)),
                  pl.BlockSpec((tm, LANES), lambda i: (i, 0)), cst, cst],
        out_specs=pl.BlockSpec((tm, D_MODEL), lambda i: (i, 0)),
        out_shape=jax.ShapeDtypeStruct((n, D_MODEL), F32),
        compiler_params=_params(("parallel",)), name="moe_combine",
    )(h, y.reshape(-1, TOP_K, SUBLANES, LANES), gate, ln_g, ln_b)


def _layer(l, hb, hs, pre_ln, bsz, seq, dec, cache, states, prm):
    (cache_meta_k, cache_meta_v, cache_win_k, cache_win_v) = cache
    (state_ssm_re, state_ssm_im, state_wkv, state_shift) = states
    n_meta = bsz * N_META
    tm_b, tm_s = 256, hs.shape[0]
    w_in = prm['w_in'][l]
    w_mix = w_in[:, :MIX_COLS].astype(BF16)
    w_gate = w_in[:, MIX_COLS:].astype(BF16)
    ln_in_g, ln_in_b = prm['ln_in_g'].reshape(1, -1), prm['ln_in_b'].reshape(1, -1)

    hb, (q_b, k_b, v_b, u_b, pc_b) = _proj(hb, ln_in_g, ln_in_b, w_mix, pre_ln, 2 * tm_b)
    hs, (q_s, k_s, v_s, u_s, pc_s) = _proj(hs, ln_in_g, ln_in_b, w_mix, pre_ln, tm_s)

    rel_bias, sinks = prm['rel_bias'], prm['attn_sinks'][l]
    k_meta, v_meta = k_s[:n_meta], v_s[:n_meta]
    oa_b = _body_attention(q_b, k_b, v_b, k_meta, v_meta, rel_bias, sinks, bsz, seq)
    oa_m = _meta_attention(q_s[:n_meta], k_meta, v_meta, rel_bias, sinks, bsz)
    kd = lambda t: t[n_meta:].reshape(dec, 1, N_KV_HEADS, HEAD_DIM)
    k_all = jnp.concatenate([cache_win_k[l].astype(F32), kd(k_s)], axis=1)
    v_all = jnp.concatenate([cache_win_v[l].astype(F32), kd(v_s)], axis=1)
    oa_d = _sample_attention(q_s[n_meta:], k_all, v_all, cache_meta_k[l], cache_meta_v[l], rel_bias, sinks)
    oa_s = jnp.concatenate([oa_m, oa_d], axis=0)
    kv4 = lambda t, b: t.reshape(b, -1, N_KV_HEADS, HEAD_DIM)
    attn_out = (kv4(k_meta, bsz), kv4(v_meta, bsz), kv4(k_b, bsz)[:, seq - WINDOW:], kv4(v_b, bsz)[:, seq - WINDOW:],
                k_all[:, 1:], v_all[:, 1:])

    ar, ai, w_b, w_c = _s5_weights(prm['ssm_a_re'][l], prm['ssm_a_im'][l], prm['ssm_log_dt'][l],
                                   prm['ssm_b_re'][l], prm['ssm_b_im'][l], prm['ssm_c_re'][l], prm['ssm_c_im'][l])
    bur_b, bui_b = _rows_call(_s5_bu_kernel, [u_b], [w_b], (SSM_N, SSM_N), 512, name="s5_bu")
    bur_s, bui_s = _rows_call(_s5_bu_kernel, [u_s], [w_b], (SSM_N, SSM_N), tm_s, name="s5_bu")
    zero_h = jnp.zeros((bsz, 1, SSM_N), F32)
    hr_m, hi_m = _s5_scan(bur_s[:n_meta], bui_s[:n_meta], zero_h, zero_h, ar, ai, bsz, N_META, N_META)
    last = lambda t: t.reshape(bsz, -1, SSM_N)[:, -1:]
    hr_b, hi_b = _s5_scan(bur_b, bui_b, last(hr_m), last(hi_m), ar, ai, bsz, seq, 512)
    hr_d, hi_d = _rows_call(_s5_step_kernel,
                            [bur_s[n_meta:], bui_s[n_meta:], state_ssm_re[l].reshape(dec, SSM_N).astype(F32),
                             state_ssm_im[l].reshape(dec, SSM_N).astype(F32)], [ar, ai], (SSM_N, SSM_N), dec,
                            name="s5_step")
    hr_s = jnp.concatenate([hr_m, hr_d], axis=0)
    hi_s = jnp.concatenate([hi_m, hi_d], axis=0)
    s5_consts = [w_c, prm['ssm_d'][l].reshape(1, -1).astype(F32), prm['ssm_w_glu'][l].astype(BF16)]
    (ob_b,) = _rows_call(_s5_out_kernel, [hr_b, hi_b, u_b], s5_consts, (SSM_WIDTH,), 512, name="s5_out")
    (ob_s,) = _rows_call(_s5_out_kernel, [hr_s, hi_s, u_s], s5_consts, (SSM_WIDTH,), tm_s, name="s5_out")
    st4 = lambda t, b: t.reshape(b, SSM_GROUPS, SSM_STATE)
    ssm_out = (st4(last(hr_b), bsz), st4(last(hi_b), bsz), st4(hr_d, dec), st4(hi_d, dec))

    pc_m = pc_s[:n_meta].reshape(bsz, N_META, RWKV_COLS)
    prev_m = jnp.concatenate([jnp.zeros((bsz, 1, RWKV_COLS), F32), pc_m[:, :-1]], axis=1)
    prev_s = jnp.concatenate([prev_m.reshape(n_meta, RWKV_COLS), state_shift[l].astype(F32)], axis=0)
    pad_rows = lambda w, lo: jnp.pad(w.astype(F32), ((lo, RWKV_LORA - lo - w.shape[0]), (0, 0))).astype(BF16)
    vec = jnp.pad(prm['rwkv_vec'][l].astype(F32), ((0, 1), (0, 0)))
    prep_consts = [prm['rwkv_mu'][l].reshape(1, -1).astype(F32), vec,
                   pad_rows(prm['rwkv_w2'][l], 0), pad_rows(prm['rwkv_a2'][l], RWKV_W_LORA),
                   pad_rows(prm['rwkv_g2'][l], RWKV_W_LORA + RWKV_A_LORA)]
    w7 = (RWKV_WIDTH,) * 7
    tm_prep = min(512, seq)
    tail_blocks = tm_prep // SUBLANES
    shift_inputs = [(pc_b, (SUBLANES, RWKV_COLS), lambda i: (jnp.maximum(i * tail_blocks - 1, 0), 0)),
                    (pc_m[:, -1:], (1, 1, RWKV_COLS), lambda i: (i // (seq // tm_prep), 0, 0))]
    r_b, kx_b, vx_b, lw_b, kk_b, bb_b, g_b = _rows_call(
        functools.partial(_rwkv_prep_shift_kernel, tiles_per_seq=seq // tm_prep), [pc_b], prep_consts, w7, tm_prep,
        name="rwkv_prep", extra_inputs=shift_inputs)
    r_s, kx_s, vx_s, lw_s, kk_s, bb_s, g_s = _rows_call(_rwkv_prep_kernel, [pc_s, prev_s], prep_consts, w7, tm_s,
                                                        name="rwkv_prep")
    meta_len = 2 * RWKV_CHUNK
    mrows = lambda t: jnp.pad(t[:n_meta].reshape(bsz, N_META, RWKV_WIDTH),
                              ((0, 0), (0, meta_len - N_META), (0, 0))).reshape(bsz * meta_len, RWKV_WIDTH)
    drows = lambda t: t[n_meta:]
    zero_s = jnp.zeros((bsz, RWKV_HEADS, RWKV_HEAD, RWKV_HEAD), F32)
    y_m, s_m = _rwkv_chunk(mrows(r_s), mrows(kx_s), mrows(vx_s), mrows(lw_s), mrows(kk_s), mrows(bb_s),
                           zero_s, bsz, meta_len)
    y_m = y_m.reshape(bsz, meta_len, RWKV_WIDTH)[:, :N_META].reshape(n_meta, RWKV_WIDTH)
    y_b, s_b = _rwkv_chunk(r_b, kx_b, vx_b, lw_b, kk_b, bb_b, s_m, bsz, seq)
    y_d, s_d = _rwkv_step(drows(r_s), drows(kx_s), drows(vx_s), drows(lw_s), drows(kk_s), drows(bb_s),
                          state_wkv[l].astype(F32))
    y_s = jnp.concatenate([y_m, y_d], axis=0)
    (oc_b,) = _rows_call(_rwkv_post_kernel, [y_b, r_b, kx_b, vx_b, g_b], [vec], (RWKV_WIDTH,), 512,
                         name="rwkv_post")
    (oc_s,) = _rows_call(_rwkv_post_kernel, [y_s, r_s, kx_s, vx_s, g_s], [vec], (RWKV_WIDTH,), tm_s,
                         name="rwkv_post")
    rwkv_out = (s_b, s_d, pc_b.reshape(bsz, seq, RWKV_COLS)[:, -1], pc_s[n_meta:])

    ln_g, ln_b = prm['ln_g'][l].astype(F32), prm['ln_b'][l].astype(F32)
    merge_consts = [w_gate, prm['w_branch'][l].astype(BF16), prm['w_out'][l].astype(BF16), ln_g[0:1], ln_b[0:1]]
    moe_layer = l % 2 == 1
    merge_outs = (D_MODEL,) + (((SUBLANES, LANES),) if moe_layer else ())
    hb, *hb_tiles = _rows_call(_merge_kernel, [hb, oa_b, ob_b, oc_b], merge_consts, merge_outs, tm_b, name="merge")
    hs, *hs_tiles = _rows_call(_merge_kernel, [hs, oa_s, ob_s, oc_s], merge_consts, merge_outs, tm_s, name="merge")

    if not moe_layer:
        w_ffn_in = prm['ffn_w_in'][l // 2].astype(BF16)[None]
        w_ffn_down = prm['ffn_w_down'][l // 2].astype(BF16)[None]
        blk_b = min(1024, hb.shape[0])
        hb = _ffn(hb, jnp.zeros((hb.shape[0] // blk_b,), jnp.int32), w_ffn_in, w_ffn_down,
                  ln_g[1:2], ln_b[1:2], blk_b, 256, post_ln=True)
        hs = _ffn(hs, jnp.zeros((1,), jnp.int32), w_ffn_in, w_ffn_down, ln_g[1:2], ln_b[1:2], tm_s, 256,
                  post_ln=True)
    else:
        router_pad = jnp.pad(prm['moe_router'][l // 2].astype(F32), ((0, 0), (0, LANES - N_EXPERTS)))
        w_moe_in = prm['moe_w_in'][l // 2]
        w_moe_down = prm['moe_w_down'][l // 2]
        hb = _moe(hb, hb_tiles[0], router_pad, w_moe_in, w_moe_down, ln_g[1:2], ln_b[1:2], 512, 1024, 512)
        hs = _moe(hs, hs_tiles[0], router_pad, w_moe_in, w_moe_down, ln_g[1:2], ln_b[1:2], tm_s, 64, 512)
    return hb, hs, attn_out, ssm_out, rwkv_out


def kernel(x_prompt, x_sample, cache_meta_k, cache_meta_v, cache_win_k, cache_win_v, state_ssm_re, state_ssm_im, state_wkv, state_shift, meta_tokens, ln_in_g, ln_in_b, w_in, rel_bias, attn_sinks, ssm_a_re, ssm_a_im, ssm_log_dt, ssm_b_re, ssm_b_im, ssm_c_re, ssm_c_im, ssm_d, ssm_w_glu, rwkv_mu, rwkv_vec, rwkv_w2, rwkv_a2, rwkv_g2, w_branch, w_out, ln_g, ln_b, ffn_w_in, ffn_w_down, moe_router, moe_w_in, moe_w_down):
    bsz, seq, _ = x_prompt.shape
    dec = x_sample.shape[0]
    assert x_sample.shape[1] == 1 and seq % (2 * RWKV_CHUNK) == 0
    prm = dict(ln_in_g=ln_in_g.astype(F32), ln_in_b=ln_in_b.astype(F32), w_in=w_in, rel_bias=rel_bias,
               attn_sinks=attn_sinks, ssm_a_re=ssm_a_re, ssm_a_im=ssm_a_im, ssm_log_dt=ssm_log_dt,
               ssm_b_re=ssm_b_re, ssm_b_im=ssm_b_im, ssm_c_re=ssm_c_re, ssm_c_im=ssm_c_im, ssm_d=ssm_d,
               ssm_w_glu=ssm_w_glu, rwkv_mu=rwkv_mu, rwkv_vec=rwkv_vec, rwkv_w2=rwkv_w2, rwkv_a2=rwkv_a2,
               rwkv_g2=rwkv_g2, w_branch=w_branch, w_out=w_out, ln_g=ln_g, ln_b=ln_b, ffn_w_in=ffn_w_in,
               ffn_w_down=ffn_w_down, moe_router=moe_router, moe_w_in=moe_w_in, moe_w_down=moe_w_down)
    hb = x_prompt.reshape(bsz * seq, D_MODEL).astype(F32)
    meta = jnp.broadcast_to(meta_tokens.astype(F32)[None], (bsz, N_META, D_MODEL)).reshape(bsz * N_META, D_MODEL)
    hs = jnp.concatenate([meta, x_sample.reshape(dec, D_MODEL).astype(F32)], axis=0)
    cache = (cache_meta_k, cache_meta_v, cache_win_k, cache_win_v)
    states = (state_ssm_re, state_ssm_im, state_wkv, state_shift)
    attn_outs, ssm_outs, rwkv_outs = [], [], []
    for l in range(DEPTH):
        hb, hs, a_o, s_o, r_o = _layer(l, hb, hs, l == 0, bsz, seq, dec, cache, states, prm)
        attn_outs.append(a_o)
        ssm_outs.append(s_o)
        rwkv_outs.append(r_o)
    stack = lambda outs, i: jnp.stack([o[i] for o in outs])
    y_prompt = hb.reshape(bsz, seq, D_MODEL)
    y_sample = hs[bsz * N_META:].reshape(dec, 1, D_MODEL)
    return (y_prompt, y_sample,
            stack(attn_outs, 0), stack(attn_outs, 1), stack(attn_outs, 2), stack(attn_outs, 3),
            stack(attn_outs, 4), stack(attn_outs, 5),
            stack(ssm_outs, 0), stack(ssm_outs, 1), stack(ssm_outs, 2), stack(ssm_outs, 3),
            stack(rwkv_outs, 0), stack(rwkv_outs, 1), stack(rwkv_outs, 2), stack(rwkv_outs, 3))
```

```python
import functools
import math

import numpy as np
import jax
import jax.numpy as jnp
from jax import lax
from jax.experimental import pallas as pl
from jax.experimental.pallas import tpu as pltpu

F32 = jnp.float32
BF16 = jnp.bfloat16

D_MODEL = 1024
DEPTH = 2
PAST_LEN = 16384
N_META = 16
WINDOW = 128
N_HEADS = 8
N_KV_HEADS = 2
HEAD_DIM = 64
Q_PER_KV = N_HEADS // N_KV_HEADS
ATTN_W = N_HEADS * HEAD_DIM
KV_W = N_KV_HEADS * HEAD_DIM
ATTN_SCALE = HEAD_DIM ** -0.5
REL_BUCKETS = 32
REL_EXACT = REL_BUCKETS // 2
REL_MAX_DIST = 128
SSM_GROUP = 16
SSM_GROUPS = 16
SSM_WIDTH = SSM_GROUP * SSM_GROUPS
SSM_STATE = 64
SSM_N = SSM_GROUPS * SSM_STATE
RWKV_HEAD = 64
RWKV_HEADS = 4
RWKV_WIDTH = RWKV_HEAD * RWKV_HEADS
RWKV_W_LORA = 32
RWKV_A_LORA = 32
RWKV_G_LORA = 64
RWKV_LORA = RWKV_W_LORA + RWKV_A_LORA + RWKV_G_LORA
RWKV_COLS = 3 * RWKV_WIDTH + RWKV_LORA
RV_W0, RV_A0, RV_KK, RV_KA, RV_RK, RV_GNW, RV_GNB = 0, 1, 2, 3, 4, 5, 6
N_BRANCH = 3
MIX_COLS = ATTN_W + 2 * KV_W + SSM_WIDTH + RWKV_COLS
N_EXPERTS = 8
TOP_K = 2
LN_EPS = 1e-5
RWKV_GN_EPS = 64e-5
NEG_INF = -1e30
DEEPNORM_ALPHA = (2 * DEPTH) ** 0.25

LANES = 128
SUBLANES = 8
VMEM_LIMIT = 48 * 1024 * 1024
RWKV_CHUNK = 64


def _params(sem):
    return pltpu.CompilerParams(dimension_semantics=sem, vmem_limit_bytes=VMEM_LIMIT)


def _dot(a, b):
    return jnp.dot(a.astype(BF16), b.astype(BF16), preferred_element_type=F32)


def _dot_nt(a, b):
    return lax.dot_general(a.astype(BF16), b.astype(BF16), (((1,), (1,)), ((), ())),
                           preferred_element_type=F32)


def _split3(x):
    h1 = x.astype(BF16)
    r1 = x - h1.astype(F32)
    h2 = r1.astype(BF16)
    h3 = (r1 - h2.astype(F32)).astype(BF16)
    return h1, h2, h3


def _dot_exact_rhs(x, m):
    h1, h2, h3 = _split3(x)
    dot = functools.partial(jnp.dot, preferred_element_type=F32)
    return dot(h1, m) + dot(h2, m) + dot(h3, m)


def _dot_exact_lhs(m, x):
    h1, h2, h3 = _split3(x)
    dot = functools.partial(jnp.dot, preferred_element_type=F32)
    return dot(m, h1) + dot(m, h2) + dot(m, h3)


def _layer_norm(x, g, b):
    mu = jnp.mean(x, axis=-1, keepdims=True)
    xc = x - mu
    var = jnp.mean(xc * xc, axis=-1, keepdims=True)
    return xc * lax.rsqrt(var + LN_EPS) * g + b


def _sigmoid(x):
    return 1.0 / (1.0 + jnp.exp(-x))


def _rows_to_tiles(x):
    slabs = [x[:, t * LANES:(t + 1) * LANES] for t in range(SUBLANES)]
    return jnp.swapaxes(jnp.stack(slabs, axis=0), 0, 1)


def _tiles_to_cols(x):
    xt = jnp.swapaxes(x, 0, 1)
    return [xt[t] for t in range(SUBLANES)]


def _rows_call(body, row_inputs, const_inputs, out_widths, tm, out_dtypes=None, name=None, extra_inputs=()):
    n = row_inputs[0].shape[0]
    tm = min(tm, n)
    assert n % tm == 0, (n, tm)
    out_dtypes = out_dtypes or [F32] * len(out_widths)
    in_specs = [pl.BlockSpec((tm, a.shape[1]), lambda i: (i, 0)) for a in row_inputs]
    in_specs += [pl.BlockSpec(shape, fn) for _, shape, fn in extra_inputs]
    row_inputs = list(row_inputs) + [a for a, _, _ in extra_inputs]
    in_specs += [pl.BlockSpec(c.shape, lambda i, nd=c.ndim: (0,) * nd) for c in const_inputs]
    tails = [w if isinstance(w, tuple) else (w,) for w in out_widths]
    out_specs = [pl.BlockSpec((tm,) + w, lambda i, nd=len(w): (i,) + (0,) * nd) for w in tails]
    out_shape = [jax.ShapeDtypeStruct((n,) + w, dt) for w, dt in zip(tails, out_dtypes)]
    return pl.pallas_call(
        body, grid=(n // tm,), in_specs=in_specs, out_specs=out_specs, out_shape=out_shape,
        compiler_params=_params(("parallel",)), name=name,
    )(*row_inputs, *const_inputs)


PROJ_WIDTHS = (ATTN_W, KV_W, KV_W, SSM_WIDTH, RWKV_COLS)


def _proj_kernel(x_ref, g_ref, b_ref, w_ref, *out_refs, pre_ln):
    x = x_ref[...]
    if pre_ln:
        x = _layer_norm(x, g_ref[...], b_ref[...])
        out_refs[0][...] = x
        out_refs = out_refs[1:]
    xb = x.astype(BF16)
    col = 0
    for o_ref in out_refs:
        n = o_ref.shape[-1]
        o_ref[...] = jnp.dot(xb, w_ref[:, col:col + n], preferred_element_type=F32)
        col += n


def _proj(x, ln_g, ln_b, w_mix, pre_ln, tm):
    widths = ((D_MODEL,) if pre_ln else ()) + PROJ_WIDTHS
    outs = _rows_call(functools.partial(_proj_kernel, pre_ln=pre_ln), [x], [ln_g, ln_b, w_mix],
                      widths, tm, name="proj")
    if pre_ln:
        return outs[0], outs[1:]
    return x, outs


def _attn_kernel(tab_ref, q_ref, k_ref, v_ref, bias_ref, sink_ref, o_ref):
    del tab_ref
    for bb in range(q_ref.shape[0]):
        for h in range(N_KV_HEADS):
            q = q_ref[bb, h].astype(BF16)
            k = k_ref[bb, h].astype(BF16)
            v = v_ref[bb, h].astype(BF16)
            s = lax.dot_general(q, k, (((1,), (1,)), ((), ())), preferred_element_type=F32)
            s = s * ATTN_SCALE + bias_ref[0, h]
            sk = sink_ref[h]
            m = jnp.maximum(jnp.max(s, axis=-1, keepdims=True), sk)
            p = jnp.exp(s - m)
            den = jnp.sum(p, axis=-1, keepdims=True) + jnp.exp(sk - m)
            o = jnp.dot(p.astype(BF16), v, preferred_element_type=F32)
            o_ref[bb, h] = o / den


def _attention(q, k, v, bias, sinks, tab_idx, bblk):
    p, _, mq, _ = q.shape
    nk = k.shape[2]
    assert p % bblk == 0
    grid_spec = pltpu.PrefetchScalarGridSpec(
        num_scalar_prefetch=1, grid=(p // bblk,),
        in_specs=[
            pl.BlockSpec((bblk, N_KV_HEADS, mq, HEAD_DIM), lambda i, t: (i, 0, 0, 0)),
            pl.BlockSpec((bblk, N_KV_HEADS, nk, HEAD_DIM), lambda i, t: (i, 0, 0, 0)),
            pl.BlockSpec((bblk, N_KV_HEADS, nk, HEAD_DIM), lambda i, t: (i, 0, 0, 0)),
            pl.BlockSpec((1, N_KV_HEADS, mq, nk), lambda i, t: (t[i], 0, 0, 0)),
            pl.BlockSpec((N_KV_HEADS, mq, 1), lambda i, t: (0, 0, 0)),
        ],
        out_specs=pl.BlockSpec((bblk, N_KV_HEADS, mq, HEAD_DIM), lambda i, t: (i, 0, 0, 0)),
    )
    return pl.pallas_call(
        _attn_kernel, grid_spec=grid_spec,
        out_shape=jax.ShapeDtypeStruct(q.shape, F32),
        compiler_params=_params(("arbitrary",)), name="attention",
    )(tab_idx, q, k, v, bias, sinks)


def _t5_bucket(dist):
    n = np.maximum(dist, 0)
    scaled = (np.log(np.maximum(n, 1).astype(np.float32) / np.float32(REL_EXACT))
              / np.float32(math.log(REL_MAX_DIST / REL_EXACT)) * np.float32(REL_BUCKETS - REL_EXACT))
    frac = np.abs(scaled - np.round(scaled))
    assert np.all((n <= REL_EXACT) | (n >= REL_MAX_DIST) | (frac > 1e-3))
    large = np.minimum(REL_EXACT + scaled.astype(np.int32), REL_BUCKETS - 1)
    return np.where(n < REL_EXACT, n, large)


def _bias_table(rel_bias, dist, valid, mq_pad=None, nk_pad=None):
    tq, nk = dist.shape
    onehot = np.eye(REL_BUCKETS, dtype=np.float32)[_t5_bucket(dist).reshape(-1)]
    bias = jnp.dot(jnp.asarray(onehot), rel_bias.astype(F32), precision=lax.Precision.HIGHEST)
    bias = bias.reshape(tq, nk, N_HEADS)
    bias = jnp.where(jnp.asarray(valid)[..., None], bias, NEG_INF)
    bias = jnp.moveaxis(bias, -1, 0).reshape(N_KV_HEADS, Q_PER_KV * tq, nk)
    mq_pad = mq_pad or Q_PER_KV * tq
    nk_pad = nk_pad or nk
    bias = jnp.pad(bias, ((0, 0), (0, mq_pad - Q_PER_KV * tq), (0, 0)))
    return jnp.pad(bias, ((0, 0), (0, 0), (0, nk_pad - nk)), constant_values=NEG_INF)


def _sink_rows(sinks, tq, mq_pad=None):
    s = jnp.repeat(sinks.astype(F32).reshape(N_KV_HEADS, Q_PER_KV, 1), tq, axis=2)
    s = s.reshape(N_KV_HEADS, Q_PER_KV * tq, 1)
    mq_pad = mq_pad or Q_PER_KV * tq
    return jnp.pad(s, ((0, 0), (0, mq_pad - Q_PER_KV * tq), (0, 0)))


def _heads_q(q, nb, tq):
    q = q.reshape(nb, tq, N_KV_HEADS, Q_PER_KV, HEAD_DIM)
    return jnp.transpose(q, (0, 2, 3, 1, 4)).reshape(nb, N_KV_HEADS, Q_PER_KV * tq, HEAD_DIM)


def _unheads_o(o, nb, tq):
    o = o[:, :, :Q_PER_KV * tq].reshape(nb, N_KV_HEADS, Q_PER_KV, tq, HEAD_DIM)
    return jnp.transpose(o, (0, 3, 1, 2, 4)).reshape(nb * tq, ATTN_W)


BODY_KEYS = N_META + 2 * WINDOW + 16


def _body_attn_kernel(q_ref, ko_ref, kp_ref, km_ref, vo_ref, vp_ref, vm_ref, bias_ref, o_ref):
    kv_heads = range(N_KV_HEADS)
    hs = [slice(h * HEAD_DIM, (h + 1) * HEAD_DIM) for h in kv_heads]
    heads = [[h * Q_PER_KV + g for g in range(Q_PER_KV)] for h in kv_heads]
    pad = jnp.zeros((BODY_KEYS - N_META - 2 * WINDOW, HEAD_DIM), F32)
    ones = jnp.ones((BODY_KEYS, HEAD_DIM), BF16)
    k = [jnp.concatenate([km_ref[:, hs[h]], kp_ref[:, hs[h]], ko_ref[:, hs[h]], pad], axis=0).astype(BF16)
         for h in kv_heads]
    v = [jnp.concatenate([vm_ref[:, hs[h]], vp_ref[:, hs[h]], vo_ref[:, hs[h]], pad], axis=0).astype(BF16)
         for h in kv_heads]
    q = [jnp.concatenate([q_ref[:, qh * HEAD_DIM:(qh + 1) * HEAD_DIM] for qh in heads[h]], axis=0).astype(BF16)
         for h in kv_heads]
    s = [lax.dot_general(q[h], k[h], (((1,), (1,)), ((), ())), preferred_element_type=F32) for h in kv_heads]
    s = [s[h] * ATTN_SCALE + bias_ref[0, h] for h in kv_heads]
    p = [jnp.exp(s[h] - jnp.max(s[h], axis=-1, keepdims=True)).astype(BF16) for h in kv_heads]
    o = [jnp.dot(p[h], v[h], preferred_element_type=F32) / jnp.dot(p[h], ones, preferred_element_type=F32)
         for h in kv_heads]
    for h in kv_heads:
        for g, qh in enumerate(heads[h]):
            o_ref[:, qh * HEAD_DIM:(qh + 1) * HEAD_DIM] = o[h][g * WINDOW:(g + 1) * WINDOW]


def _body_attention(q, k, v, k_meta, v_meta, rel_bias, sinks, bsz, seq):
    nblk = seq // WINDOW
    i = np.arange(WINDOW)[:, None]
    c = np.arange(WINDOW)[None, :]
    sink_col = _sink_rows(sinks, WINDOW)
    tabs = []
    for m in (0, 1):
        q_pos = N_META + WINDOW * m + i
        meta_pos = np.arange(N_META)[None, :]
        dist = np.concatenate([q_pos - meta_pos, WINDOW + i - c, i - c], axis=1)
        valid = np.concatenate([np.ones((WINDOW, N_META), bool),
                                (c >= i) & (m > 0), c <= i], axis=1)
        tab = _bias_table(rel_bias, dist, valid, nk_pad=BODY_KEYS)
        tabs.append(tab.at[:, :, N_META + 2 * WINDOW].set(sink_col[:, :, 0]))
    bias = jnp.stack(tabs)
    nk = BODY_KEYS
    mq = Q_PER_KV * WINDOW
    own = lambda w: pl.BlockSpec((WINDOW, w), lambda b, m: (b * nblk + m, 0))
    prev = lambda w: pl.BlockSpec((WINDOW, w), lambda b, m: (b * nblk + jnp.maximum(m - 1, 0), 0))
    meta = pl.BlockSpec((N_META, KV_W), lambda b, m: (b, 0))
    return pl.pallas_call(
        _body_attn_kernel, grid=(bsz, nblk),
        in_specs=[own(ATTN_W), own(KV_W), prev(KV_W), meta, own(KV_W), prev(KV_W), meta,
                  pl.BlockSpec((1, N_KV_HEADS, mq, nk), lambda b, m: (jnp.minimum(m, 1), 0, 0, 0))],
        out_specs=own(ATTN_W), out_shape=jax.ShapeDtypeStruct(q.shape, F32),
        compiler_params=_params(("parallel", "arbitrary")), name="body_attention",
    )(q, k, k, k_meta, v, v, v_meta, bias)


def _meta_attention(q, k, v, rel_bias, sinks, bsz):
    i = np.arange(N_META)
    dist = i[:, None] - i[None, :]
    bias = _bias_table(rel_bias, dist, dist >= 0)[None]
    kv = lambda t: jnp.transpose(t.reshape(bsz, N_META, N_KV_HEADS, HEAD_DIM), (0, 2, 1, 3))
    o = _attention(_heads_q(q, bsz, N_META), kv(k), kv(v), bias, _sink_rows(sinks, N_META),
                   jnp.zeros((bsz,), jnp.int32), bsz)
    return _unheads_o(o, bsz, N_META)


def _sample_attention(q, k_all, v_all, meta_k, meta_v, rel_bias, sinks):
    bsz = q.shape[0]
    wc = k_all.shape[1] - 1
    nk = N_META + wc + 1
    nk_pad = -(-nk // LANES) * LANES
    mq_pad = 8
    k_pos = np.concatenate([np.arange(N_META), PAST_LEN - wc + np.arange(wc + 1)])
    dist = (PAST_LEN - k_pos)[None, :]
    is_meta = (np.arange(nk) < N_META)[None, :]
    valid = (dist >= 0) & (is_meta | ((k_pos[None, :] >= N_META) & (dist <= WINDOW)))
    bias = _bias_table(rel_bias, dist, valid, mq_pad, nk_pad)[None]

    def kv(meta, t):
        full = jnp.concatenate([meta.astype(F32), t], axis=1)
        full = jnp.pad(full, ((0, 0), (0, nk_pad - nk), (0, 0), (0, 0)))
        return jnp.transpose(full, (0, 2, 1, 3))

    qh = jnp.pad(_heads_q(q, bsz, 1), ((0, 0), (0, 0), (0, mq_pad - Q_PER_KV), (0, 0)))
    o = _attention(qh, kv(meta_k, k_all), kv(meta_v, v_all), bias, _sink_rows(sinks, 1, mq_pad),
                   jnp.zeros((bsz // 8,), jnp.int32), 8)
    return _unheads_o(o, bsz, 1)


def _s5_bu_kernel(u_ref, w_ref, re_ref, im_ref):
    r = jnp.dot(u_ref[...].astype(BF16), w_ref[...], preferred_element_type=F32)
    re_ref[...] = r[:, :SSM_N]
    im_ref[...] = r[:, SSM_N:]


def _s5_scan_kernel(bur_ref, bui_ref, h0r_ref, h0i_ref, ar_ref, ai_ref, hr_ref, hi_ref, cr_ref, ci_ref):
    @pl.when(pl.program_id(1) == 0)
    def _():
        cr_ref[...] = h0r_ref[0]
        ci_ref[...] = h0i_ref[0]

    ar = ar_ref[...]
    ai = ai_ref[...]

    def step(t, carry):
        hr, hi = carry
        nr = ar * hr - ai * hi + bur_ref[pl.ds(t, 1), :]
        ni = ar * hi + ai * hr + bui_ref[pl.ds(t, 1), :]
        hr_ref[pl.ds(t, 1), :] = nr
        hi_ref[pl.ds(t, 1), :] = ni
        return nr, ni

    hr, hi = lax.fori_loop(0, bur_ref.shape[0], step, (cr_ref[...], ci_ref[...]), unroll=8)
    cr_ref[...] = hr
    ci_ref[...] = hi


def _s5_scan(bur, bui, h0r, h0i, ar, ai, bsz, seq, tt):
    tt = min(tt, seq)
    nt = seq // tt
    row = pl.BlockSpec((tt, SSM_N), lambda b, t: (b * nt + t, 0))
    st = pl.BlockSpec((1, 1, SSM_N), lambda b, t: (b, 0, 0))
    cst = pl.BlockSpec((1, SSM_N), lambda b, t: (0, 0))
    return pl.pallas_call(
        _s5_scan_kernel, grid=(bsz, nt), in_specs=[row, row, st, st, cst, cst], out_specs=[row, row],
        out_shape=[jax.ShapeDtypeStruct(bur.shape, F32)] * 2,
        scratch_shapes=[pltpu.VMEM((1, SSM_N), F32)] * 2,
        compiler_params=_params(("arbitrary", "arbitrary")), name="s5_scan",
    )(bur, bui, h0r, h0i, ar, ai)


def _s5_step_kernel(bur_ref, bui_ref, h0r_ref, h0i_ref, ar_ref, ai_ref, hr_ref, hi_ref):
    ar, ai, hr, hi = ar_ref[...], ai_ref[...], h0r_ref[...], h0i_ref[...]
    hr_ref[...] = ar * hr - ai * hi + bur_ref[...]
    hi_ref[...] = ar * hi + ai * hr + bui_ref[...]


def _s5_out_kernel(hr_ref, hi_ref, u_ref, wc_ref, d_ref, wg_ref, o_ref):
    y = (_dot(hr_ref[...], wc_ref[:SSM_N]) + _dot(hi_ref[...], wc_ref[SSM_N:])
         + d_ref[...] * u_ref[...])
    z = jax.nn.gelu(y)
    o_ref[...] = z * _sigmoid(_dot(z, wg_ref[...]))


def _block_diag(blocks):
    g, a, b = blocks.shape
    eye = jnp.eye(g, dtype=blocks.dtype)
    return (eye[:, None, :, None] * blocks[:, :, None, :]).reshape(g * a, g * b)


def _s5_weights(a_re, a_im, log_dt, b_re, b_im, c_re, c_im):
    a_re = a_re.astype(F32)
    a_im = a_im.astype(F32)
    dt = jnp.exp(log_dt.astype(F32))[:, None]
    mag = jnp.exp(a_re * dt)
    ab_re = mag * jnp.cos(a_im * dt)
    ab_im = mag * jnp.sin(a_im * dt)
    den = a_re * a_re + a_im * a_im
    nr = ab_re - 1.0
    cf_re = (nr * a_re + ab_im * a_im) / den
    cf_im = (ab_im * a_re - nr * a_im) / den
    b_re = b_re.astype(F32)
    b_im = b_im.astype(F32)
    bb_re = cf_re[..., None] * b_re - cf_im[..., None] * b_im
    bb_im = cf_re[..., None] * b_im + cf_im[..., None] * b_re
    w_b = jnp.concatenate([_block_diag(jnp.swapaxes(bb_re, 1, 2)),
                           _block_diag(jnp.swapaxes(bb_im, 1, 2))], axis=1)
    w_c = jnp.concatenate([_block_diag(jnp.swapaxes(c_re.astype(F32), 1, 2)),
                           -_block_diag(jnp.swapaxes(c_im.astype(F32), 1, 2))], axis=0)
    return ab_re.reshape(1, SSM_N), ab_im.reshape(1, SSM_N), w_b.astype(BF16), w_c.astype(BF16)


def _seg_ones():
    r = lax.broadcasted_iota(jnp.int32, (RWKV_WIDTH, RWKV_WIDTH), 0) // RWKV_HEAD
    c = lax.broadcasted_iota(jnp.int32, (RWKV_WIDTH, RWKV_WIDTH), 1) // RWKV_HEAD
    return (r == c).astype(BF16)


def _rwkv_prep_kernel(pc_ref, prev_ref, *rest):
    _rwkv_prep(pc_ref[...], prev_ref[...], *rest)


def _rwkv_prep_shift_kernel(pc_ref, tail_ref, first_ref, *rest, tiles_per_seq):
    pc = pc_ref[...]
    seq_start = pl.program_id(0) % tiles_per_seq == 0
    above = jnp.where(seq_start, first_ref[0], tail_ref[SUBLANES - 1:SUBLANES, :])
    row = lax.broadcasted_iota(jnp.int32, pc.shape, 0)
    _rwkv_prep(pc, jnp.where(row == 0, above, pltpu.roll(pc, 1, axis=0)), *rest)


def _rwkv_prep(pc, prev, mu_ref, vec_ref, w2_ref, a2_ref, g2_ref,
               r_ref, k_ref, v_ref, lw_ref, kk_ref, bb_ref, g_ref):
    xm = pc + (prev - pc) * mu_ref[...]
    rw = RWKV_WIDTH
    xr, xk, xv, xl = xm[:, :rw], xm[:, rw:2 * rw], xm[:, 2 * rw:3 * rw], xm[:, 3 * rw:]
    vec = vec_ref[...]
    wpre = -(vec[RV_W0:RV_W0 + 1] + _dot(jnp.tanh(xl), w2_ref[...]))
    softplus = jnp.maximum(wpre, 0.0) + jnp.log(1.0 + jnp.exp(-jnp.abs(wpre)))
    lw_ref[...] = -jnp.exp(-softplus - 0.5)
    a = _sigmoid(vec[RV_A0:RV_A0 + 1] + _dot(xl, a2_ref[...]))
    g_ref[...] = _dot(_sigmoid(xl), g2_ref[...])
    kk = xk * vec[RV_KK:RV_KK + 1]
    norm = jnp.sqrt(_dot_exact_rhs(kk * kk, _seg_ones()))
    kk = kk / jnp.maximum(norm, 1e-12)
    r_ref[...] = xr
    k_ref[...] = xk * (1.0 + (a - 1.0) * vec[RV_KA:RV_KA + 1])
    v_ref[...] = xv
    kk_ref[...] = kk
    bb_ref[...] = kk * a


def _rwkv_post_kernel(y_ref, r_ref, k_ref, v_ref, g_ref, vec_ref, o_ref):
    ones = _seg_ones()
    vec = vec_ref[...]
    y = y_ref[...]
    yc = y - _dot_exact_rhs(y, ones) * (1.0 / RWKV_HEAD)
    yv = _dot_exact_rhs(yc * yc, ones) * (1.0 / RWKV_HEAD)
    yn = yc * lax.rsqrt(yv + RWKV_GN_EPS) * vec[RV_GNW:RV_GNW + 1] + vec[RV_GNB:RV_GNB + 1]
    bonus = _dot_exact_rhs(r_ref[...] * k_ref[...] * vec[RV_RK:RV_RK + 1], ones) * v_ref[...]
    o_ref[...] = (yn + bonus) * g_ref[...]


def _rwkv_chunk_kernel(*refs, chunk, nb):
    r_refs, k_refs, v_refs, lw_refs, kk_refs, bb_refs = (refs[i * nb:(i + 1) * nb] for i in range(6))
    s0_ref = refs[6 * nb]
    y_refs = refs[6 * nb + 1:7 * nb + 1]
    so_ref, s_ref = refs[7 * nb + 1:]
    t = pl.program_id(0)

    @pl.when(t == 0)
    def _():
        s_ref[...] = s0_ref[...]

    c = chunk
    row = lax.broadcasted_iota(jnp.int32, (c, c), 0)
    col = lax.broadcasted_iota(jnp.int32, (c, c), 1)
    incl = (row >= col).astype(F32)
    strict = (row > col).astype(F32)
    eye = (row == col).astype(F32)
    nsub = r_refs[0].shape[0] // c
    heads = range(RWKV_HEADS)
    hsl = [slice(h * RWKV_HEAD, (h + 1) * RWKV_HEAD) for h in heads]
    tsl = [slice(sub * c, (sub + 1) * c) for sub in range(nsub)]
    seqs = [(b, h) for b in range(nb) for h in heads]
    units = [(b, sub, h) for b in range(nb) for sub in range(nsub) for h in heads]

    scaled = {}
    for b in range(nb):
        for sub in range(nsub):
            lw = lw_refs[b][tsl[sub], :]
            cum = _dot_exact_lhs(incl.astype(BF16), lw)
            ecum = jnp.exp(cum)
            einv = jnp.exp(-cum)
            scaled[b, sub] = (kk_refs[b][tsl[sub], :] * jnp.exp(cum - lw), bb_refs[b][tsl[sub], :] * einv,
                              k_refs[b][tsl[sub], :] * einv, r_refs[b][tsl[sub], :] * ecum, ecum[c - 1:c, :])
    kt = {u: scaled[u[0], u[1]][0][:, hsl[u[2]]] for u in units}
    bt = {u: scaled[u[0], u[1]][1][:, hsl[u[2]]] for u in units}
    kkt = {u: scaled[u[0], u[1]][2][:, hsl[u[2]]] for u in units}
    rt = {u: scaled[u[0], u[1]][3][:, hsl[u[2]]] for u in units}
    v_t = [v_refs[b][...].T for b in range(nb)]
    vt = {u: v_t[u[0]][hsl[u[2]], tsl[u[1]]] for u in units}
    a_b = {u: strict * _dot_nt(kt[u], bt[u]) for u in units}
    a_k = {u: strict * _dot_nt(kt[u], kkt[u]) for u in units}
    r_b = {u: incl * _dot_nt(rt[u], bt[u]) for u in units}
    r_k = {u: incl * _dot_nt(rt[u], kkt[u]) for u in units}
    pw = {u: -a_b[u] for u in units}
    tinv = {u: eye + pw[u] for u in units}
    n = 1
    while 2 * n < c:
        pw = {u: _dot(pw[u], pw[u]) for u in units}
        tinv = {u: tinv[u] + _dot(tinv[u], pw[u]) for u in units}
        n *= 2
    x = {u: _dot_nt(vt[u], a_k[u]) for u in units}
    w1t = {u: _dot_nt(x[u], tinv[u]) for u in units}
    w2 = {u: _dot(tinv[u], kt[u]) for u in units}
    yt_local = {u: _dot_nt(vt[u], r_k[u]) for u in units}
    s_local = {u: _dot(vt[u], kkt[u]) for u in units}

    s = {q: s_ref[q[0], q[1]] for q in seqs}
    yt = {}
    for sub in range(nsub):
        ut = {(b, h): -(_dot_nt(s[b, h], w2[b, sub, h]) + w1t[b, sub, h]) for b, h in seqs}
        for b, h in seqs:
            yt[b, sub, h] = (_dot_nt(s[b, h], rt[b, sub, h]) + _dot_nt(ut[b, h], r_b[b, sub, h])
                             + yt_local[b, sub, h])
        s = {(b, h): (s[b, h] + _dot(ut[b, h], bt[b, sub, h]) + s_local[b, sub, h]) * scaled[b, sub][4][:, hsl[h]]
             for b, h in seqs}
    for b, h in seqs:
        s_ref[b, h] = s[b, h]
    for b in range(nb):
        y_t = jnp.concatenate([jnp.concatenate([yt[b, sub, h] for sub in range(nsub)], axis=1) for h in heads],
                              axis=0)
        y_refs[b][...] = y_t.T

    @pl.when(t == pl.num_programs(0) - 1)
    def _():
        so_ref[...] = s_ref[...]


def _rwkv_chunk(r, k, v, lw, kk, bb, s0, bsz, seq):
    tb = 2 * RWKV_CHUNK
    assert seq % tb == 0
    nt = seq // tb
    tiles = [pl.BlockSpec((tb, RWKV_WIDTH), lambda t, b=b: (b * nt + t, 0)) for b in range(bsz)]
    st = pl.BlockSpec(s0.shape, lambda t: (0, 0, 0, 0))
    *ys, s_out = pl.pallas_call(
        functools.partial(_rwkv_chunk_kernel, chunk=RWKV_CHUNK, nb=bsz), grid=(nt,),
        in_specs=tiles * 6 + [st], out_specs=[pl.BlockSpec((tb, RWKV_WIDTH), lambda t: (t, 0))] * bsz + [st],
        out_shape=[jax.ShapeDtypeStruct((seq, RWKV_WIDTH), F32)] * bsz + [jax.ShapeDtypeStruct(s0.shape, F32)],
        scratch_shapes=[pltpu.VMEM(s0.shape, F32)],
        compiler_params=_params(("arbitrary",)), name="rwkv_chunk",
    )(*([r] * bsz + [k] * bsz + [v] * bsz + [lw] * bsz + [kk] * bsz + [bb] * bsz), s0)
    return jnp.concatenate(ys, axis=0), s_out


def _rwkv_step_kernel(s_ref, r_ref, k_ref, lw_ref, kk_ref, bb_ref, v_ref, so_ref, y_ref):
    s = s_ref[...]
    sa = jnp.sum(s * (-kk_ref[...]), axis=-1, keepdims=True)
    s = s * jnp.exp(lw_ref[...]) + sa * bb_ref[...] + v_ref[...] * k_ref[...]
    so_ref[...] = s
    y_ref[...] = jnp.sum(s * r_ref[...], axis=-1, keepdims=True)


def _rwkv_step(r, k, v, lw, kk, bb, s0):
    bsz = r.shape[0]
    p = bsz * RWKV_HEADS
    nb = min(64, p)
    rowv = lambda a: a.reshape(p, 1, RWKV_HEAD)
    rs = pl.BlockSpec((nb, 1, RWKV_HEAD), lambda i: (i, 0, 0))
    cs = pl.BlockSpec((nb, RWKV_HEAD, 1), lambda i: (i, 0, 0))
    ss = pl.BlockSpec((nb, RWKV_HEAD, RWKV_HEAD), lambda i: (i, 0, 0))
    s_out, y = pl.pallas_call(
        _rwkv_step_kernel, grid=(p // nb,), in_specs=[ss, rs, rs, rs, rs, rs, cs], out_specs=[ss, cs],
        out_shape=[jax.ShapeDtypeStruct((p, RWKV_HEAD, RWKV_HEAD), F32),
                   jax.ShapeDtypeStruct((p, RWKV_HEAD, 1), F32)],
        compiler_params=_params(("parallel",)), name="rwkv_step",
    )(s0.reshape(p, RWKV_HEAD, RWKV_HEAD), rowv(r), rowv(k), rowv(lw), rowv(kk), rowv(bb),
      v.reshape(p, RWKV_HEAD, 1))
    return y.reshape(bsz, RWKV_WIDTH), s_out.reshape(s0.shape)


def _merge_kernel(h_ref, oa_ref, ob_ref, oc_ref, wg_ref, wb_ref, wo_ref, g_ref, b_ref, o_ref, *tile_refs):
    h = h_ref[...]
    gates = _sigmoid(jnp.dot(h.astype(BF16), wg_ref[...], preferred_element_type=F32))
    d = D_MODEL
    merged = (gates[:, :d] * _dot(oa_ref[...], wb_ref[:ATTN_W])
              + gates[:, d:2 * d] * _dot(ob_ref[...], wb_ref[ATTN_W:ATTN_W + SSM_WIDTH])
              + gates[:, 2 * d:] * _dot(oc_ref[...], wb_ref[ATTN_W + SSM_WIDTH:]))
    mix = _dot(merged, wo_ref[...])
    out = _layer_norm(DEEPNORM_ALPHA * h + mix, g_ref[...], b_ref[...])
    o_ref[...] = out
    for t_ref in tile_refs:
        t_ref[...] = _rows_to_tiles(out)


def _ffn_kernel(be_ref, x_ref, wg_ref, wu_ref, wd_ref, g_ref, b_ref, o_ref, xb_ref, acc_ref, *, post_ln):
    del be_ref
    j = pl.program_id(1)

    @pl.when(j == 0)
    def _():
        xb_ref[...] = x_ref[...].astype(BF16)
        acc_ref[...] = jnp.zeros_like(acc_ref)

    xb = xb_ref[...]
    gate = jnp.dot(xb, wg_ref[0], preferred_element_type=F32)
    up = jnp.dot(xb, wu_ref[0], preferred_element_type=F32)
    act = gate * _sigmoid(gate) * up
    acc_ref[...] += jnp.dot(act.astype(BF16), wd_ref[0], preferred_element_type=F32)

    @pl.when(j == pl.num_programs(1) - 1)
    def _():
        if post_ln:
            o_ref[...] = _layer_norm(DEEPNORM_ALPHA * x_ref[...] + acc_ref[...], g_ref[...], b_ref[...])
        else:
            o_ref[...] = acc_ref[...]


def _ffn(x, block_e, w_in, w_down, ln_g, ln_b, blk, tf, post_ln):
    rows = x.shape[0]
    assert rows % blk == 0
    f = w_down.shape[1]
    nf = f // tf
    grid_spec = pltpu.PrefetchScalarGridSpec(
        num_scalar_prefetch=1, grid=(rows // blk, nf),
        in_specs=[
            pl.BlockSpec((blk, D_MODEL), lambda i, j, be: (i, 0)),
            pl.BlockSpec((1, D_MODEL, tf), lambda i, j, be: (be[i], 0, j)),
            pl.BlockSpec((1, D_MODEL, tf), lambda i, j, be: (be[i], 0, nf + j)),
            pl.BlockSpec((1, tf, D_MODEL), lambda i, j, be: (be[i], j, 0)),
            pl.BlockSpec((1, D_MODEL), lambda i, j, be: (0, 0)),
            pl.BlockSpec((1, D_MODEL), lambda i, j, be: (0, 0)),
        ],
        out_specs=pl.BlockSpec((blk, D_MODEL), lambda i, j, be: (i, 0)),
        scratch_shapes=[pltpu.VMEM((blk, D_MODEL), BF16), pltpu.VMEM((blk, D_MODEL), F32)],
    )
    return pl.pallas_call(
        functools.partial(_ffn_kernel, post_ln=post_ln), grid_spec=grid_spec,
        out_shape=jax.ShapeDtypeStruct((rows, D_MODEL), F32),
        compiler_params=_params(("arbitrary", "arbitrary")), name="ffn",
    )(block_e, x, w_in, w_in, w_down, ln_g, ln_b)


def _router_kernel(h_ref, w_ref, e_ref, g_ref):
    logits = jnp.dot(h_ref[...], w_ref[...], preferred_element_type=F32, precision=lax.Precision.HIGHEST)
    lane = lax.broadcasted_iota(jnp.int32, logits.shape, 1)
    lg = jnp.where(lane < N_EXPERTS, logits, -jnp.inf)
    m1 = jnp.max(lg, axis=-1, keepdims=True)
    i1 = jnp.min(jnp.where(lg == m1, lane, LANES), axis=-1, keepdims=True)
    lg2 = jnp.where(lane == i1, -jnp.inf, lg)
    m2 = jnp.max(lg2, axis=-1, keepdims=True)
    i2 = jnp.min(jnp.where(lg2 == m2, lane, LANES), axis=-1, keepdims=True)
    e2 = jnp.exp(m2 - m1)
    den = 1.0 + e2
    e_ref[...] = jnp.where(lane == 0, i1, jnp.where(lane == 1, i2, 0))
    g_ref[...] = jnp.where(lane == 0, 1.0 / den, jnp.where(lane == 1, e2 / den, 0.0))


def _moe_ffn_kernel(be_ref, nv_ref, first_ref, nxt_ref, prev_dst_ref, last_dst_ref, h_hbm, wg_ref, wu_ref, wd_ref,
                    out_hbm, xbuf, xb_ref, acc_ref, stage, gsem, ssem):
    del be_ref
    i, j = pl.program_id(0), pl.program_id(1)
    nblk, nf = pl.num_programs(0), pl.num_programs(1)
    blk = xb_ref.shape[0]
    n_valid = nv_ref[0]
    valid = i < n_valid
    slot = lax.rem(i, 2)
    spare_base = out_hbm.shape[0] - blk

    def gather_row(idx_ref, r, s):
        pltpu.make_async_copy(h_hbm.at[idx_ref[0, 0, r]], xbuf.at[s, r], gsem.at[s]).start(priority=1)

    def scatter_row(r, d):
        pltpu.make_async_copy(stage.at[r], out_hbm.at[d], ssem).start(priority=1)

    def wait_gather(s):
        pltpu.make_async_copy(h_hbm.at[pl.ds(0, blk)], xbuf.at[s], gsem.at[s]).wait()

    def wait_scatter():
        pltpu.make_async_copy(stage, out_hbm.at[pl.ds(0, blk)], ssem).wait()

    def for_rows(fn):
        def body(r, c):
            fn(r)
            return c
        lax.fori_loop(0, blk, body, 0, unroll=8)

    def compute():
        xb = xb_ref[...]
        gate = jnp.dot(xb, wg_ref[0], preferred_element_type=F32)
        up = jnp.dot(xb, wu_ref[0], preferred_element_type=F32)
        act = gate * _sigmoid(gate) * up
        acc_ref[...] += jnp.dot(act.astype(BF16), wd_ref[0], preferred_element_type=F32)

    @pl.when((i == 0) & (j == 0))
    def _():
        for_rows(lambda r: gather_row(first_ref, r, 0))
        stage[...] = jnp.zeros_like(stage)

    @pl.when(valid & (j == 0))
    def _():
        wait_gather(slot)
        for t, cols in enumerate(_tiles_to_cols(xbuf[slot])):
            xb_ref[:, t * LANES:(t + 1) * LANES] = cols.astype(BF16)
        acc_ref[...] = jnp.zeros_like(acc_ref)

    @pl.when(valid & (j < nf - 1))
    def _():
        compute()
        per_step = blk // (nf - 1)
        for rr in range(per_step):
            r = j * per_step + rr
            gather_row(nxt_ref, r, 1 - slot)
            scatter_row(r, jnp.where(i > 0, prev_dst_ref[0, 0, r], spare_base + r))

    @pl.when(valid & (j == nf - 1))
    def _():
        compute()
        wait_scatter()
        stage[...] = _rows_to_tiles(acc_ref[...])

    @pl.when((i == nblk - 1) & (j == nf - 1))
    def _():
        for_rows(lambda r: scatter_row(r, last_dst_ref[0, 0, r]))
        wait_scatter()
        wait_gather(lax.rem(n_valid, 2))


def _moe_ffn(h_tiles, rows_tok, rows_dst, block_e, n_valid, w_in, w_down, blk, tf):
    n = h_tiles.shape[0]
    rows = rows_tok.shape[0]
    nblk = rows // blk
    nf = w_down.shape[1] // tf
    assert blk % (nf - 1) == 0 and n >= blk
    idx3 = lambda a: a.reshape(nblk, 1, blk)
    smem = lambda fn: pl.BlockSpec((1, 1, blk), fn, memory_space=pltpu.SMEM)
    ftile = lambda i, j, nv: jnp.where(i < nv[0], j, nf - 1)
    grid_spec = pltpu.PrefetchScalarGridSpec(
        num_scalar_prefetch=2, grid=(nblk, nf),
        in_specs=[
            smem(lambda i, j, be, nv: (0, 0, 0)),
            smem(lambda i, j, be, nv: (jnp.minimum(i + 1, nblk - 1), 0, 0)),
            smem(lambda i, j, be, nv: (jnp.maximum(i - 1, 0), 0, 0)),
            smem(lambda i, j, be, nv: (nv[0] - 1, 0, 0)),
            pl.BlockSpec(memory_space=pl.ANY),
            pl.BlockSpec((1, D_MODEL, tf), lambda i, j, be, nv: (be[i], 0, ftile(i, j, nv))),
            pl.BlockSpec((1, D_MODEL, tf), lambda i, j, be, nv: (be[i], 0, nf + ftile(i, j, nv))),
            pl.BlockSpec((1, tf, D_MODEL), lambda i, j, be, nv: (be[i], ftile(i, j, nv), 0)),
        ],
        out_specs=pl.BlockSpec(memory_space=pl.ANY),
        scratch_shapes=[pltpu.VMEM((2, blk, SUBLANES, LANES), F32), pltpu.VMEM((blk, D_MODEL), BF16),
                        pltpu.VMEM((blk, D_MODEL), F32), pltpu.VMEM((blk, SUBLANES, LANES), F32),
                        pltpu.SemaphoreType.DMA((2,)), pltpu.SemaphoreType.DMA(())],
    )
    return pl.pallas_call(
        _moe_ffn_kernel, grid_spec=grid_spec,
        out_shape=jax.ShapeDtypeStruct((TOP_K * n + blk, SUBLANES, LANES), F32),
        compiler_params=_params(("arbitrary", "arbitrary")), name="moe_ffn",
    )(block_e, n_valid, idx3(rows_tok), idx3(rows_tok), idx3(rows_dst), idx3(rows_dst), h_tiles,
      w_in, w_in, w_down)


def _combine_kernel(h_ref, y_ref, gate_ref, g_ref, bias_ref, o_ref):
    gate = gate_ref[...]
    rows = lambda s: jnp.concatenate(_tiles_to_cols(y_ref[:, s]), axis=-1)
    f = rows(0) * gate[:, 0:1] + rows(1) * gate[:, 1:2]
    o_ref[...] = _layer_norm(DEEPNORM_ALPHA * h_ref[...] + f, g_ref[...], bias_ref[...])


def _moe(h, h_tiles, router_pad, w_in, w_down, ln_g, ln_b, tm, blk, tf):
    n = h.shape[0]
    e_pad, gate = _rows_call(_router_kernel, [h], [router_pad], (LANES, LANES), tm,
                             out_dtypes=[jnp.int32, F32], name="router")
    flat_e = e_pad[:, :TOP_K].reshape(-1)
    n_assign = n * TOP_K
    n_blocks = -(-(n_assign + N_EXPERTS * (blk - 1)) // blk)
    onehot = (flat_e[:, None] == jnp.arange(N_EXPERTS, dtype=jnp.int32)[None, :]).astype(jnp.int32)
    csum = jnp.cumsum(onehot, axis=0)
    rank = jnp.sum(csum * onehot, axis=1) - 1
    counts = csum[-1]
    padded = (counts + blk - 1) // blk * blk
    pad_end = jnp.cumsum(padded)
    dest = (pad_end - padded)[flat_e] + rank
    assign = jnp.arange(n_assign, dtype=jnp.int32)
    rows_dst = jnp.full((n_blocks * blk,), -1, jnp.int32).at[dest].set(assign, unique_indices=True)
    rows_tok = jnp.maximum(rows_dst, 0) // TOP_K
    spare = n_assign + jnp.arange(n_blocks * blk, dtype=jnp.int32) % blk
    rows_dst = jnp.where(rows_dst < 0, spare, rows_dst)
    block_e = jnp.minimum(jnp.searchsorted(pad_end, jnp.arange(n_blocks, dtype=jnp.int32) * blk, side='right'),
                          N_EXPERTS - 1).astype(jnp.int32)
    n_valid = (pad_end[-1:] // blk).astype(jnp.int32)
    y = _moe_ffn(h_tiles, rows_tok, rows_dst, block_e, n_valid, w_in, w_down, blk, tf)
    tm = min(tm, n)
    cst = pl.BlockSpec((1, D_MODEL), lambda i: (0, 0))
    return pl.pallas_call(
        _combine_kernel, grid=(n // tm,),
        in_specs=[pl.BlockSpec((tm, D_MODEL), lambda i: (i, 0)),
                  pl.BlockSpec((tm, TOP_K, SUBLANES, LANES), lambda i: (i, 0, 0, 0)),
                  pl.BlockSpec((tm, LANES), lambda i: (i, 0)), cst, cst],
        out_specs=pl.BlockSpec((tm, D_MODEL), lambda i: (i, 0)),
        out_shape=jax.ShapeDtypeStruct((n, D_MODEL), F32),
        compiler_params=_params(("parallel",)), name="moe_combine",
    )(h, y.reshape(-1, TOP_K, SUBLANES, LANES), gate, ln_g, ln_b)


def _layer(l, hb, hs, pre_ln, bsz, seq, dec, cache, states, prm):
    (cache_meta_k, cache_meta_v, cache_win_k, cache_win_v) = cache
    (state_ssm_re, state_ssm_im, state_wkv, state_shift) = states
    n_meta = bsz * N_META
    tm_b, tm_s = 512, hs.shape[0]
    w_in = prm['w_in'][l]
    w_mix = w_in[:, :MIX_COLS].astype(BF16)
    w_gate = w_in[:, MIX_COLS:].astype(BF16)
    ln_in_g, ln_in_b = prm['ln_in_g'].reshape(1, -1), prm['ln_in_b'].reshape(1, -1)

    hb, (q_b, k_b, v_b, u_b, pc_b) = _proj(hb, ln_in_g, ln_in_b, w_mix, pre_ln, tm_b)
    hs, (q_s, k_s, v_s, u_s, pc_s) = _proj(hs, ln_in_g, ln_in_b, w_mix, pre_ln, tm_s)

    rel_bias, sinks = prm['rel_bias'], prm['attn_sinks'][l]
    k_meta, v_meta = k_s[:n_meta], v_s[:n_meta]
    oa_b = _body_attention(q_b, k_b, v_b, k_meta, v_meta, rel_bias, sinks, bsz, seq)
    oa_m = _meta_attention(q_s[:n_meta], k_meta, v_meta, rel_bias, sinks, bsz)
    kd = lambda t: t[n_meta:].reshape(dec, 1, N_KV_HEADS, HEAD_DIM)
    k_all = jnp.concatenate([cache_win_k[l].astype(F32), kd(k_s)], axis=1)
    v_all = jnp.concatenate([cache_win_v[l].astype(F32), kd(v_s)], axis=1)
    oa_d = _sample_attention(q_s[n_meta:], k_all, v_all, cache_meta_k[l], cache_meta_v[l], rel_bias, sinks)
    oa_s = jnp.concatenate([oa_m, oa_d], axis=0)
    kv4 = lambda t, b: t.reshape(b, -1, N_KV_HEADS, HEAD_DIM)
    tail = lambda t: kv4(jnp.concatenate([t[(b + 1) * seq - WINDOW:(b + 1) * seq] for b in range(bsz)], axis=0), bsz)
    attn_out = (kv4(k_meta, bsz), kv4(v_meta, bsz), tail(k_b), tail(v_b), k_all[:, 1:], v_all[:, 1:])

    ar, ai, w_b, w_c = _s5_weights(prm['ssm_a_re'][l], prm['ssm_a_im'][l], prm['ssm_log_dt'][l],
                                   prm['ssm_b_re'][l], prm['ssm_b_im'][l], prm['ssm_c_re'][l], prm['ssm_c_im'][l])
    bur_b, bui_b = _rows_call(_s5_bu_kernel, [u_b], [w_b], (SSM_N, SSM_N), 512, name="s5_bu")
    bur_s, bui_s = _rows_call(_s5_bu_kernel, [u_s], [w_b], (SSM_N, SSM_N), tm_s, name="s5_bu")
    zero_h = jnp.zeros((bsz, 1, SSM_N), F32)
    hr_m, hi_m = _s5_scan(bur_s[:n_meta], bui_s[:n_meta], zero_h, zero_h, ar, ai, bsz, N_META, N_META)
    last = lambda t: t.reshape(bsz, -1, SSM_N)[:, -1:]
    hr_b, hi_b = _s5_scan(bur_b, bui_b, last(hr_m), last(hi_m), ar, ai, bsz, seq, 512)
    hr_d, hi_d = _rows_call(_s5_step_kernel,
                            [bur_s[n_meta:], bui_s[n_meta:], state_ssm_re[l].reshape(dec, SSM_N).astype(F32),
                             state_ssm_im[l].reshape(dec, SSM_N).astype(F32)], [ar, ai], (SSM_N, SSM_N), dec,
                            name="s5_step")
    hr_s = jnp.concatenate([hr_m, hr_d], axis=0)
    hi_s = jnp.concatenate([hi_m, hi_d], axis=0)
    s5_consts = [w_c, prm['ssm_d'][l].reshape(1, -1).astype(F32), prm['ssm_w_glu'][l].astype(BF16)]
    (ob_b,) = _rows_call(_s5_out_kernel, [hr_b, hi_b, u_b], s5_consts, (SSM_WIDTH,), 512, name="s5_out")
    (ob_s,) = _rows_call(_s5_out_kernel, [hr_s, hi_s, u_s], s5_consts, (SSM_WIDTH,), tm_s, name="s5_out")
    st4 = lambda t, b: t.reshape(b, SSM_GROUPS, SSM_STATE)
    ssm_out = (st4(last(hr_b), bsz), st4(last(hi_b), bsz), st4(hr_d, dec), st4(hi_d, dec))

    pc_m = pc_s[:n_meta].reshape(bsz, N_META, RWKV_COLS)
    prev_m = jnp.concatenate([jnp.zeros((bsz, 1, RWKV_COLS), F32), pc_m[:, :-1]], axis=1)
    prev_s = jnp.concatenate([prev_m.reshape(n_meta, RWKV_COLS), state_shift[l].astype(F32)], axis=0)
    pad_rows = lambda w, lo: jnp.pad(w.astype(F32), ((lo, RWKV_LORA - lo - w.shape[0]), (0, 0))).astype(BF16)
    vec = jnp.pad(prm['rwkv_vec'][l].astype(F32), ((0, 1), (0, 0)))
    prep_consts = [prm['rwkv_mu'][l].reshape(1, -1).astype(F32), vec,
                   pad_rows(prm['rwkv_w2'][l], 0), pad_rows(prm['rwkv_a2'][l], RWKV_W_LORA),
                   pad_rows(prm['rwkv_g2'][l], RWKV_W_LORA + RWKV_A_LORA)]
    w7 = (RWKV_WIDTH,) * 7
    tm_prep = min(512, seq)
    tail_blocks = tm_prep // SUBLANES
    shift_inputs = [(pc_b, (SUBLANES, RWKV_COLS), lambda i: (jnp.maximum(i * tail_blocks - 1, 0), 0)),
                    (pc_m[:, -1:], (1, 1, RWKV_COLS), lambda i: (i // (seq // tm_prep), 0, 0))]
    r_b, kx_b, vx_b, lw_b, kk_b, bb_b, g_b = _rows_call(
        functools.partial(_rwkv_prep_shift_kernel, tiles_per_seq=seq // tm_prep), [pc_b], prep_consts, w7, tm_prep,
        name="rwkv_prep", extra_inputs=shift_inputs)
    r_s, kx_s, vx_s, lw_s, kk_s, bb_s, g_s = _rows_call(_rwkv_prep_kernel, [pc_s, prev_s], prep_consts, w7, tm_s,
                                                        name="rwkv_prep")
    meta_len = 2 * RWKV_CHUNK
    mrows = lambda t: jnp.pad(t[:n_meta].reshape(bsz, N_META, RWKV_WIDTH),
                              ((0, 0), (0, meta_len - N_META), (0, 0))).reshape(bsz * meta_len, RWKV_WIDTH)
    drows = lambda t: t[n_meta:]
    zero_s = jnp.zeros((bsz, RWKV_HEADS, RWKV_HEAD, RWKV_HEAD), F32)
    y_m, s_m = _rwkv_chunk(mrows(r_s), mrows(kx_s), mrows(vx_s), mrows(lw_s), mrows(kk_s), mrows(bb_s),
                           zero_s, bsz, meta_len)
    y_m = y_m.reshape(bsz, meta_len, RWKV_WIDTH)[:, :N_META].reshape(n_meta, RWKV_WIDTH)
    y_b, s_b = _rwkv_chunk(r_b, kx_b, vx_b, lw_b, kk_b, bb_b, s_m, bsz, seq)
    y_d, s_d = _rwkv_step(drows(r_s), drows(kx_s), drows(vx_s), drows(lw_s), drows(kk_s), drows(bb_s),
                          state_wkv[l].astype(F32))
    y_s = jnp.concatenate([y_m, y_d], axis=0)
    (oc_b,) = _rows_call(_rwkv_post_kernel, [y_b, r_b, kx_b, vx_b, g_b], [vec], (RWKV_WIDTH,), 512,
                         name="rwkv_post")
    (oc_s,) = _rows_call(_rwkv_post_kernel, [y_s, r_s, kx_s, vx_s, g_s], [vec], (RWKV_WIDTH,), tm_s,
                         name="rwkv_post")
    rwkv_out = (s_b, s_d, pc_b.reshape(bsz, seq, RWKV_COLS)[:, -1], pc_s[n_meta:])

    ln_g, ln_b = prm['ln_g'][l].astype(F32), prm['ln_b'][l].astype(F32)
    merge_consts = [w_gate, prm['w_branch'][l].astype(BF16), prm['w_out'][l].astype(BF16), ln_g[0:1], ln_b[0:1]]
    moe_layer = l % 2 == 1
    merge_outs = (D_MODEL,) + (((SUBLANES, LANES),) if moe_layer else ())
    hb, *hb_tiles = _rows_call(_merge_kernel, [hb, oa_b, ob_b, oc_b], merge_consts, merge_outs, tm_b, name="merge")
    hs, *hs_tiles = _rows_call(_merge_kernel, [hs, oa_s, ob_s, oc_s], merge_consts, merge_outs, tm_s, name="merge")

    if not moe_layer:
        w_ffn_in = prm['ffn_w_in'][l // 2].astype(BF16)[None]
        w_ffn_down = prm['ffn_w_down'][l // 2].astype(BF16)[None]
        blk_b = min(1024, hb.shape[0])
        hb = _ffn(hb, jnp.zeros((hb.shape[0] // blk_b,), jnp.int32), w_ffn_in, w_ffn_down,
                  ln_g[1:2], ln_b[1:2], blk_b, 256, post_ln=True)
        hs = _ffn(hs, jnp.zeros((1,), jnp.int32), w_ffn_in, w_ffn_down, ln_g[1:2], ln_b[1:2], tm_s, 256,
                  post_ln=True)
    else:
        router_pad = jnp.pad(prm['moe_router'][l // 2].astype(F32), ((0, 0), (0, LANES - N_EXPERTS)))
        w_moe_in = prm['moe_w_in'][l // 2].astype(BF16)
        w_moe_down = prm['moe_w_down'][l // 2].astype(BF16)
        hb = _moe(hb, hb_tiles[0], router_pad, w_moe_in, w_moe_down, ln_g[1:2], ln_b[1:2], 512, 1008, 512)
        hs = _moe(hs, hs_tiles[0], router_pad, w_moe_in, w_moe_down, ln_g[1:2], ln_b[1:2], tm_s, 96, 512)
    return hb, hs, attn_out, ssm_out, rwkv_out


def kernel(x_prompt, x_sample, cache_meta_k, cache_meta_v, cache_win_k, cache_win_v, state_ssm_re, state_ssm_im, state_wkv, state_shift, meta_tokens, ln_in_g, ln_in_b, w_in, rel_bias, attn_sinks, ssm_a_re, ssm_a_im, ssm_log_dt, ssm_b_re, ssm_b_im, ssm_c_re, ssm_c_im, ssm_d, ssm_w_glu, rwkv_mu, rwkv_vec, rwkv_w2, rwkv_a2, rwkv_g2, w_branch, w_out, ln_g, ln_b, ffn_w_in, ffn_w_down, moe_router, moe_w_in, moe_w_down):
    bsz, seq, _ = x_prompt.shape
    dec = x_sample.shape[0]
    assert x_sample.shape[1] == 1 and seq % (2 * RWKV_CHUNK) == 0
    prm = dict(ln_in_g=ln_in_g.astype(F32), ln_in_b=ln_in_b.astype(F32), w_in=w_in, rel_bias=rel_bias,
               attn_sinks=attn_sinks, ssm_a_re=ssm_a_re, ssm_a_im=ssm_a_im, ssm_log_dt=ssm_log_dt,
               ssm_b_re=ssm_b_re, ssm_b_im=ssm_b_im, ssm_c_re=ssm_c_re, ssm_c_im=ssm_c_im, ssm_d=ssm_d,
               ssm_w_glu=ssm_w_glu, rwkv_mu=rwkv_mu, rwkv_vec=rwkv_vec, rwkv_w2=rwkv_w2, rwkv_a2=rwkv_a2,
               rwkv_g2=rwkv_g2, w_branch=w_branch, w_out=w_out, ln_g=ln_g, ln_b=ln_b, ffn_w_in=ffn_w_in,
               ffn_w_down=ffn_w_down, moe_router=moe_router, moe_w_in=moe_w_in, moe_w_down=moe_w_down)
    hb = x_prompt.reshape(bsz * seq, D_MODEL).astype(F32)
    meta = jnp.broadcast_to(meta_tokens.astype(F32)[None], (bsz, N_META, D_MODEL)).reshape(bsz * N_META, D_MODEL)
    hs = jnp.concatenate([meta, x_sample.reshape(dec, D_MODEL).astype(F32)], axis=0)
    cache = (cache_meta_k, cache_meta_v, cache_win_k, cache_win_v)
    states = (state_ssm_re, state_ssm_im, state_wkv, state_shift)
    attn_outs, ssm_outs, rwkv_outs = [], [], []
    for l in range(DEPTH):
        hb, hs, a_o, s_o, r_o = _layer(l, hb, hs, l == 0, bsz, seq, dec, cache, states, prm)
        attn_outs.append(a_o)
        ssm_outs.append(s_o)
        rwkv_outs.append(r_o)
    stack = lambda outs, i: jnp.stack([o[i] for o in outs])
    y_prompt = hb.reshape(bsz, seq, D_MODEL)
    y_sample = hs[bsz * N_META:].reshape(dec, 1, D_MODEL)
    return (y_prompt, y_sample,
            stack(attn_outs, 0), stack(attn_outs, 1), stack(attn_outs, 2), stack(attn_outs, 3),
            stack(attn_outs, 4), stack(attn_outs, 5),
            stack(ssm_outs, 0), stack(ssm_outs, 1), stack(ssm_outs, 2), stack(ssm_outs, 3),
            stack(rwkv_outs, 0), stack(rwkv_outs, 1), stack(rwkv_outs, 2), stack(rwkv_outs, 3))
```

```python
import functools
import math

import numpy as np
import jax
import jax.numpy as jnp
from jax import lax
from jax.experimental import pallas as pl
from jax.experimental.pallas import tpu as pltpu

F32 = jnp.float32
BF16 = jnp.bfloat16

D_MODEL = 1024
DEPTH = 2
PAST_LEN = 16384
N_META = 16
WINDOW = 128
N_HEADS = 8
N_KV_HEADS = 2
HEAD_DIM = 64
Q_PER_KV = N_HEADS // N_KV_HEADS
ATTN_W = N_HEADS * HEAD_DIM
KV_W = N_KV_HEADS * HEAD_DIM
ATTN_SCALE = HEAD_DIM ** -0.5
REL_BUCKETS = 32
REL_EXACT = REL_BUCKETS // 2
REL_MAX_DIST = 128
SSM_GROUP = 16
SSM_GROUPS = 16
SSM_WIDTH = SSM_GROUP * SSM_GROUPS
SSM_STATE = 64
SSM_N = SSM_GROUPS * SSM_STATE
RWKV_HEAD = 64
RWKV_HEADS = 4
RWKV_WIDTH = RWKV_HEAD * RWKV_HEADS
RWKV_W_LORA = 32
RWKV_A_LORA = 32
RWKV_G_LORA = 64
RWKV_LORA = RWKV_W_LORA + RWKV_A_LORA + RWKV_G_LORA
RWKV_COLS = 3 * RWKV_WIDTH + RWKV_LORA
RV_W0, RV_A0, RV_KK, RV_KA, RV_RK, RV_GNW, RV_GNB = 0, 1, 2, 3, 4, 5, 6
N_BRANCH = 3
MIX_COLS = ATTN_W + 2 * KV_W + SSM_WIDTH + RWKV_COLS
N_EXPERTS = 8
TOP_K = 2
LN_EPS = 1e-5
RWKV_GN_EPS = 64e-5
NEG_INF = -1e30
DEEPNORM_ALPHA = (2 * DEPTH) ** 0.25

LANES = 128
SUBLANES = 8
VMEM_LIMIT = 48 * 1024 * 1024
RWKV_CHUNK = 64


def _params(sem):
    return pltpu.CompilerParams(dimension_semantics=sem, vmem_limit_bytes=VMEM_LIMIT)


def _dot(a, b):
    return jnp.dot(a.astype(BF16), b.astype(BF16), preferred_element_type=F32)


def _dot_nt(a, b):
    return lax.dot_general(a.astype(BF16), b.astype(BF16), (((1,), (1,)), ((), ())),
                           preferred_element_type=F32)


def _split3(x):
    h1 = x.astype(BF16)
    r1 = x - h1.astype(F32)
    h2 = r1.astype(BF16)
    h3 = (r1 - h2.astype(F32)).astype(BF16)
    return h1, h2, h3


def _dot_exact_rhs(x, m):
    h1, h2, h3 = _split3(x)
    dot = functools.partial(jnp.dot, preferred_element_type=F32)
    return dot(h1, m) + dot(h2, m) + dot(h3, m)


def _dot_exact_lhs(m, x):
    h1, h2, h3 = _split3(x)
    dot = functools.partial(jnp.dot, preferred_element_type=F32)
    return dot(m, h1) + dot(m, h2) + dot(m, h3)


def _layer_norm(x, g, b):
    mu = jnp.mean(x, axis=-1, keepdims=True)
    xc = x - mu
    var = jnp.mean(xc * xc, axis=-1, keepdims=True)
    return xc * lax.rsqrt(var + LN_EPS) * g + b


def _sigmoid(x):
    return 1.0 / (1.0 + jnp.exp(-x))


def _rows_to_tiles(x):
    slabs = [x[:, t * LANES:(t + 1) * LANES] for t in range(SUBLANES)]
    return jnp.swapaxes(jnp.stack(slabs, axis=0), 0, 1)


def _tiles_to_cols(x):
    xt = jnp.swapaxes(x, 0, 1)
    return [xt[t] for t in range(SUBLANES)]


def _rows_call(body, row_inputs, const_inputs, out_widths, tm, out_dtypes=None, name=None, extra_inputs=()):
    n = row_inputs[0].shape[0]
    tm = min(tm, n)
    assert n % tm == 0, (n, tm)
    out_dtypes = out_dtypes or [F32] * len(out_widths)
    in_specs = [pl.BlockSpec((tm, a.shape[1]), lambda i: (i, 0)) for a in row_inputs]
    in_specs += [pl.BlockSpec(shape, fn) for _, shape, fn in extra_inputs]
    row_inputs = list(row_inputs) + [a for a, _, _ in extra_inputs]
    in_specs += [pl.BlockSpec(c.shape, lambda i, nd=c.ndim: (0,) * nd) for c in const_inputs]
    tails = [w if isinstance(w, tuple) else (w,) for w in out_widths]
    out_specs = [pl.BlockSpec((tm,) + w, lambda i, nd=len(w): (i,) + (0,) * nd) for w in tails]
    out_shape = [jax.ShapeDtypeStruct((n,) + w, dt) for w, dt in zip(tails, out_dtypes)]
    return pl.pallas_call(
        body, grid=(n // tm,), in_specs=in_specs, out_specs=out_specs, out_shape=out_shape,
        compiler_params=_params(("parallel",)), name=name,
    )(*row_inputs, *const_inputs)


PROJ_WIDTHS = (ATTN_W, KV_W, KV_W, SSM_WIDTH, RWKV_COLS)


def _proj_kernel(x_ref, g_ref, b_ref, w_ref, *out_refs, pre_ln):
    x = x_ref[...]
    if pre_ln:
        x = _layer_norm(x, g_ref[...], b_ref[...])
        out_refs[0][...] = x
        out_refs = out_refs[1:]
    xb = x.astype(BF16)
    col = 0
    for o_ref in out_refs:
        n = o_ref.shape[-1]
        o_ref[...] = jnp.dot(xb, w_ref[:, col:col + n], preferred_element_type=F32)
        col += n


def _proj(x, ln_g, ln_b, w_mix, pre_ln, tm):
    widths = ((D_MODEL,) if pre_ln else ()) + PROJ_WIDTHS
    outs = _rows_call(functools.partial(_proj_kernel, pre_ln=pre_ln), [x], [ln_g, ln_b, w_mix],
                      widths, tm, name="proj")
    if pre_ln:
        return outs[0], outs[1:]
    return x, outs


def _attn_kernel(tab_ref, q_ref, k_ref, v_ref, bias_ref, sink_ref, o_ref):
    del tab_ref
    for bb in range(q_ref.shape[0]):
        for h in range(N_KV_HEADS):
            q = q_ref[bb, h].astype(BF16)
            k = k_ref[bb, h].astype(BF16)
            v = v_ref[bb, h].astype(BF16)
            s = lax.dot_general(q, k, (((1,), (1,)), ((), ())), preferred_element_type=F32)
            s = s * ATTN_SCALE + bias_ref[0, h]
            sk = sink_ref[h]
            m = jnp.maximum(jnp.max(s, axis=-1, keepdims=True), sk)
            p = jnp.exp(s - m)
            den = jnp.sum(p, axis=-1, keepdims=True) + jnp.exp(sk - m)
            o = jnp.dot(p.astype(BF16), v, preferred_element_type=F32)
            o_ref[bb, h] = o / den


def _attention(q, k, v, bias, sinks, tab_idx, bblk):
    p, _, mq, _ = q.shape
    nk = k.shape[2]
    assert p % bblk == 0
    grid_spec = pltpu.PrefetchScalarGridSpec(
        num_scalar_prefetch=1, grid=(p // bblk,),
        in_specs=[
            pl.BlockSpec((bblk, N_KV_HEADS, mq, HEAD_DIM), lambda i, t: (i, 0, 0, 0)),
            pl.BlockSpec((bblk, N_KV_HEADS, nk, HEAD_DIM), lambda i, t: (i, 0, 0, 0)),
            pl.BlockSpec((bblk, N_KV_HEADS, nk, HEAD_DIM), lambda i, t: (i, 0, 0, 0)),
            pl.BlockSpec((1, N_KV_HEADS, mq, nk), lambda i, t: (t[i], 0, 0, 0)),
            pl.BlockSpec((N_KV_HEADS, mq, 1), lambda i, t: (0, 0, 0)),
        ],
        out_specs=pl.BlockSpec((bblk, N_KV_HEADS, mq, HEAD_DIM), lambda i, t: (i, 0, 0, 0)),
    )
    return pl.pallas_call(
        _attn_kernel, grid_spec=grid_spec,
        out_shape=jax.ShapeDtypeStruct(q.shape, F32),
        compiler_params=_params(("arbitrary",)), name="attention",
    )(tab_idx, q, k, v, bias, sinks)


def _t5_bucket(dist):
    n = np.maximum(dist, 0)
    scaled = (np.log(np.maximum(n, 1).astype(np.float32) / np.float32(REL_EXACT))
              / np.float32(math.log(REL_MAX_DIST / REL_EXACT)) * np.float32(REL_BUCKETS - REL_EXACT))
    frac = np.abs(scaled - np.round(scaled))
    assert np.all((n <= REL_EXACT) | (n >= REL_MAX_DIST) | (frac > 1e-3))
    large = np.minimum(REL_EXACT + scaled.astype(np.int32), REL_BUCKETS - 1)
    return np.where(n < REL_EXACT, n, large)


def _bias_table(rel_bias, dist, valid, mq_pad=None, nk_pad=None):
    tq, nk = dist.shape
    onehot = np.eye(REL_BUCKETS, dtype=np.float32)[_t5_bucket(dist).reshape(-1)]
    bias = jnp.dot(jnp.asarray(onehot), rel_bias.astype(F32), precision=lax.Precision.HIGHEST)
    bias = bias.reshape(tq, nk, N_HEADS)
    bias = jnp.where(jnp.asarray(valid)[..., None], bias, NEG_INF)
    bias = jnp.moveaxis(bias, -1, 0).reshape(N_KV_HEADS, Q_PER_KV * tq, nk)
    mq_pad = mq_pad or Q_PER_KV * tq
    nk_pad = nk_pad or nk
    bias = jnp.pad(bias, ((0, 0), (0, mq_pad - Q_PER_KV * tq), (0, 0)))
    return jnp.pad(bias, ((0, 0), (0, 0), (0, nk_pad - nk)), constant_values=NEG_INF)


def _sink_rows(sinks, tq, mq_pad=None):
    s = jnp.repeat(sinks.astype(F32).reshape(N_KV_HEADS, Q_PER_KV, 1), tq, axis=2)
    s = s.reshape(N_KV_HEADS, Q_PER_KV * tq, 1)
    mq_pad = mq_pad or Q_PER_KV * tq
    return jnp.pad(s, ((0, 0), (0, mq_pad - Q_PER_KV * tq), (0, 0)))


def _heads_q(q, nb, tq):
    q = q.reshape(nb, tq, N_KV_HEADS, Q_PER_KV, HEAD_DIM)
    return jnp.transpose(q, (0, 2, 3, 1, 4)).reshape(nb, N_KV_HEADS, Q_PER_KV * tq, HEAD_DIM)


def _unheads_o(o, nb, tq):
    o = o[:, :, :Q_PER_KV * tq].reshape(nb, N_KV_HEADS, Q_PER_KV, tq, HEAD_DIM)
    return jnp.transpose(o, (0, 3, 1, 2, 4)).reshape(nb * tq, ATTN_W)


BODY_KEYS = N_META + 2 * WINDOW + 16


def _body_attn_kernel(q_ref, ko_ref, kp_ref, km_ref, vo_ref, vp_ref, vm_ref, bias_ref, o_ref):
    kv_heads = range(N_KV_HEADS)
    hs = [slice(h * HEAD_DIM, (h + 1) * HEAD_DIM) for h in kv_heads]
    heads = [[h * Q_PER_KV + g for g in range(Q_PER_KV)] for h in kv_heads]
    pad = jnp.zeros((BODY_KEYS - N_META - 2 * WINDOW, HEAD_DIM), F32)
    ones = jnp.ones((BODY_KEYS, HEAD_DIM), BF16)
    k = [jnp.concatenate([km_ref[:, hs[h]], kp_ref[:, hs[h]], ko_ref[:, hs[h]], pad], axis=0).astype(BF16)
         for h in kv_heads]
    v = [jnp.concatenate([vm_ref[:, hs[h]], vp_ref[:, hs[h]], vo_ref[:, hs[h]], pad], axis=0).astype(BF16)
         for h in kv_heads]
    q = [jnp.concatenate([q_ref[:, qh * HEAD_DIM:(qh + 1) * HEAD_DIM] for qh in heads[h]], axis=0).astype(BF16)
         for h in kv_heads]
    s = [lax.dot_general(q[h], k[h], (((1,), (1,)), ((), ())), preferred_element_type=F32) for h in kv_heads]
    s = [s[h] * ATTN_SCALE + bias_ref[0, h] for h in kv_heads]
    p = [jnp.exp(s[h] - jnp.max(s[h], axis=-1, keepdims=True)).astype(BF16) for h in kv_heads]
    o = [jnp.dot(p[h], v[h], preferred_element_type=F32) / jnp.dot(p[h], ones, preferred_element_type=F32)
         for h in kv_heads]
    for h in kv_heads:
        for g, qh in enumerate(heads[h]):
            o_ref[:, qh * HEAD_DIM:(qh + 1) * HEAD_DIM] = o[h][g * WINDOW:(g + 1) * WINDOW]


def _body_attention(q, k, v, k_meta, v_meta, rel_bias, sinks, bsz, seq):
    nblk = seq // WINDOW
    i = np.arange(WINDOW)[:, None]
    c = np.arange(WINDOW)[None, :]
    sink_col = _sink_rows(sinks, WINDOW)
    tabs = []
    for m in (0, 1):
        q_pos = N_META + WINDOW * m + i
        meta_pos = np.arange(N_META)[None, :]
        dist = np.concatenate([q_pos - meta_pos, WINDOW + i - c, i - c], axis=1)
        valid = np.concatenate([np.ones((WINDOW, N_META), bool),
                                (c >= i) & (m > 0), c <= i], axis=1)
        tab = _bias_table(rel_bias, dist, valid, nk_pad=BODY_KEYS)
        tabs.append(tab.at[:, :, N_META + 2 * WINDOW].set(sink_col[:, :, 0]))
    bias = jnp.stack(tabs)
    nk = BODY_KEYS
    mq = Q_PER_KV * WINDOW
    own = lambda w: pl.BlockSpec((WINDOW, w), lambda b, m: (b * nblk + m, 0))
    prev = lambda w: pl.BlockSpec((WINDOW, w), lambda b, m: (b * nblk + jnp.maximum(m - 1, 0), 0))
    meta = pl.BlockSpec((N_META, KV_W), lambda b, m: (b, 0))
    return pl.pallas_call(
        _body_attn_kernel, grid=(bsz, nblk),
        in_specs=[own(ATTN_W), own(KV_W), prev(KV_W), meta, own(KV_W), prev(KV_W), meta,
                  pl.BlockSpec((1, N_KV_HEADS, mq, nk), lambda b, m: (jnp.minimum(m, 1), 0, 0, 0))],
        out_specs=own(ATTN_W), out_shape=jax.ShapeDtypeStruct(q.shape, F32),
        compiler_params=_params(("parallel", "arbitrary")), name="body_attention",
    )(q, k, k, k_meta, v, v, v_meta, bias)


def _meta_attention(q, k, v, rel_bias, sinks, bsz):
    i = np.arange(N_META)
    dist = i[:, None] - i[None, :]
    bias = _bias_table(rel_bias, dist, dist >= 0)[None]
    kv = lambda t: jnp.transpose(t.reshape(bsz, N_META, N_KV_HEADS, HEAD_DIM), (0, 2, 1, 3))
    o = _attention(_heads_q(q, bsz, N_META), kv(k), kv(v), bias, _sink_rows(sinks, N_META),
                   jnp.zeros((bsz,), jnp.int32), bsz)
    return _unheads_o(o, bsz, N_META)


def _sample_attention(q, k_all, v_all, meta_k, meta_v, rel_bias, sinks):
    bsz = q.shape[0]
    wc = k_all.shape[1] - 1
    nk = N_META + wc + 1
    nk_pad = -(-nk // LANES) * LANES
    mq_pad = 8
    k_pos = np.concatenate([np.arange(N_META), PAST_LEN - wc + np.arange(wc + 1)])
    dist = (PAST_LEN - k_pos)[None, :]
    is_meta = (np.arange(nk) < N_META)[None, :]
    valid = (dist >= 0) & (is_meta | ((k_pos[None, :] >= N_META) & (dist <= WINDOW)))
    bias = _bias_table(rel_bias, dist, valid, mq_pad, nk_pad)[None]

    def kv(meta, t):
        full = jnp.concatenate([meta.astype(F32), t], axis=1)
        full = jnp.pad(full, ((0, 0), (0, nk_pad - nk), (0, 0), (0, 0)))
        return jnp.transpose(full, (0, 2, 1, 3))

    qh = jnp.pad(_heads_q(q, bsz, 1), ((0, 0), (0, 0), (0, mq_pad - Q_PER_KV), (0, 0)))
    o = _attention(qh, kv(meta_k, k_all), kv(meta_v, v_all), bias, _sink_rows(sinks, 1, mq_pad),
                   jnp.zeros((bsz // 8,), jnp.int32), 8)
    return _unheads_o(o, bsz, 1)


def _s5_bu_kernel(u_ref, w_ref, re_ref, im_ref):
    r = jnp.dot(u_ref[...].astype(BF16), w_ref[...], preferred_element_type=F32)
    re_ref[...] = r[:, :SSM_N]
    im_ref[...] = r[:, SSM_N:]


def _s5_scan_kernel(bur_ref, bui_ref, h0r_ref, h0i_ref, ar_ref, ai_ref, hr_ref, hi_ref, cr_ref, ci_ref):
    @pl.when(pl.program_id(1) == 0)
    def _():
        cr_ref[...] = h0r_ref[0]
        ci_ref[...] = h0i_ref[0]

    ar = ar_ref[...]
    ai = ai_ref[...]

    def step(t, carry):
        hr, hi = carry
        nr = ar * hr - ai * hi + bur_ref[pl.ds(t, 1), :]
        ni = ar * hi + ai * hr + bui_ref[pl.ds(t, 1), :]
        hr_ref[pl.ds(t, 1), :] = nr
        hi_ref[pl.ds(t, 1), :] = ni
        return nr, ni

    hr, hi = lax.fori_loop(0, bur_ref.shape[0], step, (cr_ref[...], ci_ref[...]), unroll=8)
    cr_ref[...] = hr
    ci_ref[...] = hi


def _s5_scan(bur, bui, h0r, h0i, ar, ai, bsz, seq, tt):
    tt = min(tt, seq)
    nt = seq // tt
    row = pl.BlockSpec((tt, SSM_N), lambda b, t: (b * nt + t, 0))
    st = pl.BlockSpec((1, 1, SSM_N), lambda b, t: (b, 0, 0))
    cst = pl.BlockSpec((1, SSM_N), lambda b, t: (0, 0))
    return pl.pallas_call(
        _s5_scan_kernel, grid=(bsz, nt), in_specs=[row, row, st, st, cst, cst], out_specs=[row, row],
        out_shape=[jax.ShapeDtypeStruct(bur.shape, F32)] * 2,
        scratch_shapes=[pltpu.VMEM((1, SSM_N), F32)] * 2,
        compiler_params=_params(("arbitrary", "arbitrary")), name="s5_scan",
    )(bur, bui, h0r, h0i, ar, ai)


def _s5_step_kernel(bur_ref, bui_ref, h0r_ref, h0i_ref, ar_ref, ai_ref, hr_ref, hi_ref):
    ar, ai, hr, hi = ar_ref[...], ai_ref[...], h0r_ref[...], h0i_ref[...]
    hr_ref[...] = ar * hr - ai * hi + bur_ref[...]
    hi_ref[...] = ar * hi + ai * hr + bui_ref[...]


def _s5_out_kernel(u_ref, hr_ref, hi_ref, wc_ref, d_ref, wg_ref, o_ref):
    y = (_dot(hr_ref[...], wc_ref[:SSM_N]) + _dot(hi_ref[...], wc_ref[SSM_N:])
         + d_ref[...] * u_ref[...])
    z = jax.nn.gelu(y)
    o_ref[...] = z * _sigmoid(_dot(z, wg_ref[...]))


def _block_diag(blocks):
    g, a, b = blocks.shape
    eye = jnp.eye(g, dtype=blocks.dtype)
    return (eye[:, None, :, None] * blocks[:, :, None, :]).reshape(g * a, g * b)


def _s5_weights(a_re, a_im, log_dt, b_re, b_im, c_re, c_im):
    a_re = a_re.astype(F32)
    a_im = a_im.astype(F32)
    dt = jnp.exp(log_dt.astype(F32))[:, None]
    mag = jnp.exp(a_re * dt)
    ab_re = mag * jnp.cos(a_im * dt)
    ab_im = mag * jnp.sin(a_im * dt)
    den = a_re * a_re + a_im * a_im
    nr = ab_re - 1.0
    cf_re = (nr * a_re + ab_im * a_im) / den
    cf_im = (ab_im * a_re - nr * a_im) / den
    b_re = b_re.astype(F32)
    b_im = b_im.astype(F32)
    bb_re = cf_re[..., None] * b_re - cf_im[..., None] * b_im
    bb_im = cf_re[..., None] * b_im + cf_im[..., None] * b_re
    w_b = jnp.concatenate([_block_diag(jnp.swapaxes(bb_re, 1, 2)),
                           _block_diag(jnp.swapaxes(bb_im, 1, 2))], axis=1)
    w_c = jnp.concatenate([_block_diag(jnp.swapaxes(c_re.astype(F32), 1, 2)),
                           -_block_diag(jnp.swapaxes(c_im.astype(F32), 1, 2))], axis=0)
    return ab_re.reshape(1, SSM_N), ab_im.reshape(1, SSM_N), w_b.astype(BF16), w_c.astype(BF16)


def _seg_ones():
    r = lax.broadcasted_iota(jnp.int32, (RWKV_WIDTH, RWKV_WIDTH), 0) // RWKV_HEAD
    c = lax.broadcasted_iota(jnp.int32, (RWKV_WIDTH, RWKV_WIDTH), 1) // RWKV_HEAD
    return (r == c).astype(BF16)


def _rwkv_prep_kernel(pc_ref, prev_ref, *rest):
    _rwkv_prep(pc_ref[...], prev_ref[...], *rest)


def _rwkv_prep_shift_kernel(pc_ref, tail_ref, first_ref, *rest, tiles_per_seq):
    pc = pc_ref[...]
    seq_start = pl.program_id(0) % tiles_per_seq == 0
    above = jnp.where(seq_start, first_ref[0], tail_ref[SUBLANES - 1:SUBLANES, :])
    row = lax.broadcasted_iota(jnp.int32, pc.shape, 0)
    _rwkv_prep(pc, jnp.where(row == 0, above, pltpu.roll(pc, 1, axis=0)), *rest)


def _rwkv_prep(pc, prev, mu_ref, vec_ref, w2_ref, a2_ref, g2_ref,
               r_ref, k_ref, v_ref, lw_ref, kk_ref, bb_ref, g_ref):
    xm = pc + (prev - pc) * mu_ref[...]
    rw = RWKV_WIDTH
    xr, xk, xv, xl = xm[:, :rw], xm[:, rw:2 * rw], xm[:, 2 * rw:3 * rw], xm[:, 3 * rw:]
    vec = vec_ref[...]
    wpre = -(vec[RV_W0:RV_W0 + 1] + _dot(jnp.tanh(xl), w2_ref[...]))
    softplus = jnp.maximum(wpre, 0.0) + jnp.log(1.0 + jnp.exp(-jnp.abs(wpre)))
    lw_ref[...] = -jnp.exp(-softplus - 0.5)
    a = _sigmoid(vec[RV_A0:RV_A0 + 1] + _dot(xl, a2_ref[...]))
    g_ref[...] = _dot(_sigmoid(xl), g2_ref[...])
    kk = xk * vec[RV_KK:RV_KK + 1]
    norm = jnp.sqrt(_dot_exact_rhs(kk * kk, _seg_ones()))
    kk = kk / jnp.maximum(norm, 1e-12)
    r_ref[...] = xr
    k_ref[...] = xk * (1.0 + (a - 1.0) * vec[RV_KA:RV_KA + 1])
    v_ref[...] = xv
    kk_ref[...] = kk
    bb_ref[...] = kk * a


def _rwkv_post_kernel(r_ref, k_ref, v_ref, g_ref, y_ref, vec_ref, o_ref):
    ones = _seg_ones()
    vec = vec_ref[...]
    y = y_ref[...]
    yc = y - _dot_exact_rhs(y, ones) * (1.0 / RWKV_HEAD)
    yv = _dot_exact_rhs(yc * yc, ones) * (1.0 / RWKV_HEAD)
    yn = yc * lax.rsqrt(yv + RWKV_GN_EPS) * vec[RV_GNW:RV_GNW + 1] + vec[RV_GNB:RV_GNB + 1]
    bonus = _dot_exact_rhs(r_ref[...] * k_ref[...] * vec[RV_RK:RV_RK + 1], ones) * v_ref[...]
    o_ref[...] = (yn + bonus) * g_ref[...]


def _rwkv_chunk_kernel(*refs, chunk, nb):
    r_refs, k_refs, v_refs, lw_refs, kk_refs, bb_refs = (refs[i * nb:(i + 1) * nb] for i in range(6))
    s0_ref, y_ref, so_ref, s_ref = refs[6 * nb:]
    t = pl.program_id(0)

    @pl.when(t == 0)
    def _():
        s_ref[...] = s0_ref[...]

    c = chunk
    row = lax.broadcasted_iota(jnp.int32, (c, c), 0)
    col = lax.broadcasted_iota(jnp.int32, (c, c), 1)
    incl = (row >= col).astype(F32)
    strict = (row > col).astype(F32)
    eye = (row == col).astype(F32)
    nsub = r_refs[0].shape[0] // c
    heads = range(RWKV_HEADS)
    hsl = [slice(h * RWKV_HEAD, (h + 1) * RWKV_HEAD) for h in heads]
    tsl = [slice(sub * c, (sub + 1) * c) for sub in range(nsub)]
    seqs = [(b, h) for b in range(nb) for h in heads]
    units = [(b, sub, h) for b in range(nb) for sub in range(nsub) for h in heads]

    scaled = {}
    for b in range(nb):
        for sub in range(nsub):
            lw = lw_refs[b][tsl[sub], :]
            cum = _dot_exact_lhs(incl.astype(BF16), lw)
            ecum = jnp.exp(cum)
            einv = jnp.exp(-cum)
            scaled[b, sub] = (kk_refs[b][tsl[sub], :] * jnp.exp(cum - lw), bb_refs[b][tsl[sub], :] * einv,
                              k_refs[b][tsl[sub], :] * einv, r_refs[b][tsl[sub], :] * ecum, ecum[c - 1:c, :])
    kt = {u: scaled[u[0], u[1]][0][:, hsl[u[2]]] for u in units}
    bt = {u: scaled[u[0], u[1]][1][:, hsl[u[2]]] for u in units}
    kkt = {u: scaled[u[0], u[1]][2][:, hsl[u[2]]] for u in units}
    rt = {u: scaled[u[0], u[1]][3][:, hsl[u[2]]] for u in units}
    v_t = [v_refs[b][...].T for b in range(nb)]
    vt = {u: v_t[u[0]][hsl[u[2]], tsl[u[1]]] for u in units}
    a_b = {u: strict * _dot_nt(kt[u], bt[u]) for u in units}
    a_k = {u: strict * _dot_nt(kt[u], kkt[u]) for u in units}
    r_b = {u: incl * _dot_nt(rt[u], bt[u]) for u in units}
    r_k = {u: incl * _dot_nt(rt[u], kkt[u]) for u in units}
    pw = {u: -a_b[u] for u in units}
    tinv = {u: eye + pw[u] for u in units}
    n = 1
    while 2 * n < c:
        pw = {u: _dot(pw[u], pw[u]) for u in units}
        tinv = {u: tinv[u] + _dot(tinv[u], pw[u]) for u in units}
        n *= 2
    x = {u: _dot_nt(vt[u], a_k[u]) for u in units}
    w1t = {u: _dot_nt(x[u], tinv[u]) for u in units}
    w2 = {u: _dot(tinv[u], kt[u]) for u in units}
    yt_local = {u: _dot_nt(vt[u], r_k[u]) for u in units}
    s_local = {u: _dot(vt[u], kkt[u]) for u in units}

    s = {q: s_ref[q[0], q[1]] for q in seqs}
    yt = {}
    for sub in range(nsub):
        ut = {(b, h): -(_dot_nt(s[b, h], w2[b, sub, h]) + w1t[b, sub, h]) for b, h in seqs}
        for b, h in seqs:
            yt[b, sub, h] = (_dot_nt(s[b, h], rt[b, sub, h]) + _dot_nt(ut[b, h], r_b[b, sub, h])
                             + yt_local[b, sub, h])
        s = {(b, h): (s[b, h] + _dot(ut[b, h], bt[b, sub, h]) + s_local[b, sub, h]) * scaled[b, sub][4][:, hsl[h]]
             for b, h in seqs}
    for b, h in seqs:
        s_ref[b, h] = s[b, h]
    for b in range(nb):
        y_t = jnp.concatenate([jnp.concatenate([yt[b, sub, h] for sub in range(nsub)], axis=1) for h in heads],
                              axis=0)
        y_ref[:, b * RWKV_WIDTH:(b + 1) * RWKV_WIDTH] = y_t.T

    @pl.when(t == pl.num_programs(0) - 1)
    def _():
        so_ref[...] = s_ref[...]


def _rwkv_chunk(r, k, v, lw, kk, bb, s0, bsz, seq):
    tb = 2 * RWKV_CHUNK
    assert seq % tb == 0
    nt = seq // tb
    tiles = [pl.BlockSpec((tb, RWKV_WIDTH), lambda t, b=b: (b * nt + t, 0)) for b in range(bsz)]
    st = pl.BlockSpec(s0.shape, lambda t: (0, 0, 0, 0))
    return pl.pallas_call(
        functools.partial(_rwkv_chunk_kernel, chunk=RWKV_CHUNK, nb=bsz), grid=(nt,),
        in_specs=tiles * 6 + [st], out_specs=[pl.BlockSpec((tb, bsz * RWKV_WIDTH), lambda t: (t, 0)), st],
        out_shape=[jax.ShapeDtypeStruct((seq, bsz * RWKV_WIDTH), F32), jax.ShapeDtypeStruct(s0.shape, F32)],
        scratch_shapes=[pltpu.VMEM(s0.shape, F32)],
        compiler_params=_params(("arbitrary",)), name="rwkv_chunk",
    )(*([r] * bsz + [k] * bsz + [v] * bsz + [lw] * bsz + [kk] * bsz + [bb] * bsz), s0)


def _rwkv_step_kernel(s_ref, r_ref, k_ref, lw_ref, kk_ref, bb_ref, v_ref, so_ref, y_ref):
    s = s_ref[...]
    sa = jnp.sum(s * (-kk_ref[...]), axis=-1, keepdims=True)
    s = s * jnp.exp(lw_ref[...]) + sa * bb_ref[...] + v_ref[...] * k_ref[...]
    so_ref[...] = s
    y_ref[...] = jnp.sum(s * r_ref[...], axis=-1, keepdims=True)


def _rwkv_step(r, k, v, lw, kk, bb, s0):
    bsz = r.shape[0]
    p = bsz * RWKV_HEADS
    nb = min(64, p)
    rowv = lambda a: a.reshape(p, 1, RWKV_HEAD)
    rs = pl.BlockSpec((nb, 1, RWKV_HEAD), lambda i: (i, 0, 0))
    cs = pl.BlockSpec((nb, RWKV_HEAD, 1), lambda i: (i, 0, 0))
    ss = pl.BlockSpec((nb, RWKV_HEAD, RWKV_HEAD), lambda i: (i, 0, 0))
    s_out, y = pl.pallas_call(
        _rwkv_step_kernel, grid=(p // nb,), in_specs=[ss, rs, rs, rs, rs, rs, cs], out_specs=[ss, cs],
        out_shape=[jax.ShapeDtypeStruct((p, RWKV_HEAD, RWKV_HEAD), F32),
                   jax.ShapeDtypeStruct((p, RWKV_HEAD, 1), F32)],
        compiler_params=_params(("parallel",)), name="rwkv_step",
    )(s0.reshape(p, RWKV_HEAD, RWKV_HEAD), rowv(r), rowv(k), rowv(lw), rowv(kk), rowv(bb),
      v.reshape(p, RWKV_HEAD, 1))
    return y.reshape(bsz, RWKV_WIDTH), s_out.reshape(s0.shape)


def _merge_kernel(h_ref, oa_ref, ob_ref, oc_ref, wg_ref, wb_ref, wo_ref, g_ref, b_ref, o_ref, *tile_refs):
    h = h_ref[...]
    gates = _sigmoid(jnp.dot(h.astype(BF16), wg_ref[...], preferred_element_type=F32))
    d = D_MODEL
    merged = (gates[:, :d] * _dot(oa_ref[...], wb_ref[:ATTN_W])
              + gates[:, d:2 * d] * _dot(ob_ref[...], wb_ref[ATTN_W:ATTN_W + SSM_WIDTH])
              + gates[:, 2 * d:] * _dot(oc_ref[...], wb_ref[ATTN_W + SSM_WIDTH:]))
    mix = _dot(merged, wo_ref[...])
    out = _layer_norm(DEEPNORM_ALPHA * h + mix, g_ref[...], b_ref[...])
    o_ref[...] = out
    for t_ref in tile_refs:
        t_ref[...] = _rows_to_tiles(out)


def _ffn_kernel(be_ref, x_ref, wg_ref, wu_ref, wd_ref, g_ref, b_ref, o_ref, xb_ref, acc_ref, *, post_ln):
    del be_ref
    j = pl.program_id(1)

    @pl.when(j == 0)
    def _():
        xb_ref[...] = x_ref[...].astype(BF16)
        acc_ref[...] = jnp.zeros_like(acc_ref)

    xb = xb_ref[...]
    gate = jnp.dot(xb, wg_ref[0], preferred_element_type=F32)
    up = jnp.dot(xb, wu_ref[0], preferred_element_type=F32)
    act = gate * _sigmoid(gate) * up
    acc_ref[...] += jnp.dot(act.astype(BF16), wd_ref[0], preferred_element_type=F32)

    @pl.when(j == pl.num_programs(1) - 1)
    def _():
        if post_ln:
            o_ref[...] = _layer_norm(DEEPNORM_ALPHA * x_ref[...] + acc_ref[...], g_ref[...], b_ref[...])
        else:
            o_ref[...] = acc_ref[...]


def _ffn(x, block_e, w_in, w_down, ln_g, ln_b, blk, tf, post_ln):
    rows = x.shape[0]
    assert rows % blk == 0
    f = w_down.shape[1]
    nf = f // tf
    grid_spec = pltpu.PrefetchScalarGridSpec(
        num_scalar_prefetch=1, grid=(rows // blk, nf),
        in_specs=[
            pl.BlockSpec((blk, D_MODEL), lambda i, j, be: (i, 0)),
            pl.BlockSpec((1, D_MODEL, tf), lambda i, j, be: (be[i], 0, j)),
            pl.BlockSpec((1, D_MODEL, tf), lambda i, j, be: (be[i], 0, nf + j)),
            pl.BlockSpec((1, tf, D_MODEL), lambda i, j, be: (be[i], j, 0)),
            pl.BlockSpec((1, D_MODEL), lambda i, j, be: (0, 0)),
            pl.BlockSpec((1, D_MODEL), lambda i, j, be: (0, 0)),
        ],
        out_specs=pl.BlockSpec((blk, D_MODEL), lambda i, j, be: (i, 0)),
        scratch_shapes=[pltpu.VMEM((blk, D_MODEL), BF16), pltpu.VMEM((blk, D_MODEL), F32)],
    )
    return pl.pallas_call(
        functools.partial(_ffn_kernel, post_ln=post_ln), grid_spec=grid_spec,
        out_shape=jax.ShapeDtypeStruct((rows, D_MODEL), F32),
        compiler_params=_params(("arbitrary", "arbitrary")), name="ffn",
    )(block_e, x, w_in, w_in, w_down, ln_g, ln_b)


def _router_kernel(h_ref, w_ref, e_ref, g_ref):
    logits = jnp.dot(h_ref[...], w_ref[...], preferred_element_type=F32, precision=lax.Precision.HIGHEST)
    lane = lax.broadcasted_iota(jnp.int32, logits.shape, 1)
    lg = jnp.where(lane < N_EXPERTS, logits, -jnp.inf)
    m1 = jnp.max(lg, axis=-1, keepdims=True)
    i1 = jnp.min(jnp.where(lg == m1, lane, LANES), axis=-1, keepdims=True)
    lg2 = jnp.where(lane == i1, -jnp.inf, lg)
    m2 = jnp.max(lg2, axis=-1, keepdims=True)
    i2 = jnp.min(jnp.where(lg2 == m2, lane, LANES), axis=-1, keepdims=True)
    e2 = jnp.exp(m2 - m1)
    den = 1.0 + e2
    e_ref[...] = jnp.where(lane == 0, i1, jnp.where(lane == 1, i2, 0))
    g_ref[...] = jnp.where(lane == 0, 1.0 / den, jnp.where(lane == 1, e2 / den, 0.0))


def _moe_ffn_kernel(be_ref, nv_ref, first_ref, nxt_ref, prev_dst_ref, last_dst_ref, h_hbm, wg_ref, wu_ref, wd_ref,
                    out_hbm, xbuf, xb_ref, acc_ref, stage, gsem, ssem):
    del be_ref
    i, j = pl.program_id(0), pl.program_id(1)
    nblk, nf = pl.num_programs(0), pl.num_programs(1)
    blk = xb_ref.shape[0]
    n_valid = nv_ref[0]
    valid = i < n_valid
    slot = lax.rem(i, 2)
    spare_base = out_hbm.shape[0] - blk

    def gather_row(idx_ref, r, s):
        pltpu.make_async_copy(h_hbm.at[idx_ref[0, 0, r]], xbuf.at[s, r], gsem.at[s]).start(priority=0)

    def scatter_row(r, d):
        pltpu.make_async_copy(stage.at[r], out_hbm.at[d], ssem).start(priority=1)

    def wait_gather(s):
        pltpu.make_async_copy(h_hbm.at[pl.ds(0, blk)], xbuf.at[s], gsem.at[s]).wait()

    def wait_scatter():
        pltpu.make_async_copy(stage, out_hbm.at[pl.ds(0, blk)], ssem).wait()

    def for_rows(fn):
        def body(r, c):
            fn(r)
            return c
        lax.fori_loop(0, blk, body, 0, unroll=8)

    def compute():
        xb = xb_ref[...]
        gate = jnp.dot(xb, wg_ref[0], preferred_element_type=F32)
        up = jnp.dot(xb, wu_ref[0], preferred_element_type=F32)
        act = gate * _sigmoid(gate) * up
        acc_ref[...] += jnp.dot(act.astype(BF16), wd_ref[0], preferred_element_type=F32)

    @pl.when((i == 0) & (j == 0))
    def _():
        for_rows(lambda r: gather_row(first_ref, r, 0))
        stage[...] = jnp.zeros_like(stage)

    @pl.when(valid & (j == 0))
    def _():
        wait_gather(slot)
        for t, cols in enumerate(_tiles_to_cols(xbuf[slot])):
            xb_ref[:, t * LANES:(t + 1) * LANES] = cols.astype(BF16)
        acc_ref[...] = jnp.zeros_like(acc_ref)

    @pl.when(valid & (j < nf - 1))
    def _():
        compute()
        per_step = blk // (nf - 1)
        for rr in range(per_step):
            r = j * per_step + rr
            gather_row(nxt_ref, r, 1 - slot)
            scatter_row(r, jnp.where(i > 0, prev_dst_ref[0, 0, r], spare_base + r))

    @pl.when(valid & (j == nf - 1))
    def _():
        compute()
        wait_scatter()
        stage[...] = _rows_to_tiles(acc_ref[...])

    @pl.when((i == nblk - 1) & (j == nf - 1))
    def _():
        for_rows(lambda r: scatter_row(r, last_dst_ref[0, 0, r]))
        wait_scatter()
        wait_gather(lax.rem(n_valid, 2))


def _moe_ffn(h_tiles, rows_tok, rows_dst, block_e, n_valid, w_in, w_down, blk, tf):
    n = h_tiles.shape[0]
    rows = rows_tok.shape[0]
    nblk = rows // blk
    nf = w_down.shape[1] // tf
    assert blk % (nf - 1) == 0 and n >= blk
    idx3 = lambda a: a.reshape(nblk, 1, blk)
    smem = lambda fn: pl.BlockSpec((1, 1, blk), fn, memory_space=pltpu.SMEM)
    ftile = lambda i, j, nv: jnp.where(i < nv[0], j, nf - 1)
    grid_spec = pltpu.PrefetchScalarGridSpec(
        num_scalar_prefetch=2, grid=(nblk, nf),
        in_specs=[
            smem(lambda i, j, be, nv: (0, 0, 0)),
            smem(lambda i, j, be, nv: (jnp.minimum(i + 1, nblk - 1), 0, 0)),
            smem(lambda i, j, be, nv: (jnp.maximum(i - 1, 0), 0, 0)),
            smem(lambda i, j, be, nv: (nv[0] - 1, 0, 0)),
            pl.BlockSpec(memory_space=pl.ANY),
            pl.BlockSpec((1, D_MODEL, tf), lambda i, j, be, nv: (be[i], 0, ftile(i, j, nv))),
            pl.BlockSpec((1, D_MODEL, tf), lambda i, j, be, nv: (be[i], 0, nf + ftile(i, j, nv))),
            pl.BlockSpec((1, tf, D_MODEL), lambda i, j, be, nv: (be[i], ftile(i, j, nv), 0)),
        ],
        out_specs=pl.BlockSpec(memory_space=pl.ANY),
        scratch_shapes=[pltpu.VMEM((2, blk, SUBLANES, LANES), F32), pltpu.VMEM((blk, D_MODEL), BF16),
                        pltpu.VMEM((blk, D_MODEL), F32), pltpu.VMEM((blk, SUBLANES, LANES), F32),
                        pltpu.SemaphoreType.DMA((2,)), pltpu.SemaphoreType.DMA(())],
    )
    return pl.pallas_call(
        _moe_ffn_kernel, grid_spec=grid_spec,
        out_shape=jax.ShapeDtypeStruct((TOP_K * n + blk, SUBLANES, LANES), F32),
        compiler_params=_params(("arbitrary", "arbitrary")), name="moe_ffn",
    )(block_e, n_valid, idx3(rows_tok), idx3(rows_tok), idx3(rows_dst), idx3(rows_dst), h_tiles,
      w_in, w_in, w_down)


def _combine_kernel(h_ref, y0_ref, y1_ref, gate_ref, g_ref, bias_ref, o_ref):
    gate = gate_ref[...]
    rows = lambda y_ref: jnp.concatenate(_tiles_to_cols(y_ref[...]), axis=-1)
    f = rows(y0_ref) * gate[:, 0:1] + rows(y1_ref) * gate[:, 1:2]
    o_ref[...] = _layer_norm(DEEPNORM_ALPHA * h_ref[...] + f, g_ref[...], bias_ref[...])


def _moe(h, h_tiles, router_pad, w_in, w_down, ln_g, ln_b, tm, blk, tf):
    n = h.shape[0]
    e_pad, gate = _rows_call(_router_kernel, [h], [router_pad], (LANES, LANES), tm,
                             out_dtypes=[jnp.int32, F32], name="router")
    flat_e = jnp.concatenate([e_pad[:, s] for s in range(TOP_K)])
    n_assign = n * TOP_K
    n_blocks = -(-(n_assign + N_EXPERTS * (blk - 1)) // blk)
    experts = jnp.arange(N_EXPERTS, dtype=jnp.int32)
    onehot = (flat_e[:, None] == experts[None, :]).astype(jnp.int32)
    csum = jnp.cumsum(onehot, axis=0)
    counts = csum[-1]
    padded = (counts + blk - 1) // blk * blk
    pad_end = jnp.cumsum(padded)
    dest = jnp.sum((csum - 1 + (pad_end - padded)[None, :]) * onehot, axis=1)
    assign = jnp.arange(n_assign, dtype=jnp.int32)
    rows_dst = jnp.full((n_blocks * blk,), -1, jnp.int32).at[dest].set(assign, unique_indices=True)
    rows_tok = jnp.maximum(rows_dst, 0) % n
    spare = n_assign + jnp.arange(n_blocks * blk, dtype=jnp.int32) % blk
    rows_dst = jnp.where(rows_dst < 0, spare, rows_dst)
    block_start = jnp.arange(n_blocks, dtype=jnp.int32) * blk
    block_e = jnp.minimum(jnp.sum((block_start[:, None] >= pad_end[None, :]).astype(jnp.int32), axis=1),
                          N_EXPERTS - 1)
    n_valid = (pad_end[-1:] // blk).astype(jnp.int32)
    y = _moe_ffn(h_tiles, rows_tok, rows_dst, block_e, n_valid, w_in, w_down, blk, tf)
    tm = min(tm, n)
    nt = n // tm
    cst = pl.BlockSpec((1, D_MODEL), lambda i: (0, 0))
    slot = lambda s: pl.BlockSpec((tm, SUBLANES, LANES), lambda i: (i + s * nt, 0, 0))
    return pl.pallas_call(
        _combine_kernel, grid=(nt,),
        in_specs=[pl.BlockSpec((tm, D_MODEL), lambda i: (i, 0)), slot(0), slot(1),
                  pl.BlockSpec((tm, LANES), lambda i: (i, 0)), cst, cst],
        out_specs=pl.BlockSpec((tm, D_MODEL), lambda i: (i, 0)),
        out_shape=jax.ShapeDtypeStruct((n, D_MODEL), F32),
        compiler_params=_params(("parallel",)), name="moe_combine",
    )(h, y, y, gate, ln_g, ln_b)


def _layer(l, hb, hs, pre_ln, bsz, seq, dec, cache, states, prm):
    (cache_meta_k, cache_meta_v, cache_win_k, cache_win_v) = cache
    (state_ssm_re, state_ssm_im, state_wkv, state_shift) = states
    n_meta = bsz * N_META
    tm_b, tm_s = 512, hs.shape[0]
    w_in = prm['w_in'][l]
    w_mix = w_in[:, :MIX_COLS].astype(BF16)
    w_gate = w_in[:, MIX_COLS:].astype(BF16)
    ln_in_g, ln_in_b = prm['ln_in_g'].reshape(1, -1), prm['ln_in_b'].reshape(1, -1)

    hb, (q_b, k_b, v_b, u_b, pc_b) = _proj(hb, ln_in_g, ln_in_b, w_mix, pre_ln, tm_b)
    hs, (q_s, k_s, v_s, u_s, pc_s) = _proj(hs, ln_in_g, ln_in_b, w_mix, pre_ln, tm_s)

    rel_bias, sinks = prm['rel_bias'], prm['attn_sinks'][l]
    k_meta, v_meta = k_s[:n_meta], v_s[:n_meta]
    oa_b = _body_attention(q_b, k_b, v_b, k_meta, v_meta, rel_bias, sinks, bsz, seq)
    oa_m = _meta_attention(q_s[:n_meta], k_meta, v_meta, rel_bias, sinks, bsz)
    kd = lambda t: t[n_meta:].reshape(dec, 1, N_KV_HEADS, HEAD_DIM)
    k_all = jnp.concatenate([cache_win_k[l].astype(F32), kd(k_s)], axis=1)
    v_all = jnp.concatenate([cache_win_v[l].astype(F32), kd(v_s)], axis=1)
    oa_d = _sample_attention(q_s[n_meta:], k_all, v_all, cache_meta_k[l], cache_meta_v[l], rel_bias, sinks)
    oa_s = jnp.concatenate([oa_m, oa_d], axis=0)
    kv4 = lambda t, b: t.reshape(b, -1, N_KV_HEADS, HEAD_DIM)
    tail = lambda t: kv4(jnp.concatenate([t[(b + 1) * seq - WINDOW:(b + 1) * seq] for b in range(bsz)], axis=0), bsz)
    attn_out = (kv4(k_meta, bsz), kv4(v_meta, bsz), tail(k_b), tail(v_b), k_all[:, 1:], v_all[:, 1:])

    ar, ai, w_b, w_c = _s5_weights(prm['ssm_a_re'][l], prm['ssm_a_im'][l], prm['ssm_log_dt'][l],
                                   prm['ssm_b_re'][l], prm['ssm_b_im'][l], prm['ssm_c_re'][l], prm['ssm_c_im'][l])
    bur_b, bui_b = _rows_call(_s5_bu_kernel, [u_b], [w_b], (SSM_N, SSM_N), 512, name="s5_bu")
    bur_s, bui_s = _rows_call(_s5_bu_kernel, [u_s], [w_b], (SSM_N, SSM_N), tm_s, name="s5_bu")
    zero_h = jnp.zeros((bsz, 1, SSM_N), F32)
    hr_m, hi_m = _s5_scan(bur_s[:n_meta], bui_s[:n_meta], zero_h, zero_h, ar, ai, bsz, N_META, N_META)
    last = lambda t: t.reshape(bsz, -1, SSM_N)[:, -1:]
    hr_b, hi_b = _s5_scan(bur_b, bui_b, last(hr_m), last(hi_m), ar, ai, bsz, seq, 512)
    hr_d, hi_d = _rows_call(_s5_step_kernel,
                            [bur_s[n_meta:], bui_s[n_meta:], state_ssm_re[l].reshape(dec, SSM_N).astype(F32),
                             state_ssm_im[l].reshape(dec, SSM_N).astype(F32)], [ar, ai], (SSM_N, SSM_N), dec,
                            name="s5_step")
    hr_s = jnp.concatenate([hr_m, hr_d], axis=0)
    hi_s = jnp.concatenate([hi_m, hi_d], axis=0)
    s5_consts = [w_c, prm['ssm_d'][l].reshape(1, -1).astype(F32), prm['ssm_w_glu'][l].astype(BF16)]
    (ob_b,) = _rows_call(_s5_out_kernel, [u_b, hr_b, hi_b], s5_consts, (SSM_WIDTH,), 512, name="s5_out")
    (ob_s,) = _rows_call(_s5_out_kernel, [u_s, hr_s, hi_s], s5_consts, (SSM_WIDTH,), tm_s, name="s5_out")
    st4 = lambda t, b: t.reshape(b, SSM_GROUPS, SSM_STATE)
    ssm_out = (st4(last(hr_b), bsz), st4(last(hi_b), bsz), st4(hr_d, dec), st4(hi_d, dec))

    pc_m = pc_s[:n_meta].reshape(bsz, N_META, RWKV_COLS)
    prev_m = jnp.concatenate([jnp.zeros((bsz, 1, RWKV_COLS), F32), pc_m[:, :-1]], axis=1)
    prev_s = jnp.concatenate([prev_m.reshape(n_meta, RWKV_COLS), state_shift[l].astype(F32)], axis=0)
    pad_rows = lambda w, lo: jnp.pad(w.astype(F32), ((lo, RWKV_LORA - lo - w.shape[0]), (0, 0))).astype(BF16)
    vec = jnp.pad(prm['rwkv_vec'][l].astype(F32), ((0, 1), (0, 0)))
    prep_consts = [prm['rwkv_mu'][l].reshape(1, -1).astype(F32), vec,
                   pad_rows(prm['rwkv_w2'][l], 0), pad_rows(prm['rwkv_a2'][l], RWKV_W_LORA),
                   pad_rows(prm['rwkv_g2'][l], RWKV_W_LORA + RWKV_A_LORA)]
    w7 = (RWKV_WIDTH,) * 7
    tm_prep = min(512, seq)
    tail_blocks = tm_prep // SUBLANES
    shift_inputs = [(pc_b, (SUBLANES, RWKV_COLS), lambda i: (jnp.maximum(i * tail_blocks - 1, 0), 0)),
                    (pc_m[:, -1:], (1, 1, RWKV_COLS), lambda i: (i // (seq // tm_prep), 0, 0))]
    r_b, kx_b, vx_b, lw_b, kk_b, bb_b, g_b = _rows_call(
        functools.partial(_rwkv_prep_shift_kernel, tiles_per_seq=seq // tm_prep), [pc_b], prep_consts, w7, tm_prep,
        name="rwkv_prep", extra_inputs=shift_inputs)
    r_s, kx_s, vx_s, lw_s, kk_s, bb_s, g_s = _rows_call(_rwkv_prep_kernel, [pc_s, prev_s], prep_consts, w7, tm_s,
                                                        name="rwkv_prep")
    meta_len = 2 * RWKV_CHUNK
    mrows = lambda t: jnp.pad(t[:n_meta].reshape(bsz, N_META, RWKV_WIDTH),
                              ((0, 0), (0, meta_len - N_META), (0, 0))).reshape(bsz * meta_len, RWKV_WIDTH)
    drows = lambda t: t[n_meta:]
    zero_s = jnp.zeros((bsz, RWKV_HEADS, RWKV_HEAD, RWKV_HEAD), F32)
    y_m, s_m = _rwkv_chunk(mrows(r_s), mrows(kx_s), mrows(vx_s), mrows(lw_s), mrows(kk_s), mrows(bb_s),
                           zero_s, bsz, meta_len)
    y_m = jnp.swapaxes(y_m[:N_META].reshape(N_META, bsz, RWKV_WIDTH), 0, 1).reshape(n_meta, RWKV_WIDTH)
    y_b, s_b = _rwkv_chunk(r_b, kx_b, vx_b, lw_b, kk_b, bb_b, s_m, bsz, seq)
    y_d, s_d = _rwkv_step(drows(r_s), drows(kx_s), drows(vx_s), drows(lw_s), drows(kk_s), drows(bb_s),
                          state_wkv[l].astype(F32))
    y_s = jnp.concatenate([y_m, y_d], axis=0)
    tm_post = min(512, seq)
    y_tile = (y_b, (tm_post, RWKV_WIDTH), lambda i: (i % (seq // tm_post), i // (seq // tm_post)))
    (oc_b,) = _rows_call(_rwkv_post_kernel, [r_b, kx_b, vx_b, g_b], [vec], (RWKV_WIDTH,), tm_post,
                         name="rwkv_post", extra_inputs=[y_tile])
    (oc_s,) = _rows_call(_rwkv_post_kernel, [r_s, kx_s, vx_s, g_s, y_s], [vec], (RWKV_WIDTH,), tm_s,
                         name="rwkv_post")
    rwkv_out = (s_b, s_d, pc_b.reshape(bsz, seq, RWKV_COLS)[:, -1], pc_s[n_meta:])

    ln_g, ln_b = prm['ln_g'][l].astype(F32), prm['ln_b'][l].astype(F32)
    merge_consts = [w_gate, prm['w_branch'][l].astype(BF16), prm['w_out'][l].astype(BF16), ln_g[0:1], ln_b[0:1]]
    moe_layer = l % 2 == 1
    merge_outs = (D_MODEL,) + (((SUBLANES, LANES),) if moe_layer else ())
    hb, *hb_tiles = _rows_call(_merge_kernel, [hb, oa_b, ob_b, oc_b], merge_consts, merge_outs, tm_b, name="merge")
    hs, *hs_tiles = _rows_call(_merge_kernel, [hs, oa_s, ob_s, oc_s], merge_consts, merge_outs, tm_s, name="merge")

    if not moe_layer:
        w_ffn_in = prm['ffn_w_in'][l // 2].astype(BF16)[None]
        w_ffn_down = prm['ffn_w_down'][l // 2].astype(BF16)[None]
        blk_b = min(1024, hb.shape[0])
        hb = _ffn(hb, jnp.zeros((hb.shape[0] // blk_b,), jnp.int32), w_ffn_in, w_ffn_down,
                  ln_g[1:2], ln_b[1:2], blk_b, 256, post_ln=True)
        hs = _ffn(hs, jnp.zeros((1,), jnp.int32), w_ffn_in, w_ffn_down, ln_g[1:2], ln_b[1:2], tm_s, 256,
                  post_ln=True)
    else:
        router_pad = jnp.pad(prm['moe_router'][l // 2].astype(F32), ((0, 0), (0, LANES - N_EXPERTS)))
        w_moe_in = prm['moe_w_in'][l // 2].astype(BF16)
        w_moe_down = prm['moe_w_down'][l // 2].astype(BF16)
        hb = _moe(hb, hb_tiles[0], router_pad, w_moe_in, w_moe_down, ln_g[1:2], ln_b[1:2], 512, 1008, 512)
        hs = _moe(hs, hs_tiles[0], router_pad, w_moe_in, w_moe_down, ln_g[1:2], ln_b[1:2], tm_s, 96, 512)
    return hb, hs, attn_out, ssm_out, rwkv_out


def kernel(x_prompt, x_sample, cache_meta_k, cache_meta_v, cache_win_k, cache_win_v, state_ssm_re, state_ssm_im, state_wkv, state_shift, meta_tokens, ln_in_g, ln_in_b, w_in, rel_bias, attn_sinks, ssm_a_re, ssm_a_im, ssm_log_dt, ssm_b_re, ssm_b_im, ssm_c_re, ssm_c_im, ssm_d, ssm_w_glu, rwkv_mu, rwkv_vec, rwkv_w2, rwkv_a2, rwkv_g2, w_branch, w_out, ln_g, ln_b, ffn_w_in, ffn_w_down, moe_router, moe_w_in, moe_w_down):
    bsz, seq, _ = x_prompt.shape
    dec = x_sample.shape[0]
    assert x_sample.shape[1] == 1 and seq % (2 * RWKV_CHUNK) == 0
    prm = dict(ln_in_g=ln_in_g.astype(F32), ln_in_b=ln_in_b.astype(F32), w_in=w_in, rel_bias=rel_bias,
               attn_sinks=attn_sinks, ssm_a_re=ssm_a_re, ssm_a_im=ssm_a_im, ssm_log_dt=ssm_log_dt,
               ssm_b_re=ssm_b_re, ssm_b_im=ssm_b_im, ssm_c_re=ssm_c_re, ssm_c_im=ssm_c_im, ssm_d=ssm_d,
               ssm_w_glu=ssm_w_glu, rwkv_mu=rwkv_mu, rwkv_vec=rwkv_vec, rwkv_w2=rwkv_w2, rwkv_a2=rwkv_a2,
               rwkv_g2=rwkv_g2, w_branch=w_branch, w_out=w_out, ln_g=ln_g, ln_b=ln_b, ffn_w_in=ffn_w_in,
               ffn_w_down=ffn_w_down, moe_router=moe_router, moe_w_in=moe_w_in, moe_w_down=moe_w_down)
    hb = x_prompt.reshape(bsz * seq, D_MODEL).astype(F32)
    meta = jnp.broadcast_to(meta_tokens.astype(F32)[None], (bsz, N_META, D_MODEL)).reshape(bsz * N_META, D_MODEL)
    hs = jnp.concatenate([meta, x_sample.reshape(dec, D_MODEL).astype(F32)], axis=0)
    cache = (cache_meta_k, cache_meta_v, cache_win_k, cache_win_v)
    states = (state_ssm_re, state_ssm_im, state_wkv, state_shift)
    attn_outs, ssm_outs, rwkv_outs = [], [], []
    for l in range(DEPTH):
        hb, hs, a_o, s_o, r_o = _layer(l, hb, hs, l == 0, bsz, seq, dec, cache, states, prm)
        attn_outs.append(a_o)
        ssm_outs.append(s_o)
        rwkv_outs.append(r_o)
    stack = lambda outs, i: jnp.stack([o[i] for o in outs])
    y_prompt = hb.reshape(bsz, seq, D_MODEL)
    y_sample = hs[bsz * N_META:].reshape(dec, 1, D_MODEL)
    return (y_prompt, y_sample,
            stack(attn_outs, 0), stack(attn_outs, 1), stack(attn_outs, 2), stack(attn_outs, 3),
            stack(attn_outs, 4), stack(attn_outs, 5),
            stack(ssm_outs, 0), stack(ssm_outs, 1), stack(ssm_outs, 2), stack(ssm_outs, 3),
            stack(rwkv_outs, 0), stack(rwkv_outs, 1), stack(rwkv_outs, 2), stack(rwkv_outs, 3))
```

```python
import functools
import math

import numpy as np
import jax
import jax.numpy as jnp
from jax import lax
from jax.experimental import pallas as pl
from jax.experimental.pallas import tpu as pltpu

F32 = jnp.float32
BF16 = jnp.bfloat16

D_MODEL = 1024
DEPTH = 2
PAST_LEN = 16384
N_META = 16
WINDOW = 128
N_HEADS = 8
N_KV_HEADS = 2
HEAD_DIM = 64
Q_PER_KV = N_HEADS // N_KV_HEADS
ATTN_W = N_HEADS * HEAD_DIM
KV_W = N_KV_HEADS * HEAD_DIM
ATTN_SCALE = HEAD_DIM ** -0.5
REL_BUCKETS = 32
REL_EXACT = REL_BUCKETS // 2
REL_MAX_DIST = 128
SSM_GROUP = 16
SSM_GROUPS = 16
SSM_WIDTH = SSM_GROUP * SSM_GROUPS
SSM_STATE = 64
SSM_N = SSM_GROUPS * SSM_STATE
RWKV_HEAD = 64
RWKV_HEADS = 4
RWKV_WIDTH = RWKV_HEAD * RWKV_HEADS
RWKV_W_LORA = 32
RWKV_A_LORA = 32
RWKV_G_LORA = 64
RWKV_LORA = RWKV_W_LORA + RWKV_A_LORA + RWKV_G_LORA
RWKV_COLS = 3 * RWKV_WIDTH + RWKV_LORA
RV_W0, RV_A0, RV_KK, RV_KA, RV_RK, RV_GNW, RV_GNB = 0, 1, 2, 3, 4, 5, 6
N_BRANCH = 3
MIX_COLS = ATTN_W + 2 * KV_W + SSM_WIDTH + RWKV_COLS
N_EXPERTS = 8
TOP_K = 2
LN_EPS = 1e-5
RWKV_GN_EPS = 64e-5
NEG_INF = -1e30
DEEPNORM_ALPHA = (2 * DEPTH) ** 0.25

LANES = 128
SUBLANES = 8
VMEM_LIMIT = 48 * 1024 * 1024
RWKV_CHUNK = 64

ROW_TILE = 512
FFN_ROWS, FFN_F_TILE = 1024, 256
MOE_F_TILE = 512
MOE_ROWS, MOE_ROWS_SMALL = 1008, 96
SAMPLE_ATTN_BLOCK = 8
RWKV_STEP_PAIRS = 64


def _params(sem):
    return pltpu.CompilerParams(dimension_semantics=sem, vmem_limit_bytes=VMEM_LIMIT)


def _dot(a, b):
    return jnp.dot(a.astype(BF16), b.astype(BF16), preferred_element_type=F32)


def _dot_nt(a, b):
    return lax.dot_general(a.astype(BF16), b.astype(BF16), (((1,), (1,)), ((), ())),
                           preferred_element_type=F32)


def _split3(x):
    h1 = x.astype(BF16)
    r1 = x - h1.astype(F32)
    h2 = r1.astype(BF16)
    h3 = (r1 - h2.astype(F32)).astype(BF16)
    return h1, h2, h3


def _dot_exact_rhs(x, m):
    h1, h2, h3 = _split3(x)
    dot = functools.partial(jnp.dot, preferred_element_type=F32)
    return dot(h1, m) + dot(h2, m) + dot(h3, m)


def _dot_exact_lhs(m, x):
    h1, h2, h3 = _split3(x)
    dot = functools.partial(jnp.dot, preferred_element_type=F32)
    return dot(m, h1) + dot(m, h2) + dot(m, h3)


def _layer_norm(x, g, b):
    mu = jnp.mean(x, axis=-1, keepdims=True)
    xc = x - mu
    var = jnp.mean(xc * xc, axis=-1, keepdims=True)
    return xc * lax.rsqrt(var + LN_EPS) * g + b


def _sigmoid(x):
    return 1.0 / (1.0 + jnp.exp(-x))


def _rows_to_tiles(x):
    slabs = [x[:, t * LANES:(t + 1) * LANES] for t in range(SUBLANES)]
    return jnp.swapaxes(jnp.stack(slabs, axis=0), 0, 1)


def _tiles_to_cols(x):
    xt = jnp.swapaxes(x, 0, 1)
    return [xt[t] for t in range(SUBLANES)]


def _rows_call(body, row_inputs, const_inputs, out_widths, tm, out_dtypes=None, name=None, extra_inputs=()):
    n = row_inputs[0].shape[0]
    tm = min(tm, n)
    assert n % tm == 0, (n, tm)
    out_dtypes = out_dtypes or [F32] * len(out_widths)
    in_specs = [pl.BlockSpec((tm, a.shape[1]), lambda i: (i, 0)) for a in row_inputs]
    in_specs += [pl.BlockSpec(shape, fn) for _, shape, fn in extra_inputs]
    row_inputs = list(row_inputs) + [a for a, _, _ in extra_inputs]
    in_specs += [pl.BlockSpec(c.shape, lambda i, nd=c.ndim: (0,) * nd) for c in const_inputs]
    tails = [w if isinstance(w, tuple) else (w,) for w in out_widths]
    out_specs = [pl.BlockSpec((tm,) + w, lambda i, nd=len(w): (i,) + (0,) * nd) for w in tails]
    out_shape = [jax.ShapeDtypeStruct((n,) + w, dt) for w, dt in zip(tails, out_dtypes)]
    return pl.pallas_call(
        body, grid=(n // tm,), in_specs=in_specs, out_specs=out_specs, out_shape=out_shape,
        compiler_params=_params(("parallel",)), name=name,
    )(*row_inputs, *const_inputs)


PROJ_WIDTHS = (ATTN_W, KV_W, KV_W, SSM_WIDTH, RWKV_COLS)


def _proj_kernel(x_ref, g_ref, b_ref, w_ref, *out_refs, pre_ln):
    x = x_ref[...]
    if pre_ln:
        x = _layer_norm(x, g_ref[...], b_ref[...])
        out_refs[0][...] = x
        out_refs = out_refs[1:]
    xb = x.astype(BF16)
    col = 0
    for o_ref in out_refs:
        n = o_ref.shape[-1]
        o_ref[...] = jnp.dot(xb, w_ref[:, col:col + n], preferred_element_type=F32)
        col += n


def _proj(x, ln_g, ln_b, w_mix, pre_ln, tm):
    widths = ((D_MODEL,) if pre_ln else ()) + PROJ_WIDTHS
    outs = _rows_call(functools.partial(_proj_kernel, pre_ln=pre_ln), [x], [ln_g, ln_b, w_mix],
                      widths, tm, name="proj")
    if pre_ln:
        return outs[0], outs[1:]
    return x, outs


def _attn_kernel(q_ref, k_ref, v_ref, bias_ref, sink_ref, o_ref):
    units = [(bb, h) for bb in range(q_ref.shape[0]) for h in range(N_KV_HEADS)]
    hs = [slice(h * HEAD_DIM, (h + 1) * HEAD_DIM) for h in range(N_KV_HEADS)]
    q = {u: q_ref[u[0], u[1]].astype(BF16) for u in units}
    k = {u: k_ref[u[0], :, hs[u[1]]].astype(BF16) for u in units}
    v = {u: v_ref[u[0], :, hs[u[1]]].astype(BF16) for u in units}
    s = {u: lax.dot_general(q[u], k[u], (((1,), (1,)), ((), ())), preferred_element_type=F32) for u in units}
    s = {u: s[u] * ATTN_SCALE + bias_ref[u[1]] for u in units}
    m = {u: jnp.maximum(jnp.max(s[u], axis=-1, keepdims=True), sink_ref[u[1]]) for u in units}
    p = {u: jnp.exp(s[u] - m[u]) for u in units}
    den = {u: jnp.sum(p[u], axis=-1, keepdims=True) + jnp.exp(sink_ref[u[1]] - m[u]) for u in units}
    o = {u: jnp.dot(p[u].astype(BF16), v[u], preferred_element_type=F32) for u in units}
    for u in units:
        o_ref[u[0], u[1]] = o[u] / den[u]


def _attention(q, k, v, bias, sinks, bblk):
    p, _, mq, _ = q.shape
    nk = k.shape[1]
    assert p % bblk == 0
    kv_spec = pl.BlockSpec((bblk, nk, KV_W), lambda i: (i, 0, 0))
    qo_spec = pl.BlockSpec((bblk, N_KV_HEADS, mq, HEAD_DIM), lambda i: (i, 0, 0, 0))
    return pl.pallas_call(
        _attn_kernel, grid=(p // bblk,),
        in_specs=[qo_spec, kv_spec, kv_spec, pl.BlockSpec((N_KV_HEADS, mq, nk), lambda i: (0, 0, 0)),
                  pl.BlockSpec((N_KV_HEADS, mq, 1), lambda i: (0, 0, 0))],
        out_specs=qo_spec, out_shape=jax.ShapeDtypeStruct(q.shape, F32),
        compiler_params=_params(("parallel",)), name="attention",
    )(q, k, v, bias, sinks)


def _t5_bucket(dist):
    n = np.maximum(dist, 0)
    scaled = (np.log(np.maximum(n, 1).astype(np.float32) / np.float32(REL_EXACT))
              / np.float32(math.log(REL_MAX_DIST / REL_EXACT)) * np.float32(REL_BUCKETS - REL_EXACT))
    frac = np.abs(scaled - np.round(scaled))
    assert np.all((n <= REL_EXACT) | (n >= REL_MAX_DIST) | (frac > 1e-3))
    large = np.minimum(REL_EXACT + scaled.astype(np.int32), REL_BUCKETS - 1)
    return np.where(n < REL_EXACT, n, large)


def _bias_table(rel_bias, dist, valid, mq_pad=None, nk_pad=None):
    tq, nk = dist.shape
    onehot = np.eye(REL_BUCKETS, dtype=np.float32)[_t5_bucket(dist).reshape(-1)]
    bias = jnp.dot(jnp.asarray(onehot), rel_bias.astype(F32), precision=lax.Precision.HIGHEST)
    bias = bias.reshape(tq, nk, N_HEADS)
    bias = jnp.where(jnp.asarray(valid)[..., None], bias, NEG_INF)
    bias = jnp.moveaxis(bias, -1, 0).reshape(N_KV_HEADS, Q_PER_KV * tq, nk)
    mq_pad = mq_pad or Q_PER_KV * tq
    nk_pad = nk_pad or nk
    bias = jnp.pad(bias, ((0, 0), (0, mq_pad - Q_PER_KV * tq), (0, 0)))
    return jnp.pad(bias, ((0, 0), (0, 0), (0, nk_pad - nk)), constant_values=NEG_INF)


def _sink_rows(sinks, tq, mq_pad=None):
    s = jnp.repeat(sinks.astype(F32).reshape(N_KV_HEADS, Q_PER_KV, 1), tq, axis=2)
    s = s.reshape(N_KV_HEADS, Q_PER_KV * tq, 1)
    mq_pad = mq_pad or Q_PER_KV * tq
    return jnp.pad(s, ((0, 0), (0, mq_pad - Q_PER_KV * tq), (0, 0)))


def _heads_q(q, nb, tq):
    q = q.reshape(nb, tq, N_KV_HEADS, Q_PER_KV, HEAD_DIM)
    return jnp.transpose(q, (0, 2, 3, 1, 4)).reshape(nb, N_KV_HEADS, Q_PER_KV * tq, HEAD_DIM)


def _unheads_o(o, nb, tq):
    o = o[:, :, :Q_PER_KV * tq].reshape(nb, N_KV_HEADS, Q_PER_KV, tq, HEAD_DIM)
    return jnp.transpose(o, (0, 3, 1, 2, 4)).reshape(nb * tq, ATTN_W)


BODY_KEYS = N_META + 2 * WINDOW + 16


def _body_attn_kernel(q_ref, ko_ref, kp_ref, km_ref, vo_ref, vp_ref, vm_ref, bias_ref, o_ref):
    kv_heads = range(N_KV_HEADS)
    hs = [slice(h * HEAD_DIM, (h + 1) * HEAD_DIM) for h in kv_heads]
    heads = [[h * Q_PER_KV + g for g in range(Q_PER_KV)] for h in kv_heads]
    pad = jnp.zeros((BODY_KEYS - N_META - 2 * WINDOW, HEAD_DIM), F32)
    ones = jnp.ones((BODY_KEYS, HEAD_DIM), BF16)
    k = [jnp.concatenate([km_ref[:, hs[h]], kp_ref[:, hs[h]], ko_ref[:, hs[h]], pad], axis=0).astype(BF16)
         for h in kv_heads]
    v = [jnp.concatenate([vm_ref[:, hs[h]], vp_ref[:, hs[h]], vo_ref[:, hs[h]], pad], axis=0).astype(BF16)
         for h in kv_heads]
    q = [jnp.concatenate([q_ref[:, qh * HEAD_DIM:(qh + 1) * HEAD_DIM] for qh in heads[h]], axis=0).astype(BF16)
         for h in kv_heads]
    s = [lax.dot_general(q[h], k[h], (((1,), (1,)), ((), ())), preferred_element_type=F32) for h in kv_heads]
    s = [s[h] * ATTN_SCALE + bias_ref[0, h] for h in kv_heads]
    p = [jnp.exp(s[h] - jnp.max(s[h], axis=-1, keepdims=True)).astype(BF16) for h in kv_heads]
    o = [jnp.dot(p[h], v[h], preferred_element_type=F32) / jnp.dot(p[h], ones, preferred_element_type=F32)
         for h in kv_heads]
    for h in kv_heads:
        for g, qh in enumerate(heads[h]):
            o_ref[:, qh * HEAD_DIM:(qh + 1) * HEAD_DIM] = o[h][g * WINDOW:(g + 1) * WINDOW]


def _body_attention(q, k, v, k_meta, v_meta, rel_bias, sinks, bsz, seq):
    nblk = seq // WINDOW
    i = np.arange(WINDOW)[:, None]
    c = np.arange(WINDOW)[None, :]
    sink_col = _sink_rows(sinks, WINDOW)
    tabs = []
    for m in (0, 1):
        q_pos = N_META + WINDOW * m + i
        meta_pos = np.arange(N_META)[None, :]
        dist = np.concatenate([q_pos - meta_pos, WINDOW + i - c, i - c], axis=1)
        valid = np.concatenate([np.ones((WINDOW, N_META), bool),
                                (c >= i) & (m > 0), c <= i], axis=1)
        tab = _bias_table(rel_bias, dist, valid, nk_pad=BODY_KEYS)
        tabs.append(tab.at[:, :, N_META + 2 * WINDOW].set(sink_col[:, :, 0]))
    bias = jnp.stack(tabs)
    nk = BODY_KEYS
    mq = Q_PER_KV * WINDOW
    own = lambda w: pl.BlockSpec((WINDOW, w), lambda b, m: (b * nblk + m, 0))
    prev = lambda w: pl.BlockSpec((WINDOW, w), lambda b, m: (b * nblk + jnp.maximum(m - 1, 0), 0))
    meta = pl.BlockSpec((N_META, KV_W), lambda b, m: (b, 0))
    return pl.pallas_call(
        _body_attn_kernel, grid=(bsz, nblk),
        in_specs=[own(ATTN_W), own(KV_W), prev(KV_W), meta, own(KV_W), prev(KV_W), meta,
                  pl.BlockSpec((1, N_KV_HEADS, mq, nk), lambda b, m: (jnp.minimum(m, 1), 0, 0, 0))],
        out_specs=own(ATTN_W), out_shape=jax.ShapeDtypeStruct(q.shape, F32),
        compiler_params=_params(("parallel", "arbitrary")), name="body_attention",
    )(q, k, k, k_meta, v, v, v_meta, bias)


def _meta_attention(q, k, v, rel_bias, sinks, bsz):
    i = np.arange(N_META)
    dist = i[:, None] - i[None, :]
    bias = _bias_table(rel_bias, dist, dist >= 0)
    kv = lambda t: t.reshape(bsz, N_META, KV_W)
    o = _attention(_heads_q(q, bsz, N_META), kv(k), kv(v), bias, _sink_rows(sinks, N_META), bsz)
    return _unheads_o(o, bsz, N_META)


def _sample_attention(q, k_all, v_all, meta_k, meta_v, rel_bias, sinks):
    bsz = q.shape[0]
    wc = k_all.shape[1] - 1
    nk = N_META + wc + 1
    nk_pad = -(-nk // LANES) * LANES
    mq_pad = SUBLANES
    k_pos = np.concatenate([np.arange(N_META), PAST_LEN - wc + np.arange(wc + 1)])
    dist = (PAST_LEN - k_pos)[None, :]
    is_meta = (np.arange(nk) < N_META)[None, :]
    valid = (dist >= 0) & (is_meta | ((k_pos[None, :] >= N_META) & (dist <= WINDOW)))
    bias = _bias_table(rel_bias, dist, valid, mq_pad, nk_pad)

    def kv(meta, t):
        full = jnp.concatenate([meta.astype(F32), t], axis=1).reshape(bsz, nk, KV_W)
        return jnp.pad(full, ((0, 0), (0, nk_pad - nk), (0, 0)))

    qh = jnp.pad(_heads_q(q, bsz, 1), ((0, 0), (0, 0), (0, mq_pad - Q_PER_KV), (0, 0)))
    o = _attention(qh, kv(meta_k, k_all), kv(meta_v, v_all), bias, _sink_rows(sinks, 1, mq_pad),
                   SAMPLE_ATTN_BLOCK)
    return _unheads_o(o, bsz, 1)


def _s5_bu_kernel(u_ref, w_ref, re_ref, im_ref):
    r = jnp.dot(u_ref[...].astype(BF16), w_ref[...], preferred_element_type=F32)
    re_ref[...] = r[:, :SSM_N]
    im_ref[...] = r[:, SSM_N:]


def _s5_scan_kernel(u_ref, w_ref, h0r_ref, h0i_ref, ar_ref, ai_ref, hr_ref, hi_ref, fr_ref, fi_ref,
                    xr_ref, xi_ref, cr_ref, ci_ref):
    @pl.when(pl.program_id(1) == 0)
    def _():
        cr_ref[...] = h0r_ref[0]
        ci_ref[...] = h0i_ref[0]

    bu = jnp.dot(u_ref[...].astype(BF16), w_ref[...], preferred_element_type=F32)
    xr_ref[...] = bu[:, :SSM_N]
    xi_ref[...] = bu[:, SSM_N:]
    ar = ar_ref[...]
    ai = ai_ref[...]

    def step(t, carry):
        hr, hi = carry
        nr = ar * hr - ai * hi + xr_ref[pl.ds(t, 1), :]
        ni = ar * hi + ai * hr + xi_ref[pl.ds(t, 1), :]
        xr_ref[pl.ds(t, 1), :] = nr
        xi_ref[pl.ds(t, 1), :] = ni
        return nr, ni

    hr, hi = lax.fori_loop(0, xr_ref.shape[0], step, (cr_ref[...], ci_ref[...]), unroll=8)
    cr_ref[...] = hr
    ci_ref[...] = hi
    hr_ref[...] = xr_ref[...].astype(BF16)
    hi_ref[...] = xi_ref[...].astype(BF16)

    @pl.when(pl.program_id(1) == pl.num_programs(1) - 1)
    def _():
        fr_ref[0] = hr
        fi_ref[0] = hi


def _s5_scan(u, w_b, h0r, h0i, ar, ai, bsz, seq, tt):
    tt = min(tt, seq)
    nt = seq // tt
    row = lambda w: pl.BlockSpec((tt, w), lambda b, t: (b * nt + t, 0))
    st = pl.BlockSpec((1, 1, SSM_N), lambda b, t: (b, 0, 0))
    cst = lambda a: pl.BlockSpec(a.shape, lambda b, t: (0, 0))
    return pl.pallas_call(
        _s5_scan_kernel, grid=(bsz, nt),
        in_specs=[row(SSM_WIDTH), cst(w_b), st, st, cst(ar), cst(ai)],
        out_specs=[row(SSM_N), row(SSM_N), st, st],
        out_shape=[jax.ShapeDtypeStruct((bsz * seq, SSM_N), BF16)] * 2 + [jax.ShapeDtypeStruct(h0r.shape, F32)] * 2,
        scratch_shapes=[pltpu.VMEM((tt, SSM_N), F32)] * 2 + [pltpu.VMEM((1, SSM_N), F32)] * 2,
        compiler_params=_params(("arbitrary", "arbitrary")), name="s5_scan",
    )(u, w_b, h0r, h0i, ar, ai)


def _s5_step_kernel(bur_ref, bui_ref, h0r_ref, h0i_ref, ar_ref, ai_ref, hr_ref, hi_ref):
    ar, ai, hr, hi = ar_ref[...], ai_ref[...], h0r_ref[...], h0i_ref[...]
    hr_ref[...] = ar * hr - ai * hi + bur_ref[...]
    hi_ref[...] = ar * hi + ai * hr + bui_ref[...]


def _s5_out_kernel(u_ref, hr_ref, hi_ref, wc_ref, d_ref, wg_ref, o_ref):
    y = (_dot(hr_ref[...], wc_ref[:SSM_N]) + _dot(hi_ref[...], wc_ref[SSM_N:])
         + d_ref[...] * u_ref[...])
    z = jax.nn.gelu(y)
    o_ref[...] = z * _sigmoid(_dot(z, wg_ref[...]))


def _block_diag(blocks):
    g, a, b = blocks.shape
    eye = jnp.eye(g, dtype=blocks.dtype)
    return (eye[:, None, :, None] * blocks[:, :, None, :]).reshape(g * a, g * b)


def _s5_weights(a_re, a_im, log_dt, b_re, b_im, c_re, c_im):
    a_re = a_re.astype(F32)
    a_im = a_im.astype(F32)
    dt = jnp.exp(log_dt.astype(F32))[:, None]
    mag = jnp.exp(a_re * dt)
    ab_re = mag * jnp.cos(a_im * dt)
    ab_im = mag * jnp.sin(a_im * dt)
    den = a_re * a_re + a_im * a_im
    nr = ab_re - 1.0
    cf_re = (nr * a_re + ab_im * a_im) / den
    cf_im = (ab_im * a_re - nr * a_im) / den
    b_re = b_re.astype(F32)
    b_im = b_im.astype(F32)
    bb_re = cf_re[..., None] * b_re - cf_im[..., None] * b_im
    bb_im = cf_re[..., None] * b_im + cf_im[..., None] * b_re
    w_b = jnp.concatenate([_block_diag(jnp.swapaxes(bb_re, 1, 2)),
                           _block_diag(jnp.swapaxes(bb_im, 1, 2))], axis=1)
    w_c = jnp.concatenate([_block_diag(jnp.swapaxes(c_re.astype(F32), 1, 2)),
                           -_block_diag(jnp.swapaxes(c_im.astype(F32), 1, 2))], axis=0)
    return ab_re.reshape(1, SSM_N), ab_im.reshape(1, SSM_N), w_b.astype(BF16), w_c.astype(BF16)


def _seg_ones():
    r = lax.broadcasted_iota(jnp.int32, (RWKV_WIDTH, RWKV_WIDTH), 0) // RWKV_HEAD
    c = lax.broadcasted_iota(jnp.int32, (RWKV_WIDTH, RWKV_WIDTH), 1) // RWKV_HEAD
    return (r == c).astype(BF16)


def _rwkv_prep_kernel(pc_ref, prev_ref, *rest):
    _rwkv_prep(pc_ref[...], prev_ref[...], *rest)


def _rwkv_prep_shift_kernel(pc_ref, tail_ref, first_ref, *rest, tiles_per_seq):
    pc = pc_ref[...]
    seq_start = pl.program_id(0) % tiles_per_seq == 0
    above = jnp.where(seq_start, first_ref[0], tail_ref[SUBLANES - 1:SUBLANES, :])
    row = lax.broadcasted_iota(jnp.int32, pc.shape, 0)
    _rwkv_prep(pc, jnp.where(row == 0, above, pltpu.roll(pc, 1, axis=0)), *rest)


def _rwkv_prep(pc, prev, mu_ref, vec_ref, w2_ref, a2_ref, g2_ref,
               r_ref, k_ref, v_ref, lw_ref, kk_ref, bb_ref, g_ref):
    xm = pc + (prev - pc) * mu_ref[...]
    rw = RWKV_WIDTH
    xr, xk, xv, xl = xm[:, :rw], xm[:, rw:2 * rw], xm[:, 2 * rw:3 * rw], xm[:, 3 * rw:]
    vec = vec_ref[...]
    wpre = -(vec[RV_W0:RV_W0 + 1] + _dot(jnp.tanh(xl), w2_ref[...]))
    softplus = jnp.maximum(wpre, 0.0) + jnp.log(1.0 + jnp.exp(-jnp.abs(wpre)))
    lw_ref[...] = -jnp.exp(-softplus - 0.5)
    a = _sigmoid(vec[RV_A0:RV_A0 + 1] + _dot(xl, a2_ref[...]))
    g_ref[...] = _dot(_sigmoid(xl), g2_ref[...])
    kk = xk * vec[RV_KK:RV_KK + 1]
    norm = jnp.sqrt(_dot_exact_rhs(kk * kk, _seg_ones()))
    kk = kk / jnp.maximum(norm, 1e-12)
    r_ref[...] = xr
    k_ref[...] = xk * (1.0 + (a - 1.0) * vec[RV_KA:RV_KA + 1])
    v_ref[...] = xv
    kk_ref[...] = kk
    bb_ref[...] = kk * a


def _rwkv_post_kernel(r_ref, k_ref, v_ref, g_ref, y_ref, vec_ref, o_ref):
    ones = _seg_ones()
    vec = vec_ref[...]
    y = y_ref[...]
    yc = y - _dot_exact_rhs(y, ones) * (1.0 / RWKV_HEAD)
    yv = _dot_exact_rhs(yc * yc, ones) * (1.0 / RWKV_HEAD)
    yn = yc * lax.rsqrt(yv + RWKV_GN_EPS) * vec[RV_GNW:RV_GNW + 1] + vec[RV_GNB:RV_GNB + 1]
    bonus = _dot_exact_rhs(r_ref[...] * k_ref[...] * vec[RV_RK:RV_RK + 1], ones) * v_ref[...]
    o_ref[...] = (yn + bonus) * g_ref[...]


def _rwkv_chunk_kernel(*refs, chunk, nb):
    r_refs, k_refs, v_refs, lw_refs, kk_refs, bb_refs = (refs[i * nb:(i + 1) * nb] for i in range(6))
    s0_ref, y_ref, so_ref, s_ref = refs[6 * nb:]
    t = pl.program_id(0)

    @pl.when(t == 0)
    def _():
        s_ref[...] = s0_ref[...]

    c = chunk
    row = lax.broadcasted_iota(jnp.int32, (c, c), 0)
    col = lax.broadcasted_iota(jnp.int32, (c, c), 1)
    incl = (row >= col).astype(F32)
    strict = (row > col).astype(F32)
    eye = (row == col).astype(F32)
    nsub = r_refs[0].shape[0] // c
    heads = range(RWKV_HEADS)
    hsl = [slice(h * RWKV_HEAD, (h + 1) * RWKV_HEAD) for h in heads]
    tsl = [slice(sub * c, (sub + 1) * c) for sub in range(nsub)]
    seqs = [(b, h) for b in range(nb) for h in heads]
    units = [(b, sub, h) for b in range(nb) for sub in range(nsub) for h in heads]

    scaled = {}
    for b in range(nb):
        for sub in range(nsub):
            lw = lw_refs[b][tsl[sub], :]
            cum = _dot_exact_lhs(incl.astype(BF16), lw)
            ecum = jnp.exp(cum)
            einv = jnp.exp(-cum)
            scaled[b, sub] = (kk_refs[b][tsl[sub], :] * jnp.exp(cum - lw), bb_refs[b][tsl[sub], :] * einv,
                              k_refs[b][tsl[sub], :] * einv, r_refs[b][tsl[sub], :] * ecum, ecum[c - 1:c, :])
    kt = {u: scaled[u[0], u[1]][0][:, hsl[u[2]]] for u in units}
    bt = {u: scaled[u[0], u[1]][1][:, hsl[u[2]]] for u in units}
    kkt = {u: scaled[u[0], u[1]][2][:, hsl[u[2]]] for u in units}
    rt = {u: scaled[u[0], u[1]][3][:, hsl[u[2]]] for u in units}
    v_t = [v_refs[b][...].T for b in range(nb)]
    vt = {u: v_t[u[0]][hsl[u[2]], tsl[u[1]]] for u in units}
    a_b = {u: strict * _dot_nt(kt[u], bt[u]) for u in units}
    a_k = {u: strict * _dot_nt(kt[u], kkt[u]) for u in units}
    r_b = {u: incl * _dot_nt(rt[u], bt[u]) for u in units}
    r_k = {u: incl * _dot_nt(rt[u], kkt[u]) for u in units}
    pw = {u: -a_b[u] for u in units}
    tinv = {u: eye + pw[u] for u in units}
    n = 1
    while 2 * n < c:
        pw = {u: _dot(pw[u], pw[u]) for u in units}
        tinv = {u: tinv[u] + _dot(tinv[u], pw[u]) for u in units}
        n *= 2
    x = {u: _dot_nt(vt[u], a_k[u]) for u in units}
    w1t = {u: _dot_nt(x[u], tinv[u]) for u in units}
    w2 = {u: _dot(tinv[u], kt[u]) for u in units}
    yt_local = {u: _dot_nt(vt[u], r_k[u]) for u in units}
    s_local = {u: _dot(vt[u], kkt[u]) for u in units}

    s = {q: s_ref[q[0], q[1]] for q in seqs}
    yt = {}
    for sub in range(nsub):
        ut = {(b, h): -(_dot_nt(s[b, h], w2[b, sub, h]) + w1t[b, sub, h]) for b, h in seqs}
        for b, h in seqs:
            yt[b, sub, h] = (_dot_nt(s[b, h], rt[b, sub, h]) + _dot_nt(ut[b, h], r_b[b, sub, h])
                             + yt_local[b, sub, h])
        s = {(b, h): (s[b, h] + _dot(ut[b, h], bt[b, sub, h]) + s_local[b, sub, h]) * scaled[b, sub][4][:, hsl[h]]
             for b, h in seqs}
    for b, h in seqs:
        s_ref[b, h] = s[b, h]
    for b in range(nb):
        y_t = jnp.concatenate([jnp.concatenate([yt[b, sub, h] for sub in range(nsub)], axis=1) for h in heads],
                              axis=0)
        y_ref[:, b * RWKV_WIDTH:(b + 1) * RWKV_WIDTH] = y_t.T

    @pl.when(t == pl.num_programs(0) - 1)
    def _():
        so_ref[...] = s_ref[...]


def _rwkv_chunk(r, k, v, lw, kk, bb, s0, bsz, seq):
    tb = 2 * RWKV_CHUNK
    assert seq % tb == 0
    nt = seq // tb
    tiles = [pl.BlockSpec((tb, RWKV_WIDTH), lambda t, b=b: (b * nt + t, 0)) for b in range(bsz)]
    st = pl.BlockSpec(s0.shape, lambda t: (0, 0, 0, 0))
    return pl.pallas_call(
        functools.partial(_rwkv_chunk_kernel, chunk=RWKV_CHUNK, nb=bsz), grid=(nt,),
        in_specs=tiles * 6 + [st], out_specs=[pl.BlockSpec((tb, bsz * RWKV_WIDTH), lambda t: (t, 0)), st],
        out_shape=[jax.ShapeDtypeStruct((seq, bsz * RWKV_WIDTH), F32), jax.ShapeDtypeStruct(s0.shape, F32)],
        scratch_shapes=[pltpu.VMEM(s0.shape, F32)],
        compiler_params=_params(("arbitrary",)), name="rwkv_chunk",
    )(*([r] * bsz + [k] * bsz + [v] * bsz + [lw] * bsz + [kk] * bsz + [bb] * bsz), s0)


def _rwkv_step_kernel(s_ref, r_ref, k_ref, lw_ref, kk_ref, bb_ref, v_ref, so_ref, y_ref):
    s = s_ref[...]
    sa = jnp.sum(s * (-kk_ref[...]), axis=-1, keepdims=True)
    s = s * jnp.exp(lw_ref[...]) + sa * bb_ref[...] + v_ref[...] * k_ref[...]
    so_ref[...] = s
    y_ref[...] = jnp.sum(s * r_ref[...], axis=-1, keepdims=True)


def _rwkv_step(r, k, v, lw, kk, bb, s0):
    bsz = r.shape[0]
    p = bsz * RWKV_HEADS
    nb = min(RWKV_STEP_PAIRS, p)
    rowv = lambda a: a.reshape(p, 1, RWKV_HEAD)
    rs = pl.BlockSpec((nb, 1, RWKV_HEAD), lambda i: (i, 0, 0))
    cs = pl.BlockSpec((nb, RWKV_HEAD, 1), lambda i: (i, 0, 0))
    ss = pl.BlockSpec((nb, RWKV_HEAD, RWKV_HEAD), lambda i: (i, 0, 0))
    s_out, y = pl.pallas_call(
        _rwkv_step_kernel, grid=(p // nb,), in_specs=[ss, rs, rs, rs, rs, rs, cs], out_specs=[ss, cs],
        out_shape=[jax.ShapeDtypeStruct((p, RWKV_HEAD, RWKV_HEAD), F32),
                   jax.ShapeDtypeStruct((p, RWKV_HEAD, 1), F32)],
        compiler_params=_params(("parallel",)), name="rwkv_step",
    )(s0.reshape(p, RWKV_HEAD, RWKV_HEAD), rowv(r), rowv(k), rowv(lw), rowv(kk), rowv(bb),
      v.reshape(p, RWKV_HEAD, 1))
    return y.reshape(bsz, RWKV_WIDTH), s_out.reshape(s0.shape)


def _merge_kernel(h_ref, oa_ref, ob_ref, oc_ref, wg_ref, wb_ref, wo_ref, g_ref, b_ref, o_ref, *tile_refs):
    h = h_ref[...]
    gates = _sigmoid(jnp.dot(h.astype(BF16), wg_ref[...], preferred_element_type=F32))
    d = D_MODEL
    merged = (gates[:, :d] * _dot(oa_ref[...], wb_ref[:ATTN_W])
              + gates[:, d:2 * d] * _dot(ob_ref[...], wb_ref[ATTN_W:ATTN_W + SSM_WIDTH])
              + gates[:, 2 * d:] * _dot(oc_ref[...], wb_ref[ATTN_W + SSM_WIDTH:]))
    mix = _dot(merged, wo_ref[...])
    out = _layer_norm(DEEPNORM_ALPHA * h + mix, g_ref[...], b_ref[...])
    o_ref[...] = out
    for t_ref in tile_refs:
        t_ref[...] = _rows_to_tiles(out)


def _ffn_kernel(be_ref, x_ref, wg_ref, wu_ref, wd_ref, g_ref, b_ref, o_ref, xb_ref, acc_ref, *, post_ln):
    del be_ref
    j = pl.program_id(1)

    @pl.when(j == 0)
    def _():
        xb_ref[...] = x_ref[...].astype(BF16)
        acc_ref[...] = jnp.zeros_like(acc_ref)

    xb = xb_ref[...]
    gate = jnp.dot(xb, wg_ref[0], preferred_element_type=F32)
    up = jnp.dot(xb, wu_ref[0], preferred_element_type=F32)
    act = gate * _sigmoid(gate) * up
    acc_ref[...] += jnp.dot(act.astype(BF16), wd_ref[0], preferred_element_type=F32)

    @pl.when(j == pl.num_programs(1) - 1)
    def _():
        if post_ln:
            o_ref[...] = _layer_norm(DEEPNORM_ALPHA * x_ref[...] + acc_ref[...], g_ref[...], b_ref[...])
        else:
            o_ref[...] = acc_ref[...]


def _ffn(x, block_e, w_in, w_down, ln_g, ln_b, blk, tf, post_ln):
    rows = x.shape[0]
    assert rows % blk == 0
    f = w_down.shape[1]
    nf = f // tf
    grid_spec = pltpu.PrefetchScalarGridSpec(
        num_scalar_prefetch=1, grid=(rows // blk, nf),
        in_specs=[
            pl.BlockSpec((blk, D_MODEL), lambda i, j, be: (i, 0)),
            pl.BlockSpec((1, D_MODEL, tf), lambda i, j, be: (be[i], 0, j)),
            pl.BlockSpec((1, D_MODEL, tf), lambda i, j, be: (be[i], 0, nf + j)),
            pl.BlockSpec((1, tf, D_MODEL), lambda i, j, be: (be[i], j, 0)),
            pl.BlockSpec((1, D_MODEL), lambda i, j, be: (0, 0)),
            pl.BlockSpec((1, D_MODEL), lambda i, j, be: (0, 0)),
        ],
        out_specs=pl.BlockSpec((blk, D_MODEL), lambda i, j, be: (i, 0)),
        scratch_shapes=[pltpu.VMEM((blk, D_MODEL), BF16), pltpu.VMEM((blk, D_MODEL), F32)],
    )
    return pl.pallas_call(
        functools.partial(_ffn_kernel, post_ln=post_ln), grid_spec=grid_spec,
        out_shape=jax.ShapeDtypeStruct((rows, D_MODEL), F32),
        compiler_params=_params(("arbitrary", "arbitrary")), name="ffn",
    )(block_e, x, w_in, w_in, w_down, ln_g, ln_b)


def _router_kernel(h_ref, w_ref, e_ref, g_ref):
    logits = jnp.dot(h_ref[...], w_ref[...], preferred_element_type=F32, precision=lax.Precision.HIGHEST)
    lane = lax.broadcasted_iota(jnp.int32, logits.shape, 1)
    lg = jnp.where(lane < N_EXPERTS, logits, -jnp.inf)
    m1 = jnp.max(lg, axis=-1, keepdims=True)
    i1 = jnp.min(jnp.where(lg == m1, lane, LANES), axis=-1, keepdims=True)
    lg2 = jnp.where(lane == i1, -jnp.inf, lg)
    m2 = jnp.max(lg2, axis=-1, keepdims=True)
    i2 = jnp.min(jnp.where(lg2 == m2, lane, LANES), axis=-1, keepdims=True)
    e2 = jnp.exp(m2 - m1)
    den = 1.0 + e2
    e_ref[...] = jnp.where(lane == 0, i1, jnp.where(lane == 1, i2, 0))
    g_ref[...] = jnp.where(lane == 0, 1.0 / den, jnp.where(lane == 1, e2 / den, 0.0))


def _moe_ffn_kernel(be_ref, nv_ref, first_ref, nxt_ref, prev_dst_ref, last_dst_ref, h_hbm, wg_ref, wu_ref, wd_ref,
                    out_hbm, xbuf, xb_ref, acc_ref, stage, gsem, ssem):
    del be_ref
    i, j = pl.program_id(0), pl.program_id(1)
    nblk, nf = pl.num_programs(0), pl.num_programs(1)
    blk = xb_ref.shape[0]
    n_valid = nv_ref[0]
    valid = i < n_valid
    slot = lax.rem(i, 2)
    spare_base = out_hbm.shape[0] - blk

    def gather_row(idx_ref, r, s):
        pltpu.make_async_copy(h_hbm.at[idx_ref[0, 0, r]], xbuf.at[s, r], gsem.at[s]).start(priority=0)

    def scatter_row(r, d):
        pltpu.make_async_copy(stage.at[r], out_hbm.at[d], ssem).start(priority=1)

    def wait_gather(s):
        pltpu.make_async_copy(h_hbm.at[pl.ds(0, blk)], xbuf.at[s], gsem.at[s]).wait()

    def wait_scatter():
        pltpu.make_async_copy(stage, out_hbm.at[pl.ds(0, blk)], ssem).wait()

    def for_rows(fn):
        def body(r, c):
            fn(r)
            return c
        lax.fori_loop(0, blk, body, 0, unroll=8)

    def compute():
        xb = xb_ref[...]
        gate = jnp.dot(xb, wg_ref[0], preferred_element_type=F32)
        up = jnp.dot(xb, wu_ref[0], preferred_element_type=F32)
        act = gate * _sigmoid(gate) * up
        acc_ref[...] += jnp.dot(act.astype(BF16), wd_ref[0], preferred_element_type=F32)

    @pl.when((i == 0) & (j == 0))
    def _():
        for_rows(lambda r: gather_row(first_ref, r, 0))
        stage[...] = jnp.zeros_like(stage)

    @pl.when(valid & (j == 0))
    def _():
        wait_gather(slot)
        for t, cols in enumerate(_tiles_to_cols(xbuf[slot])):
            xb_ref[:, t * LANES:(t + 1) * LANES] = cols.astype(BF16)
        acc_ref[...] = jnp.zeros_like(acc_ref)

    @pl.when(valid & (j < nf - 1))
    def _():
        compute()
        per_step = blk // (nf - 1)
        for rr in range(per_step):
            r = j * per_step + rr
            gather_row(nxt_ref, r, 1 - slot)
            scatter_row(r, jnp.where(i > 0, prev_dst_ref[0, 0, r], spare_base + r))

    @pl.when(valid & (j == nf - 1))
    def _():
        compute()
        wait_scatter()
        stage[...] = _rows_to_tiles(acc_ref[...])

    @pl.when((i == nblk - 1) & (j == nf - 1))
    def _():
        for_rows(lambda r: scatter_row(r, last_dst_ref[0, 0, r]))
        wait_scatter()
        wait_gather(lax.rem(n_valid, 2))


def _moe_ffn(h_tiles, rows_tok, rows_dst, block_e, n_valid, w_in, w_down, blk, tf):
    n = h_tiles.shape[0]
    rows = rows_tok.shape[0]
    nblk = rows // blk
    nf = w_down.shape[1] // tf
    assert blk % (nf - 1) == 0 and n >= blk
    idx3 = lambda a: a.reshape(nblk, 1, blk)
    smem = lambda fn: pl.BlockSpec((1, 1, blk), fn, memory_space=pltpu.SMEM)
    ftile = lambda i, j, nv: jnp.where(i < nv[0], j, nf - 1)
    grid_spec = pltpu.PrefetchScalarGridSpec(
        num_scalar_prefetch=2, grid=(nblk, nf),
        in_specs=[
            smem(lambda i, j, be, nv: (0, 0, 0)),
            smem(lambda i, j, be, nv: (jnp.minimum(i + 1, nblk - 1), 0, 0)),
            smem(lambda i, j, be, nv: (jnp.maximum(i - 1, 0), 0, 0)),
            smem(lambda i, j, be, nv: (nv[0] - 1, 0, 0)),
            pl.BlockSpec(memory_space=pl.ANY),
            pl.BlockSpec((1, D_MODEL, tf), lambda i, j, be, nv: (be[i], 0, ftile(i, j, nv))),
            pl.BlockSpec((1, D_MODEL, tf), lambda i, j, be, nv: (be[i], 0, nf + ftile(i, j, nv))),
            pl.BlockSpec((1, tf, D_MODEL), lambda i, j, be, nv: (be[i], ftile(i, j, nv), 0)),
        ],
        out_specs=pl.BlockSpec(memory_space=pl.ANY),
        scratch_shapes=[pltpu.VMEM((2, blk, SUBLANES, LANES), F32), pltpu.VMEM((blk, D_MODEL), BF16),
                        pltpu.VMEM((blk, D_MODEL), F32), pltpu.VMEM((blk, SUBLANES, LANES), F32),
                        pltpu.SemaphoreType.DMA((2,)), pltpu.SemaphoreType.DMA(())],
    )
    return pl.pallas_call(
        _moe_ffn_kernel, grid_spec=grid_spec,
        out_shape=jax.ShapeDtypeStruct((TOP_K * n + blk, SUBLANES, LANES), F32),
        compiler_params=_params(("arbitrary", "arbitrary")), name="moe_ffn",
    )(block_e, n_valid, idx3(rows_tok), idx3(rows_tok), idx3(rows_dst), idx3(rows_dst), h_tiles,
      w_in, w_in, w_down)


def _combine_kernel(h_ref, y0_ref, y1_ref, gate_ref, g_ref, bias_ref, o_ref):
    gate = gate_ref[...]
    rows = lambda y_ref: jnp.concatenate(_tiles_to_cols(y_ref[...]), axis=-1)
    f = rows(y0_ref) * gate[:, 0:1] + rows(y1_ref) * gate[:, 1:2]
    o_ref[...] = _layer_norm(DEEPNORM_ALPHA * h_ref[...] + f, g_ref[...], bias_ref[...])


def _moe(h, h_tiles, router_pad, w_in, w_down, ln_g, ln_b, tm, blk, tf):
    n = h.shape[0]
    e_pad, gate = _rows_call(_router_kernel, [h], [router_pad], (LANES, LANES), tm,
                             out_dtypes=[jnp.int32, F32], name="router")
    flat_e = jnp.concatenate([e_pad[:, s] for s in range(TOP_K)])
    n_assign = n * TOP_K
    n_blocks = -(-(n_assign + N_EXPERTS * (blk - 1)) // blk)
    experts = jnp.arange(N_EXPERTS, dtype=jnp.int32)
    onehot = (flat_e[:, None] == experts[None, :]).astype(jnp.int32)
    csum = jnp.cumsum(onehot, axis=0)
    counts = csum[-1]
    padded = (counts + blk - 1) // blk * blk
    pad_end = jnp.cumsum(padded)
    dest = jnp.sum((csum - 1 + (pad_end - padded)[None, :]) * onehot, axis=1)
    assign = jnp.arange(n_assign, dtype=jnp.int32)
    rows_dst = jnp.full((n_blocks * blk,), -1, jnp.int32).at[dest].set(assign, unique_indices=True)
    rows_tok = jnp.maximum(rows_dst, 0) % n
    spare = n_assign + jnp.arange(n_blocks * blk, dtype=jnp.int32) % blk
    rows_dst = jnp.where(rows_dst < 0, spare, rows_dst)
    block_start = jnp.arange(n_blocks, dtype=jnp.int32) * blk
    block_e = jnp.minimum(jnp.sum((block_start[:, None] >= pad_end[None, :]).astype(jnp.int32), axis=1),
                          N_EXPERTS - 1)
    n_valid = (pad_end[-1:] // blk).astype(jnp.int32)
    y = _moe_ffn(h_tiles, rows_tok, rows_dst, block_e, n_valid, w_in, w_down, blk, tf)
    tm = min(tm, n)
    nt = n // tm
    cst = pl.BlockSpec((1, D_MODEL), lambda i: (0, 0))
    slot = lambda s: pl.BlockSpec((tm, SUBLANES, LANES), lambda i: (i + s * nt, 0, 0))
    return pl.pallas_call(
        _combine_kernel, grid=(nt,),
        in_specs=[pl.BlockSpec((tm, D_MODEL), lambda i: (i, 0)), slot(0), slot(1),
                  pl.BlockSpec((tm, LANES), lambda i: (i, 0)), cst, cst],
        out_specs=pl.BlockSpec((tm, D_MODEL), lambda i: (i, 0)),
        out_shape=jax.ShapeDtypeStruct((n, D_MODEL), F32),
        compiler_params=_params(("parallel",)), name="moe_combine",
    )(h, y, y, gate, ln_g, ln_b)


def _layer(l, hb, hs, pre_ln, bsz, seq, dec, cache, states, prm):
    (cache_meta_k, cache_meta_v, cache_win_k, cache_win_v) = cache
    (state_ssm_re, state_ssm_im, state_wkv, state_shift) = states
    n_meta = bsz * N_META
    tm_b, tm_s = ROW_TILE, hs.shape[0]
    w_in = prm['w_in'][l]
    w_mix = w_in[:, :MIX_COLS].astype(BF16)
    w_gate = w_in[:, MIX_COLS:].astype(BF16)
    ln_in_g, ln_in_b = prm['ln_in_g'].reshape(1, -1), prm['ln_in_b'].reshape(1, -1)

    hb, (q_b, k_b, v_b, u_b, pc_b) = _proj(hb, ln_in_g, ln_in_b, w_mix, pre_ln, tm_b)
    hs, (q_s, k_s, v_s, u_s, pc_s) = _proj(hs, ln_in_g, ln_in_b, w_mix, pre_ln, tm_s)

    rel_bias, sinks = prm['rel_bias'], prm['attn_sinks'][l]
    k_meta, v_meta = k_s[:n_meta], v_s[:n_meta]
    oa_b = _body_attention(q_b, k_b, v_b, k_meta, v_meta, rel_bias, sinks, bsz, seq)
    oa_m = _meta_attention(q_s[:n_meta], k_meta, v_meta, rel_bias, sinks, bsz)
    kd = lambda t: t[n_meta:].reshape(dec, 1, N_KV_HEADS, HEAD_DIM)
    k_all = jnp.concatenate([cache_win_k[l].astype(F32), kd(k_s)], axis=1)
    v_all = jnp.concatenate([cache_win_v[l].astype(F32), kd(v_s)], axis=1)
    oa_d = _sample_attention(q_s[n_meta:], k_all, v_all, cache_meta_k[l], cache_meta_v[l], rel_bias, sinks)
    oa_s = jnp.concatenate([oa_m, oa_d], axis=0)
    kv4 = lambda t, b: t.reshape(b, -1, N_KV_HEADS, HEAD_DIM)
    tail = lambda t: kv4(jnp.concatenate([t[(b + 1) * seq - WINDOW:(b + 1) * seq] for b in range(bsz)], axis=0), bsz)
    attn_out = (kv4(k_meta, bsz), kv4(v_meta, bsz), tail(k_b), tail(v_b), k_all[:, 1:], v_all[:, 1:])

    ar, ai, w_b, w_c = _s5_weights(prm['ssm_a_re'][l], prm['ssm_a_im'][l], prm['ssm_log_dt'][l],
                                   prm['ssm_b_re'][l], prm['ssm_b_im'][l], prm['ssm_c_re'][l], prm['ssm_c_im'][l])
    zero_h = jnp.zeros((bsz, 1, SSM_N), F32)
    hr_m, hi_m, fr_m, fi_m = _s5_scan(u_s[:n_meta], w_b, zero_h, zero_h, ar, ai, bsz, N_META, N_META)
    hr_b, hi_b, fr_b, fi_b = _s5_scan(u_b, w_b, fr_m, fi_m, ar, ai, bsz, seq, ROW_TILE)
    bur_d, bui_d = _rows_call(_s5_bu_kernel, [u_s[n_meta:]], [w_b], (SSM_N, SSM_N), dec, name="s5_bu")
    hr_d, hi_d = _rows_call(_s5_step_kernel,
                            [bur_d, bui_d, state_ssm_re[l].reshape(dec, SSM_N).astype(F32),
                             state_ssm_im[l].reshape(dec, SSM_N).astype(F32)], [ar, ai], (SSM_N, SSM_N), dec,
                            name="s5_step")
    hr_s = jnp.concatenate([hr_m, hr_d.astype(BF16)], axis=0)
    hi_s = jnp.concatenate([hi_m, hi_d.astype(BF16)], axis=0)
    s5_consts = [w_c, prm['ssm_d'][l].reshape(1, -1).astype(F32), prm['ssm_w_glu'][l].astype(BF16)]
    (ob_b,) = _rows_call(_s5_out_kernel, [u_b, hr_b, hi_b], s5_consts, (SSM_WIDTH,), tm_b, name="s5_out")
    (ob_s,) = _rows_call(_s5_out_kernel, [u_s, hr_s, hi_s], s5_consts, (SSM_WIDTH,), tm_s, name="s5_out")
    st4 = lambda t, b: t.reshape(b, SSM_GROUPS, SSM_STATE)
    ssm_out = (st4(fr_b, bsz), st4(fi_b, bsz), st4(hr_d, dec), st4(hi_d, dec))

    pc_m = pc_s[:n_meta].reshape(bsz, N_META, RWKV_COLS)
    prev_m = jnp.concatenate([jnp.zeros((bsz, 1, RWKV_COLS), F32), pc_m[:, :-1]], axis=1)
    prev_s = jnp.concatenate([prev_m.reshape(n_meta, RWKV_COLS), state_shift[l].astype(F32)], axis=0)
    pad_rows = lambda w, lo: jnp.pad(w.astype(F32), ((lo, RWKV_LORA - lo - w.shape[0]), (0, 0))).astype(BF16)
    vec = jnp.pad(prm['rwkv_vec'][l].astype(F32), ((0, 1), (0, 0)))
    prep_consts = [prm['rwkv_mu'][l].reshape(1, -1).astype(F32), vec,
                   pad_rows(prm['rwkv_w2'][l], 0), pad_rows(prm['rwkv_a2'][l], RWKV_W_LORA),
                   pad_rows(prm['rwkv_g2'][l], RWKV_W_LORA + RWKV_A_LORA)]
    w7 = (RWKV_WIDTH,) * 7
    tm_prep = min(ROW_TILE, seq)
    tail_blocks = tm_prep // SUBLANES
    shift_inputs = [(pc_b, (SUBLANES, RWKV_COLS), lambda i: (jnp.maximum(i * tail_blocks - 1, 0), 0)),
                    (pc_m[:, -1:], (1, 1, RWKV_COLS), lambda i: (i // (seq // tm_prep), 0, 0))]
    r_b, kx_b, vx_b, lw_b, kk_b, bb_b, g_b = _rows_call(
        functools.partial(_rwkv_prep_shift_kernel, tiles_per_seq=seq // tm_prep), [pc_b], prep_consts, w7, tm_prep,
        name="rwkv_prep", extra_inputs=shift_inputs)
    r_s, kx_s, vx_s, lw_s, kk_s, bb_s, g_s = _rows_call(_rwkv_prep_kernel, [pc_s, prev_s], prep_consts, w7, tm_s,
                                                        name="rwkv_prep")
    meta_len = 2 * RWKV_CHUNK
    mrows = lambda t: jnp.pad(t[:n_meta].reshape(bsz, N_META, RWKV_WIDTH),
                              ((0, 0), (0, meta_len - N_META), (0, 0))).reshape(bsz * meta_len, RWKV_WIDTH)
    drows = lambda t: t[n_meta:]
    zero_s = jnp.zeros((bsz, RWKV_HEADS, RWKV_HEAD, RWKV_HEAD), F32)
    y_m, s_m = _rwkv_chunk(mrows(r_s), mrows(kx_s), mrows(vx_s), mrows(lw_s), mrows(kk_s), mrows(bb_s),
                           zero_s, bsz, meta_len)
    y_m = jnp.swapaxes(y_m[:N_META].reshape(N_META, bsz, RWKV_WIDTH), 0, 1).reshape(n_meta, RWKV_WIDTH)
    y_b, s_b = _rwkv_chunk(r_b, kx_b, vx_b, lw_b, kk_b, bb_b, s_m, bsz, seq)
    y_d, s_d = _rwkv_step(drows(r_s), drows(kx_s), drows(vx_s), drows(lw_s), drows(kk_s), drows(bb_s),
                          state_wkv[l].astype(F32))
    y_s = jnp.concatenate([y_m, y_d], axis=0)
    tm_post = min(ROW_TILE, seq)
    y_tile = (y_b, (tm_post, RWKV_WIDTH), lambda i: (i % (seq // tm_post), i // (seq // tm_post)))
    (oc_b,) = _rows_call(_rwkv_post_kernel, [r_b, kx_b, vx_b, g_b], [vec], (RWKV_WIDTH,), tm_post,
                         name="rwkv_post", extra_inputs=[y_tile])
    (oc_s,) = _rows_call(_rwkv_post_kernel, [r_s, kx_s, vx_s, g_s, y_s], [vec], (RWKV_WIDTH,), tm_s,
                         name="rwkv_post")
    rwkv_out = (s_b, s_d, pc_b.reshape(bsz, seq, RWKV_COLS)[:, -1], pc_s[n_meta:])

    ln_g, ln_b = prm['ln_g'][l].astype(F32), prm['ln_b'][l].astype(F32)
    merge_consts = [w_gate, prm['w_branch'][l].astype(BF16), prm['w_out'][l].astype(BF16), ln_g[0:1], ln_b[0:1]]
    moe_layer = l % 2 == 1
    merge_outs = (D_MODEL,) + (((SUBLANES, LANES),) if moe_layer else ())
    hb, *hb_tiles = _rows_call(_merge_kernel, [hb, oa_b, ob_b, oc_b], merge_consts, merge_outs, tm_b, name="merge")
    hs, *hs_tiles = _rows_call(_merge_kernel, [hs, oa_s, ob_s, oc_s], merge_consts, merge_outs, tm_s, name="merge")

    if not moe_layer:
        w_ffn_in = prm['ffn_w_in'][l // 2].astype(BF16)[None]
        w_ffn_down = prm['ffn_w_down'][l // 2].astype(BF16)[None]
        blk_b = min(FFN_ROWS, hb.shape[0])
        hb = _ffn(hb, jnp.zeros((hb.shape[0] // blk_b,), jnp.int32), w_ffn_in, w_ffn_down,
                  ln_g[1:2], ln_b[1:2], blk_b, FFN_F_TILE, post_ln=True)
        hs = _ffn(hs, jnp.zeros((1,), jnp.int32), w_ffn_in, w_ffn_down, ln_g[1:2], ln_b[1:2], tm_s, FFN_F_TILE,
                  post_ln=True)
    else:
        router_pad = jnp.pad(prm['moe_router'][l // 2].astype(F32), ((0, 0), (0, LANES - N_EXPERTS)))
        w_moe_in = prm['moe_w_in'][l // 2].astype(BF16)
        w_moe_down = prm['moe_w_down'][l // 2].astype(BF16)
        hb = _moe(hb, hb_tiles[0], router_pad, w_moe_in, w_moe_down, ln_g[1:2], ln_b[1:2], tm_b, MOE_ROWS,
                  MOE_F_TILE)
        hs = _moe(hs, hs_tiles[0], router_pad, w_moe_in, w_moe_down, ln_g[1:2], ln_b[1:2], tm_s, MOE_ROWS_SMALL,
                  MOE_F_TILE)
    return hb, hs, attn_out, ssm_out, rwkv_out


def kernel(x_prompt, x_sample, cache_meta_k, cache_meta_v, cache_win_k, cache_win_v, state_ssm_re, state_ssm_im, state_wkv, state_shift, meta_tokens, ln_in_g, ln_in_b, w_in, rel_bias, attn_sinks, ssm_a_re, ssm_a_im, ssm_log_dt, ssm_b_re, ssm_b_im, ssm_c_re, ssm_c_im, ssm_d, ssm_w_glu, rwkv_mu, rwkv_vec, rwkv_w2, rwkv_a2, rwkv_g2, w_branch, w_out, ln_g, ln_b, ffn_w_in, ffn_w_down, moe_router, moe_w_in, moe_w_down):
    bsz, seq, _ = x_prompt.shape
    dec = x_sample.shape[0]
    assert x_sample.shape[1] == 1 and seq % (2 * RWKV_CHUNK) == 0
    prm = dict(ln_in_g=ln_in_g.astype(F32), ln_in_b=ln_in_b.astype(F32), w_in=w_in, rel_bias=rel_bias,
               attn_sinks=attn_sinks, ssm_a_re=ssm_a_re, ssm_a_im=ssm_a_im, ssm_log_dt=ssm_log_dt,
               ssm_b_re=ssm_b_re, ssm_b_im=ssm_b_im, ssm_c_re=ssm_c_re, ssm_c_im=ssm_c_im, ssm_d=ssm_d,
               ssm_w_glu=ssm_w_glu, rwkv_mu=rwkv_mu, rwkv_vec=rwkv_vec, rwkv_w2=rwkv_w2, rwkv_a2=rwkv_a2,
               rwkv_g2=rwkv_g2, w_branch=w_branch, w_out=w_out, ln_g=ln_g, ln_b=ln_b, ffn_w_in=ffn_w_in,
               ffn_w_down=ffn_w_down, moe_router=moe_router, moe_w_in=moe_w_in, moe_w_down=moe_w_down)
    hb = x_prompt.reshape(bsz * seq, D_MODEL).astype(F32)
    meta = jnp.broadcast_to(meta_tokens.astype(F32)[None], (bsz, N_META, D_MODEL)).reshape(bsz * N_META, D_MODEL)
    hs = jnp.concatenate([meta, x_sample.reshape(dec, D_MODEL).astype(F32)], axis=0)
    cache = (cache_meta_k, cache_meta_v, cache_win_k, cache_win_v)
    states = (state_ssm_re, state_ssm_im, state_wkv, state_shift)
    attn_outs, ssm_outs, rwkv_outs = [], [], []
    for l in range(DEPTH):
        hb, hs, a_o, s_o, r_o = _layer(l, hb, hs, l == 0, bsz, seq, dec, cache, states, prm)
        attn_outs.append(a_o)
        ssm_outs.append(s_o)
        rwkv_outs.append(r_o)
    stack = lambda outs, i: jnp.stack([o[i] for o in outs])
    y_prompt = hb.reshape(bsz, seq, D_MODEL)
    y_sample = hs[bsz * N_META:].reshape(dec, 1, D_MODEL)
    return (y_prompt, y_sample,
            stack(attn_outs, 0), stack(attn_outs, 1), stack(attn_outs, 2), stack(attn_outs, 3),
            stack(attn_outs, 4), stack(attn_outs, 5),
            stack(ssm_outs, 0), stack(ssm_outs, 1), stack(ssm_outs, 2), stack(ssm_outs, 3),
            stack(rwkv_outs, 0), stack(rwkv_outs, 1), stack(rwkv_outs, 2), stack(rwkv_outs, 3))
```

```python
import functools
import math

import numpy as np
import jax
import jax.numpy as jnp
from jax import lax
from jax.experimental import pallas as pl
from jax.experimental.pallas import tpu as pltpu

F32 = jnp.float32
BF16 = jnp.bfloat16

D_MODEL = 1024
DEPTH = 2
PAST_LEN = 16384
N_META = 16
WINDOW = 128
N_HEADS = 8
N_KV_HEADS = 2
HEAD_DIM = 64
Q_PER_KV = N_HEADS // N_KV_HEADS
ATTN_W = N_HEADS * HEAD_DIM
KV_W = N_KV_HEADS * HEAD_DIM
ATTN_SCALE = HEAD_DIM ** -0.5
REL_BUCKETS = 32
REL_EXACT = REL_BUCKETS // 2
REL_MAX_DIST = 128
SSM_GROUP = 16
SSM_GROUPS = 16
SSM_WIDTH = SSM_GROUP * SSM_GROUPS
SSM_STATE = 64
SSM_N = SSM_GROUPS * SSM_STATE
RWKV_HEAD = 64
RWKV_HEADS = 4
RWKV_WIDTH = RWKV_HEAD * RWKV_HEADS
RWKV_W_LORA = 32
RWKV_A_LORA = 32
RWKV_G_LORA = 64
RWKV_LORA = RWKV_W_LORA + RWKV_A_LORA + RWKV_G_LORA
RWKV_COLS = 3 * RWKV_WIDTH + RWKV_LORA
RV_W0, RV_A0, RV_KK, RV_KA, RV_RK, RV_GNW, RV_GNB = 0, 1, 2, 3, 4, 5, 6
N_BRANCH = 3
MIX_COLS = ATTN_W + 2 * KV_W + SSM_WIDTH + RWKV_COLS
N_EXPERTS = 8
TOP_K = 2
LN_EPS = 1e-5
RWKV_GN_EPS = 64e-5
NEG_INF = -1e30
DEEPNORM_ALPHA = (2 * DEPTH) ** 0.25

LANES = 128
SUBLANES = 8
VMEM_LIMIT = 48 * 1024 * 1024
RWKV_CHUNK = 64

ROW_TILE = 512
FFN_ROWS, FFN_F_TILE = 1024, 256
MOE_F_TILE = 512
MOE_ROWS, MOE_ROWS_SMALL = 1008, 96
SAMPLE_ATTN_BLOCK = 8
RWKV_STEP_PAIRS = 64


def _params(sem):
    return pltpu.CompilerParams(dimension_semantics=sem, vmem_limit_bytes=VMEM_LIMIT)


def _dot(a, b):
    return jnp.dot(a.astype(BF16), b.astype(BF16), preferred_element_type=F32)


def _dot_nt(a, b):
    return lax.dot_general(a.astype(BF16), b.astype(BF16), (((1,), (1,)), ((), ())),
                           preferred_element_type=F32)


def _split3(x):
    h1 = x.astype(BF16)
    r1 = x - h1.astype(F32)
    h2 = r1.astype(BF16)
    h3 = (r1 - h2.astype(F32)).astype(BF16)
    return h1, h2, h3


def _dot_exact_rhs(x, m):
    h1, h2, h3 = _split3(x)
    dot = functools.partial(jnp.dot, preferred_element_type=F32)
    return dot(h1, m) + dot(h2, m) + dot(h3, m)


def _dot_exact_lhs(m, x):
    h1, h2, h3 = _split3(x)
    dot = functools.partial(jnp.dot, preferred_element_type=F32)
    return dot(m, h1) + dot(m, h2) + dot(m, h3)


def _layer_norm(x, g, b):
    mu = jnp.mean(x, axis=-1, keepdims=True)
    xc = x - mu
    var = jnp.mean(xc * xc, axis=-1, keepdims=True)
    return xc * lax.rsqrt(var + LN_EPS) * g + b


def _sigmoid(x):
    return 1.0 / (1.0 + jnp.exp(-x))


def _rows_to_tiles(x):
    slabs = [x[:, t * LANES:(t + 1) * LANES] for t in range(SUBLANES)]
    return jnp.swapaxes(jnp.stack(slabs, axis=0), 0, 1)


def _tiles_to_cols(x):
    xt = jnp.swapaxes(x, 0, 1)
    return [xt[t] for t in range(SUBLANES)]


def _rows_call(body, row_inputs, const_inputs, out_widths, tm, out_dtypes=None, name=None, extra_inputs=()):
    n = row_inputs[0].shape[0]
    tm = min(tm, n)
    assert n % tm == 0, (n, tm)
    out_dtypes = out_dtypes or [F32] * len(out_widths)
    in_specs = [pl.BlockSpec((tm, a.shape[1]), lambda i: (i, 0)) for a in row_inputs]
    in_specs += [pl.BlockSpec(shape, fn) for _, shape, fn in extra_inputs]
    row_inputs = list(row_inputs) + [a for a, _, _ in extra_inputs]
    in_specs += [pl.BlockSpec(c.shape, lambda i, nd=c.ndim: (0,) * nd) for c in const_inputs]
    tails = [w if isinstance(w, tuple) else (w,) for w in out_widths]
    out_specs = [pl.BlockSpec((tm,) + w, lambda i, nd=len(w): (i,) + (0,) * nd) for w in tails]
    out_shape = [jax.ShapeDtypeStruct((n,) + w, dt) for w, dt in zip(tails, out_dtypes)]
    return pl.pallas_call(
        body, grid=(n // tm,), in_specs=in_specs, out_specs=out_specs, out_shape=out_shape,
        compiler_params=_params(("parallel",)), name=name,
    )(*row_inputs, *const_inputs)


PROJ_WIDTHS = (ATTN_W, KV_W, KV_W, SSM_WIDTH, RWKV_COLS)


def _proj_kernel(x_ref, g_ref, b_ref, w_ref, *out_refs, pre_ln):
    x = x_ref[...]
    if pre_ln:
        x = _layer_norm(x, g_ref[...], b_ref[...])
        out_refs[0][...] = x
        out_refs = out_refs[1:]
    xb = x.astype(BF16)
    col = 0
    for o_ref in out_refs:
        n = o_ref.shape[-1]
        o_ref[...] = jnp.dot(xb, w_ref[:, col:col + n], preferred_element_type=F32)
        col += n


def _proj(x, ln_g, ln_b, w_mix, pre_ln, tm):
    widths = ((D_MODEL,) if pre_ln else ()) + PROJ_WIDTHS
    outs = _rows_call(functools.partial(_proj_kernel, pre_ln=pre_ln), [x], [ln_g, ln_b, w_mix],
                      widths, tm, name="proj")
    if pre_ln:
        return outs[0], outs[1:]
    return x, outs


def _attn_kernel(q_ref, k_ref, v_ref, bias_ref, sink_ref, o_ref):
    units = [(bb, h) for bb in range(q_ref.shape[0]) for h in range(N_KV_HEADS)]
    hs = [slice(h * HEAD_DIM, (h + 1) * HEAD_DIM) for h in range(N_KV_HEADS)]
    q = {u: q_ref[u[0], u[1]].astype(BF16) for u in units}
    k = {u: k_ref[u[0], :, hs[u[1]]].astype(BF16) for u in units}
    v = {u: v_ref[u[0], :, hs[u[1]]].astype(BF16) for u in units}
    s = {u: lax.dot_general(q[u], k[u], (((1,), (1,)), ((), ())), preferred_element_type=F32) for u in units}
    s = {u: s[u] * ATTN_SCALE + bias_ref[u[1]] for u in units}
    m = {u: jnp.maximum(jnp.max(s[u], axis=-1, keepdims=True), sink_ref[u[1]]) for u in units}
    p = {u: jnp.exp(s[u] - m[u]) for u in units}
    den = {u: jnp.sum(p[u], axis=-1, keepdims=True) + jnp.exp(sink_ref[u[1]] - m[u]) for u in units}
    o = {u: jnp.dot(p[u].astype(BF16), v[u], preferred_element_type=F32) for u in units}
    for u in units:
        o_ref[u[0], u[1]] = o[u] / den[u]


def _attention(q, k, v, bias, sinks, bblk):
    p, _, mq, _ = q.shape
    nk = k.shape[1]
    assert p % bblk == 0
    kv_spec = pl.BlockSpec((bblk, nk, KV_W), lambda i: (i, 0, 0))
    qo_spec = pl.BlockSpec((bblk, N_KV_HEADS, mq, HEAD_DIM), lambda i: (i, 0, 0, 0))
    return pl.pallas_call(
        _attn_kernel, grid=(p // bblk,),
        in_specs=[qo_spec, kv_spec, kv_spec, pl.BlockSpec((N_KV_HEADS, mq, nk), lambda i: (0, 0, 0)),
                  pl.BlockSpec((N_KV_HEADS, mq, 1), lambda i: (0, 0, 0))],
        out_specs=qo_spec, out_shape=jax.ShapeDtypeStruct(q.shape, F32),
        compiler_params=_params(("parallel",)), name="attention",
    )(q, k, v, bias, sinks)


def _t5_bucket(dist):
    n = np.maximum(dist, 0)
    scaled = (np.log(np.maximum(n, 1).astype(np.float32) / np.float32(REL_EXACT))
              / np.float32(math.log(REL_MAX_DIST / REL_EXACT)) * np.float32(REL_BUCKETS - REL_EXACT))
    frac = np.abs(scaled - np.round(scaled))
    assert np.all((n <= REL_EXACT) | (n >= REL_MAX_DIST) | (frac > 1e-3))
    large = np.minimum(REL_EXACT + scaled.astype(np.int32), REL_BUCKETS - 1)
    return np.where(n < REL_EXACT, n, large)


def _bias_table(rel_bias, dist, valid, mq_pad=None, nk_pad=None):
    tq, nk = dist.shape
    onehot = np.eye(REL_BUCKETS, dtype=np.float32)[_t5_bucket(dist).reshape(-1)]
    bias = jnp.dot(jnp.asarray(onehot), rel_bias.astype(F32), precision=lax.Precision.HIGHEST)
    bias = bias.reshape(tq, nk, N_HEADS)
    bias = jnp.where(jnp.asarray(valid)[..., None], bias, NEG_INF)
    bias = jnp.moveaxis(bias, -1, 0).reshape(N_KV_HEADS, Q_PER_KV * tq, nk)
    mq_pad = mq_pad or Q_PER_KV * tq
    nk_pad = nk_pad or nk
    bias = jnp.pad(bias, ((0, 0), (0, mq_pad - Q_PER_KV * tq), (0, 0)))
    return jnp.pad(bias, ((0, 0), (0, 0), (0, nk_pad - nk)), constant_values=NEG_INF)


def _sink_rows(sinks, tq, mq_pad=None):
    s = jnp.repeat(sinks.astype(F32).reshape(N_KV_HEADS, Q_PER_KV, 1), tq, axis=2)
    s = s.reshape(N_KV_HEADS, Q_PER_KV * tq, 1)
    mq_pad = mq_pad or Q_PER_KV * tq
    return jnp.pad(s, ((0, 0), (0, mq_pad - Q_PER_KV * tq), (0, 0)))


def _heads_q(q, nb, tq):
    q = q.reshape(nb, tq, N_KV_HEADS, Q_PER_KV, HEAD_DIM)
    return jnp.transpose(q, (0, 2, 3, 1, 4)).reshape(nb, N_KV_HEADS, Q_PER_KV * tq, HEAD_DIM)


def _unheads_o(o, nb, tq):
    o = o[:, :, :Q_PER_KV * tq].reshape(nb, N_KV_HEADS, Q_PER_KV, tq, HEAD_DIM)
    return jnp.transpose(o, (0, 3, 1, 2, 4)).reshape(nb * tq, ATTN_W)


BODY_KEYS = N_META + 2 * WINDOW + 16


def _body_attn_kernel(q_ref, ko_ref, kp_ref, km_ref, vo_ref, vp_ref, vm_ref, bias_ref, o_ref):
    kv_heads = range(N_KV_HEADS)
    hs = [slice(h * HEAD_DIM, (h + 1) * HEAD_DIM) for h in kv_heads]
    heads = [[h * Q_PER_KV + g for g in range(Q_PER_KV)] for h in kv_heads]
    pad = jnp.zeros((BODY_KEYS - N_META - 2 * WINDOW, HEAD_DIM), F32)
    ones = jnp.ones((BODY_KEYS, HEAD_DIM), BF16)
    k = [jnp.concatenate([km_ref[:, hs[h]], kp_ref[:, hs[h]], ko_ref[:, hs[h]], pad], axis=0).astype(BF16)
         for h in kv_heads]
    v = [jnp.concatenate([vm_ref[:, hs[h]], vp_ref[:, hs[h]], vo_ref[:, hs[h]], pad], axis=0).astype(BF16)
         for h in kv_heads]
    q = [jnp.concatenate([q_ref[:, qh * HEAD_DIM:(qh + 1) * HEAD_DIM] for qh in heads[h]], axis=0).astype(BF16)
         for h in kv_heads]
    s = [lax.dot_general(q[h], k[h], (((1,), (1,)), ((), ())), preferred_element_type=F32) for h in kv_heads]
    s = [s[h] * ATTN_SCALE + bias_ref[0, h] for h in kv_heads]
    p = [jnp.exp(s[h] - jnp.max(s[h], axis=-1, keepdims=True)).astype(BF16) for h in kv_heads]
    o = [jnp.dot(p[h], v[h], preferred_element_type=F32) / jnp.dot(p[h], ones, preferred_element_type=F32)
         for h in kv_heads]
    for h in kv_heads:
        for g, qh in enumerate(heads[h]):
            o_ref[:, qh * HEAD_DIM:(qh + 1) * HEAD_DIM] = o[h][g * WINDOW:(g + 1) * WINDOW]


def _body_attention(q, k, v, k_meta, v_meta, rel_bias, sinks, bsz, seq):
    nblk = seq // WINDOW
    i = np.arange(WINDOW)[:, None]
    c = np.arange(WINDOW)[None, :]
    sink_col = _sink_rows(sinks, WINDOW)
    tabs = []
    for m in (0, 1):
        q_pos = N_META + WINDOW * m + i
        meta_pos = np.arange(N_META)[None, :]
        dist = np.concatenate([q_pos - meta_pos, WINDOW + i - c, i - c], axis=1)
        valid = np.concatenate([np.ones((WINDOW, N_META), bool),
                                (c >= i) & (m > 0), c <= i], axis=1)
        tab = _bias_table(rel_bias, dist, valid, nk_pad=BODY_KEYS)
        tabs.append(tab.at[:, :, N_META + 2 * WINDOW].set(sink_col[:, :, 0]))
    bias = jnp.stack(tabs)
    nk = BODY_KEYS
    mq = Q_PER_KV * WINDOW
    own = lambda w: pl.BlockSpec((WINDOW, w), lambda b, m: (b * nblk + m, 0))
    prev = lambda w: pl.BlockSpec((WINDOW, w), lambda b, m: (b * nblk + jnp.maximum(m - 1, 0), 0))
    meta = pl.BlockSpec((N_META, KV_W), lambda b, m: (b, 0))
    return pl.pallas_call(
        _body_attn_kernel, grid=(bsz, nblk),
        in_specs=[own(ATTN_W), own(KV_W), prev(KV_W), meta, own(KV_W), prev(KV_W), meta,
                  pl.BlockSpec((1, N_KV_HEADS, mq, nk), lambda b, m: (jnp.minimum(m, 1), 0, 0, 0))],
        out_specs=own(ATTN_W), out_shape=jax.ShapeDtypeStruct(q.shape, F32),
        compiler_params=_params(("parallel", "arbitrary")), name="body_attention",
    )(q, k, k, k_meta, v, v, v_meta, bias)


def _meta_attention(q, k, v, rel_bias, sinks, bsz):
    i = np.arange(N_META)
    dist = i[:, None] - i[None, :]
    bias = _bias_table(rel_bias, dist, dist >= 0)
    kv = lambda t: t.reshape(bsz, N_META, KV_W)
    o = _attention(_heads_q(q, bsz, N_META), kv(k), kv(v), bias, _sink_rows(sinks, N_META), bsz)
    return _unheads_o(o, bsz, N_META)


def _sample_attention(q, k_all, v_all, meta_k, meta_v, rel_bias, sinks):
    bsz = q.shape[0]
    wc = k_all.shape[1] - 1
    nk = N_META + wc + 1
    nk_pad = -(-nk // LANES) * LANES
    mq_pad = SUBLANES
    k_pos = np.concatenate([np.arange(N_META), PAST_LEN - wc + np.arange(wc + 1)])
    dist = (PAST_LEN - k_pos)[None, :]
    is_meta = (np.arange(nk) < N_META)[None, :]
    valid = (dist >= 0) & (is_meta | ((k_pos[None, :] >= N_META) & (dist <= WINDOW)))
    bias = _bias_table(rel_bias, dist, valid, mq_pad, nk_pad)

    def kv(meta, t):
        full = jnp.concatenate([meta.astype(F32), t], axis=1).reshape(bsz, nk, KV_W)
        return jnp.pad(full, ((0, 0), (0, nk_pad - nk), (0, 0)))

    qh = jnp.pad(_heads_q(q, bsz, 1), ((0, 0), (0, 0), (0, mq_pad - Q_PER_KV), (0, 0)))
    o = _attention(qh, kv(meta_k, k_all), kv(meta_v, v_all), bias, _sink_rows(sinks, 1, mq_pad),
                   SAMPLE_ATTN_BLOCK)
    return _unheads_o(o, bsz, 1)


def _s5_bu_kernel(u_ref, w_ref, re_ref, im_ref):
    r = jnp.dot(u_ref[...].astype(BF16), w_ref[...], preferred_element_type=F32)
    re_ref[...] = r[:, :SSM_N]
    im_ref[...] = r[:, SSM_N:]


def _s5_scan_kernel(u_ref, w_ref, h0r_ref, h0i_ref, ar_ref, ai_ref, hr_ref, hi_ref, fr_ref, fi_ref,
                    xr_ref, xi_ref, cr_ref, ci_ref):
    @pl.when(pl.program_id(1) == 0)
    def _():
        cr_ref[...] = h0r_ref[0]
        ci_ref[...] = h0i_ref[0]

    bu = jnp.dot(u_ref[...].astype(BF16), w_ref[...], preferred_element_type=F32)
    xr_ref[...] = bu[:, :SSM_N]
    xi_ref[...] = bu[:, SSM_N:]
    ar = ar_ref[...]
    ai = ai_ref[...]

    def step(t, carry):
        hr, hi = carry
        nr = ar * hr - ai * hi + xr_ref[pl.ds(t, 1), :]
        ni = ar * hi + ai * hr + xi_ref[pl.ds(t, 1), :]
        xr_ref[pl.ds(t, 1), :] = nr
        xi_ref[pl.ds(t, 1), :] = ni
        return nr, ni

    hr, hi = lax.fori_loop(0, xr_ref.shape[0], step, (cr_ref[...], ci_ref[...]), unroll=8)
    cr_ref[...] = hr
    ci_ref[...] = hi
    hr_ref[...] = xr_ref[...].astype(BF16)
    hi_ref[...] = xi_ref[...].astype(BF16)

    @pl.when(pl.program_id(1) == pl.num_programs(1) - 1)
    def _():
        fr_ref[0] = hr
        fi_ref[0] = hi


def _s5_scan(u, w_b, h0r, h0i, ar, ai, bsz, seq, tt):
    tt = min(tt, seq)
    nt = seq // tt
    row = lambda w: pl.BlockSpec((tt, w), lambda b, t: (b * nt + t, 0))
    st = pl.BlockSpec((1, 1, SSM_N), lambda b, t: (b, 0, 0))
    cst = lambda a: pl.BlockSpec(a.shape, lambda b, t: (0, 0))
    return pl.pallas_call(
        _s5_scan_kernel, grid=(bsz, nt),
        in_specs=[row(SSM_WIDTH), cst(w_b), st, st, cst(ar), cst(ai)],
        out_specs=[row(SSM_N), row(SSM_N), st, st],
        out_shape=[jax.ShapeDtypeStruct((bsz * seq, SSM_N), BF16)] * 2 + [jax.ShapeDtypeStruct(h0r.shape, F32)] * 2,
        scratch_shapes=[pltpu.VMEM((tt, SSM_N), F32)] * 2 + [pltpu.VMEM((1, SSM_N), F32)] * 2,
        compiler_params=_params(("arbitrary", "arbitrary")), name="s5_scan",
    )(u, w_b, h0r, h0i, ar, ai)


def _s5_step_kernel(bur_ref, bui_ref, h0r_ref, h0i_ref, ar_ref, ai_ref, hr_ref, hi_ref):
    ar, ai, hr, hi = ar_ref[...], ai_ref[...], h0r_ref[...], h0i_ref[...]
    hr_ref[...] = ar * hr - ai * hi + bur_ref[...]
    hi_ref[...] = ar * hi + ai * hr + bui_ref[...]


def _s5_out_kernel(u_ref, hr_ref, hi_ref, wc_ref, d_ref, wg_ref, o_ref):
    y = (_dot(hr_ref[...], wc_ref[:SSM_N]) + _dot(hi_ref[...], wc_ref[SSM_N:])
         + d_ref[...] * u_ref[...])
    z = jax.nn.gelu(y)
    o_ref[...] = z * _sigmoid(_dot(z, wg_ref[...]))


def _block_diag(blocks):
    g, a, b = blocks.shape
    eye = jnp.eye(g, dtype=blocks.dtype)
    return (eye[:, None, :, None] * blocks[:, :, None, :]).reshape(g * a, g * b)


def _s5_weights(a_re, a_im, log_dt, b_re, b_im, c_re, c_im):
    a_re = a_re.astype(F32)
    a_im = a_im.astype(F32)
    dt = jnp.exp(log_dt.astype(F32))[:, None]
    mag = jnp.exp(a_re * dt)
    ab_re = mag * jnp.cos(a_im * dt)
    ab_im = mag * jnp.sin(a_im * dt)
    den = a_re * a_re + a_im * a_im
    nr = ab_re - 1.0
    cf_re = (nr * a_re + ab_im * a_im) / den
    cf_im = (ab_im * a_re - nr * a_im) / den
    b_re = b_re.astype(F32)
    b_im = b_im.astype(F32)
    bb_re = cf_re[..., None] * b_re - cf_im[..., None] * b_im
    bb_im = cf_re[..., None] * b_im + cf_im[..., None] * b_re
    w_b = jnp.concatenate([_block_diag(jnp.swapaxes(bb_re, 1, 2)),
                           _block_diag(jnp.swapaxes(bb_im, 1, 2))], axis=1)
    w_c = jnp.concatenate([_block_diag(jnp.swapaxes(c_re.astype(F32), 1, 2)),
                           -_block_diag(jnp.swapaxes(c_im.astype(F32), 1, 2))], axis=0)
    return ab_re.reshape(1, SSM_N), ab_im.reshape(1, SSM_N), w_b.astype(BF16), w_c.astype(BF16)


def _seg_ones():
    r = lax.broadcasted_iota(jnp.int32, (RWKV_WIDTH, RWKV_WIDTH), 0) // RWKV_HEAD
    c = lax.broadcasted_iota(jnp.int32, (RWKV_WIDTH, RWKV_WIDTH), 1) // RWKV_HEAD
    return (r == c).astype(BF16)


def _rwkv_prep_kernel(pc_ref, prev_ref, mu_ref, vec_ref, w2_ref, a2_ref, g2_ref, *out_refs):
    outs = _rwkv_prep(pc_ref[...], prev_ref[...], mu_ref[...], vec_ref[...], w2_ref[...], a2_ref[...], g2_ref[...])
    for o_ref, val in zip(out_refs, outs):
        o_ref[...] = val


def _rwkv_prep(pc, prev, mu, vec, w2, a2, g2):
    xm = pc + (prev - pc) * mu
    rw = RWKV_WIDTH
    xr, xk, xv, xl = xm[:, :rw], xm[:, rw:2 * rw], xm[:, 2 * rw:3 * rw], xm[:, 3 * rw:]
    wpre = -(vec[RV_W0:RV_W0 + 1] + _dot(jnp.tanh(xl), w2))
    softplus = jnp.maximum(wpre, 0.0) + jnp.log(1.0 + jnp.exp(-jnp.abs(wpre)))
    lw = -jnp.exp(-softplus - 0.5)
    a = _sigmoid(vec[RV_A0:RV_A0 + 1] + _dot(xl, a2))
    g = _dot(_sigmoid(xl), g2)
    kk = xk * vec[RV_KK:RV_KK + 1]
    norm = jnp.sqrt(_dot_exact_rhs(kk * kk, _seg_ones()))
    kk = kk / jnp.maximum(norm, 1e-12)
    return xr, xk * (1.0 + (a - 1.0) * vec[RV_KA:RV_KA + 1]), xv, lw, kk, kk * a, g


def _rwkv_post_kernel(r_ref, k_ref, v_ref, g_ref, y_ref, vec_ref, o_ref):
    o_ref[...] = _rwkv_post(y_ref[...], r_ref[...], k_ref[...], v_ref[...], g_ref[...], vec_ref[...])


def _rwkv_post(y, r, k, v, g, vec):
    ones = _seg_ones()
    yc = y - _dot_exact_rhs(y, ones) * (1.0 / RWKV_HEAD)
    yv = _dot_exact_rhs(yc * yc, ones) * (1.0 / RWKV_HEAD)
    yn = yc * lax.rsqrt(yv + RWKV_GN_EPS) * vec[RV_GNW:RV_GNW + 1] + vec[RV_GNB:RV_GNB + 1]
    bonus = _dot_exact_rhs(r * k * vec[RV_RK:RV_RK + 1], ones) * v
    return (yn + bonus) * g


def _rwkv_seq_kernel(*refs, chunk, nb, valid_len):
    pc_refs = refs[:nb]
    first_ref, s0_ref, mu_ref, vec_ref, w2_ref, a2_ref, g2_ref, o_ref, so_ref, s_ref, above_ref = refs[nb:]
    t = pl.program_id(0)

    @pl.when(t == 0)
    def _():
        s_ref[...] = s0_ref[...]
        above_ref[...] = first_ref[...]

    tb = pc_refs[0].shape[0]
    vec = vec_ref[...]
    row_id = lax.broadcasted_iota(jnp.int32, (tb, 1), 0)
    r_v, k_v, v_v, lw_v, kk_v, bb_v, g_v = [], [], [], [], [], [], []
    for b in range(nb):
        pc = pc_refs[b][...]
        prev = jnp.where(row_id == 0, above_ref[b:b + 1, :], pltpu.roll(pc, 1, axis=0))
        above_ref[b:b + 1, :] = pc[tb - 1:tb, :]
        vals = _rwkv_prep(pc, prev, mu_ref[...], vec, w2_ref[...], a2_ref[...], g2_ref[...])
        if valid_len < tb:
            keep = (row_id < valid_len).astype(F32)
            vals = tuple(val * keep for val in vals)
        for dst, val in zip((r_v, k_v, v_v, lw_v, kk_v, bb_v, g_v), vals):
            dst.append(val)

    c = chunk
    row = lax.broadcasted_iota(jnp.int32, (c, c), 0)
    col = lax.broadcasted_iota(jnp.int32, (c, c), 1)
    incl = (row >= col).astype(F32)
    strict = (row > col).astype(F32)
    eye = (row == col).astype(F32)
    nsub = tb // c
    heads = range(RWKV_HEADS)
    hsl = [slice(h * RWKV_HEAD, (h + 1) * RWKV_HEAD) for h in heads]
    tsl = [slice(sub * c, (sub + 1) * c) for sub in range(nsub)]
    seqs = [(b, h) for b in range(nb) for h in heads]
    units = [(b, sub, h) for b in range(nb) for sub in range(nsub) for h in heads]

    scaled = {}
    for b in range(nb):
        for sub in range(nsub):
            lw = lw_v[b][tsl[sub], :]
            cum = _dot_exact_lhs(incl.astype(BF16), lw)
            ecum = jnp.exp(cum)
            einv = jnp.exp(-cum)
            scaled[b, sub] = (kk_v[b][tsl[sub], :] * jnp.exp(cum - lw), bb_v[b][tsl[sub], :] * einv,
                              k_v[b][tsl[sub], :] * einv, r_v[b][tsl[sub], :] * ecum, ecum[c - 1:c, :])
    kt = {u: scaled[u[0], u[1]][0][:, hsl[u[2]]] for u in units}
    bt = {u: scaled[u[0], u[1]][1][:, hsl[u[2]]] for u in units}
    kkt = {u: scaled[u[0], u[1]][2][:, hsl[u[2]]] for u in units}
    rt = {u: scaled[u[0], u[1]][3][:, hsl[u[2]]] for u in units}
    v_t = [v_v[b].T for b in range(nb)]
    vt = {u: v_t[u[0]][hsl[u[2]], tsl[u[1]]] for u in units}
    a_b = {u: strict * _dot_nt(kt[u], bt[u]) for u in units}
    a_k = {u: strict * _dot_nt(kt[u], kkt[u]) for u in units}
    r_b = {u: incl * _dot_nt(rt[u], bt[u]) for u in units}
    r_k = {u: incl * _dot_nt(rt[u], kkt[u]) for u in units}
    pw = {u: -a_b[u] for u in units}
    tinv = {u: eye + pw[u] for u in units}
    n = 1
    while 2 * n < c:
        pw = {u: _dot(pw[u], pw[u]) for u in units}
        tinv = {u: tinv[u] + _dot(tinv[u], pw[u]) for u in units}
        n *= 2
    x = {u: _dot_nt(vt[u], a_k[u]) for u in units}
    w1t = {u: _dot_nt(x[u], tinv[u]) for u in units}
    w2 = {u: _dot(tinv[u], kt[u]) for u in units}
    yt_local = {u: _dot_nt(vt[u], r_k[u]) for u in units}
    s_local = {u: _dot(vt[u], kkt[u]) for u in units}

    s = {q: s_ref[q[0], q[1]] for q in seqs}
    yt = {}
    for sub in range(nsub):
        ut = {(b, h): -(_dot_nt(s[b, h], w2[b, sub, h]) + w1t[b, sub, h]) for b, h in seqs}
        for b, h in seqs:
            yt[b, sub, h] = (_dot_nt(s[b, h], rt[b, sub, h]) + _dot_nt(ut[b, h], r_b[b, sub, h])
                             + yt_local[b, sub, h])
        s = {(b, h): (s[b, h] + _dot(ut[b, h], bt[b, sub, h]) + s_local[b, sub, h]) * scaled[b, sub][4][:, hsl[h]]
             for b, h in seqs}
    for b, h in seqs:
        s_ref[b, h] = s[b, h]
    for b in range(nb):
        y_t = jnp.concatenate([jnp.concatenate([yt[b, sub, h] for sub in range(nsub)], axis=1) for h in heads],
                              axis=0)
        o_ref[:, b * RWKV_WIDTH:(b + 1) * RWKV_WIDTH] = _rwkv_post(y_t.T, r_v[b], k_v[b], v_v[b], g_v[b], vec)

    @pl.when(t == pl.num_programs(0) - 1)
    def _():
        so_ref[...] = s_ref[...]


def _rwkv_seq(pc, first, s0, consts, bsz, seq, valid_len):
    tb = 2 * RWKV_CHUNK
    assert seq % tb == 0 and (valid_len == seq or seq == tb)
    nt = seq // tb
    tiles = [pl.BlockSpec((tb, RWKV_COLS), lambda t, b=b: (b * nt + t, 0)) for b in range(bsz)]
    whole = lambda a: pl.BlockSpec(a.shape, lambda t, nd=a.ndim: (0,) * nd)
    return pl.pallas_call(
        functools.partial(_rwkv_seq_kernel, chunk=RWKV_CHUNK, nb=bsz, valid_len=min(valid_len, tb)), grid=(nt,),
        in_specs=tiles + [whole(first), whole(s0)] + [whole(c) for c in consts],
        out_specs=[pl.BlockSpec((tb, bsz * RWKV_WIDTH), lambda t: (t, 0)), whole(s0)],
        out_shape=[jax.ShapeDtypeStruct((seq, bsz * RWKV_WIDTH), F32), jax.ShapeDtypeStruct(s0.shape, F32)],
        scratch_shapes=[pltpu.VMEM(s0.shape, F32), pltpu.VMEM((bsz, RWKV_COLS), F32)],
        compiler_params=_params(("arbitrary",)), name="rwkv_seq",
    )(*([pc] * bsz), first, s0, *consts)


def _rwkv_step_kernel(s_ref, r_ref, k_ref, lw_ref, kk_ref, bb_ref, v_ref, so_ref, y_ref):
    s = s_ref[...]
    sa = jnp.sum(s * (-kk_ref[...]), axis=-1, keepdims=True)
    s = s * jnp.exp(lw_ref[...]) + sa * bb_ref[...] + v_ref[...] * k_ref[...]
    so_ref[...] = s
    y_ref[...] = jnp.sum(s * r_ref[...], axis=-1, keepdims=True)


def _rwkv_step(r, k, v, lw, kk, bb, s0):
    bsz = r.shape[0]
    p = bsz * RWKV_HEADS
    nb = min(RWKV_STEP_PAIRS, p)
    rowv = lambda a: a.reshape(p, 1, RWKV_HEAD)
    rs = pl.BlockSpec((nb, 1, RWKV_HEAD), lambda i: (i, 0, 0))
    cs = pl.BlockSpec((nb, RWKV_HEAD, 1), lambda i: (i, 0, 0))
    ss = pl.BlockSpec((nb, RWKV_HEAD, RWKV_HEAD), lambda i: (i, 0, 0))
    s_out, y = pl.pallas_call(
        _rwkv_step_kernel, grid=(p // nb,), in_specs=[ss, rs, rs, rs, rs, rs, cs], out_specs=[ss, cs],
        out_shape=[jax.ShapeDtypeStruct((p, RWKV_HEAD, RWKV_HEAD), F32),
                   jax.ShapeDtypeStruct((p, RWKV_HEAD, 1), F32)],
        compiler_params=_params(("parallel",)), name="rwkv_step",
    )(s0.reshape(p, RWKV_HEAD, RWKV_HEAD), rowv(r), rowv(k), rowv(lw), rowv(kk), rowv(bb),
      v.reshape(p, RWKV_HEAD, 1))
    return y.reshape(bsz, RWKV_WIDTH), s_out.reshape(s0.shape)


def _merge_kernel(h_ref, oa_ref, ob_ref, oc_ref, wg_ref, wb_ref, wo_ref, g_ref, b_ref, o_ref, *tile_refs):
    h = h_ref[...]
    gates = _sigmoid(jnp.dot(h.astype(BF16), wg_ref[...], preferred_element_type=F32))
    d = D_MODEL
    merged = (gates[:, :d] * _dot(oa_ref[...], wb_ref[:ATTN_W])
              + gates[:, d:2 * d] * _dot(ob_ref[...], wb_ref[ATTN_W:ATTN_W + SSM_WIDTH])
              + gates[:, 2 * d:] * _dot(oc_ref[...], wb_ref[ATTN_W + SSM_WIDTH:]))
    mix = _dot(merged, wo_ref[...])
    out = _layer_norm(DEEPNORM_ALPHA * h + mix, g_ref[...], b_ref[...])
    o_ref[...] = out
    for t_ref in tile_refs:
        t_ref[...] = _rows_to_tiles(out)


def _ffn_kernel(be_ref, x_ref, wg_ref, wu_ref, wd_ref, g_ref, b_ref, o_ref, xb_ref, acc_ref, *, post_ln):
    del be_ref
    j = pl.program_id(1)

    @pl.when(j == 0)
    def _():
        xb_ref[...] = x_ref[...].astype(BF16)
        acc_ref[...] = jnp.zeros_like(acc_ref)

    xb = xb_ref[...]
    gate = jnp.dot(xb, wg_ref[0], preferred_element_type=F32)
    up = jnp.dot(xb, wu_ref[0], preferred_element_type=F32)
    act = gate * _sigmoid(gate) * up
    acc_ref[...] += jnp.dot(act.astype(BF16), wd_ref[0], preferred_element_type=F32)

    @pl.when(j == pl.num_programs(1) - 1)
    def _():
        if post_ln:
            o_ref[...] = _layer_norm(DEEPNORM_ALPHA * x_ref[...] + acc_ref[...], g_ref[...], b_ref[...])
        else:
            o_ref[...] = acc_ref[...]


def _ffn(x, block_e, w_in, w_down, ln_g, ln_b, blk, tf, post_ln):
    rows = x.shape[0]
    assert rows % blk == 0
    f = w_down.shape[1]
    nf = f // tf
    grid_spec = pltpu.PrefetchScalarGridSpec(
        num_scalar_prefetch=1, grid=(rows // blk, nf),
        in_specs=[
            pl.BlockSpec((blk, D_MODEL), lambda i, j, be: (i, 0)),
            pl.BlockSpec((1, D_MODEL, tf), lambda i, j, be: (be[i], 0, j)),
            pl.BlockSpec((1, D_MODEL, tf), lambda i, j, be: (be[i], 0, nf + j)),
            pl.BlockSpec((1, tf, D_MODEL), lambda i, j, be: (be[i], j, 0)),
            pl.BlockSpec((1, D_MODEL), lambda i, j, be: (0, 0)),
            pl.BlockSpec((1, D_MODEL), lambda i, j, be: (0, 0)),
        ],
        out_specs=pl.BlockSpec((blk, D_MODEL), lambda i, j, be: (i, 0)),
        scratch_shapes=[pltpu.VMEM((blk, D_MODEL), BF16), pltpu.VMEM((blk, D_MODEL), F32)],
    )
    return pl.pallas_call(
        functools.partial(_ffn_kernel, post_ln=post_ln), grid_spec=grid_spec,
        out_shape=jax.ShapeDtypeStruct((rows, D_MODEL), F32),
        compiler_params=_params(("arbitrary", "arbitrary")), name="ffn",
    )(block_e, x, w_in, w_in, w_down, ln_g, ln_b)


def _router_kernel(h_ref, w_ref, e_ref, g_ref):
    logits = jnp.dot(h_ref[...], w_ref[...], preferred_element_type=F32, precision=lax.Precision.HIGHEST)
    lane = lax.broadcasted_iota(jnp.int32, logits.shape, 1)
    lg = jnp.where(lane < N_EXPERTS, logits, -jnp.inf)
    m1 = jnp.max(lg, axis=-1, keepdims=True)
    i1 = jnp.min(jnp.where(lg == m1, lane, LANES), axis=-1, keepdims=True)
    lg2 = jnp.where(lane == i1, -jnp.inf, lg)
    m2 = jnp.max(lg2, axis=-1, keepdims=True)
    i2 = jnp.min(jnp.where(lg2 == m2, lane, LANES), axis=-1, keepdims=True)
    e2 = jnp.exp(m2 - m1)
    den = 1.0 + e2
    e_ref[...] = jnp.where(lane == 0, i1, jnp.where(lane == 1, i2, 0))
    g_ref[...] = jnp.where(lane == 0, 1.0 / den, jnp.where(lane == 1, e2 / den, 0.0))


def _moe_ffn_kernel(be_ref, nv_ref, first_ref, nxt_ref, prev_dst_ref, last_dst_ref, h_hbm, wg_ref, wu_ref, wd_ref,
                    out_hbm, xbuf, xb_ref, acc_ref, stage, gsem, ssem):
    del be_ref
    i, j = pl.program_id(0), pl.program_id(1)
    nblk, nf = pl.num_programs(0), pl.num_programs(1)
    blk = xb_ref.shape[0]
    n_valid = nv_ref[0]
    valid = i < n_valid
    slot = lax.rem(i, 2)
    spare_base = out_hbm.shape[0] - blk

    def gather_row(idx_ref, r, s):
        pltpu.make_async_copy(h_hbm.at[idx_ref[0, 0, r]], xbuf.at[s, r], gsem.at[s]).start(priority=0)

    def scatter_row(r, d):
        pltpu.make_async_copy(stage.at[r], out_hbm.at[d], ssem).start(priority=1)

    def wait_gather(s):
        pltpu.make_async_copy(h_hbm.at[pl.ds(0, blk)], xbuf.at[s], gsem.at[s]).wait()

    def wait_scatter():
        pltpu.make_async_copy(stage, out_hbm.at[pl.ds(0, blk)], ssem).wait()

    def for_rows(fn):
        def body(r, c):
            fn(r)
            return c
        lax.fori_loop(0, blk, body, 0, unroll=8)

    def compute():
        xb = xb_ref[...]
        gate = jnp.dot(xb, wg_ref[0], preferred_element_type=F32)
        up = jnp.dot(xb, wu_ref[0], preferred_element_type=F32)
        act = gate * _sigmoid(gate) * up
        acc_ref[...] += jnp.dot(act.astype(BF16), wd_ref[0], preferred_element_type=F32)

    @pl.when((i == 0) & (j == 0))
    def _():
        for_rows(lambda r: gather_row(first_ref, r, 0))
        stage[...] = jnp.zeros_like(stage)

    @pl.when(valid & (j == 0))
    def _():
        wait_gather(slot)
        for t, cols in enumerate(_tiles_to_cols(xbuf[slot])):
            xb_ref[:, t * LANES:(t + 1) * LANES] = cols.astype(BF16)
        acc_ref[...] = jnp.zeros_like(acc_ref)

    @pl.when(valid & (j < nf - 1))
    def _():
        compute()
        per_step = blk // (nf - 1)
        for rr in range(per_step):
            r = j * per_step + rr
            gather_row(nxt_ref, r, 1 - slot)
            scatter_row(r, jnp.where(i > 0, prev_dst_ref[0, 0, r], spare_base + r))

    @pl.when(valid & (j == nf - 1))
    def _():
        compute()
        wait_scatter()
        stage[...] = _rows_to_tiles(acc_ref[...])

    @pl.when((i == nblk - 1) & (j == nf - 1))
    def _():
        for_rows(lambda r: scatter_row(r, last_dst_ref[0, 0, r]))
        wait_scatter()
        wait_gather(lax.rem(n_valid, 2))


def _moe_ffn(h_tiles, rows_tok, rows_dst, block_e, n_valid, w_in, w_down, blk, tf):
    n = h_tiles.shape[0]
    rows = rows_tok.shape[0]
    nblk = rows // blk
    nf = w_down.shape[1] // tf
    assert blk % (nf - 1) == 0 and n >= blk
    idx3 = lambda a: a.reshape(nblk, 1, blk)
    smem = lambda fn: pl.BlockSpec((1, 1, blk), fn, memory_space=pltpu.SMEM)
    ftile = lambda i, j, nv: jnp.where(i < nv[0], j, nf - 1)
    grid_spec = pltpu.PrefetchScalarGridSpec(
        num_scalar_prefetch=2, grid=(nblk, nf),
        in_specs=[
            smem(lambda i, j, be, nv: (0, 0, 0)),
            smem(lambda i, j, be, nv: (jnp.minimum(i + 1, nblk - 1), 0, 0)),
            smem(lambda i, j, be, nv: (jnp.maximum(i - 1, 0), 0, 0)),
            smem(lambda i, j, be, nv: (nv[0] - 1, 0, 0)),
            pl.BlockSpec(memory_space=pl.ANY),
            pl.BlockSpec((1, D_MODEL, tf), lambda i, j, be, nv: (be[i], 0, ftile(i, j, nv))),
            pl.BlockSpec((1, D_MODEL, tf), lambda i, j, be, nv: (be[i], 0, nf + ftile(i, j, nv))),
            pl.BlockSpec((1, tf, D_MODEL), lambda i, j, be, nv: (be[i], ftile(i, j, nv), 0)),
        ],
        out_specs=pl.BlockSpec(memory_space=pl.ANY),
        scratch_shapes=[pltpu.VMEM((2, blk, SUBLANES, LANES), F32), pltpu.VMEM((blk, D_MODEL), BF16),
                        pltpu.VMEM((blk, D_MODEL), F32), pltpu.VMEM((blk, SUBLANES, LANES), F32),
                        pltpu.SemaphoreType.DMA((2,)), pltpu.SemaphoreType.DMA(())],
    )
    return pl.pallas_call(
        _moe_ffn_kernel, grid_spec=grid_spec,
        out_shape=jax.ShapeDtypeStruct((TOP_K * n + blk, SUBLANES, LANES), F32),
        compiler_params=_params(("arbitrary", "arbitrary")), name="moe_ffn",
    )(block_e, n_valid, idx3(rows_tok), idx3(rows_tok), idx3(rows_dst), idx3(rows_dst), h_tiles,
      w_in, w_in, w_down)


def _combine_kernel(h_ref, y0_ref, y1_ref, gate_ref, g_ref, bias_ref, o_ref):
    gate = gate_ref[...]
    rows = lambda y_ref: jnp.concatenate(_tiles_to_cols(y_ref[...]), axis=-1)
    f = rows(y0_ref) * gate[:, 0:1] + rows(y1_ref) * gate[:, 1:2]
    o_ref[...] = _layer_norm(DEEPNORM_ALPHA * h_ref[...] + f, g_ref[...], bias_ref[...])


def _moe(h, h_tiles, router_pad, w_in, w_down, ln_g, ln_b, tm, blk, tf):
    n = h.shape[0]
    e_pad, gate = _rows_call(_router_kernel, [h], [router_pad], (LANES, LANES), tm,
                             out_dtypes=[jnp.int32, F32], name="router")
    flat_e = jnp.concatenate([e_pad[:, s] for s in range(TOP_K)])
    n_assign = n * TOP_K
    n_blocks = -(-(n_assign + N_EXPERTS * (blk - 1)) // blk)
    experts = jnp.arange(N_EXPERTS, dtype=jnp.int32)
    onehot = (flat_e[:, None] == experts[None, :]).astype(jnp.int32)
    csum = jnp.cumsum(onehot, axis=0)
    counts = csum[-1]
    padded = (counts + blk - 1) // blk * blk
    pad_end = jnp.cumsum(padded)
    dest = jnp.sum((csum - 1 + (pad_end - padded)[None, :]) * onehot, axis=1)
    assign = jnp.arange(n_assign, dtype=jnp.int32)
    rows_dst = jnp.full((n_blocks * blk,), -1, jnp.int32).at[dest].set(assign, unique_indices=True)
    rows_tok = jnp.maximum(rows_dst, 0) % n
    spare = n_assign + jnp.arange(n_blocks * blk, dtype=jnp.int32) % blk
    rows_dst = jnp.where(rows_dst < 0, spare, rows_dst)
    block_start = jnp.arange(n_blocks, dtype=jnp.int32) * blk
    block_e = jnp.minimum(jnp.sum((block_start[:, None] >= pad_end[None, :]).astype(jnp.int32), axis=1),
                          N_EXPERTS - 1)
    n_valid = (pad_end[-1:] // blk).astype(jnp.int32)
    y = _moe_ffn(h_tiles, rows_tok, rows_dst, block_e, n_valid, w_in, w_down, blk, tf)
    tm = min(tm, n)
    nt = n // tm
    cst = pl.BlockSpec((1, D_MODEL), lambda i: (0, 0))
    slot = lambda s: pl.BlockSpec((tm, SUBLANES, LANES), lambda i: (i + s * nt, 0, 0))
    return pl.pallas_call(
        _combine_kernel, grid=(nt,),
        in_specs=[pl.BlockSpec((tm, D_MODEL), lambda i: (i, 0)), slot(0), slot(1),
                  pl.BlockSpec((tm, LANES), lambda i: (i, 0)), cst, cst],
        out_specs=pl.BlockSpec((tm, D_MODEL), lambda i: (i, 0)),
        out_shape=jax.ShapeDtypeStruct((n, D_MODEL), F32),
        compiler_params=_params(("parallel",)), name="moe_combine",
    )(h, y, y, gate, ln_g, ln_b)


def _layer(l, hb, hs, pre_ln, bsz, seq, dec, cache, states, prm):
    (cache_meta_k, cache_meta_v, cache_win_k, cache_win_v) = cache
    (state_ssm_re, state_ssm_im, state_wkv, state_shift) = states
    n_meta = bsz * N_META
    tm_b, tm_s = ROW_TILE, hs.shape[0]
    w_in = prm['w_in'][l]
    w_mix = w_in[:, :MIX_COLS].astype(BF16)
    w_gate = w_in[:, MIX_COLS:].astype(BF16)
    ln_in_g, ln_in_b = prm['ln_in_g'].reshape(1, -1), prm['ln_in_b'].reshape(1, -1)

    hb, (q_b, k_b, v_b, u_b, pc_b) = _proj(hb, ln_in_g, ln_in_b, w_mix, pre_ln, tm_b)
    hs, (q_s, k_s, v_s, u_s, pc_s) = _proj(hs, ln_in_g, ln_in_b, w_mix, pre_ln, tm_s)

    rel_bias, sinks = prm['rel_bias'], prm['attn_sinks'][l]
    k_meta, v_meta = k_s[:n_meta], v_s[:n_meta]
    oa_b = _body_attention(q_b, k_b, v_b, k_meta, v_meta, rel_bias, sinks, bsz, seq)
    oa_m = _meta_attention(q_s[:n_meta], k_meta, v_meta, rel_bias, sinks, bsz)
    kd = lambda t: t[n_meta:].reshape(dec, 1, N_KV_HEADS, HEAD_DIM)
    k_all = jnp.concatenate([cache_win_k[l].astype(F32), kd(k_s)], axis=1)
    v_all = jnp.concatenate([cache_win_v[l].astype(F32), kd(v_s)], axis=1)
    oa_d = _sample_attention(q_s[n_meta:], k_all, v_all, cache_meta_k[l], cache_meta_v[l], rel_bias, sinks)
    oa_s = jnp.concatenate([oa_m, oa_d], axis=0)
    kv4 = lambda t, b: t.reshape(b, -1, N_KV_HEADS, HEAD_DIM)
    tail = lambda t: kv4(jnp.concatenate([t[(b + 1) * seq - WINDOW:(b + 1) * seq] for b in range(bsz)], axis=0), bsz)
    attn_out = (kv4(k_meta, bsz), kv4(v_meta, bsz), tail(k_b), tail(v_b), k_all[:, 1:], v_all[:, 1:])

    ar, ai, w_b, w_c = _s5_weights(prm['ssm_a_re'][l], prm['ssm_a_im'][l], prm['ssm_log_dt'][l],
                                   prm['ssm_b_re'][l], prm['ssm_b_im'][l], prm['ssm_c_re'][l], prm['ssm_c_im'][l])
    zero_h = jnp.zeros((bsz, 1, SSM_N), F32)
    hr_m, hi_m, fr_m, fi_m = _s5_scan(u_s[:n_meta], w_b, zero_h, zero_h, ar, ai, bsz, N_META, N_META)
    hr_b, hi_b, fr_b, fi_b = _s5_scan(u_b, w_b, fr_m, fi_m, ar, ai, bsz, seq, ROW_TILE)
    bur_d, bui_d = _rows_call(_s5_bu_kernel, [u_s[n_meta:]], [w_b], (SSM_N, SSM_N), dec, name="s5_bu")
    hr_d, hi_d = _rows_call(_s5_step_kernel,
                            [bur_d, bui_d, state_ssm_re[l].reshape(dec, SSM_N).astype(F32),
                             state_ssm_im[l].reshape(dec, SSM_N).astype(F32)], [ar, ai], (SSM_N, SSM_N), dec,
                            name="s5_step")
    hr_s = jnp.concatenate([hr_m, hr_d.astype(BF16)], axis=0)
    hi_s = jnp.concatenate([hi_m, hi_d.astype(BF16)], axis=0)
    s5_consts = [w_c, prm['ssm_d'][l].reshape(1, -1).astype(F32), prm['ssm_w_glu'][l].astype(BF16)]
    (ob_b,) = _rows_call(_s5_out_kernel, [u_b, hr_b, hi_b], s5_consts, (SSM_WIDTH,), tm_b, name="s5_out")
    (ob_s,) = _rows_call(_s5_out_kernel, [u_s, hr_s, hi_s], s5_consts, (SSM_WIDTH,), tm_s, name="s5_out")
    st4 = lambda t, b: t.reshape(b, SSM_GROUPS, SSM_STATE)
    ssm_out = (st4(fr_b, bsz), st4(fi_b, bsz), st4(hr_d, dec), st4(hi_d, dec))

    pad_rows = lambda w, lo: jnp.pad(w.astype(F32), ((lo, RWKV_LORA - lo - w.shape[0]), (0, 0))).astype(BF16)
    vec = jnp.pad(prm['rwkv_vec'][l].astype(F32), ((0, 1), (0, 0)))
    prep_consts = [prm['rwkv_mu'][l].reshape(1, -1).astype(F32), vec,
                   pad_rows(prm['rwkv_w2'][l], 0), pad_rows(prm['rwkv_a2'][l], RWKV_W_LORA),
                   pad_rows(prm['rwkv_g2'][l], RWKV_W_LORA + RWKV_A_LORA)]
    pc_m = pc_s[:n_meta].reshape(bsz, N_META, RWKV_COLS)
    meta_len = 2 * RWKV_CHUNK
    pc_m_pad = jnp.pad(pc_m, ((0, 0), (0, meta_len - N_META), (0, 0))).reshape(bsz * meta_len, RWKV_COLS)
    zero_s = jnp.zeros((bsz, RWKV_HEADS, RWKV_HEAD, RWKV_HEAD), F32)
    oc_m, s_m = _rwkv_seq(pc_m_pad, jnp.zeros((bsz, RWKV_COLS), F32), zero_s, prep_consts, bsz, meta_len, N_META)
    oc_m = jnp.swapaxes(oc_m[:N_META].reshape(N_META, bsz, RWKV_WIDTH), 0, 1).reshape(n_meta, RWKV_WIDTH)
    oc_b, s_b = _rwkv_seq(pc_b, pc_m[:, -1], s_m, prep_consts, bsz, seq, seq)
    pc_d = pc_s[n_meta:]
    r_d, k_d, v_d, lw_d, kk_d, bb_d, g_d = _rows_call(_rwkv_prep_kernel, [pc_d, state_shift[l].astype(F32)],
                                                      prep_consts, (RWKV_WIDTH,) * 7, dec, name="rwkv_prep")
    y_d, s_d = _rwkv_step(r_d, k_d, v_d, lw_d, kk_d, bb_d, state_wkv[l].astype(F32))
    (oc_d,) = _rows_call(_rwkv_post_kernel, [r_d, k_d, v_d, g_d, y_d], [vec], (RWKV_WIDTH,), dec, name="rwkv_post")
    oc_s = jnp.concatenate([oc_m, oc_d], axis=0)
    rwkv_out = (s_b, s_d, pc_b.reshape(bsz, seq, RWKV_COLS)[:, -1], pc_d)

    ln_g, ln_b = prm['ln_g'][l].astype(F32), prm['ln_b'][l].astype(F32)
    merge_consts = [w_gate, prm['w_branch'][l].astype(BF16), prm['w_out'][l].astype(BF16), ln_g[0:1], ln_b[0:1]]
    moe_layer = l % 2 == 1
    merge_outs = (D_MODEL,) + (((SUBLANES, LANES),) if moe_layer else ())
    tm_merge = min(tm_b, seq)
    oc_tile = (oc_b, (tm_merge, RWKV_WIDTH), lambda i: (i % (seq // tm_merge), i // (seq // tm_merge)))
    hb, *hb_tiles = _rows_call(_merge_kernel, [hb, oa_b, ob_b], merge_consts, merge_outs, tm_merge, name="merge",
                               extra_inputs=[oc_tile])
    hs, *hs_tiles = _rows_call(_merge_kernel, [hs, oa_s, ob_s, oc_s], merge_consts, merge_outs, tm_s, name="merge")

    if not moe_layer:
        w_ffn_in = prm['ffn_w_in'][l // 2].astype(BF16)[None]
        w_ffn_down = prm['ffn_w_down'][l // 2].astype(BF16)[None]
        blk_b = min(FFN_ROWS, hb.shape[0])
        hb = _ffn(hb, jnp.zeros((hb.shape[0] // blk_b,), jnp.int32), w_ffn_in, w_ffn_down,
                  ln_g[1:2], ln_b[1:2], blk_b, FFN_F_TILE, post_ln=True)
        hs = _ffn(hs, jnp.zeros((1,), jnp.int32), w_ffn_in, w_ffn_down, ln_g[1:2], ln_b[1:2], tm_s, FFN_F_TILE,
                  post_ln=True)
    else:
        router_pad = jnp.pad(prm['moe_router'][l // 2].astype(F32), ((0, 0), (0, LANES - N_EXPERTS)))
        w_moe_in = prm['moe_w_in'][l // 2].astype(BF16)
        w_moe_down = prm['moe_w_down'][l // 2].astype(BF16)
        hb = _moe(hb, hb_tiles[0], router_pad, w_moe_in, w_moe_down, ln_g[1:2], ln_b[1:2], tm_b, MOE_ROWS,
                  MOE_F_TILE)
        hs = _moe(hs, hs_tiles[0], router_pad, w_moe_in, w_moe_down, ln_g[1:2], ln_b[1:2], tm_s, MOE_ROWS_SMALL,
                  MOE_F_TILE)
    return hb, hs, attn_out, ssm_out, rwkv_out


def kernel(x_prompt, x_sample, cache_meta_k, cache_meta_v, cache_win_k, cache_win_v, state_ssm_re, state_ssm_im, state_wkv, state_shift, meta_tokens, ln_in_g, ln_in_b, w_in, rel_bias, attn_sinks, ssm_a_re, ssm_a_im, ssm_log_dt, ssm_b_re, ssm_b_im, ssm_c_re, ssm_c_im, ssm_d, ssm_w_glu, rwkv_mu, rwkv_vec, rwkv_w2, rwkv_a2, rwkv_g2, w_branch, w_out, ln_g, ln_b, ffn_w_in, ffn_w_down, moe_router, moe_w_in, moe_w_down):
    bsz, seq, _ = x_prompt.shape
    dec = x_sample.shape[0]
    assert x_sample.shape[1] == 1 and seq % (2 * RWKV_CHUNK) == 0
    prm = dict(ln_in_g=ln_in_g.astype(F32), ln_in_b=ln_in_b.astype(F32), w_in=w_in, rel_bias=rel_bias,
               attn_sinks=attn_sinks, ssm_a_re=ssm_a_re, ssm_a_im=ssm_a_im, ssm_log_dt=ssm_log_dt,
               ssm_b_re=ssm_b_re, ssm_b_im=ssm_b_im, ssm_c_re=ssm_c_re, ssm_c_im=ssm_c_im, ssm_d=ssm_d,
               ssm_w_glu=ssm_w_glu, rwkv_mu=rwkv_mu, rwkv_vec=rwkv_vec, rwkv_w2=rwkv_w2, rwkv_a2=rwkv_a2,
               rwkv_g2=rwkv_g2, w_branch=w_branch, w_out=w_out, ln_g=ln_g, ln_b=ln_b, ffn_w_in=ffn_w_in,
               ffn_w_down=ffn_w_down, moe_router=moe_router, moe_w_in=moe_w_in, moe_w_down=moe_w_down)
    hb = x_prompt.reshape(bsz * seq, D_MODEL).astype(F32)
    meta = jnp.broadcast_to(meta_tokens.astype(F32)[None], (bsz, N_META, D_MODEL)).reshape(bsz * N_META, D_MODEL)
    hs = jnp.concatenate([meta, x_sample.reshape(dec, D_MODEL).astype(F32)], axis=0)
    cache = (cache_meta_k, cache_meta_v, cache_win_k, cache_win_v)
    states = (state_ssm_re, state_ssm_im, state_wkv, state_shift)
    attn_outs, ssm_outs, rwkv_outs = [], [], []
    for l in range(DEPTH):
        hb, hs, a_o, s_o, r_o = _layer(l, hb, hs, l == 0, bsz, seq, dec, cache, states, prm)
        attn_outs.append(a_o)
        ssm_outs.append(s_o)
        rwkv_outs.append(r_o)
    stack = lambda outs, i: jnp.stack([o[i] for o in outs])
    y_prompt = hb.reshape(bsz, seq, D_MODEL)
    y_sample = hs[bsz * N_META:].reshape(dec, 1, D_MODEL)
    return (y_prompt, y_sample,
            stack(attn_outs, 0), stack(attn_outs, 1), stack(attn_outs, 2), stack(attn_outs, 3),
            stack(attn_outs, 4), stack(attn_outs, 5),
            stack(ssm_outs, 0), stack(ssm_outs, 1), stack(ssm_outs, 2), stack(ssm_outs, 3),
            stack(rwkv_outs, 0), stack(rwkv_outs, 1), stack(rwkv_outs, 2), stack(rwkv_outs, 3))
```

```python
import functools
import math

import numpy as np
import jax
import jax.numpy as jnp
from jax import lax
from jax.experimental import pallas as pl
from jax.experimental.pallas import tpu as pltpu

F32 = jnp.float32
BF16 = jnp.bfloat16

D_MODEL = 1024
DEPTH = 2
PAST_LEN = 16384
N_META = 16
WINDOW = 128
N_HEADS = 8
N_KV_HEADS = 2
HEAD_DIM = 64
Q_PER_KV = N_HEADS // N_KV_HEADS
ATTN_W = N_HEADS * HEAD_DIM
KV_W = N_KV_HEADS * HEAD_DIM
ATTN_SCALE = HEAD_DIM ** -0.5
REL_BUCKETS = 32
REL_EXACT = REL_BUCKETS // 2
REL_MAX_DIST = 128
SSM_GROUP = 16
SSM_GROUPS = 16
SSM_WIDTH = SSM_GROUP * SSM_GROUPS
SSM_STATE = 64
SSM_N = SSM_GROUPS * SSM_STATE
RWKV_HEAD = 64
RWKV_HEADS = 4
RWKV_WIDTH = RWKV_HEAD * RWKV_HEADS
RWKV_W_LORA = 32
RWKV_A_LORA = 32
RWKV_G_LORA = 64
RWKV_LORA = RWKV_W_LORA + RWKV_A_LORA + RWKV_G_LORA
RWKV_COLS = 3 * RWKV_WIDTH + RWKV_LORA
RV_W0, RV_A0, RV_KK, RV_KA, RV_RK, RV_GNW, RV_GNB = 0, 1, 2, 3, 4, 5, 6
N_BRANCH = 3
MIX_COLS = ATTN_W + 2 * KV_W + SSM_WIDTH + RWKV_COLS
N_EXPERTS = 8
TOP_K = 2
LN_EPS = 1e-5
RWKV_GN_EPS = 64e-5
NEG_INF = -1e30
DEEPNORM_ALPHA = (2 * DEPTH) ** 0.25

LANES = 128
SUBLANES = 8
VMEM_LIMIT = 48 * 1024 * 1024
RWKV_CHUNK = 64

ROW_TILE = 512
FFN_ROWS, FFN_F_TILE = 1024, 256
MOE_F_TILE = 512
MOE_ROWS, MOE_ROWS_SMALL = 1008, 96
SAMPLE_ATTN_BLOCK = 8
RWKV_STEP_PAIRS = 64


def _params(sem):
    return pltpu.CompilerParams(dimension_semantics=sem, vmem_limit_bytes=VMEM_LIMIT)


def _dot(a, b):
    return jnp.dot(a.astype(BF16), b.astype(BF16), preferred_element_type=F32)


def _dot_nt(a, b):
    return lax.dot_general(a.astype(BF16), b.astype(BF16), (((1,), (1,)), ((), ())),
                           preferred_element_type=F32)


def _split3(x):
    h1 = x.astype(BF16)
    r1 = x - h1.astype(F32)
    h2 = r1.astype(BF16)
    h3 = (r1 - h2.astype(F32)).astype(BF16)
    return h1, h2, h3


def _dot_exact_rhs(x, m):
    h1, h2, h3 = _split3(x)
    dot = functools.partial(jnp.dot, preferred_element_type=F32)
    return dot(h1, m) + dot(h2, m) + dot(h3, m)


def _dot_exact_lhs(m, x):
    h1, h2, h3 = _split3(x)
    dot = functools.partial(jnp.dot, preferred_element_type=F32)
    return dot(m, h1) + dot(m, h2) + dot(m, h3)


def _layer_norm(x, g, b):
    mu = jnp.mean(x, axis=-1, keepdims=True)
    xc = x - mu
    var = jnp.mean(xc * xc, axis=-1, keepdims=True)
    return xc * lax.rsqrt(var + LN_EPS) * g + b


def _sigmoid(x):
    return 1.0 / (1.0 + jnp.exp(-x))


def _rows_to_tiles(x):
    slabs = [x[:, t * LANES:(t + 1) * LANES] for t in range(SUBLANES)]
    return jnp.swapaxes(jnp.stack(slabs, axis=0), 0, 1)


def _tiles_to_cols(x):
    xt = jnp.swapaxes(x, 0, 1)
    return [xt[t] for t in range(SUBLANES)]


def _rows_call(body, row_inputs, const_inputs, out_widths, tm, out_dtypes=None, name=None, extra_inputs=()):
    n = row_inputs[0].shape[0]
    tm = min(tm, n)
    assert n % tm == 0, (n, tm)
    out_dtypes = out_dtypes or [F32] * len(out_widths)
    in_specs = [pl.BlockSpec((tm, a.shape[1]), lambda i: (i, 0)) for a in row_inputs]
    in_specs += [pl.BlockSpec(shape, fn) for _, shape, fn in extra_inputs]
    row_inputs = list(row_inputs) + [a for a, _, _ in extra_inputs]
    in_specs += [pl.BlockSpec(c.shape, lambda i, nd=c.ndim: (0,) * nd) for c in const_inputs]
    tails = [w if isinstance(w, tuple) else (w,) for w in out_widths]
    out_specs = [pl.BlockSpec((tm,) + w, lambda i, nd=len(w): (i,) + (0,) * nd) for w in tails]
    out_shape = [jax.ShapeDtypeStruct((n,) + w, dt) for w, dt in zip(tails, out_dtypes)]
    return pl.pallas_call(
        body, grid=(n // tm,), in_specs=in_specs, out_specs=out_specs, out_shape=out_shape,
        compiler_params=_params(("parallel",)), name=name,
    )(*row_inputs, *const_inputs)


PROJ_WIDTHS = (ATTN_W, KV_W, KV_W, SSM_WIDTH, RWKV_COLS)


def _proj_kernel(x_ref, g_ref, b_ref, w_ref, *out_refs, pre_ln):
    x = x_ref[...]
    if pre_ln:
        x = _layer_norm(x, g_ref[...], b_ref[...])
        out_refs[0][...] = x
        out_refs = out_refs[1:]
    xb = x.astype(BF16)
    col = 0
    for o_ref in out_refs:
        n = o_ref.shape[-1]
        o_ref[...] = jnp.dot(xb, w_ref[:, col:col + n], preferred_element_type=F32)
        col += n


def _proj(x, ln_g, ln_b, w_mix, pre_ln, tm):
    widths = ((D_MODEL,) if pre_ln else ()) + PROJ_WIDTHS
    outs = _rows_call(functools.partial(_proj_kernel, pre_ln=pre_ln), [x], [ln_g, ln_b, w_mix],
                      widths, tm, name="proj")
    if pre_ln:
        return outs[0], outs[1:]
    return x, outs


def _attn_kernel(q_ref, k_ref, v_ref, bias_ref, sink_ref, o_ref):
    units = [(bb, h) for bb in range(q_ref.shape[0]) for h in range(N_KV_HEADS)]
    hs = [slice(h * HEAD_DIM, (h + 1) * HEAD_DIM) for h in range(N_KV_HEADS)]
    q = {u: q_ref[u[0], u[1]].astype(BF16) for u in units}
    k = {u: k_ref[u[0], :, hs[u[1]]].astype(BF16) for u in units}
    v = {u: v_ref[u[0], :, hs[u[1]]].astype(BF16) for u in units}
    s = {u: lax.dot_general(q[u], k[u], (((1,), (1,)), ((), ())), preferred_element_type=F32) for u in units}
    s = {u: s[u] * ATTN_SCALE + bias_ref[u[1]] for u in units}
    m = {u: jnp.maximum(jnp.max(s[u], axis=-1, keepdims=True), sink_ref[u[1]]) for u in units}
    p = {u: jnp.exp(s[u] - m[u]) for u in units}
    den = {u: jnp.sum(p[u], axis=-1, keepdims=True) + jnp.exp(sink_ref[u[1]] - m[u]) for u in units}
    o = {u: jnp.dot(p[u].astype(BF16), v[u], preferred_element_type=F32) for u in units}
    for u in units:
        o_ref[u[0], u[1]] = o[u] / den[u]


def _attention(q, k, v, bias, sinks, bblk):
    p, _, mq, _ = q.shape
    nk = k.shape[1]
    assert p % bblk == 0
    kv_spec = pl.BlockSpec((bblk, nk, KV_W), lambda i: (i, 0, 0))
    qo_spec = pl.BlockSpec((bblk, N_KV_HEADS, mq, HEAD_DIM), lambda i: (i, 0, 0, 0))
    return pl.pallas_call(
        _attn_kernel, grid=(p // bblk,),
        in_specs=[qo_spec, kv_spec, kv_spec, pl.BlockSpec((N_KV_HEADS, mq, nk), lambda i: (0, 0, 0)),
                  pl.BlockSpec((N_KV_HEADS, mq, 1), lambda i: (0, 0, 0))],
        out_specs=qo_spec, out_shape=jax.ShapeDtypeStruct(q.shape, F32),
        compiler_params=_params(("parallel",)), name="attention",
    )(q, k, v, bias, sinks)


def _t5_bucket(dist):
    n = np.maximum(dist, 0)
    scaled = (np.log(np.maximum(n, 1).astype(np.float32) / np.float32(REL_EXACT))
              / np.float32(math.log(REL_MAX_DIST / REL_EXACT)) * np.float32(REL_BUCKETS - REL_EXACT))
    frac = np.abs(scaled - np.round(scaled))
    assert np.all((n <= REL_EXACT) | (n >= REL_MAX_DIST) | (frac > 1e-3))
    large = np.minimum(REL_EXACT + scaled.astype(np.int32), REL_BUCKETS - 1)
    return np.where(n < REL_EXACT, n, large)


def _bias_table(rel_bias, dist, valid, mq_pad=None, nk_pad=None):
    tq, nk = dist.shape
    onehot = np.eye(REL_BUCKETS, dtype=np.float32)[_t5_bucket(dist).reshape(-1)]
    bias = jnp.dot(jnp.asarray(onehot), rel_bias.astype(F32), precision=lax.Precision.HIGHEST)
    bias = bias.reshape(tq, nk, N_HEADS)
    bias = jnp.where(jnp.asarray(valid)[..., None], bias, NEG_INF)
    bias = jnp.moveaxis(bias, -1, 0).reshape(N_KV_HEADS, Q_PER_KV * tq, nk)
    mq_pad = mq_pad or Q_PER_KV * tq
    nk_pad = nk_pad or nk
    bias = jnp.pad(bias, ((0, 0), (0, mq_pad - Q_PER_KV * tq), (0, 0)))
    return jnp.pad(bias, ((0, 0), (0, 0), (0, nk_pad - nk)), constant_values=NEG_INF)


def _sink_rows(sinks, tq, mq_pad=None):
    s = jnp.repeat(sinks.astype(F32).reshape(N_KV_HEADS, Q_PER_KV, 1), tq, axis=2)
    s = s.reshape(N_KV_HEADS, Q_PER_KV * tq, 1)
    mq_pad = mq_pad or Q_PER_KV * tq
    return jnp.pad(s, ((0, 0), (0, mq_pad - Q_PER_KV * tq), (0, 0)))


def _heads_q(q, nb, tq):
    q = q.reshape(nb, tq, N_KV_HEADS, Q_PER_KV, HEAD_DIM)
    return jnp.transpose(q, (0, 2, 3, 1, 4)).reshape(nb, N_KV_HEADS, Q_PER_KV * tq, HEAD_DIM)


def _unheads_o(o, nb, tq):
    o = o[:, :, :Q_PER_KV * tq].reshape(nb, N_KV_HEADS, Q_PER_KV, tq, HEAD_DIM)
    return jnp.transpose(o, (0, 3, 1, 2, 4)).reshape(nb * tq, ATTN_W)


BODY_KEYS = N_META + 2 * WINDOW + 16


def _body_attn_kernel(q_ref, ko_ref, kp_ref, km_ref, vo_ref, vp_ref, vm_ref, bias_ref, o_ref):
    kv_heads = range(N_KV_HEADS)
    hs = [slice(h * HEAD_DIM, (h + 1) * HEAD_DIM) for h in kv_heads]
    heads = [[h * Q_PER_KV + g for g in range(Q_PER_KV)] for h in kv_heads]
    pad = jnp.zeros((BODY_KEYS - N_META - 2 * WINDOW, HEAD_DIM), F32)
    ones = jnp.ones((BODY_KEYS, HEAD_DIM), BF16)
    k = [jnp.concatenate([km_ref[:, hs[h]], kp_ref[:, hs[h]], ko_ref[:, hs[h]], pad], axis=0).astype(BF16)
         for h in kv_heads]
    v = [jnp.concatenate([vm_ref[:, hs[h]], vp_ref[:, hs[h]], vo_ref[:, hs[h]], pad], axis=0).astype(BF16)
         for h in kv_heads]
    q = [jnp.concatenate([q_ref[:, qh * HEAD_DIM:(qh + 1) * HEAD_DIM] for qh in heads[h]], axis=0).astype(BF16)
         for h in kv_heads]
    s = [lax.dot_general(q[h], k[h], (((1,), (1,)), ((), ())), preferred_element_type=F32) for h in kv_heads]
    s = [s[h] * ATTN_SCALE + bias_ref[0, h] for h in kv_heads]
    p = [jnp.exp(s[h] - jnp.max(s[h], axis=-1, keepdims=True)).astype(BF16) for h in kv_heads]
    o = [jnp.dot(p[h], v[h], preferred_element_type=F32) / jnp.dot(p[h], ones, preferred_element_type=F32)
         for h in kv_heads]
    for h in kv_heads:
        for g, qh in enumerate(heads[h]):
            o_ref[:, qh * HEAD_DIM:(qh + 1) * HEAD_DIM] = o[h][g * WINDOW:(g + 1) * WINDOW]


def _body_attention(q, k, v, k_meta, v_meta, rel_bias, sinks, bsz, seq):
    nblk = seq // WINDOW
    i = np.arange(WINDOW)[:, None]
    c = np.arange(WINDOW)[None, :]
    sink_col = _sink_rows(sinks, WINDOW)
    tabs = []
    for m in (0, 1):
        q_pos = N_META + WINDOW * m + i
        meta_pos = np.arange(N_META)[None, :]
        dist = np.concatenate([q_pos - meta_pos, WINDOW + i - c, i - c], axis=1)
        valid = np.concatenate([np.ones((WINDOW, N_META), bool),
                                (c >= i) & (m > 0), c <= i], axis=1)
        tab = _bias_table(rel_bias, dist, valid, nk_pad=BODY_KEYS)
        tabs.append(tab.at[:, :, N_META + 2 * WINDOW].set(sink_col[:, :, 0]))
    bias = jnp.stack(tabs)
    nk = BODY_KEYS
    mq = Q_PER_KV * WINDOW
    own = lambda w: pl.BlockSpec((WINDOW, w), lambda b, m: (b * nblk + m, 0))
    prev = lambda w: pl.BlockSpec((WINDOW, w), lambda b, m: (b * nblk + jnp.maximum(m - 1, 0), 0))
    meta = pl.BlockSpec((N_META, KV_W), lambda b, m: (b, 0))
    return pl.pallas_call(
        _body_attn_kernel, grid=(bsz, nblk),
        in_specs=[own(ATTN_W), own(KV_W), prev(KV_W), meta, own(KV_W), prev(KV_W), meta,
                  pl.BlockSpec((1, N_KV_HEADS, mq, nk), lambda b, m: (jnp.minimum(m, 1), 0, 0, 0))],
        out_specs=own(ATTN_W), out_shape=jax.ShapeDtypeStruct(q.shape, F32),
        compiler_params=_params(("parallel", "arbitrary")), name="body_attention",
    )(q, k, k, k_meta, v, v, v_meta, bias)


def _meta_attention(q, k, v, rel_bias, sinks, bsz):
    i = np.arange(N_META)
    dist = i[:, None] - i[None, :]
    bias = _bias_table(rel_bias, dist, dist >= 0)
    kv = lambda t: t.reshape(bsz, N_META, KV_W)
    o = _attention(_heads_q(q, bsz, N_META), kv(k), kv(v), bias, _sink_rows(sinks, N_META), bsz)
    return _unheads_o(o, bsz, N_META)


def _sample_attention(q, k_all, v_all, meta_k, meta_v, rel_bias, sinks):
    bsz = q.shape[0]
    wc = k_all.shape[1] - 1
    nk = N_META + wc + 1
    nk_pad = -(-nk // LANES) * LANES
    mq_pad = SUBLANES
    k_pos = np.concatenate([np.arange(N_META), PAST_LEN - wc + np.arange(wc + 1)])
    dist = (PAST_LEN - k_pos)[None, :]
    is_meta = (np.arange(nk) < N_META)[None, :]
    valid = (dist >= 0) & (is_meta | ((k_pos[None, :] >= N_META) & (dist <= WINDOW)))
    bias = _bias_table(rel_bias, dist, valid, mq_pad, nk_pad)

    def kv(meta, t):
        full = jnp.concatenate([meta.astype(F32), t], axis=1).reshape(bsz, nk, KV_W)
        return jnp.pad(full, ((0, 0), (0, nk_pad - nk), (0, 0)))

    qh = jnp.pad(_heads_q(q, bsz, 1), ((0, 0), (0, 0), (0, mq_pad - Q_PER_KV), (0, 0)))
    o = _attention(qh, kv(meta_k, k_all), kv(meta_v, v_all), bias, _sink_rows(sinks, 1, mq_pad),
                   SAMPLE_ATTN_BLOCK)
    return _unheads_o(o, bsz, 1)


def _s5_bu_kernel(u_ref, w_ref, re_ref, im_ref):
    r = jnp.dot(u_ref[...].astype(BF16), w_ref[...], preferred_element_type=F32)
    re_ref[...] = r[:, :SSM_N]
    im_ref[...] = r[:, SSM_N:]


def _s5_scan_kernel(u_ref, w_ref, h0r_ref, h0i_ref, ar_ref, ai_ref, wc_ref, d_ref, wg_ref, o_ref, fr_ref, fi_ref,
                    xr_ref, xi_ref, cr_ref, ci_ref):
    @pl.when(pl.program_id(1) == 0)
    def _():
        cr_ref[...] = h0r_ref[0]
        ci_ref[...] = h0i_ref[0]

    bu = jnp.dot(u_ref[...].astype(BF16), w_ref[...], preferred_element_type=F32)
    xr_ref[...] = bu[:, :SSM_N]
    xi_ref[...] = bu[:, SSM_N:]
    ar = ar_ref[...]
    ai = ai_ref[...]

    def step(t, carry):
        hr, hi = carry
        nr = ar * hr - ai * hi + xr_ref[pl.ds(t, 1), :]
        ni = ar * hi + ai * hr + xi_ref[pl.ds(t, 1), :]
        xr_ref[pl.ds(t, 1), :] = nr
        xi_ref[pl.ds(t, 1), :] = ni
        return nr, ni

    hr, hi = lax.fori_loop(0, xr_ref.shape[0], step, (cr_ref[...], ci_ref[...]), unroll=8)
    cr_ref[...] = hr
    ci_ref[...] = hi
    o_ref[...] = _s5_output(u_ref[...], xr_ref[...], xi_ref[...], wc_ref, d_ref[...], wg_ref[...])

    @pl.when(pl.program_id(1) == pl.num_programs(1) - 1)
    def _():
        fr_ref[0] = hr
        fi_ref[0] = hi


def _s5_scan(u, w_b, h0r, h0i, ar, ai, out_consts, bsz, seq, tt):
    tt = min(tt, seq)
    nt = seq // tt
    row = pl.BlockSpec((tt, SSM_WIDTH), lambda b, t: (b * nt + t, 0))
    st = pl.BlockSpec((1, 1, SSM_N), lambda b, t: (b, 0, 0))
    cst = lambda a: pl.BlockSpec(a.shape, lambda b, t: (0, 0))
    return pl.pallas_call(
        _s5_scan_kernel, grid=(bsz, nt),
        in_specs=[row, cst(w_b), st, st, cst(ar), cst(ai)] + [cst(c) for c in out_consts],
        out_specs=[row, st, st],
        out_shape=[jax.ShapeDtypeStruct(u.shape, F32)] + [jax.ShapeDtypeStruct(h0r.shape, F32)] * 2,
        scratch_shapes=[pltpu.VMEM((tt, SSM_N), F32)] * 2 + [pltpu.VMEM((1, SSM_N), F32)] * 2,
        compiler_params=_params(("arbitrary", "arbitrary")), name="s5_scan",
    )(u, w_b, h0r, h0i, ar, ai, *out_consts)


def _s5_step_kernel(bur_ref, bui_ref, h0r_ref, h0i_ref, ar_ref, ai_ref, hr_ref, hi_ref):
    ar, ai, hr, hi = ar_ref[...], ai_ref[...], h0r_ref[...], h0i_ref[...]
    hr_ref[...] = ar * hr - ai * hi + bur_ref[...]
    hi_ref[...] = ar * hi + ai * hr + bui_ref[...]


def _s5_out_kernel(u_ref, hr_ref, hi_ref, wc_ref, d_ref, wg_ref, o_ref):
    o_ref[...] = _s5_output(u_ref[...], hr_ref[...], hi_ref[...], wc_ref, d_ref[...], wg_ref[...])


def _s5_output(u, hr, hi, wc_ref, d, wg):
    y = _dot(hr, wc_ref[:SSM_N]) + _dot(hi, wc_ref[SSM_N:]) + d * u
    z = jax.nn.gelu(y)
    return z * _sigmoid(_dot(z, wg))


def _block_diag(blocks):
    g, a, b = blocks.shape
    eye = jnp.eye(g, dtype=blocks.dtype)
    return (eye[:, None, :, None] * blocks[:, :, None, :]).reshape(g * a, g * b)


def _s5_weights(a_re, a_im, log_dt, b_re, b_im, c_re, c_im):
    a_re = a_re.astype(F32)
    a_im = a_im.astype(F32)
    dt = jnp.exp(log_dt.astype(F32))[:, None]
    mag = jnp.exp(a_re * dt)
    ab_re = mag * jnp.cos(a_im * dt)
    ab_im = mag * jnp.sin(a_im * dt)
    den = a_re * a_re + a_im * a_im
    nr = ab_re - 1.0
    cf_re = (nr * a_re + ab_im * a_im) / den
    cf_im = (ab_im * a_re - nr * a_im) / den
    b_re = b_re.astype(F32)
    b_im = b_im.astype(F32)
    bb_re = cf_re[..., None] * b_re - cf_im[..., None] * b_im
    bb_im = cf_re[..., None] * b_im + cf_im[..., None] * b_re
    w_b = jnp.concatenate([_block_diag(jnp.swapaxes(bb_re, 1, 2)),
                           _block_diag(jnp.swapaxes(bb_im, 1, 2))], axis=1)
    w_c = jnp.concatenate([_block_diag(jnp.swapaxes(c_re.astype(F32), 1, 2)),
                           -_block_diag(jnp.swapaxes(c_im.astype(F32), 1, 2))], axis=0)
    return ab_re.reshape(1, SSM_N), ab_im.reshape(1, SSM_N), w_b.astype(BF16), w_c.astype(BF16)


def _seg_ones():
    r = lax.broadcasted_iota(jnp.int32, (RWKV_WIDTH, RWKV_WIDTH), 0) // RWKV_HEAD
    c = lax.broadcasted_iota(jnp.int32, (RWKV_WIDTH, RWKV_WIDTH), 1) // RWKV_HEAD
    return (r == c).astype(BF16)


def _rwkv_prep_kernel(pc_ref, prev_ref, mu_ref, vec_ref, w2_ref, a2_ref, g2_ref, *out_refs):
    outs = _rwkv_prep(pc_ref[...], prev_ref[...], mu_ref[...], vec_ref[...], w2_ref[...], a2_ref[...], g2_ref[...])
    for o_ref, val in zip(out_refs, outs):
        o_ref[...] = val


def _rwkv_prep(pc, prev, mu, vec, w2, a2, g2):
    xm = pc + (prev - pc) * mu
    rw = RWKV_WIDTH
    xr, xk, xv, xl = xm[:, :rw], xm[:, rw:2 * rw], xm[:, 2 * rw:3 * rw], xm[:, 3 * rw:]
    wpre = -(vec[RV_W0:RV_W0 + 1] + _dot(jnp.tanh(xl), w2))
    softplus = jnp.maximum(wpre, 0.0) + jnp.log(1.0 + jnp.exp(-jnp.abs(wpre)))
    lw = -jnp.exp(-softplus - 0.5)
    a = _sigmoid(vec[RV_A0:RV_A0 + 1] + _dot(xl, a2))
    g = _dot(_sigmoid(xl), g2)
    kk = xk * vec[RV_KK:RV_KK + 1]
    norm = jnp.sqrt(_dot_exact_rhs(kk * kk, _seg_ones()))
    kk = kk / jnp.maximum(norm, 1e-12)
    return xr, xk * (1.0 + (a - 1.0) * vec[RV_KA:RV_KA + 1]), xv, lw, kk, kk * a, g


def _rwkv_post_kernel(r_ref, k_ref, v_ref, g_ref, y_ref, vec_ref, o_ref):
    o_ref[...] = _rwkv_post(y_ref[...], r_ref[...], k_ref[...], v_ref[...], g_ref[...], vec_ref[...])


def _rwkv_post(y, r, k, v, g, vec):
    ones = _seg_ones()
    yc = y - _dot_exact_rhs(y, ones) * (1.0 / RWKV_HEAD)
    yv = _dot_exact_rhs(yc * yc, ones) * (1.0 / RWKV_HEAD)
    yn = yc * lax.rsqrt(yv + RWKV_GN_EPS) * vec[RV_GNW:RV_GNW + 1] + vec[RV_GNB:RV_GNB + 1]
    bonus = _dot_exact_rhs(r * k * vec[RV_RK:RV_RK + 1], ones) * v
    return (yn + bonus) * g


def _rwkv_seq_kernel(*refs, chunk, nb, valid_len):
    pc_refs = refs[:nb]
    first_ref, s0_ref, mu_ref, vec_ref, w2_ref, a2_ref, g2_ref, o_ref, so_ref, s_ref, above_ref = refs[nb:]
    t = pl.program_id(0)

    @pl.when(t == 0)
    def _():
        s_ref[...] = s0_ref[...]
        above_ref[...] = first_ref[...]

    tb = pc_refs[0].shape[0]
    vec = vec_ref[...]
    row_id = lax.broadcasted_iota(jnp.int32, (tb, 1), 0)
    r_v, k_v, v_v, lw_v, kk_v, bb_v, g_v = [], [], [], [], [], [], []
    for b in range(nb):
        pc = pc_refs[b][...]
        prev = jnp.where(row_id == 0, above_ref[b:b + 1, :], pltpu.roll(pc, 1, axis=0))
        above_ref[b:b + 1, :] = pc[tb - 1:tb, :]
        vals = _rwkv_prep(pc, prev, mu_ref[...], vec, w2_ref[...], a2_ref[...], g2_ref[...])
        if valid_len < tb:
            keep = (row_id < valid_len).astype(F32)
            vals = tuple(val * keep for val in vals)
        for dst, val in zip((r_v, k_v, v_v, lw_v, kk_v, bb_v, g_v), vals):
            dst.append(val)

    c = chunk
    row = lax.broadcasted_iota(jnp.int32, (c, c), 0)
    col = lax.broadcasted_iota(jnp.int32, (c, c), 1)
    incl = (row >= col).astype(F32)
    strict = (row > col).astype(F32)
    eye = (row == col).astype(F32)
    nsub = tb // c
    heads = range(RWKV_HEADS)
    hsl = [slice(h * RWKV_HEAD, (h + 1) * RWKV_HEAD) for h in heads]
    tsl = [slice(sub * c, (sub + 1) * c) for sub in range(nsub)]
    seqs = [(b, h) for b in range(nb) for h in heads]
    units = [(b, sub, h) for b in range(nb) for sub in range(nsub) for h in heads]

    scaled = {}
    for b in range(nb):
        for sub in range(nsub):
            lw = lw_v[b][tsl[sub], :]
            cum = _dot_exact_lhs(incl.astype(BF16), lw)
            ecum = jnp.exp(cum)
            einv = jnp.exp(-cum)
            scaled[b, sub] = (kk_v[b][tsl[sub], :] * jnp.exp(cum - lw), bb_v[b][tsl[sub], :] * einv,
                              k_v[b][tsl[sub], :] * einv, r_v[b][tsl[sub], :] * ecum, ecum[c - 1:c, :])
    kt = {u: scaled[u[0], u[1]][0][:, hsl[u[2]]] for u in units}
    bt = {u: scaled[u[0], u[1]][1][:, hsl[u[2]]] for u in units}
    kkt = {u: scaled[u[0], u[1]][2][:, hsl[u[2]]] for u in units}
    rt = {u: scaled[u[0], u[1]][3][:, hsl[u[2]]] for u in units}
    v_t = [v_v[b].T for b in range(nb)]
    vt = {u: v_t[u[0]][hsl[u[2]], tsl[u[1]]] for u in units}
    a_b = {u: strict * _dot_nt(kt[u], bt[u]) for u in units}
    a_k = {u: strict * _dot_nt(kt[u], kkt[u]) for u in units}
    r_b = {u: incl * _dot_nt(rt[u], bt[u]) for u in units}
    r_k = {u: incl * _dot_nt(rt[u], kkt[u]) for u in units}
    pw = {u: -a_b[u] for u in units}
    tinv = {u: eye + pw[u] for u in units}
    n = 1
    while 2 * n < c:
        pw = {u: _dot(pw[u], pw[u]) for u in units}
        tinv = {u: tinv[u] + _dot(tinv[u], pw[u]) for u in units}
        n *= 2
    x = {u: _dot_nt(vt[u], a_k[u]) for u in units}
    w1t = {u: _dot_nt(x[u], tinv[u]) for u in units}
    w2 = {u: _dot(tinv[u], kt[u]) for u in units}
    yt_local = {u: _dot_nt(vt[u], r_k[u]) for u in units}
    s_local = {u: _dot(vt[u], kkt[u]) for u in units}

    s = {q: s_ref[q[0], q[1]] for q in seqs}
    yt = {}
    for sub in range(nsub):
        ut = {(b, h): -(_dot_nt(s[b, h], w2[b, sub, h]) + w1t[b, sub, h]) for b, h in seqs}
        for b, h in seqs:
            yt[b, sub, h] = (_dot_nt(s[b, h], rt[b, sub, h]) + _dot_nt(ut[b, h], r_b[b, sub, h])
                             + yt_local[b, sub, h])
        s = {(b, h): (s[b, h] + _dot(ut[b, h], bt[b, sub, h]) + s_local[b, sub, h]) * scaled[b, sub][4][:, hsl[h]]
             for b, h in seqs}
    for b, h in seqs:
        s_ref[b, h] = s[b, h]
    for b in range(nb):
        y_t = jnp.concatenate([jnp.concatenate([yt[b, sub, h] for sub in range(nsub)], axis=1) for h in heads],
                              axis=0)
        o_ref[:, b * RWKV_WIDTH:(b + 1) * RWKV_WIDTH] = _rwkv_post(y_t.T, r_v[b], k_v[b], v_v[b], g_v[b], vec)

    @pl.when(t == pl.num_programs(0) - 1)
    def _():
        so_ref[...] = s_ref[...]


def _rwkv_seq(pc, first, s0, consts, bsz, seq, valid_len):
    tb = 2 * RWKV_CHUNK
    assert seq % tb == 0 and (valid_len == seq or seq == tb)
    nt = seq // tb
    tiles = [pl.BlockSpec((tb, RWKV_COLS), lambda t, b=b: (b * nt + t, 0)) for b in range(bsz)]
    whole = lambda a: pl.BlockSpec(a.shape, lambda t, nd=a.ndim: (0,) * nd)
    return pl.pallas_call(
        functools.partial(_rwkv_seq_kernel, chunk=RWKV_CHUNK, nb=bsz, valid_len=min(valid_len, tb)), grid=(nt,),
        in_specs=tiles + [whole(first), whole(s0)] + [whole(c) for c in consts],
        out_specs=[pl.BlockSpec((tb, bsz * RWKV_WIDTH), lambda t: (t, 0)), whole(s0)],
        out_shape=[jax.ShapeDtypeStruct((seq, bsz * RWKV_WIDTH), F32), jax.ShapeDtypeStruct(s0.shape, F32)],
        scratch_shapes=[pltpu.VMEM(s0.shape, F32), pltpu.VMEM((bsz, RWKV_COLS), F32)],
        compiler_params=_params(("arbitrary",)), name="rwkv_seq",
    )(*([pc] * bsz), first, s0, *consts)


def _rwkv_step_kernel(s_ref, r_ref, k_ref, lw_ref, kk_ref, bb_ref, v_ref, so_ref, y_ref):
    s = s_ref[...]
    sa = jnp.sum(s * (-kk_ref[...]), axis=-1, keepdims=True)
    s = s * jnp.exp(lw_ref[...]) + sa * bb_ref[...] + v_ref[...] * k_ref[...]
    so_ref[...] = s
    y_ref[...] = jnp.sum(s * r_ref[...], axis=-1, keepdims=True)


def _rwkv_step(r, k, v, lw, kk, bb, s0):
    bsz = r.shape[0]
    p = bsz * RWKV_HEADS
    nb = min(RWKV_STEP_PAIRS, p)
    rowv = lambda a: a.reshape(p, 1, RWKV_HEAD)
    rs = pl.BlockSpec((nb, 1, RWKV_HEAD), lambda i: (i, 0, 0))
    cs = pl.BlockSpec((nb, RWKV_HEAD, 1), lambda i: (i, 0, 0))
    ss = pl.BlockSpec((nb, RWKV_HEAD, RWKV_HEAD), lambda i: (i, 0, 0))
    s_out, y = pl.pallas_call(
        _rwkv_step_kernel, grid=(p // nb,), in_specs=[ss, rs, rs, rs, rs, rs, cs], out_specs=[ss, cs],
        out_shape=[jax.ShapeDtypeStruct((p, RWKV_HEAD, RWKV_HEAD), F32),
                   jax.ShapeDtypeStruct((p, RWKV_HEAD, 1), F32)],
        compiler_params=_params(("parallel",)), name="rwkv_step",
    )(s0.reshape(p, RWKV_HEAD, RWKV_HEAD), rowv(r), rowv(k), rowv(lw), rowv(kk), rowv(bb),
      v.reshape(p, RWKV_HEAD, 1))
    return y.reshape(bsz, RWKV_WIDTH), s_out.reshape(s0.shape)


def _merge_kernel(h_ref, oa_ref, ob_ref, oc_ref, wg_ref, wb_ref, wo_ref, g_ref, b_ref, o_ref, *tile_refs):
    h = h_ref[...]
    gates = _sigmoid(jnp.dot(h.astype(BF16), wg_ref[...], preferred_element_type=F32))
    d = D_MODEL
    merged = (gates[:, :d] * _dot(oa_ref[...], wb_ref[:ATTN_W])
              + gates[:, d:2 * d] * _dot(ob_ref[...], wb_ref[ATTN_W:ATTN_W + SSM_WIDTH])
              + gates[:, 2 * d:] * _dot(oc_ref[...], wb_ref[ATTN_W + SSM_WIDTH:]))
    mix = _dot(merged, wo_ref[...])
    out = _layer_norm(DEEPNORM_ALPHA * h + mix, g_ref[...], b_ref[...])
    o_ref[...] = out
    for t_ref in tile_refs:
        t_ref[...] = _rows_to_tiles(out)


def _ffn_kernel(be_ref, x_ref, wg_ref, wu_ref, wd_ref, g_ref, b_ref, o_ref, xb_ref, acc_ref, *, post_ln):
    del be_ref
    j = pl.program_id(1)

    @pl.when(j == 0)
    def _():
        xb_ref[...] = x_ref[...].astype(BF16)
        acc_ref[...] = jnp.zeros_like(acc_ref)

    xb = xb_ref[...]
    gate = jnp.dot(xb, wg_ref[0], preferred_element_type=F32)
    up = jnp.dot(xb, wu_ref[0], preferred_element_type=F32)
    act = gate * _sigmoid(gate) * up
    acc_ref[...] += jnp.dot(act.astype(BF16), wd_ref[0], preferred_element_type=F32)

    @pl.when(j == pl.num_programs(1) - 1)
    def _():
        if post_ln:
            o_ref[...] = _layer_norm(DEEPNORM_ALPHA * x_ref[...] + acc_ref[...], g_ref[...], b_ref[...])
        else:
            o_ref[...] = acc_ref[...]


def _ffn(x, block_e, w_in, w_down, ln_g, ln_b, blk, tf, post_ln):
    rows = x.shape[0]
    assert rows % blk == 0
    f = w_down.shape[1]
    nf = f // tf
    grid_spec = pltpu.PrefetchScalarGridSpec(
        num_scalar_prefetch=1, grid=(rows // blk, nf),
        in_specs=[
            pl.BlockSpec((blk, D_MODEL), lambda i, j, be: (i, 0)),
            pl.BlockSpec((1, D_MODEL, tf), lambda i, j, be: (be[i], 0, j)),
            pl.BlockSpec((1, D_MODEL, tf), lambda i, j, be: (be[i], 0, nf + j)),
            pl.BlockSpec((1, tf, D_MODEL), lambda i, j, be: (be[i], j, 0)),
            pl.BlockSpec((1, D_MODEL), lambda i, j, be: (0, 0)),
            pl.BlockSpec((1, D_MODEL), lambda i, j, be: (0, 0)),
        ],
        out_specs=pl.BlockSpec((blk, D_MODEL), lambda i, j, be: (i, 0)),
        scratch_shapes=[pltpu.VMEM((blk, D_MODEL), BF16), pltpu.VMEM((blk, D_MODEL), F32)],
    )
    return pl.pallas_call(
        functools.partial(_ffn_kernel, post_ln=post_ln), grid_spec=grid_spec,
        out_shape=jax.ShapeDtypeStruct((rows, D_MODEL), F32),
        compiler_params=_params(("arbitrary", "arbitrary")), name="ffn",
    )(block_e, x, w_in, w_in, w_down, ln_g, ln_b)


def _router_kernel(h_ref, w_ref, e_ref, g_ref):
    logits = jnp.dot(h_ref[...], w_ref[...], preferred_element_type=F32, precision=lax.Precision.HIGHEST)
    lane = lax.broadcasted_iota(jnp.int32, logits.shape, 1)
    lg = jnp.where(lane < N_EXPERTS, logits, -jnp.inf)
    m1 = jnp.max(lg, axis=-1, keepdims=True)
    i1 = jnp.min(jnp.where(lg == m1, lane, LANES), axis=-1, keepdims=True)
    lg2 = jnp.where(lane == i1, -jnp.inf, lg)
    m2 = jnp.max(lg2, axis=-1, keepdims=True)
    i2 = jnp.min(jnp.where(lg2 == m2, lane, LANES), axis=-1, keepdims=True)
    e2 = jnp.exp(m2 - m1)
    den = 1.0 + e2
    e_ref[...] = jnp.where(lane == 0, i1, jnp.where(lane == 1, i2, 0))
    g_ref[...] = jnp.where(lane == 0, 1.0 / den, jnp.where(lane == 1, e2 / den, 0.0))


def _moe_ffn_kernel(be_ref, nv_ref, first_ref, nxt_ref, prev_dst_ref, last_dst_ref, h_hbm, wg_ref, wu_ref, wd_ref,
                    out_hbm, xbuf, xb_ref, acc_ref, stage, gsem, ssem):
    del be_ref
    i, j = pl.program_id(0), pl.program_id(1)
    nblk, nf = pl.num_programs(0), pl.num_programs(1)
    blk = xb_ref.shape[0]
    n_valid = nv_ref[0]
    valid = i < n_valid
    slot = lax.rem(i, 2)
    spare_base = out_hbm.shape[0] - blk

    def gather_row(idx_ref, r, s):
        pltpu.make_async_copy(h_hbm.at[idx_ref[0, 0, r]], xbuf.at[s, r], gsem.at[s]).start(priority=0)

    def scatter_row(r, d):
        pltpu.make_async_copy(stage.at[r], out_hbm.at[d], ssem).start(priority=1)

    def wait_gather(s):
        pltpu.make_async_copy(h_hbm.at[pl.ds(0, blk)], xbuf.at[s], gsem.at[s]).wait()

    def wait_scatter():
        pltpu.make_async_copy(stage, out_hbm.at[pl.ds(0, blk)], ssem).wait()

    def for_rows(fn):
        def body(r, c):
            fn(r)
            return c
        lax.fori_loop(0, blk, body, 0, unroll=8)

    def compute():
        xb = xb_ref[...]
        gate = jnp.dot(xb, wg_ref[0], preferred_element_type=F32)
        up = jnp.dot(xb, wu_ref[0], preferred_element_type=F32)
        act = gate * _sigmoid(gate) * up
        acc_ref[...] += jnp.dot(act.astype(BF16), wd_ref[0], preferred_element_type=F32)

    @pl.when((i == 0) & (j == 0))
    def _():
        for_rows(lambda r: gather_row(first_ref, r, 0))
        stage[...] = jnp.zeros_like(stage)

    @pl.when(valid & (j == 0))
    def _():
        wait_gather(slot)
        for t, cols in enumerate(_tiles_to_cols(xbuf[slot])):
            xb_ref[:, t * LANES:(t + 1) * LANES] = cols.astype(BF16)
        acc_ref[...] = jnp.zeros_like(acc_ref)

    @pl.when(valid & (j < nf - 1))
    def _():
        compute()
        per_step = blk // (nf - 1)
        for rr in range(per_step):
            r = j * per_step + rr
            gather_row(nxt_ref, r, 1 - slot)
            scatter_row(r, jnp.where(i > 0, prev_dst_ref[0, 0, r], spare_base + r))

    @pl.when(valid & (j == nf - 1))
    def _():
        compute()
        wait_scatter()
        stage[...] = _rows_to_tiles(acc_ref[...])

    @pl.when((i == nblk - 1) & (j == nf - 1))
    def _():
        for_rows(lambda r: scatter_row(r, last_dst_ref[0, 0, r]))
        wait_scatter()
        wait_gather(lax.rem(n_valid, 2))


def _moe_ffn(h_tiles, rows_tok, rows_dst, block_e, n_valid, w_in, w_down, blk, tf):
    n = h_tiles.shape[0]
    rows = rows_tok.shape[0]
    nblk = rows // blk
    nf = w_down.shape[1] // tf
    assert blk % (nf - 1) == 0 and n >= blk
    idx3 = lambda a: a.reshape(nblk, 1, blk)
    smem = lambda fn: pl.BlockSpec((1, 1, blk), fn, memory_space=pltpu.SMEM)
    ftile = lambda i, j, nv: jnp.where(i < nv[0], j, nf - 1)
    grid_spec = pltpu.PrefetchScalarGridSpec(
        num_scalar_prefetch=2, grid=(nblk, nf),
        in_specs=[
            smem(lambda i, j, be, nv: (0, 0, 0)),
            smem(lambda i, j, be, nv: (jnp.minimum(i + 1, nblk - 1), 0, 0)),
            smem(lambda i, j, be, nv: (jnp.maximum(i - 1, 0), 0, 0)),
            smem(lambda i, j, be, nv: (nv[0] - 1, 0, 0)),
            pl.BlockSpec(memory_space=pl.ANY),
            pl.BlockSpec((1, D_MODEL, tf), lambda i, j, be, nv: (be[i], 0, ftile(i, j, nv))),
            pl.BlockSpec((1, D_MODEL, tf), lambda i, j, be, nv: (be[i], 0, nf + ftile(i, j, nv))),
            pl.BlockSpec((1, tf, D_MODEL), lambda i, j, be, nv: (be[i], ftile(i, j, nv), 0)),
        ],
        out_specs=pl.BlockSpec(memory_space=pl.ANY),
        scratch_shapes=[pltpu.VMEM((2, blk, SUBLANES, LANES), F32), pltpu.VMEM((blk, D_MODEL), BF16),
                        pltpu.VMEM((blk, D_MODEL), F32), pltpu.VMEM((blk, SUBLANES, LANES), F32),
                        pltpu.SemaphoreType.DMA((2,)), pltpu.SemaphoreType.DMA(())],
    )
    return pl.pallas_call(
        _moe_ffn_kernel, grid_spec=grid_spec,
        out_shape=jax.ShapeDtypeStruct((TOP_K * n + blk, SUBLANES, LANES), F32),
        compiler_params=_params(("arbitrary", "arbitrary")), name="moe_ffn",
    )(block_e, n_valid, idx3(rows_tok), idx3(rows_tok), idx3(rows_dst), idx3(rows_dst), h_tiles,
      w_in, w_in, w_down)


def _combine_kernel(h_ref, y0_ref, y1_ref, gate_ref, g_ref, bias_ref, o_ref):
    gate = gate_ref[...]
    rows = lambda y_ref: jnp.concatenate(_tiles_to_cols(y_ref[...]), axis=-1)
    f = rows(y0_ref) * gate[:, 0:1] + rows(y1_ref) * gate[:, 1:2]
    o_ref[...] = _layer_norm(DEEPNORM_ALPHA * h_ref[...] + f, g_ref[...], bias_ref[...])


def _moe(h, h_tiles, router_pad, w_in, w_down, ln_g, ln_b, tm, blk, tf):
    n = h.shape[0]
    e_pad, gate = _rows_call(_router_kernel, [h], [router_pad], (LANES, LANES), tm,
                             out_dtypes=[jnp.int32, F32], name="router")
    flat_e = jnp.concatenate([e_pad[:, s] for s in range(TOP_K)])
    n_assign = n * TOP_K
    n_blocks = -(-(n_assign + N_EXPERTS * (blk - 1)) // blk)
    experts = jnp.arange(N_EXPERTS, dtype=jnp.int32)
    onehot = (flat_e[:, None] == experts[None, :]).astype(jnp.int32)
    csum = jnp.cumsum(onehot, axis=0)
    counts = csum[-1]
    padded = (counts + blk - 1) // blk * blk
    pad_end = jnp.cumsum(padded)
    dest = jnp.sum((csum - 1 + (pad_end - padded)[None, :]) * onehot, axis=1)
    assign = jnp.arange(n_assign, dtype=jnp.int32)
    rows_dst = jnp.full((n_blocks * blk,), -1, jnp.int32).at[dest].set(assign, unique_indices=True)
    rows_tok = jnp.maximum(rows_dst, 0) % n
    spare = n_assign + jnp.arange(n_blocks * blk, dtype=jnp.int32) % blk
    rows_dst = jnp.where(rows_dst < 0, spare, rows_dst)
    block_start = jnp.arange(n_blocks, dtype=jnp.int32) * blk
    block_e = jnp.minimum(jnp.sum((block_start[:, None] >= pad_end[None, :]).astype(jnp.int32), axis=1),
                          N_EXPERTS - 1)
    n_valid = (pad_end[-1:] // blk).astype(jnp.int32)
    y = _moe_ffn(h_tiles, rows_tok, rows_dst, block_e, n_valid, w_in, w_down, blk, tf)
    tm = min(tm, n)
    nt = n // tm
    cst = pl.BlockSpec((1, D_MODEL), lambda i: (0, 0))
    slot = lambda s: pl.BlockSpec((tm, SUBLANES, LANES), lambda i: (i + s * nt, 0, 0))
    return pl.pallas_call(
        _combine_kernel, grid=(nt,),
        in_specs=[pl.BlockSpec((tm, D_MODEL), lambda i: (i, 0)), slot(0), slot(1),
                  pl.BlockSpec((tm, LANES), lambda i: (i, 0)), cst, cst],
        out_specs=pl.BlockSpec((tm, D_MODEL), lambda i: (i, 0)),
        out_shape=jax.ShapeDtypeStruct((n, D_MODEL), F32),
        compiler_params=_params(("parallel",)), name="moe_combine",
    )(h, y, y, gate, ln_g, ln_b)


def _layer(l, hb, hs, pre_ln, bsz, seq, dec, cache, states, prm):
    (cache_meta_k, cache_meta_v, cache_win_k, cache_win_v) = cache
    (state_ssm_re, state_ssm_im, state_wkv, state_shift) = states
    n_meta = bsz * N_META
    tm_b, tm_s = ROW_TILE, hs.shape[0]
    w_in = prm['w_in'][l]
    w_mix = w_in[:, :MIX_COLS].astype(BF16)
    w_gate = w_in[:, MIX_COLS:].astype(BF16)
    ln_in_g, ln_in_b = prm['ln_in_g'].reshape(1, -1), prm['ln_in_b'].reshape(1, -1)

    hb, (q_b, k_b, v_b, u_b, pc_b) = _proj(hb, ln_in_g, ln_in_b, w_mix, pre_ln, tm_b)
    hs, (q_s, k_s, v_s, u_s, pc_s) = _proj(hs, ln_in_g, ln_in_b, w_mix, pre_ln, tm_s)

    rel_bias, sinks = prm['rel_bias'], prm['attn_sinks'][l]
    k_meta, v_meta = k_s[:n_meta], v_s[:n_meta]
    oa_b = _body_attention(q_b, k_b, v_b, k_meta, v_meta, rel_bias, sinks, bsz, seq)
    oa_m = _meta_attention(q_s[:n_meta], k_meta, v_meta, rel_bias, sinks, bsz)
    kd = lambda t: t[n_meta:].reshape(dec, 1, N_KV_HEADS, HEAD_DIM)
    k_all = jnp.concatenate([cache_win_k[l].astype(F32), kd(k_s)], axis=1)
    v_all = jnp.concatenate([cache_win_v[l].astype(F32), kd(v_s)], axis=1)
    oa_d = _sample_attention(q_s[n_meta:], k_all, v_all, cache_meta_k[l], cache_meta_v[l], rel_bias, sinks)
    oa_s = jnp.concatenate([oa_m, oa_d], axis=0)
    kv4 = lambda t, b: t.reshape(b, -1, N_KV_HEADS, HEAD_DIM)
    tail = lambda t: kv4(jnp.concatenate([t[(b + 1) * seq - WINDOW:(b + 1) * seq] for b in range(bsz)], axis=0), bsz)
    attn_out = (kv4(k_meta, bsz), kv4(v_meta, bsz), tail(k_b), tail(v_b), k_all[:, 1:], v_all[:, 1:])

    ar, ai, w_b, w_c = _s5_weights(prm['ssm_a_re'][l], prm['ssm_a_im'][l], prm['ssm_log_dt'][l],
                                   prm['ssm_b_re'][l], prm['ssm_b_im'][l], prm['ssm_c_re'][l], prm['ssm_c_im'][l])
    zero_h = jnp.zeros((bsz, 1, SSM_N), F32)
    s5_consts = [w_c, prm['ssm_d'][l].reshape(1, -1).astype(F32), prm['ssm_w_glu'][l].astype(BF16)]
    ob_m, fr_m, fi_m = _s5_scan(u_s[:n_meta], w_b, zero_h, zero_h, ar, ai, s5_consts, bsz, N_META, N_META)
    ob_b, fr_b, fi_b = _s5_scan(u_b, w_b, fr_m, fi_m, ar, ai, s5_consts, bsz, seq, ROW_TILE)
    u_d = u_s[n_meta:]
    bur_d, bui_d = _rows_call(_s5_bu_kernel, [u_d], [w_b], (SSM_N, SSM_N), dec, name="s5_bu")
    hr_d, hi_d = _rows_call(_s5_step_kernel,
                            [bur_d, bui_d, state_ssm_re[l].reshape(dec, SSM_N).astype(F32),
                             state_ssm_im[l].reshape(dec, SSM_N).astype(F32)], [ar, ai], (SSM_N, SSM_N), dec,
                            name="s5_step")
    (ob_d,) = _rows_call(_s5_out_kernel, [u_d, hr_d, hi_d], s5_consts, (SSM_WIDTH,), dec, name="s5_out")
    ob_s = jnp.concatenate([ob_m, ob_d], axis=0)
    st4 = lambda t, b: t.reshape(b, SSM_GROUPS, SSM_STATE)
    ssm_out = (st4(fr_b, bsz), st4(fi_b, bsz), st4(hr_d, dec), st4(hi_d, dec))

    pad_rows = lambda w, lo: jnp.pad(w.astype(F32), ((lo, RWKV_LORA - lo - w.shape[0]), (0, 0))).astype(BF16)
    vec = jnp.pad(prm['rwkv_vec'][l].astype(F32), ((0, 1), (0, 0)))
    prep_consts = [prm['rwkv_mu'][l].reshape(1, -1).astype(F32), vec,
                   pad_rows(prm['rwkv_w2'][l], 0), pad_rows(prm['rwkv_a2'][l], RWKV_W_LORA),
                   pad_rows(prm['rwkv_g2'][l], RWKV_W_LORA + RWKV_A_LORA)]
    pc_m = pc_s[:n_meta].reshape(bsz, N_META, RWKV_COLS)
    meta_len = 2 * RWKV_CHUNK
    pc_m_pad = jnp.pad(pc_m, ((0, 0), (0, meta_len - N_META), (0, 0))).reshape(bsz * meta_len, RWKV_COLS)
    zero_s = jnp.zeros((bsz, RWKV_HEADS, RWKV_HEAD, RWKV_HEAD), F32)
    oc_m, s_m = _rwkv_seq(pc_m_pad, jnp.zeros((bsz, RWKV_COLS), F32), zero_s, prep_consts, bsz, meta_len, N_META)
    oc_m = jnp.swapaxes(oc_m[:N_META].reshape(N_META, bsz, RWKV_WIDTH), 0, 1).reshape(n_meta, RWKV_WIDTH)
    oc_b, s_b = _rwkv_seq(pc_b, pc_m[:, -1], s_m, prep_consts, bsz, seq, seq)
    pc_d = pc_s[n_meta:]
    r_d, k_d, v_d, lw_d, kk_d, bb_d, g_d = _rows_call(_rwkv_prep_kernel, [pc_d, state_shift[l].astype(F32)],
                                                      prep_consts, (RWKV_WIDTH,) * 7, dec, name="rwkv_prep")
    y_d, s_d = _rwkv_step(r_d, k_d, v_d, lw_d, kk_d, bb_d, state_wkv[l].astype(F32))
    (oc_d,) = _rows_call(_rwkv_post_kernel, [r_d, k_d, v_d, g_d, y_d], [vec], (RWKV_WIDTH,), dec, name="rwkv_post")
    oc_s = jnp.concatenate([oc_m, oc_d], axis=0)
    rwkv_out = (s_b, s_d, pc_b.reshape(bsz, seq, RWKV_COLS)[:, -1], pc_d)

    ln_g, ln_b = prm['ln_g'][l].astype(F32), prm['ln_b'][l].astype(F32)
    merge_consts = [w_gate, prm['w_branch'][l].astype(BF16), prm['w_out'][l].astype(BF16), ln_g[0:1], ln_b[0:1]]
    moe_layer = l % 2 == 1
    merge_outs = (D_MODEL,) + (((SUBLANES, LANES),) if moe_layer else ())
    tm_merge = min(tm_b, seq)
    oc_tile = (oc_b, (tm_merge, RWKV_WIDTH), lambda i: (i % (seq // tm_merge), i // (seq // tm_merge)))
    hb, *hb_tiles = _rows_call(_merge_kernel, [hb, oa_b, ob_b], merge_consts, merge_outs, tm_merge, name="merge",
                               extra_inputs=[oc_tile])
    hs, *hs_tiles = _rows_call(_merge_kernel, [hs, oa_s, ob_s, oc_s], merge_consts, merge_outs, tm_s, name="merge")

    if not moe_layer:
        w_ffn_in = prm['ffn_w_in'][l // 2].astype(BF16)[None]
        w_ffn_down = prm['ffn_w_down'][l // 2].astype(BF16)[None]
        blk_b = min(FFN_ROWS, hb.shape[0])
        hb = _ffn(hb, jnp.zeros((hb.shape[0] // blk_b,), jnp.int32), w_ffn_in, w_ffn_down,
                  ln_g[1:2], ln_b[1:2], blk_b, FFN_F_TILE, post_ln=True)
        hs = _ffn(hs, jnp.zeros((1,), jnp.int32), w_ffn_in, w_ffn_down, ln_g[1:2], ln_b[1:2], tm_s, FFN_F_TILE,
                  post_ln=True)
    else:
        router_pad = jnp.pad(prm['moe_router'][l // 2].astype(F32), ((0, 0), (0, LANES - N_EXPERTS)))
        w_moe_in = prm['moe_w_in'][l // 2].astype(BF16)
        w_moe_down = prm['moe_w_down'][l // 2].astype(BF16)
        hb = _moe(hb, hb_tiles[0], router_pad, w_moe_in, w_moe_down, ln_g[1:2], ln_b[1:2], tm_b, MOE_ROWS,
                  MOE_F_TILE)
        hs = _moe(hs, hs_tiles[0], router_pad, w_moe_in, w_moe_down, ln_g[1:2], ln_b[1:2], tm_s, MOE_ROWS_SMALL,
                  MOE_F_TILE)
    return hb, hs, attn_out, ssm_out, rwkv_out


def kernel(x_prompt, x_sample, cache_meta_k, cache_meta_v, cache_win_k, cache_win_v, state_ssm_re, state_ssm_im, state_wkv, state_shift, meta_tokens, ln_in_g, ln_in_b, w_in, rel_bias, attn_sinks, ssm_a_re, ssm_a_im, ssm_log_dt, ssm_b_re, ssm_b_im, ssm_c_re, ssm_c_im, ssm_d, ssm_w_glu, rwkv_mu, rwkv_vec, rwkv_w2, rwkv_a2, rwkv_g2, w_branch, w_out, ln_g, ln_b, ffn_w_in, ffn_w_down, moe_router, moe_w_in, moe_w_down):
    bsz, seq, _ = x_prompt.shape
    dec = x_sample.shape[0]
    assert x_sample.shape[1] == 1 and seq % (2 * RWKV_CHUNK) == 0
    prm = dict(ln_in_g=ln_in_g.astype(F32), ln_in_b=ln_in_b.astype(F32), w_in=w_in, rel_bias=rel_bias,
               attn_sinks=attn_sinks, ssm_a_re=ssm_a_re, ssm_a_im=ssm_a_im, ssm_log_dt=ssm_log_dt,
               ssm_b_re=ssm_b_re, ssm_b_im=ssm_b_im, ssm_c_re=ssm_c_re, ssm_c_im=ssm_c_im, ssm_d=ssm_d,
               ssm_w_glu=ssm_w_glu, rwkv_mu=rwkv_mu, rwkv_vec=rwkv_vec, rwkv_w2=rwkv_w2, rwkv_a2=rwkv_a2,
               rwkv_g2=rwkv_g2, w_branch=w_branch, w_out=w_out, ln_g=ln_g, ln_b=ln_b, ffn_w_in=ffn_w_in,
               ffn_w_down=ffn_w_down, moe_router=moe_router, moe_w_in=moe_w_in, moe_w_down=moe_w_down)
    hb = x_prompt.reshape(bsz * seq, D_MODEL).astype(F32)
    meta = jnp.broadcast_to(meta_tokens.astype(F32)[None], (bsz, N_META, D_MODEL)).reshape(bsz * N_META, D_MODEL)
    hs = jnp.concatenate([meta, x_sample.reshape(dec, D_MODEL).astype(F32)], axis=0)
    cache = (cache_meta_k, cache_meta_v, cache_win_k, cache_win_v)
    states = (state_ssm_re, state_ssm_im, state_wkv, state_shift)
    attn_outs, ssm_outs, rwkv_outs = [], [], []
    for l in range(DEPTH):
        hb, hs, a_o, s_o, r_o = _layer(l, hb, hs, l == 0, bsz, seq, dec, cache, states, prm)
        attn_outs.append(a_o)
        ssm_outs.append(s_o)
        rwkv_outs.append(r_o)
    stack = lambda outs, i: jnp.stack([o[i] for o in outs])
    y_prompt = hb.reshape(bsz, seq, D_MODEL)
    y_sample = hs[bsz * N_META:].reshape(dec, 1, D_MODEL)
    return (y_prompt, y_sample,
            stack(attn_outs, 0), stack(attn_outs, 1), stack(attn_outs, 2), stack(attn_outs, 3),
            stack(attn_outs, 4), stack(attn_outs, 5),
            stack(ssm_outs, 0), stack(ssm_outs, 1), stack(ssm_outs, 2), stack(ssm_outs, 3),
            stack(rwkv_outs, 0), stack(rwkv_outs, 1), stack(rwkv_outs, 2), stack(rwkv_outs, 3))
```

```python
import functools
import math

import numpy as np
import jax
import jax.numpy as jnp
from jax import lax
from jax.experimental import pallas as pl
from jax.experimental.pallas import tpu as pltpu

F32 = jnp.float32
BF16 = jnp.bfloat16

D_MODEL = 1024
DEPTH = 2
PAST_LEN = 16384
N_META = 16
WINDOW = 128
N_HEADS = 8
N_KV_HEADS = 2
HEAD_DIM = 64
Q_PER_KV = N_HEADS // N_KV_HEADS
ATTN_W = N_HEADS * HEAD_DIM
KV_W = N_KV_HEADS * HEAD_DIM
ATTN_SCALE = HEAD_DIM ** -0.5
REL_BUCKETS = 32
REL_EXACT = REL_BUCKETS // 2
REL_MAX_DIST = 128
SSM_GROUP = 16
SSM_GROUPS = 16
SSM_WIDTH = SSM_GROUP * SSM_GROUPS
SSM_STATE = 64
SSM_N = SSM_GROUPS * SSM_STATE
RWKV_HEAD = 64
RWKV_HEADS = 4
RWKV_WIDTH = RWKV_HEAD * RWKV_HEADS
RWKV_W_LORA = 32
RWKV_A_LORA = 32
RWKV_G_LORA = 64
RWKV_LORA = RWKV_W_LORA + RWKV_A_LORA + RWKV_G_LORA
RWKV_COLS = 3 * RWKV_WIDTH + RWKV_LORA
RV_W0, RV_A0, RV_KK, RV_KA, RV_RK, RV_GNW, RV_GNB = 0, 1, 2, 3, 4, 5, 6
N_BRANCH = 3
MIX_COLS = ATTN_W + 2 * KV_W + SSM_WIDTH + RWKV_COLS
N_EXPERTS = 8
TOP_K = 2
LN_EPS = 1e-5
RWKV_GN_EPS = 64e-5
NEG_INF = -1e30
DEEPNORM_ALPHA = (2 * DEPTH) ** 0.25

LANES = 128
SUBLANES = 8
VMEM_LIMIT = 48 * 1024 * 1024
RWKV_CHUNK = 64

ROW_TILE = 512
FFN_ROWS, FFN_F_TILE = 1024, 256
MOE_F_TILE = 512
MOE_ROWS, MOE_ROWS_SMALL = 1008, 96
SAMPLE_ATTN_BLOCK = 8
RWKV_STEP_PAIRS = 64


def _params(sem):
    return pltpu.CompilerParams(dimension_semantics=sem, vmem_limit_bytes=VMEM_LIMIT)


def _dot(a, b):
    return jnp.dot(a.astype(BF16), b.astype(BF16), preferred_element_type=F32)


def _dot_nt(a, b):
    return lax.dot_general(a.astype(BF16), b.astype(BF16), (((1,), (1,)), ((), ())),
                           preferred_element_type=F32)


def _split3(x):
    h1 = x.astype(BF16)
    r1 = x - h1.astype(F32)
    h2 = r1.astype(BF16)
    h3 = (r1 - h2.astype(F32)).astype(BF16)
    return h1, h2, h3


def _dot_exact_rhs(x, m):
    h1, h2, h3 = _split3(x)
    dot = functools.partial(jnp.dot, preferred_element_type=F32)
    return dot(h1, m) + dot(h2, m) + dot(h3, m)


def _dot_exact_lhs(m, x):
    h1, h2, h3 = _split3(x)
    dot = functools.partial(jnp.dot, preferred_element_type=F32)
    return dot(m, h1) + dot(m, h2) + dot(m, h3)


def _layer_norm(x, g, b):
    mu = jnp.mean(x, axis=-1, keepdims=True)
    xc = x - mu
    var = jnp.mean(xc * xc, axis=-1, keepdims=True)
    return xc * lax.rsqrt(var + LN_EPS) * g + b


def _sigmoid(x):
    return 1.0 / (1.0 + jnp.exp(-x))


def _rows_to_tiles(x):
    slabs = [x[:, t * LANES:(t + 1) * LANES] for t in range(SUBLANES)]
    return jnp.swapaxes(jnp.stack(slabs, axis=0), 0, 1)


def _tiles_to_cols(x):
    xt = jnp.swapaxes(x, 0, 1)
    return [xt[t] for t in range(SUBLANES)]


def _rows_call(body, row_inputs, const_inputs, out_widths, tm, out_dtypes=None, name=None, extra_inputs=()):
    n = row_inputs[0].shape[0]
    tm = min(tm, n)
    assert n % tm == 0, (n, tm)
    out_dtypes = out_dtypes or [F32] * len(out_widths)
    in_specs = [pl.BlockSpec((tm, a.shape[1]), lambda i: (i, 0)) for a in row_inputs]
    in_specs += [pl.BlockSpec(shape, fn) for _, shape, fn in extra_inputs]
    row_inputs = list(row_inputs) + [a for a, _, _ in extra_inputs]
    in_specs += [pl.BlockSpec(c.shape, lambda i, nd=c.ndim: (0,) * nd) for c in const_inputs]
    tails = [w if isinstance(w, tuple) else (w,) for w in out_widths]
    out_specs = [pl.BlockSpec((tm,) + w, lambda i, nd=len(w): (i,) + (0,) * nd) for w in tails]
    out_shape = [jax.ShapeDtypeStruct((n,) + w, dt) for w, dt in zip(tails, out_dtypes)]
    return pl.pallas_call(
        body, grid=(n // tm,), in_specs=in_specs, out_specs=out_specs, out_shape=out_shape,
        compiler_params=_params(("parallel",)), name=name,
    )(*row_inputs, *const_inputs)


PROJ_WIDTHS = (ATTN_W, KV_W, KV_W, SSM_WIDTH, RWKV_COLS)


def _proj_kernel(x_ref, g_ref, b_ref, w_ref, *out_refs, pre_ln):
    x = x_ref[...]
    if pre_ln:
        x = _layer_norm(x, g_ref[...], b_ref[...])
        out_refs[0][...] = x
        out_refs = out_refs[1:]
    xb = x.astype(BF16)
    col = 0
    for o_ref in out_refs:
        n = o_ref.shape[-1]
        o_ref[...] = jnp.dot(xb, w_ref[:, col:col + n], preferred_element_type=F32)
        col += n


def _proj(x, ln_g, ln_b, w_mix, pre_ln, tm):
    widths = ((D_MODEL,) if pre_ln else ()) + PROJ_WIDTHS
    outs = _rows_call(functools.partial(_proj_kernel, pre_ln=pre_ln), [x], [ln_g, ln_b, w_mix],
                      widths, tm, name="proj")
    if pre_ln:
        return outs[0], outs[1:]
    return x, outs


def _attn_kernel(q_ref, k_ref, v_ref, bias_ref, sink_ref, o_ref):
    units = [(bb, h) for bb in range(q_ref.shape[0]) for h in range(N_KV_HEADS)]
    hs = [slice(h * HEAD_DIM, (h + 1) * HEAD_DIM) for h in range(N_KV_HEADS)]
    q = {u: q_ref[u[0], u[1]].astype(BF16) for u in units}
    k = {u: k_ref[u[0], :, hs[u[1]]].astype(BF16) for u in units}
    v = {u: v_ref[u[0], :, hs[u[1]]].astype(BF16) for u in units}
    s = {u: lax.dot_general(q[u], k[u], (((1,), (1,)), ((), ())), preferred_element_type=F32) for u in units}
    s = {u: s[u] * ATTN_SCALE + bias_ref[u[1]] for u in units}
    m = {u: jnp.maximum(jnp.max(s[u], axis=-1, keepdims=True), sink_ref[u[1]]) for u in units}
    p = {u: jnp.exp(s[u] - m[u]) for u in units}
    den = {u: jnp.sum(p[u], axis=-1, keepdims=True) + jnp.exp(sink_ref[u[1]] - m[u]) for u in units}
    o = {u: jnp.dot(p[u].astype(BF16), v[u], preferred_element_type=F32) for u in units}
    for u in units:
        o_ref[u[0], u[1]] = o[u] / den[u]


def _attention(q, k, v, bias, sinks, bblk):
    p, _, mq, _ = q.shape
    nk = k.shape[1]
    assert p % bblk == 0
    kv_spec = pl.BlockSpec((bblk, nk, KV_W), lambda i: (i, 0, 0))
    qo_spec = pl.BlockSpec((bblk, N_KV_HEADS, mq, HEAD_DIM), lambda i: (i, 0, 0, 0))
    return pl.pallas_call(
        _attn_kernel, grid=(p // bblk,),
        in_specs=[qo_spec, kv_spec, kv_spec, pl.BlockSpec((N_KV_HEADS, mq, nk), lambda i: (0, 0, 0)),
                  pl.BlockSpec((N_KV_HEADS, mq, 1), lambda i: (0, 0, 0))],
        out_specs=qo_spec, out_shape=jax.ShapeDtypeStruct(q.shape, F32),
        compiler_params=_params(("parallel",)), name="attention",
    )(q, k, v, bias, sinks)


def _t5_bucket(dist):
    n = np.maximum(dist, 0)
    scaled = (np.log(np.maximum(n, 1).astype(np.float32) / np.float32(REL_EXACT))
              / np.float32(math.log(REL_MAX_DIST / REL_EXACT)) * np.float32(REL_BUCKETS - REL_EXACT))
    frac = np.abs(scaled - np.round(scaled))
    assert np.all((n <= REL_EXACT) | (n >= REL_MAX_DIST) | (frac > 1e-3))
    large = np.minimum(REL_EXACT + scaled.astype(np.int32), REL_BUCKETS - 1)
    return np.where(n < REL_EXACT, n, large)


def _bias_table(rel_bias, dist, valid, mq_pad=None, nk_pad=None):
    tq, nk = dist.shape
    onehot = np.eye(REL_BUCKETS, dtype=np.float32)[_t5_bucket(dist).reshape(-1)]
    bias = jnp.dot(jnp.asarray(onehot), rel_bias.astype(F32), precision=lax.Precision.HIGHEST)
    bias = bias.reshape(tq, nk, N_HEADS)
    bias = jnp.where(jnp.asarray(valid)[..., None], bias, NEG_INF)
    bias = jnp.moveaxis(bias, -1, 0).reshape(N_KV_HEADS, Q_PER_KV * tq, nk)
    mq_pad = mq_pad or Q_PER_KV * tq
    nk_pad = nk_pad or nk
    bias = jnp.pad(bias, ((0, 0), (0, mq_pad - Q_PER_KV * tq), (0, 0)))
    return jnp.pad(bias, ((0, 0), (0, 0), (0, nk_pad - nk)), constant_values=NEG_INF)


def _sink_rows(sinks, tq, mq_pad=None):
    s = jnp.repeat(sinks.astype(F32).reshape(N_KV_HEADS, Q_PER_KV, 1), tq, axis=2)
    s = s.reshape(N_KV_HEADS, Q_PER_KV * tq, 1)
    mq_pad = mq_pad or Q_PER_KV * tq
    return jnp.pad(s, ((0, 0), (0, mq_pad - Q_PER_KV * tq), (0, 0)))


def _heads_q(q, nb, tq):
    q = q.reshape(nb, tq, N_KV_HEADS, Q_PER_KV, HEAD_DIM)
    return jnp.transpose(q, (0, 2, 3, 1, 4)).reshape(nb, N_KV_HEADS, Q_PER_KV * tq, HEAD_DIM)


def _unheads_o(o, nb, tq):
    o = o[:, :, :Q_PER_KV * tq].reshape(nb, N_KV_HEADS, Q_PER_KV, tq, HEAD_DIM)
    return jnp.transpose(o, (0, 3, 1, 2, 4)).reshape(nb * tq, ATTN_W)


BODY_KEYS = N_META + 2 * WINDOW + 16


def _body_attn_kernel(q_ref, kc_ref, kp_ref, km_ref, vc_ref, vp_ref, vm_ref, bias0_ref, bias1_ref, o_ref):
    units = [(h, j) for h in range(N_KV_HEADS) for j in range(2)]
    hs = [slice(h * HEAD_DIM, (h + 1) * HEAD_DIM) for h in range(N_KV_HEADS)]
    heads = [[h * Q_PER_KV + g for g in range(Q_PER_KV)] for h in range(N_KV_HEADS)]
    rows = [slice(j * WINDOW, (j + 1) * WINDOW) for j in range(2)]
    pad = jnp.zeros((BODY_KEYS - N_META - 2 * WINDOW, HEAD_DIM), F32)
    ones = jnp.ones((BODY_KEYS, HEAD_DIM), BF16)

    def keys(cur_ref, prev_ref, meta_ref, h, j):
        before = prev_ref[:, hs[h]] if j == 0 else cur_ref[rows[0], hs[h]]
        return jnp.concatenate([meta_ref[:, hs[h]], before, cur_ref[rows[j], hs[h]], pad], axis=0).astype(BF16)

    k = {u: keys(kc_ref, kp_ref, km_ref, *u) for u in units}
    v = {u: keys(vc_ref, vp_ref, vm_ref, *u) for u in units}
    q = {(h, j): jnp.concatenate([q_ref[rows[j], qh * HEAD_DIM:(qh + 1) * HEAD_DIM] for qh in heads[h]],
                                 axis=0).astype(BF16) for h, j in units}
    bias = [bias0_ref, bias1_ref]
    s = {u: lax.dot_general(q[u], k[u], (((1,), (1,)), ((), ())), preferred_element_type=F32) for u in units}
    s = {(h, j): s[h, j] * ATTN_SCALE + bias[j][0, h] for h, j in units}
    p = {u: jnp.exp(s[u] - jnp.max(s[u], axis=-1, keepdims=True)).astype(BF16) for u in units}
    o = {u: jnp.dot(p[u], v[u], preferred_element_type=F32) / jnp.dot(p[u], ones, preferred_element_type=F32)
         for u in units}
    for h, j in units:
        for g, qh in enumerate(heads[h]):
            o_ref[rows[j], qh * HEAD_DIM:(qh + 1) * HEAD_DIM] = o[h, j][g * WINDOW:(g + 1) * WINDOW]


def _body_attention(q, k, v, k_meta, v_meta, rel_bias, sinks, bsz, seq):
    nblk = seq // WINDOW
    i = np.arange(WINDOW)[:, None]
    c = np.arange(WINDOW)[None, :]
    sink_col = _sink_rows(sinks, WINDOW)
    tabs = []
    for m in (0, 1):
        q_pos = N_META + WINDOW * m + i
        meta_pos = np.arange(N_META)[None, :]
        dist = np.concatenate([q_pos - meta_pos, WINDOW + i - c, i - c], axis=1)
        valid = np.concatenate([np.ones((WINDOW, N_META), bool),
                                (c >= i) & (m > 0), c <= i], axis=1)
        tab = _bias_table(rel_bias, dist, valid, nk_pad=BODY_KEYS)
        tabs.append(tab.at[:, :, N_META + 2 * WINDOW].set(sink_col[:, :, 0]))
    bias = jnp.stack(tabs)
    nk = BODY_KEYS
    mq = Q_PER_KV * WINDOW
    assert nblk % 2 == 0
    npair = nblk // 2
    pair = lambda w: pl.BlockSpec((2 * WINDOW, w), lambda b, p: (b * npair + p, 0))
    prev = lambda w: pl.BlockSpec((WINDOW, w), lambda b, p: (b * nblk + jnp.maximum(2 * p - 1, 0), 0))
    meta = pl.BlockSpec((N_META, KV_W), lambda b, p: (b, 0))
    table = lambda fn: pl.BlockSpec((1, N_KV_HEADS, mq, nk), fn)
    return pl.pallas_call(
        _body_attn_kernel, grid=(bsz, npair),
        in_specs=[pair(ATTN_W), pair(KV_W), prev(KV_W), meta, pair(KV_W), prev(KV_W), meta,
                  table(lambda b, p: (jnp.minimum(p, 1), 0, 0, 0)), table(lambda b, p: (1, 0, 0, 0))],
        out_specs=pair(ATTN_W), out_shape=jax.ShapeDtypeStruct(q.shape, F32),
        compiler_params=_params(("parallel", "arbitrary")), name="body_attention",
    )(q, k, k, k_meta, v, v, v_meta, bias, bias)


def _meta_attention(q, k, v, rel_bias, sinks, bsz):
    i = np.arange(N_META)
    dist = i[:, None] - i[None, :]
    bias = _bias_table(rel_bias, dist, dist >= 0)
    kv = lambda t: t.reshape(bsz, N_META, KV_W)
    o = _attention(_heads_q(q, bsz, N_META), kv(k), kv(v), bias, _sink_rows(sinks, N_META), bsz)
    return _unheads_o(o, bsz, N_META)


def _sample_attention(q, k_all, v_all, meta_k, meta_v, rel_bias, sinks):
    bsz = q.shape[0]
    wc = k_all.shape[1] - 1
    nk = N_META + wc + 1
    nk_pad = -(-nk // LANES) * LANES
    mq_pad = SUBLANES
    k_pos = np.concatenate([np.arange(N_META), PAST_LEN - wc + np.arange(wc + 1)])
    dist = (PAST_LEN - k_pos)[None, :]
    is_meta = (np.arange(nk) < N_META)[None, :]
    valid = (dist >= 0) & (is_meta | ((k_pos[None, :] >= N_META) & (dist <= WINDOW)))
    bias = _bias_table(rel_bias, dist, valid, mq_pad, nk_pad)

    def kv(meta, t):
        full = jnp.concatenate([meta.astype(F32), t], axis=1).reshape(bsz, nk, KV_W)
        return jnp.pad(full, ((0, 0), (0, nk_pad - nk), (0, 0)))

    qh = jnp.pad(_heads_q(q, bsz, 1), ((0, 0), (0, 0), (0, mq_pad - Q_PER_KV), (0, 0)))
    o = _attention(qh, kv(meta_k, k_all), kv(meta_v, v_all), bias, _sink_rows(sinks, 1, mq_pad),
                   SAMPLE_ATTN_BLOCK)
    return _unheads_o(o, bsz, 1)


def _s5_bu_kernel(u_ref, w_ref, re_ref, im_ref):
    r = jnp.dot(u_ref[...].astype(BF16), w_ref[...], preferred_element_type=F32)
    re_ref[...] = r[:, :SSM_N]
    im_ref[...] = r[:, SSM_N:]


def _s5_scan_kernel(u_ref, w_ref, h0r_ref, h0i_ref, ar_ref, ai_ref, hr_ref, hi_ref, fr_ref, fi_ref,
                    xr_ref, xi_ref, cr_ref, ci_ref):
    @pl.when(pl.program_id(1) == 0)
    def _():
        cr_ref[...] = h0r_ref[0]
        ci_ref[...] = h0i_ref[0]

    bu = jnp.dot(u_ref[...].astype(BF16), w_ref[...], preferred_element_type=F32)
    xr_ref[...] = bu[:, :SSM_N]
    xi_ref[...] = bu[:, SSM_N:]
    ar = ar_ref[...]
    ai = ai_ref[...]

    def step(t, carry):
        hr, hi = carry
        nr = ar * hr - ai * hi + xr_ref[pl.ds(t, 1), :]
        ni = ar * hi + ai * hr + xi_ref[pl.ds(t, 1), :]
        xr_ref[pl.ds(t, 1), :] = nr
        xi_ref[pl.ds(t, 1), :] = ni
        return nr, ni

    hr, hi = lax.fori_loop(0, xr_ref.shape[0], step, (cr_ref[...], ci_ref[...]), unroll=8)
    cr_ref[...] = hr
    ci_ref[...] = hi
    hr_ref[...] = xr_ref[...].astype(BF16)
    hi_ref[...] = xi_ref[...].astype(BF16)

    @pl.when(pl.program_id(1) == pl.num_programs(1) - 1)
    def _():
        fr_ref[0] = hr
        fi_ref[0] = hi


def _s5_scan(u, w_b, h0r, h0i, ar, ai, bsz, seq, tt):
    tt = min(tt, seq)
    nt = seq // tt
    row = lambda w: pl.BlockSpec((tt, w), lambda b, t: (b * nt + t, 0))
    st = pl.BlockSpec((1, 1, SSM_N), lambda b, t: (b, 0, 0))
    cst = lambda a: pl.BlockSpec(a.shape, lambda b, t: (0, 0))
    return pl.pallas_call(
        _s5_scan_kernel, grid=(bsz, nt),
        in_specs=[row(SSM_WIDTH), cst(w_b), st, st, cst(ar), cst(ai)],
        out_specs=[row(SSM_N), row(SSM_N), st, st],
        out_shape=[jax.ShapeDtypeStruct((bsz * seq, SSM_N), BF16)] * 2 + [jax.ShapeDtypeStruct(h0r.shape, F32)] * 2,
        scratch_shapes=[pltpu.VMEM((tt, SSM_N), F32)] * 2 + [pltpu.VMEM((1, SSM_N), F32)] * 2,
        compiler_params=_params(("arbitrary", "arbitrary")), name="s5_scan",
    )(u, w_b, h0r, h0i, ar, ai)


def _s5_step_kernel(bur_ref, bui_ref, h0r_ref, h0i_ref, ar_ref, ai_ref, hr_ref, hi_ref):
    ar, ai, hr, hi = ar_ref[...], ai_ref[...], h0r_ref[...], h0i_ref[...]
    hr_ref[...] = ar * hr - ai * hi + bur_ref[...]
    hi_ref[...] = ar * hi + ai * hr + bui_ref[...]


def _s5_out_kernel(u_ref, hr_ref, hi_ref, wc_ref, d_ref, wg_ref, o_ref):
    y = (_dot(hr_ref[...], wc_ref[:SSM_N]) + _dot(hi_ref[...], wc_ref[SSM_N:])
         + d_ref[...] * u_ref[...])
    z = jax.nn.gelu(y)
    o_ref[...] = z * _sigmoid(_dot(z, wg_ref[...]))


def _block_diag(blocks):
    g, a, b = blocks.shape
    eye = jnp.eye(g, dtype=blocks.dtype)
    return (eye[:, None, :, None] * blocks[:, :, None, :]).reshape(g * a, g * b)


def _s5_weights(a_re, a_im, log_dt, b_re, b_im, c_re, c_im):
    a_re = a_re.astype(F32)
    a_im = a_im.astype(F32)
    dt = jnp.exp(log_dt.astype(F32))[:, None]
    mag = jnp.exp(a_re * dt)
    ab_re = mag * jnp.cos(a_im * dt)
    ab_im = mag * jnp.sin(a_im * dt)
    den = a_re * a_re + a_im * a_im
    nr = ab_re - 1.0
    cf_re = (nr * a_re + ab_im * a_im) / den
    cf_im = (ab_im * a_re - nr * a_im) / den
    b_re = b_re.astype(F32)
    b_im = b_im.astype(F32)
    bb_re = cf_re[..., None] * b_re - cf_im[..., None] * b_im
    bb_im = cf_re[..., None] * b_im + cf_im[..., None] * b_re
    w_b = jnp.concatenate([_block_diag(jnp.swapaxes(bb_re, 1, 2)),
                           _block_diag(jnp.swapaxes(bb_im, 1, 2))], axis=1)
    w_c = jnp.concatenate([_block_diag(jnp.swapaxes(c_re.astype(F32), 1, 2)),
                           -_block_diag(jnp.swapaxes(c_im.astype(F32), 1, 2))], axis=0)
    return ab_re.reshape(1, SSM_N), ab_im.reshape(1, SSM_N), w_b.astype(BF16), w_c.astype(BF16)


def _seg_ones():
    r = lax.broadcasted_iota(jnp.int32, (RWKV_WIDTH, RWKV_WIDTH), 0) // RWKV_HEAD
    c = lax.broadcasted_iota(jnp.int32, (RWKV_WIDTH, RWKV_WIDTH), 1) // RWKV_HEAD
    return (r == c).astype(BF16)


def _rwkv_prep_kernel(pc_ref, prev_ref, mu_ref, vec_ref, w2_ref, a2_ref, g2_ref, *out_refs):
    outs = _rwkv_prep(pc_ref[...], prev_ref[...], mu_ref[...], vec_ref[...], w2_ref[...], a2_ref[...], g2_ref[...])
    for o_ref, val in zip(out_refs, outs):
        o_ref[...] = val


def _rwkv_prep(pc, prev, mu, vec, w2, a2, g2):
    xm = pc + (prev - pc) * mu
    rw = RWKV_WIDTH
    xr, xk, xv, xl = xm[:, :rw], xm[:, rw:2 * rw], xm[:, 2 * rw:3 * rw], xm[:, 3 * rw:]
    wpre = -(vec[RV_W0:RV_W0 + 1] + _dot(jnp.tanh(xl), w2))
    softplus = jnp.maximum(wpre, 0.0) + jnp.log(1.0 + jnp.exp(-jnp.abs(wpre)))
    lw = -jnp.exp(-softplus - 0.5)
    a = _sigmoid(vec[RV_A0:RV_A0 + 1] + _dot(xl, a2))
    g = _dot(_sigmoid(xl), g2)
    kk = xk * vec[RV_KK:RV_KK + 1]
    norm = jnp.sqrt(_dot_exact_rhs(kk * kk, _seg_ones()))
    kk = kk / jnp.maximum(norm, 1e-12)
    return xr, xk * (1.0 + (a - 1.0) * vec[RV_KA:RV_KA + 1]), xv, lw, kk, kk * a, g


def _rwkv_post_kernel(r_ref, k_ref, v_ref, g_ref, y_ref, vec_ref, o_ref):
    o_ref[...] = _rwkv_post(y_ref[...], r_ref[...], k_ref[...], v_ref[...], g_ref[...], vec_ref[...])


def _rwkv_post(y, r, k, v, g, vec):
    ones = _seg_ones()
    yc = y - _dot_exact_rhs(y, ones) * (1.0 / RWKV_HEAD)
    yv = _dot_exact_rhs(yc * yc, ones) * (1.0 / RWKV_HEAD)
    yn = yc * lax.rsqrt(yv + RWKV_GN_EPS) * vec[RV_GNW:RV_GNW + 1] + vec[RV_GNB:RV_GNB + 1]
    bonus = _dot_exact_rhs(r * k * vec[RV_RK:RV_RK + 1], ones) * v
    return (yn + bonus) * g


def _rwkv_seq_kernel(*refs, chunk, nb, valid_len):
    pc_refs = refs[:nb]
    first_ref, s0_ref, mu_ref, vec_ref, w2_ref, a2_ref, g2_ref, o_ref, so_ref, s_ref, above_ref = refs[nb:]
    t = pl.program_id(0)

    @pl.when(t == 0)
    def _():
        s_ref[...] = s0_ref[...]
        above_ref[...] = first_ref[...]

    tb = pc_refs[0].shape[0]
    vec = vec_ref[...]
    row_id = lax.broadcasted_iota(jnp.int32, (tb, 1), 0)
    r_v, k_v, v_v, lw_v, kk_v, bb_v, g_v = [], [], [], [], [], [], []
    for b in range(nb):
        pc = pc_refs[b][...]
        prev = jnp.where(row_id == 0, above_ref[b:b + 1, :], pltpu.roll(pc, 1, axis=0))
        above_ref[b:b + 1, :] = pc[tb - 1:tb, :]
        vals = _rwkv_prep(pc, prev, mu_ref[...], vec, w2_ref[...], a2_ref[...], g2_ref[...])
        if valid_len < tb:
            keep = (row_id < valid_len).astype(F32)
            vals = tuple(val * keep for val in vals)
        for dst, val in zip((r_v, k_v, v_v, lw_v, kk_v, bb_v, g_v), vals):
            dst.append(val)

    c = chunk
    row = lax.broadcasted_iota(jnp.int32, (c, c), 0)
    col = lax.broadcasted_iota(jnp.int32, (c, c), 1)
    incl = (row >= col).astype(F32)
    strict = (row > col).astype(F32)
    eye = (row == col).astype(F32)
    nsub = tb // c
    heads = range(RWKV_HEADS)
    hsl = [slice(h * RWKV_HEAD, (h + 1) * RWKV_HEAD) for h in heads]
    tsl = [slice(sub * c, (sub + 1) * c) for sub in range(nsub)]
    seqs = [(b, h) for b in range(nb) for h in heads]
    units = [(b, sub, h) for b in range(nb) for sub in range(nsub) for h in heads]

    scaled = {}
    for b in range(nb):
        for sub in range(nsub):
            lw = lw_v[b][tsl[sub], :]
            cum = _dot_exact_lhs(incl.astype(BF16), lw)
            ecum = jnp.exp(cum)
            einv = jnp.exp(-cum)
            scaled[b, sub] = (kk_v[b][tsl[sub], :] * jnp.exp(cum - lw), bb_v[b][tsl[sub], :] * einv,
                              k_v[b][tsl[sub], :] * einv, r_v[b][tsl[sub], :] * ecum, ecum[c - 1:c, :])
    kt = {u: scaled[u[0], u[1]][0][:, hsl[u[2]]] for u in units}
    bt = {u: scaled[u[0], u[1]][1][:, hsl[u[2]]] for u in units}
    kkt = {u: scaled[u[0], u[1]][2][:, hsl[u[2]]] for u in units}
    rt = {u: scaled[u[0], u[1]][3][:, hsl[u[2]]] for u in units}
    v_t = [v_v[b].T for b in range(nb)]
    vt = {u: v_t[u[0]][hsl[u[2]], tsl[u[1]]] for u in units}
    a_b = {u: strict * _dot_nt(kt[u], bt[u]) for u in units}
    a_k = {u: strict * _dot_nt(kt[u], kkt[u]) for u in units}
    r_b = {u: incl * _dot_nt(rt[u], bt[u]) for u in units}
    r_k = {u: incl * _dot_nt(rt[u], kkt[u]) for u in units}
    pw = {u: -a_b[u] for u in units}
    tinv = {u: eye + pw[u] for u in units}
    n = 1
    while 2 * n < c:
        pw = {u: _dot(pw[u], pw[u]) for u in units}
        tinv = {u: tinv[u] + _dot(tinv[u], pw[u]) for u in units}
        n *= 2
    x = {u: _dot_nt(vt[u], a_k[u]) for u in units}
    w1t = {u: _dot_nt(x[u], tinv[u]) for u in units}
    w2 = {u: _dot(tinv[u], kt[u]) for u in units}
    yt_local = {u: _dot_nt(vt[u], r_k[u]) for u in units}
    s_local = {u: _dot(vt[u], kkt[u]) for u in units}

    s = {q: s_ref[q[0], q[1]] for q in seqs}
    yt = {}
    for sub in range(nsub):
        ut = {(b, h): -(_dot_nt(s[b, h], w2[b, sub, h]) + w1t[b, sub, h]) for b, h in seqs}
        for b, h in seqs:
            yt[b, sub, h] = (_dot_nt(s[b, h], rt[b, sub, h]) + _dot_nt(ut[b, h], r_b[b, sub, h])
                             + yt_local[b, sub, h])
        s = {(b, h): (s[b, h] + _dot(ut[b, h], bt[b, sub, h]) + s_local[b, sub, h]) * scaled[b, sub][4][:, hsl[h]]
             for b, h in seqs}
    for b, h in seqs:
        s_ref[b, h] = s[b, h]
    for b in range(nb):
        y_t = jnp.concatenate([jnp.concatenate([yt[b, sub, h] for sub in range(nsub)], axis=1) for h in heads],
                              axis=0)
        o_ref[:, b * RWKV_WIDTH:(b + 1) * RWKV_WIDTH] = _rwkv_post(y_t.T, r_v[b], k_v[b], v_v[b], g_v[b], vec)

    @pl.when(t == pl.num_programs(0) - 1)
    def _():
        so_ref[...] = s_ref[...]


def _rwkv_seq(pc, first, s0, consts, bsz, seq, valid_len):
    tb = 2 * RWKV_CHUNK
    assert seq % tb == 0 and (valid_len == seq or seq == tb)
    nt = seq // tb
    tiles = [pl.BlockSpec((tb, RWKV_COLS), lambda t, b=b: (b * nt + t, 0)) for b in range(bsz)]
    whole = lambda a: pl.BlockSpec(a.shape, lambda t, nd=a.ndim: (0,) * nd)
    return pl.pallas_call(
        functools.partial(_rwkv_seq_kernel, chunk=RWKV_CHUNK, nb=bsz, valid_len=min(valid_len, tb)), grid=(nt,),
        in_specs=tiles + [whole(first), whole(s0)] + [whole(c) for c in consts],
        out_specs=[pl.BlockSpec((tb, bsz * RWKV_WIDTH), lambda t: (t, 0)), whole(s0)],
        out_shape=[jax.ShapeDtypeStruct((seq, bsz * RWKV_WIDTH), F32), jax.ShapeDtypeStruct(s0.shape, F32)],
        scratch_shapes=[pltpu.VMEM(s0.shape, F32), pltpu.VMEM((bsz, RWKV_COLS), F32)],
        compiler_params=_params(("arbitrary",)), name="rwkv_seq",
    )(*([pc] * bsz), first, s0, *consts)


def _rwkv_step_kernel(s_ref, r_ref, k_ref, lw_ref, kk_ref, bb_ref, v_ref, so_ref, y_ref):
    s = s_ref[...]
    sa = jnp.sum(s * (-kk_ref[...]), axis=-1, keepdims=True)
    s = s * jnp.exp(lw_ref[...]) + sa * bb_ref[...] + v_ref[...] * k_ref[...]
    so_ref[...] = s
    y_ref[...] = jnp.sum(s * r_ref[...], axis=-1, keepdims=True)


def _rwkv_step(r, k, v, lw, kk, bb, s0):
    bsz = r.shape[0]
    p = bsz * RWKV_HEADS
    nb = min(RWKV_STEP_PAIRS, p)
    rowv = lambda a: a.reshape(p, 1, RWKV_HEAD)
    rs = pl.BlockSpec((nb, 1, RWKV_HEAD), lambda i: (i, 0, 0))
    cs = pl.BlockSpec((nb, RWKV_HEAD, 1), lambda i: (i, 0, 0))
    ss = pl.BlockSpec((nb, RWKV_HEAD, RWKV_HEAD), lambda i: (i, 0, 0))
    s_out, y = pl.pallas_call(
        _rwkv_step_kernel, grid=(p // nb,), in_specs=[ss, rs, rs, rs, rs, rs, cs], out_specs=[ss, cs],
        out_shape=[jax.ShapeDtypeStruct((p, RWKV_HEAD, RWKV_HEAD), F32),
                   jax.ShapeDtypeStruct((p, RWKV_HEAD, 1), F32)],
        compiler_params=_params(("parallel",)), name="rwkv_step",
    )(s0.reshape(p, RWKV_HEAD, RWKV_HEAD), rowv(r), rowv(k), rowv(lw), rowv(kk), rowv(bb),
      v.reshape(p, RWKV_HEAD, 1))
    return y.reshape(bsz, RWKV_WIDTH), s_out.reshape(s0.shape)


def _merge_kernel(h_ref, oa_ref, ob_ref, oc_ref, wg_ref, wb_ref, wo_ref, g_ref, b_ref, o_ref, *tile_refs):
    h = h_ref[...]
    gates = _sigmoid(jnp.dot(h.astype(BF16), wg_ref[...], preferred_element_type=F32))
    d = D_MODEL
    merged = (gates[:, :d] * _dot(oa_ref[...], wb_ref[:ATTN_W])
              + gates[:, d:2 * d] * _dot(ob_ref[...], wb_ref[ATTN_W:ATTN_W + SSM_WIDTH])
              + gates[:, 2 * d:] * _dot(oc_ref[...], wb_ref[ATTN_W + SSM_WIDTH:]))
    mix = _dot(merged, wo_ref[...])
    out = _layer_norm(DEEPNORM_ALPHA * h + mix, g_ref[...], b_ref[...])
    o_ref[...] = out
    for t_ref in tile_refs:
        t_ref[...] = _rows_to_tiles(out)


def _ffn_kernel(be_ref, x_ref, wg_ref, wu_ref, wd_ref, g_ref, b_ref, o_ref, xb_ref, acc_ref, *, post_ln):
    del be_ref
    j = pl.program_id(1)

    @pl.when(j == 0)
    def _():
        xb_ref[...] = x_ref[...].astype(BF16)
        acc_ref[...] = jnp.zeros_like(acc_ref)

    xb = xb_ref[...]
    gate = jnp.dot(xb, wg_ref[0], preferred_element_type=F32)
    up = jnp.dot(xb, wu_ref[0], preferred_element_type=F32)
    act = gate * _sigmoid(gate) * up
    acc_ref[...] += jnp.dot(act.astype(BF16), wd_ref[0], preferred_element_type=F32)

    @pl.when(j == pl.num_programs(1) - 1)
    def _():
        if post_ln:
            o_ref[...] = _layer_norm(DEEPNORM_ALPHA * x_ref[...] + acc_ref[...], g_ref[...], b_ref[...])
        else:
            o_ref[...] = acc_ref[...]


def _ffn(x, block_e, w_in, w_down, ln_g, ln_b, blk, tf, post_ln):
    rows = x.shape[0]
    assert rows % blk == 0
    f = w_down.shape[1]
    nf = f // tf
    grid_spec = pltpu.PrefetchScalarGridSpec(
        num_scalar_prefetch=1, grid=(rows // blk, nf),
        in_specs=[
            pl.BlockSpec((blk, D_MODEL), lambda i, j, be: (i, 0)),
            pl.BlockSpec((1, D_MODEL, tf), lambda i, j, be: (be[i], 0, j)),
            pl.BlockSpec((1, D_MODEL, tf), lambda i, j, be: (be[i], 0, nf + j)),
            pl.BlockSpec((1, tf, D_MODEL), lambda i, j, be: (be[i], j, 0)),
            pl.BlockSpec((1, D_MODEL), lambda i, j, be: (0, 0)),
            pl.BlockSpec((1, D_MODEL), lambda i, j, be: (0, 0)),
        ],
        out_specs=pl.BlockSpec((blk, D_MODEL), lambda i, j, be: (i, 0)),
        scratch_shapes=[pltpu.VMEM((blk, D_MODEL), BF16), pltpu.VMEM((blk, D_MODEL), F32)],
    )
    return pl.pallas_call(
        functools.partial(_ffn_kernel, post_ln=post_ln), grid_spec=grid_spec,
        out_shape=jax.ShapeDtypeStruct((rows, D_MODEL), F32),
        compiler_params=_params(("arbitrary", "arbitrary")), name="ffn",
    )(block_e, x, w_in, w_in, w_down, ln_g, ln_b)


def _router_kernel(h_ref, w_ref, e_ref, g_ref):
    logits = jnp.dot(h_ref[...], w_ref[...], preferred_element_type=F32, precision=lax.Precision.HIGHEST)
    lane = lax.broadcasted_iota(jnp.int32, logits.shape, 1)
    lg = jnp.where(lane < N_EXPERTS, logits, -jnp.inf)
    m1 = jnp.max(lg, axis=-1, keepdims=True)
    i1 = jnp.min(jnp.where(lg == m1, lane, LANES), axis=-1, keepdims=True)
    lg2 = jnp.where(lane == i1, -jnp.inf, lg)
    m2 = jnp.max(lg2, axis=-1, keepdims=True)
    i2 = jnp.min(jnp.where(lg2 == m2, lane, LANES), axis=-1, keepdims=True)
    e2 = jnp.exp(m2 - m1)
    den = 1.0 + e2
    e_ref[...] = jnp.where(lane == 0, i1, jnp.where(lane == 1, i2, 0))
    g_ref[...] = jnp.where(lane == 0, 1.0 / den, jnp.where(lane == 1, e2 / den, 0.0))


def _moe_ffn_kernel(be_ref, nv_ref, first_ref, nxt_ref, prev_dst_ref, last_dst_ref, h_hbm, wg_ref, wu_ref, wd_ref,
                    out_hbm, xbuf, xb_ref, acc_ref, stage, gsem, ssem):
    del be_ref
    i, j = pl.program_id(0), pl.program_id(1)
    nblk, nf = pl.num_programs(0), pl.num_programs(1)
    blk = xb_ref.shape[0]
    n_valid = nv_ref[0]
    valid = i < n_valid
    slot = lax.rem(i, 2)
    spare_base = out_hbm.shape[0] - blk

    def gather_row(idx_ref, r, s):
        pltpu.make_async_copy(h_hbm.at[idx_ref[0, 0, r]], xbuf.at[s, r], gsem.at[s]).start(priority=0)

    def scatter_row(r, d):
        pltpu.make_async_copy(stage.at[r], out_hbm.at[d], ssem).start(priority=1)

    def wait_gather(s):
        pltpu.make_async_copy(h_hbm.at[pl.ds(0, blk)], xbuf.at[s], gsem.at[s]).wait()

    def wait_scatter():
        pltpu.make_async_copy(stage, out_hbm.at[pl.ds(0, blk)], ssem).wait()

    def for_rows(fn):
        def body(r, c):
            fn(r)
            return c
        lax.fori_loop(0, blk, body, 0, unroll=8)

    def compute():
        xb = xb_ref[...]
        gate = jnp.dot(xb, wg_ref[0], preferred_element_type=F32)
        up = jnp.dot(xb, wu_ref[0], preferred_element_type=F32)
        act = gate * _sigmoid(gate) * up
        acc_ref[...] += jnp.dot(act.astype(BF16), wd_ref[0], preferred_element_type=F32)

    @pl.when((i == 0) & (j == 0))
    def _():
        for_rows(lambda r: gather_row(first_ref, r, 0))
        stage[...] = jnp.zeros_like(stage)

    @pl.when(valid & (j == 0))
    def _():
        wait_gather(slot)
        for t, cols in enumerate(_tiles_to_cols(xbuf[slot])):
            xb_ref[:, t * LANES:(t + 1) * LANES] = cols.astype(BF16)
        acc_ref[...] = jnp.zeros_like(acc_ref)

    @pl.when(valid & (j < nf - 1))
    def _():
        compute()
        per_step = blk // (nf - 1)
        for rr in range(per_step):
            r = j * per_step + rr
            gather_row(nxt_ref, r, 1 - slot)
            scatter_row(r, jnp.where(i > 0, prev_dst_ref[0, 0, r], spare_base + r))

    @pl.when(valid & (j == nf - 1))
    def _():
        compute()
        wait_scatter()
        stage[...] = _rows_to_tiles(acc_ref[...])

    @pl.when((i == nblk - 1) & (j == nf - 1))
    def _():
        for_rows(lambda r: scatter_row(r, last_dst_ref[0, 0, r]))
        wait_scatter()
        wait_gather(lax.rem(n_valid, 2))


def _moe_ffn(h_tiles, rows_tok, rows_dst, block_e, n_valid, w_in, w_down, blk, tf):
    n = h_tiles.shape[0]
    rows = rows_tok.shape[0]
    nblk = rows // blk
    nf = w_down.shape[1] // tf
    assert blk % (nf - 1) == 0 and n >= blk
    idx3 = lambda a: a.reshape(nblk, 1, blk)
    smem = lambda fn: pl.BlockSpec((1, 1, blk), fn, memory_space=pltpu.SMEM)
    ftile = lambda i, j, nv: jnp.where(i < nv[0], j, nf - 1)
    grid_spec = pltpu.PrefetchScalarGridSpec(
        num_scalar_prefetch=2, grid=(nblk, nf),
        in_specs=[
            smem(lambda i, j, be, nv: (0, 0, 0)),
            smem(lambda i, j, be, nv: (jnp.minimum(i + 1, nblk - 1), 0, 0)),
            smem(lambda i, j, be, nv: (jnp.maximum(i - 1, 0), 0, 0)),
            smem(lambda i, j, be, nv: (nv[0] - 1, 0, 0)),
            pl.BlockSpec(memory_space=pl.ANY),
            pl.BlockSpec((1, D_MODEL, tf), lambda i, j, be, nv: (be[i], 0, ftile(i, j, nv))),
            pl.BlockSpec((1, D_MODEL, tf), lambda i, j, be, nv: (be[i], 0, nf + ftile(i, j, nv))),
            pl.BlockSpec((1, tf, D_MODEL), lambda i, j, be, nv: (be[i], ftile(i, j, nv), 0)),
        ],
        out_specs=pl.BlockSpec(memory_space=pl.ANY),
        scratch_shapes=[pltpu.VMEM((2, blk, SUBLANES, LANES), F32), pltpu.VMEM((blk, D_MODEL), BF16),
                        pltpu.VMEM((blk, D_MODEL), F32), pltpu.VMEM((blk, SUBLANES, LANES), F32),
                        pltpu.SemaphoreType.DMA((2,)), pltpu.SemaphoreType.DMA(())],
    )
    return pl.pallas_call(
        _moe_ffn_kernel, grid_spec=grid_spec,
        out_shape=jax.ShapeDtypeStruct((TOP_K * n + blk, SUBLANES, LANES), F32),
        compiler_params=_params(("arbitrary", "arbitrary")), name="moe_ffn",
    )(block_e, n_valid, idx3(rows_tok), idx3(rows_tok), idx3(rows_dst), idx3(rows_dst), h_tiles,
      w_in, w_in, w_down)


def _combine_kernel(h_ref, y0_ref, y1_ref, gate_ref, g_ref, bias_ref, o_ref):
    gate = gate_ref[...]
    rows = lambda y_ref: jnp.concatenate(_tiles_to_cols(y_ref[...]), axis=-1)
    f = rows(y0_ref) * gate[:, 0:1] + rows(y1_ref) * gate[:, 1:2]
    o_ref[...] = _layer_norm(DEEPNORM_ALPHA * h_ref[...] + f, g_ref[...], bias_ref[...])


def _moe(h, h_tiles, router_pad, w_in, w_down, ln_g, ln_b, tm, blk, tf):
    n = h.shape[0]
    e_pad, gate = _rows_call(_router_kernel, [h], [router_pad], (LANES, LANES), tm,
                             out_dtypes=[jnp.int32, F32], name="router")
    flat_e = jnp.concatenate([e_pad[:, s] for s in range(TOP_K)])
    n_assign = n * TOP_K
    n_blocks = -(-(n_assign + N_EXPERTS * (blk - 1)) // blk)
    experts = jnp.arange(N_EXPERTS, dtype=jnp.int32)
    onehot = (flat_e[:, None] == experts[None, :]).astype(jnp.int32)
    csum = jnp.cumsum(onehot, axis=0)
    counts = csum[-1]
    padded = (counts + blk - 1) // blk * blk
    pad_end = jnp.cumsum(padded)
    dest = jnp.sum((csum - 1 + (pad_end - padded)[None, :]) * onehot, axis=1)
    assign = jnp.arange(n_assign, dtype=jnp.int32)
    rows_dst = jnp.full((n_blocks * blk,), -1, jnp.int32).at[dest].set(assign, unique_indices=True)
    rows_tok = jnp.maximum(rows_dst, 0) % n
    spare = n_assign + jnp.arange(n_blocks * blk, dtype=jnp.int32) % blk
    rows_dst = jnp.where(rows_dst < 0, spare, rows_dst)
    block_start = jnp.arange(n_blocks, dtype=jnp.int32) * blk
    block_e = jnp.minimum(jnp.sum((block_start[:, None] >= pad_end[None, :]).astype(jnp.int32), axis=1),
                          N_EXPERTS - 1)
    n_valid = (pad_end[-1:] // blk).astype(jnp.int32)
    y = _moe_ffn(h_tiles, rows_tok, rows_dst, block_e, n_valid, w_in, w_down, blk, tf)
    tm = min(tm, n)
    nt = n // tm
    cst = pl.BlockSpec((1, D_MODEL), lambda i: (0, 0))
    slot = lambda s: pl.BlockSpec((tm, SUBLANES, LANES), lambda i: (i + s * nt, 0, 0))
    return pl.pallas_call(
        _combine_kernel, grid=(nt,),
        in_specs=[pl.BlockSpec((tm, D_MODEL), lambda i: (i, 0)), slot(0), slot(1),
                  pl.BlockSpec((tm, LANES), lambda i: (i, 0)), cst, cst],
        out_specs=pl.BlockSpec((tm, D_MODEL), lambda i: (i, 0)),
        out_shape=jax.ShapeDtypeStruct((n, D_MODEL), F32),
        compiler_params=_params(("parallel",)), name="moe_combine",
    )(h, y, y, gate, ln_g, ln_b)


def _layer(l, hb, hs, pre_ln, bsz, seq, dec, cache, states, prm):
    (cache_meta_k, cache_meta_v, cache_win_k, cache_win_v) = cache
    (state_ssm_re, state_ssm_im, state_wkv, state_shift) = states
    n_meta = bsz * N_META
    tm_b, tm_s = ROW_TILE, hs.shape[0]
    w_in = prm['w_in'][l]
    w_mix = w_in[:, :MIX_COLS].astype(BF16)
    w_gate = w_in[:, MIX_COLS:].astype(BF16)
    ln_in_g, ln_in_b = prm['ln_in_g'].reshape(1, -1), prm['ln_in_b'].reshape(1, -1)

    hb, (q_b, k_b, v_b, u_b, pc_b) = _proj(hb, ln_in_g, ln_in_b, w_mix, pre_ln, tm_b)
    hs, (q_s, k_s, v_s, u_s, pc_s) = _proj(hs, ln_in_g, ln_in_b, w_mix, pre_ln, tm_s)

    rel_bias, sinks = prm['rel_bias'], prm['attn_sinks'][l]
    k_meta, v_meta = k_s[:n_meta], v_s[:n_meta]
    oa_b = _body_attention(q_b, k_b, v_b, k_meta, v_meta, rel_bias, sinks, bsz, seq)
    oa_m = _meta_attention(q_s[:n_meta], k_meta, v_meta, rel_bias, sinks, bsz)
    kd = lambda t: t[n_meta:].reshape(dec, 1, N_KV_HEADS, HEAD_DIM)
    k_all = jnp.concatenate([cache_win_k[l].astype(F32), kd(k_s)], axis=1)
    v_all = jnp.concatenate([cache_win_v[l].astype(F32), kd(v_s)], axis=1)
    oa_d = _sample_attention(q_s[n_meta:], k_all, v_all, cache_meta_k[l], cache_meta_v[l], rel_bias, sinks)
    oa_s = jnp.concatenate([oa_m, oa_d], axis=0)
    kv4 = lambda t, b: t.reshape(b, -1, N_KV_HEADS, HEAD_DIM)
    tail = lambda t: kv4(jnp.concatenate([t[(b + 1) * seq - WINDOW:(b + 1) * seq] for b in range(bsz)], axis=0), bsz)
    attn_out = (kv4(k_meta, bsz), kv4(v_meta, bsz), tail(k_b), tail(v_b), k_all[:, 1:], v_all[:, 1:])

    ar, ai, w_b, w_c = _s5_weights(prm['ssm_a_re'][l], prm['ssm_a_im'][l], prm['ssm_log_dt'][l],
                                   prm['ssm_b_re'][l], prm['ssm_b_im'][l], prm['ssm_c_re'][l], prm['ssm_c_im'][l])
    zero_h = jnp.zeros((bsz, 1, SSM_N), F32)
    hr_m, hi_m, fr_m, fi_m = _s5_scan(u_s[:n_meta], w_b, zero_h, zero_h, ar, ai, bsz, N_META, N_META)
    hr_b, hi_b, fr_b, fi_b = _s5_scan(u_b, w_b, fr_m, fi_m, ar, ai, bsz, seq, ROW_TILE)
    bur_d, bui_d = _rows_call(_s5_bu_kernel, [u_s[n_meta:]], [w_b], (SSM_N, SSM_N), dec, name="s5_bu")
    hr_d, hi_d = _rows_call(_s5_step_kernel,
                            [bur_d, bui_d, state_ssm_re[l].reshape(dec, SSM_N).astype(F32),
                             state_ssm_im[l].reshape(dec, SSM_N).astype(F32)], [ar, ai], (SSM_N, SSM_N), dec,
                            name="s5_step")
    hr_s = jnp.concatenate([hr_m, hr_d.astype(BF16)], axis=0)
    hi_s = jnp.concatenate([hi_m, hi_d.astype(BF16)], axis=0)
    s5_consts = [w_c, prm['ssm_d'][l].reshape(1, -1).astype(F32), prm['ssm_w_glu'][l].astype(BF16)]
    (ob_b,) = _rows_call(_s5_out_kernel, [u_b, hr_b, hi_b], s5_consts, (SSM_WIDTH,), tm_b, name="s5_out")
    (ob_s,) = _rows_call(_s5_out_kernel, [u_s, hr_s, hi_s], s5_consts, (SSM_WIDTH,), tm_s, name="s5_out")
    st4 = lambda t, b: t.reshape(b, SSM_GROUPS, SSM_STATE)
    ssm_out = (st4(fr_b, bsz), st4(fi_b, bsz), st4(hr_d, dec), st4(hi_d, dec))

    pad_rows = lambda w, lo: jnp.pad(w.astype(F32), ((lo, RWKV_LORA - lo - w.shape[0]), (0, 0))).astype(BF16)
    vec = jnp.pad(prm['rwkv_vec'][l].astype(F32), ((0, 1), (0, 0)))
    prep_consts = [prm['rwkv_mu'][l].reshape(1, -1).astype(F32), vec,
                   pad_rows(prm['rwkv_w2'][l], 0), pad_rows(prm['rwkv_a2'][l], RWKV_W_LORA),
                   pad_rows(prm['rwkv_g2'][l], RWKV_W_LORA + RWKV_A_LORA)]
    pc_m = pc_s[:n_meta].reshape(bsz, N_META, RWKV_COLS)
    meta_len = 2 * RWKV_CHUNK
    pc_m_pad = jnp.pad(pc_m, ((0, 0), (0, meta_len - N_META), (0, 0))).reshape(bsz * meta_len, RWKV_COLS)
    zero_s = jnp.zeros((bsz, RWKV_HEADS, RWKV_HEAD, RWKV_HEAD), F32)
    oc_m, s_m = _rwkv_seq(pc_m_pad, jnp.zeros((bsz, RWKV_COLS), F32), zero_s, prep_consts, bsz, meta_len, N_META)
    oc_m = jnp.swapaxes(oc_m[:N_META].reshape(N_META, bsz, RWKV_WIDTH), 0, 1).reshape(n_meta, RWKV_WIDTH)
    oc_b, s_b = _rwkv_seq(pc_b, pc_m[:, -1], s_m, prep_consts, bsz, seq, seq)
    pc_d = pc_s[n_meta:]
    r_d, k_d, v_d, lw_d, kk_d, bb_d, g_d = _rows_call(_rwkv_prep_kernel, [pc_d, state_shift[l].astype(F32)],
                                                      prep_consts, (RWKV_WIDTH,) * 7, dec, name="rwkv_prep")
    y_d, s_d = _rwkv_step(r_d, k_d, v_d, lw_d, kk_d, bb_d, state_wkv[l].astype(F32))
    (oc_d,) = _rows_call(_rwkv_post_kernel, [r_d, k_d, v_d, g_d, y_d], [vec], (RWKV_WIDTH,), dec, name="rwkv_post")
    oc_s = jnp.concatenate([oc_m, oc_d], axis=0)
    rwkv_out = (s_b, s_d, pc_b.reshape(bsz, seq, RWKV_COLS)[:, -1], pc_d)

    ln_g, ln_b = prm['ln_g'][l].astype(F32), prm['ln_b'][l].astype(F32)
    merge_consts = [w_gate, prm['w_branch'][l].astype(BF16), prm['w_out'][l].astype(BF16), ln_g[0:1], ln_b[0:1]]
    moe_layer = l % 2 == 1
    merge_outs = (D_MODEL,) + (((SUBLANES, LANES),) if moe_layer else ())
    tm_merge = min(tm_b, seq)
    oc_tile = (oc_b, (tm_merge, RWKV_WIDTH), lambda i: (i % (seq // tm_merge), i // (seq // tm_merge)))
    hb, *hb_tiles = _rows_call(_merge_kernel, [hb, oa_b, ob_b], merge_consts, merge_outs, tm_merge, name="merge",
                               extra_inputs=[oc_tile])
    hs, *hs_tiles = _rows_call(_merge_kernel, [hs, oa_s, ob_s, oc_s], merge_consts, merge_outs, tm_s, name="merge")

    if not moe_layer:
        w_ffn_in = prm['ffn_w_in'][l // 2].astype(BF16)[None]
        w_ffn_down = prm['ffn_w_down'][l // 2].astype(BF16)[None]
        blk_b = min(FFN_ROWS, hb.shape[0])
        hb = _ffn(hb, jnp.zeros((hb.shape[0] // blk_b,), jnp.int32), w_ffn_in, w_ffn_down,
                  ln_g[1:2], ln_b[1:2], blk_b, FFN_F_TILE, post_ln=True)
        hs = _ffn(hs, jnp.zeros((1,), jnp.int32), w_ffn_in, w_ffn_down, ln_g[1:2], ln_b[1:2], tm_s, FFN_F_TILE,
                  post_ln=True)
    else:
        router_pad = jnp.pad(prm['moe_router'][l // 2].astype(F32), ((0, 0), (0, LANES - N_EXPERTS)))
        w_moe_in = prm['moe_w_in'][l // 2].astype(BF16)
        w_moe_down = prm['moe_w_down'][l // 2].astype(BF16)
        hb = _moe(hb, hb_tiles[0], router_pad, w_moe_in, w_moe_down, ln_g[1:2], ln_b[1:2], tm_b, MOE_ROWS,
                  MOE_F_TILE)
        hs = _moe(hs, hs_tiles[0], router_pad, w_moe_in, w_moe_down, ln_g[1:2], ln_b[1:2], tm_s, MOE_ROWS_SMALL,
                  MOE_F_TILE)
    return hb, hs, attn_out, ssm_out, rwkv_out


def kernel(x_prompt, x_sample, cache_meta_k, cache_meta_v, cache_win_k, cache_win_v, state_ssm_re, state_ssm_im, state_wkv, state_shift, meta_tokens, ln_in_g, ln_in_b, w_in, rel_bias, attn_sinks, ssm_a_re, ssm_a_im, ssm_log_dt, ssm_b_re, ssm_b_im, ssm_c_re, ssm_c_im, ssm_d, ssm_w_glu, rwkv_mu, rwkv_vec, rwkv_w2, rwkv_a2, rwkv_g2, w_branch, w_out, ln_g, ln_b, ffn_w_in, ffn_w_down, moe_router, moe_w_in, moe_w_down):
    bsz, seq, _ = x_prompt.shape
    dec = x_sample.shape[0]
    assert x_sample.shape[1] == 1 and seq % (2 * RWKV_CHUNK) == 0
    prm = dict(ln_in_g=ln_in_g.astype(F32), ln_in_b=ln_in_b.astype(F32), w_in=w_in, rel_bias=rel_bias,
               attn_sinks=attn_sinks, ssm_a_re=ssm_a_re, ssm_a_im=ssm_a_im, ssm_log_dt=ssm_log_dt,
               ssm_b_re=ssm_b_re, ssm_b_im=ssm_b_im, ssm_c_re=ssm_c_re, ssm_c_im=ssm_c_im, ssm_d=ssm_d,
               ssm_w_glu=ssm_w_glu, rwkv_mu=rwkv_mu, rwkv_vec=rwkv_vec, rwkv_w2=rwkv_w2, rwkv_a2=rwkv_a2,
               rwkv_g2=rwkv_g2, w_branch=w_branch, w_out=w_out, ln_g=ln_g, ln_b=ln_b, ffn_w_in=ffn_w_in,
               ffn_w_down=ffn_w_down, moe_router=moe_router, moe_w_in=moe_w_in, moe_w_down=moe_w_down)
    hb = x_prompt.reshape(bsz * seq, D_MODEL).astype(F32)
    meta = jnp.broadcast_to(meta_tokens.astype(F32)[None], (bsz, N_META, D_MODEL)).reshape(bsz * N_META, D_MODEL)
    hs = jnp.concatenate([meta, x_sample.reshape(dec, D_MODEL).astype(F32)], axis=0)
    cache = (cache_meta_k, cache_meta_v, cache_win_k, cache_win_v)
    states = (state_ssm_re, state_ssm_im, state_wkv, state_shift)
    attn_outs, ssm_outs, rwkv_outs = [], [], []
    for l in range(DEPTH):
        hb, hs, a_o, s_o, r_o = _layer(l, hb, hs, l == 0, bsz, seq, dec, cache, states, prm)
        attn_outs.append(a_o)
        ssm_outs.append(s_o)
        rwkv_outs.append(r_o)
    stack = lambda outs, i: jnp.stack([o[i] for o in outs])
    y_prompt = hb.reshape(bsz, seq, D_MODEL)
    y_sample = hs[bsz * N_META:].reshape(dec, 1, D_MODEL)
    return (y_prompt, y_sample,
            stack(attn_outs, 0), stack(attn_outs, 1), stack(attn_outs, 2), stack(attn_outs, 3),
            stack(attn_outs, 4), stack(attn_outs, 5),
            stack(ssm_outs, 0), stack(ssm_outs, 1), stack(ssm_outs, 2), stack(ssm_outs, 3),
            stack(rwkv_outs, 0), stack(rwkv_outs, 1), stack(rwkv_outs, 2), stack(rwkv_outs, 3))
```

```python
import functools
import math

import numpy as np
import jax
import jax.numpy as jnp
from jax import lax
from jax.experimental import pallas as pl
from jax.experimental.pallas import tpu as pltpu

F32 = jnp.float32
BF16 = jnp.bfloat16

D_MODEL = 1024
DEPTH = 2
PAST_LEN = 16384
N_META = 16
WINDOW = 128
N_HEADS = 8
N_KV_HEADS = 2
HEAD_DIM = 64
Q_PER_KV = N_HEADS // N_KV_HEADS
ATTN_W = N_HEADS * HEAD_DIM
KV_W = N_KV_HEADS * HEAD_DIM
ATTN_SCALE = HEAD_DIM ** -0.5
REL_BUCKETS = 32
REL_EXACT = REL_BUCKETS // 2
REL_MAX_DIST = 128
SSM_GROUP = 16
SSM_GROUPS = 16
SSM_WIDTH = SSM_GROUP * SSM_GROUPS
SSM_STATE = 64
SSM_N = SSM_GROUPS * SSM_STATE
RWKV_HEAD = 64
RWKV_HEADS = 4
RWKV_WIDTH = RWKV_HEAD * RWKV_HEADS
RWKV_W_LORA = 32
RWKV_A_LORA = 32
RWKV_G_LORA = 64
RWKV_LORA = RWKV_W_LORA + RWKV_A_LORA + RWKV_G_LORA
RWKV_COLS = 3 * RWKV_WIDTH + RWKV_LORA
RV_W0, RV_A0, RV_KK, RV_KA, RV_RK, RV_GNW, RV_GNB = 0, 1, 2, 3, 4, 5, 6
N_BRANCH = 3
MIX_COLS = ATTN_W + 2 * KV_W + SSM_WIDTH + RWKV_COLS
N_EXPERTS = 8
TOP_K = 2
LN_EPS = 1e-5
RWKV_GN_EPS = 64e-5
NEG_INF = -1e30
DEEPNORM_ALPHA = (2 * DEPTH) ** 0.25

LANES = 128
SUBLANES = 8
VMEM_LIMIT = 48 * 1024 * 1024
RWKV_CHUNK = 64

ROW_TILE = 512
WIDE_ROW_TILE = 1024
FFN_ROWS, FFN_F_TILE = 1024, 256
MOE_F_TILE = 512
MOE_ROWS, MOE_ROWS_SMALL = 1008, 96
SAMPLE_ATTN_BLOCK = 8
ATTN_BLOCKS_PER_STEP = 4
RWKV_STEP_PAIRS = 64


def _params(sem):
    return pltpu.CompilerParams(dimension_semantics=sem, vmem_limit_bytes=VMEM_LIMIT)


def _dot(a, b):
    return jnp.dot(a.astype(BF16), b.astype(BF16), preferred_element_type=F32)


def _dot_nt(a, b):
    return lax.dot_general(a.astype(BF16), b.astype(BF16), (((1,), (1,)), ((), ())),
                           preferred_element_type=F32)


def _split3(x):
    h1 = x.astype(BF16)
    r1 = x - h1.astype(F32)
    h2 = r1.astype(BF16)
    h3 = (r1 - h2.astype(F32)).astype(BF16)
    return h1, h2, h3


def _dot_exact_rhs(x, m):
    h1, h2, h3 = _split3(x)
    dot = functools.partial(jnp.dot, preferred_element_type=F32)
    return dot(h1, m) + dot(h2, m) + dot(h3, m)


def _dot_exact_lhs(m, x):
    h1, h2, h3 = _split3(x)
    dot = functools.partial(jnp.dot, preferred_element_type=F32)
    return dot(m, h1) + dot(m, h2) + dot(m, h3)


def _layer_norm(x, g, b):
    mu = jnp.mean(x, axis=-1, keepdims=True)
    xc = x - mu
    var = jnp.mean(xc * xc, axis=-1, keepdims=True)
    return xc * lax.rsqrt(var + LN_EPS) * g + b


def _sigmoid(x):
    return 1.0 / (1.0 + jnp.exp(-x))


def _rows_to_tiles(x):
    slabs = [x[:, t * LANES:(t + 1) * LANES] for t in range(SUBLANES)]
    return jnp.swapaxes(jnp.stack(slabs, axis=0), 0, 1)


def _tiles_to_cols(x):
    xt = jnp.swapaxes(x, 0, 1)
    return [xt[t] for t in range(SUBLANES)]


def _rows_call(body, row_inputs, const_inputs, out_widths, tm, out_dtypes=None, name=None, extra_inputs=()):
    n = row_inputs[0].shape[0]
    tm = min(tm, n)
    assert n % tm == 0, (n, tm)
    out_dtypes = out_dtypes or [F32] * len(out_widths)
    in_specs = [pl.BlockSpec((tm, a.shape[1]), lambda i: (i, 0)) for a in row_inputs]
    in_specs += [pl.BlockSpec(shape, fn) for _, shape, fn in extra_inputs]
    row_inputs = list(row_inputs) + [a for a, _, _ in extra_inputs]
    in_specs += [pl.BlockSpec(c.shape, lambda i, nd=c.ndim: (0,) * nd) for c in const_inputs]
    tails = [w if isinstance(w, tuple) else (w,) for w in out_widths]
    out_specs = [pl.BlockSpec((tm,) + w, lambda i, nd=len(w): (i,) + (0,) * nd) for w in tails]
    out_shape = [jax.ShapeDtypeStruct((n,) + w, dt) for w, dt in zip(tails, out_dtypes)]
    return pl.pallas_call(
        body, grid=(n // tm,), in_specs=in_specs, out_specs=out_specs, out_shape=out_shape,
        compiler_params=_params(("parallel",)), name=name,
    )(*row_inputs, *const_inputs)


PROJ_WIDTHS = (ATTN_W, KV_W, KV_W, SSM_WIDTH, RWKV_COLS)


def _proj_kernel(x_ref, g_ref, b_ref, w_ref, *out_refs, pre_ln):
    x = x_ref[...]
    if pre_ln:
        x = _layer_norm(x, g_ref[...], b_ref[...])
        out_refs[0][...] = x
        out_refs = out_refs[1:]
    xb = x.astype(BF16)
    col = 0
    for o_ref in out_refs:
        n = o_ref.shape[-1]
        o_ref[...] = jnp.dot(xb, w_ref[:, col:col + n], preferred_element_type=F32)
        col += n


def _proj(x, ln_g, ln_b, w_mix, pre_ln, tm):
    widths = ((D_MODEL,) if pre_ln else ()) + PROJ_WIDTHS
    outs = _rows_call(functools.partial(_proj_kernel, pre_ln=pre_ln), [x], [ln_g, ln_b, w_mix],
                      widths, tm, name="proj")
    if pre_ln:
        return outs[0], outs[1:]
    return x, outs


def _attn_kernel(q_ref, k_ref, v_ref, bias_ref, sink_ref, o_ref):
    units = [(bb, h) for bb in range(q_ref.shape[0]) for h in range(N_KV_HEADS)]
    hs = [slice(h * HEAD_DIM, (h + 1) * HEAD_DIM) for h in range(N_KV_HEADS)]
    q = {u: q_ref[u[0], u[1]].astype(BF16) for u in units}
    k = {u: k_ref[u[0], :, hs[u[1]]].astype(BF16) for u in units}
    v = {u: v_ref[u[0], :, hs[u[1]]].astype(BF16) for u in units}
    s = {u: lax.dot_general(q[u], k[u], (((1,), (1,)), ((), ())), preferred_element_type=F32) for u in units}
    s = {u: s[u] * ATTN_SCALE + bias_ref[u[1]] for u in units}
    m = {u: jnp.maximum(jnp.max(s[u], axis=-1, keepdims=True), sink_ref[u[1]]) for u in units}
    p = {u: jnp.exp(s[u] - m[u]) for u in units}
    den = {u: jnp.sum(p[u], axis=-1, keepdims=True) + jnp.exp(sink_ref[u[1]] - m[u]) for u in units}
    o = {u: jnp.dot(p[u].astype(BF16), v[u], preferred_element_type=F32) for u in units}
    for u in units:
        o_ref[u[0], u[1]] = o[u] / den[u]


def _attention(q, k, v, bias, sinks, bblk):
    p, _, mq, _ = q.shape
    nk = k.shape[1]
    assert p % bblk == 0
    kv_spec = pl.BlockSpec((bblk, nk, KV_W), lambda i: (i, 0, 0))
    qo_spec = pl.BlockSpec((bblk, N_KV_HEADS, mq, HEAD_DIM), lambda i: (i, 0, 0, 0))
    return pl.pallas_call(
        _attn_kernel, grid=(p // bblk,),
        in_specs=[qo_spec, kv_spec, kv_spec, pl.BlockSpec((N_KV_HEADS, mq, nk), lambda i: (0, 0, 0)),
                  pl.BlockSpec((N_KV_HEADS, mq, 1), lambda i: (0, 0, 0))],
        out_specs=qo_spec, out_shape=jax.ShapeDtypeStruct(q.shape, F32),
        compiler_params=_params(("parallel",)), name="attention",
    )(q, k, v, bias, sinks)


def _t5_bucket(dist):
    n = np.maximum(dist, 0)
    scaled = (np.log(np.maximum(n, 1).astype(np.float32) / np.float32(REL_EXACT))
              / np.float32(math.log(REL_MAX_DIST / REL_EXACT)) * np.float32(REL_BUCKETS - REL_EXACT))
    frac = np.abs(scaled - np.round(scaled))
    assert np.all((n <= REL_EXACT) | (n >= REL_MAX_DIST) | (frac > 1e-3))
    large = np.minimum(REL_EXACT + scaled.astype(np.int32), REL_BUCKETS - 1)
    return np.where(n < REL_EXACT, n, large)


def _bias_table(rel_bias, dist, valid, mq_pad=None, nk_pad=None):
    tq, nk = dist.shape
    onehot = np.eye(REL_BUCKETS, dtype=np.float32)[_t5_bucket(dist).reshape(-1)]
    bias = jnp.dot(jnp.asarray(onehot), rel_bias.astype(F32), precision=lax.Precision.HIGHEST)
    bias = bias.reshape(tq, nk, N_HEADS)
    bias = jnp.where(jnp.asarray(valid)[..., None], bias, NEG_INF)
    bias = jnp.moveaxis(bias, -1, 0).reshape(N_KV_HEADS, Q_PER_KV * tq, nk)
    mq_pad = mq_pad or Q_PER_KV * tq
    nk_pad = nk_pad or nk
    bias = jnp.pad(bias, ((0, 0), (0, mq_pad - Q_PER_KV * tq), (0, 0)))
    return jnp.pad(bias, ((0, 0), (0, 0), (0, nk_pad - nk)), constant_values=NEG_INF)


def _sink_rows(sinks, tq, mq_pad=None):
    s = jnp.repeat(sinks.astype(F32).reshape(N_KV_HEADS, Q_PER_KV, 1), tq, axis=2)
    s = s.reshape(N_KV_HEADS, Q_PER_KV * tq, 1)
    mq_pad = mq_pad or Q_PER_KV * tq
    return jnp.pad(s, ((0, 0), (0, mq_pad - Q_PER_KV * tq), (0, 0)))


def _heads_q(q, nb, tq):
    q = q.reshape(nb, tq, N_KV_HEADS, Q_PER_KV, HEAD_DIM)
    return jnp.transpose(q, (0, 2, 3, 1, 4)).reshape(nb, N_KV_HEADS, Q_PER_KV * tq, HEAD_DIM)


def _unheads_o(o, nb, tq):
    o = o[:, :, :Q_PER_KV * tq].reshape(nb, N_KV_HEADS, Q_PER_KV, tq, HEAD_DIM)
    return jnp.transpose(o, (0, 3, 1, 2, 4)).reshape(nb * tq, ATTN_W)


BODY_KEYS = N_META + 2 * WINDOW + 16


def _body_attn_kernel(q_ref, kc_ref, kp_ref, km_ref, vc_ref, vp_ref, vm_ref, bias0_ref, bias1_ref, o_ref):
    nq = q_ref.shape[0] // WINDOW
    units = [(h, j) for h in range(N_KV_HEADS) for j in range(nq)]
    hs = [slice(h * HEAD_DIM, (h + 1) * HEAD_DIM) for h in range(N_KV_HEADS)]
    heads = [[h * Q_PER_KV + g for g in range(Q_PER_KV)] for h in range(N_KV_HEADS)]
    rows = [slice(j * WINDOW, (j + 1) * WINDOW) for j in range(nq)]
    pad = jnp.zeros((BODY_KEYS - N_META - 2 * WINDOW, HEAD_DIM), F32)
    ones = jnp.ones((BODY_KEYS, HEAD_DIM), BF16)

    def keys(cur_ref, prev_ref, meta_ref, h, j):
        before = prev_ref[:, hs[h]] if j == 0 else cur_ref[rows[j - 1], hs[h]]
        return jnp.concatenate([meta_ref[:, hs[h]], before, cur_ref[rows[j], hs[h]], pad], axis=0).astype(BF16)

    k = {u: keys(kc_ref, kp_ref, km_ref, *u) for u in units}
    v = {u: keys(vc_ref, vp_ref, vm_ref, *u) for u in units}
    q = {(h, j): jnp.concatenate([q_ref[rows[j], qh * HEAD_DIM:(qh + 1) * HEAD_DIM] for qh in heads[h]],
                                 axis=0).astype(BF16) for h, j in units}
    bias = [bias0_ref] + [bias1_ref] * (nq - 1)
    s = {u: lax.dot_general(q[u], k[u], (((1,), (1,)), ((), ())), preferred_element_type=F32) for u in units}
    s = {(h, j): s[h, j] * ATTN_SCALE + bias[j][0, h] for h, j in units}
    p = {u: jnp.exp(s[u] - jnp.max(s[u], axis=-1, keepdims=True)).astype(BF16) for u in units}
    o = {u: jnp.dot(p[u], v[u], preferred_element_type=F32) / jnp.dot(p[u], ones, preferred_element_type=F32)
         for u in units}
    for h, j in units:
        for g, qh in enumerate(heads[h]):
            o_ref[rows[j], qh * HEAD_DIM:(qh + 1) * HEAD_DIM] = o[h, j][g * WINDOW:(g + 1) * WINDOW]


def _body_attention(q, k, v, k_meta, v_meta, rel_bias, sinks, bsz, seq):
    nblk = seq // WINDOW
    i = np.arange(WINDOW)[:, None]
    c = np.arange(WINDOW)[None, :]
    sink_col = _sink_rows(sinks, WINDOW)
    tabs = []
    for m in (0, 1):
        q_pos = N_META + WINDOW * m + i
        meta_pos = np.arange(N_META)[None, :]
        dist = np.concatenate([q_pos - meta_pos, WINDOW + i - c, i - c], axis=1)
        valid = np.concatenate([np.ones((WINDOW, N_META), bool),
                                (c >= i) & (m > 0), c <= i], axis=1)
        tab = _bias_table(rel_bias, dist, valid, nk_pad=BODY_KEYS)
        tabs.append(tab.at[:, :, N_META + 2 * WINDOW].set(sink_col[:, :, 0]))
    bias = jnp.stack(tabs)
    nk = BODY_KEYS
    mq = Q_PER_KV * WINDOW
    group = math.gcd(nblk, ATTN_BLOCKS_PER_STEP)
    npair = nblk // group
    pair = lambda w: pl.BlockSpec((group * WINDOW, w), lambda b, p: (b * npair + p, 0))
    prev = lambda w: pl.BlockSpec((WINDOW, w), lambda b, p: (b * nblk + jnp.maximum(group * p - 1, 0), 0))
    meta = pl.BlockSpec((N_META, KV_W), lambda b, p: (b, 0))
    table = lambda fn: pl.BlockSpec((1, N_KV_HEADS, mq, nk), fn)
    return pl.pallas_call(
        _body_attn_kernel, grid=(bsz, npair),
        in_specs=[pair(ATTN_W), pair(KV_W), prev(KV_W), meta, pair(KV_W), prev(KV_W), meta,
                  table(lambda b, p: (jnp.minimum(p, 1), 0, 0, 0)), table(lambda b, p: (1, 0, 0, 0))],
        out_specs=pair(ATTN_W), out_shape=jax.ShapeDtypeStruct(q.shape, F32),
        compiler_params=_params(("parallel", "arbitrary")), name="body_attention",
    )(q, k, k, k_meta, v, v, v_meta, bias, bias)


def _meta_attention(q, k, v, rel_bias, sinks, bsz):
    i = np.arange(N_META)
    dist = i[:, None] - i[None, :]
    bias = _bias_table(rel_bias, dist, dist >= 0)
    kv = lambda t: t.reshape(bsz, N_META, KV_W)
    o = _attention(_heads_q(q, bsz, N_META), kv(k), kv(v), bias, _sink_rows(sinks, N_META), bsz)
    return _unheads_o(o, bsz, N_META)


def _sample_attention(q, k_all, v_all, meta_k, meta_v, rel_bias, sinks):
    bsz = q.shape[0]
    wc = k_all.shape[1] - 1
    nk = N_META + wc + 1
    nk_pad = -(-nk // LANES) * LANES
    mq_pad = SUBLANES
    k_pos = np.concatenate([np.arange(N_META), PAST_LEN - wc + np.arange(wc + 1)])
    dist = (PAST_LEN - k_pos)[None, :]
    is_meta = (np.arange(nk) < N_META)[None, :]
    valid = (dist >= 0) & (is_meta | ((k_pos[None, :] >= N_META) & (dist <= WINDOW)))
    bias = _bias_table(rel_bias, dist, valid, mq_pad, nk_pad)

    def kv(meta, t):
        full = jnp.concatenate([meta.astype(F32), t], axis=1).reshape(bsz, nk, KV_W)
        return jnp.pad(full, ((0, 0), (0, nk_pad - nk), (0, 0)))

    qh = jnp.pad(_heads_q(q, bsz, 1), ((0, 0), (0, 0), (0, mq_pad - Q_PER_KV), (0, 0)))
    o = _attention(qh, kv(meta_k, k_all), kv(meta_v, v_all), bias, _sink_rows(sinks, 1, mq_pad),
                   SAMPLE_ATTN_BLOCK)
    return _unheads_o(o, bsz, 1)


def _s5_bu_kernel(u_ref, w_ref, re_ref, im_ref):
    r = jnp.dot(u_ref[...].astype(BF16), w_ref[...], preferred_element_type=F32)
    re_ref[...] = r[:, :SSM_N]
    im_ref[...] = r[:, SSM_N:]


def _s5_scan_kernel(u_ref, w_ref, h0r_ref, h0i_ref, ar_ref, ai_ref, hr_ref, hi_ref, fr_ref, fi_ref,
                    xr_ref, xi_ref, cr_ref, ci_ref):
    @pl.when(pl.program_id(1) == 0)
    def _():
        cr_ref[...] = h0r_ref[0]
        ci_ref[...] = h0i_ref[0]

    bu = jnp.dot(u_ref[...].astype(BF16), w_ref[...], preferred_element_type=F32)
    xr_ref[...] = bu[:, :SSM_N]
    xi_ref[...] = bu[:, SSM_N:]
    ar = ar_ref[...]
    ai = ai_ref[...]

    def step(t, carry):
        hr, hi = carry
        nr = ar * hr - ai * hi + xr_ref[pl.ds(t, 1), :]
        ni = ar * hi + ai * hr + xi_ref[pl.ds(t, 1), :]
        xr_ref[pl.ds(t, 1), :] = nr
        xi_ref[pl.ds(t, 1), :] = ni
        return nr, ni

    hr, hi = lax.fori_loop(0, xr_ref.shape[0], step, (cr_ref[...], ci_ref[...]), unroll=8)
    cr_ref[...] = hr
    ci_ref[...] = hi
    hr_ref[...] = xr_ref[...].astype(BF16)
    hi_ref[...] = xi_ref[...].astype(BF16)

    @pl.when(pl.program_id(1) == pl.num_programs(1) - 1)
    def _():
        fr_ref[0] = hr
        fi_ref[0] = hi


def _s5_scan(u, w_b, h0r, h0i, ar, ai, bsz, seq, tt):
    tt = min(tt, seq)
    nt = seq // tt
    row = lambda w: pl.BlockSpec((tt, w), lambda b, t: (b * nt + t, 0))
    st = pl.BlockSpec((1, 1, SSM_N), lambda b, t: (b, 0, 0))
    cst = lambda a: pl.BlockSpec(a.shape, lambda b, t: (0, 0))
    return pl.pallas_call(
        _s5_scan_kernel, grid=(bsz, nt),
        in_specs=[row(SSM_WIDTH), cst(w_b), st, st, cst(ar), cst(ai)],
        out_specs=[row(SSM_N), row(SSM_N), st, st],
        out_shape=[jax.ShapeDtypeStruct((bsz * seq, SSM_N), BF16)] * 2 + [jax.ShapeDtypeStruct(h0r.shape, F32)] * 2,
        scratch_shapes=[pltpu.VMEM((tt, SSM_N), F32)] * 2 + [pltpu.VMEM((1, SSM_N), F32)] * 2,
        compiler_params=_params(("arbitrary", "arbitrary")), name="s5_scan",
    )(u, w_b, h0r, h0i, ar, ai)


def _s5_step_kernel(bur_ref, bui_ref, h0r_ref, h0i_ref, ar_ref, ai_ref, hr_ref, hi_ref):
    ar, ai, hr, hi = ar_ref[...], ai_ref[...], h0r_ref[...], h0i_ref[...]
    hr_ref[...] = ar * hr - ai * hi + bur_ref[...]
    hi_ref[...] = ar * hi + ai * hr + bui_ref[...]


def _s5_out_kernel(u_ref, hr_ref, hi_ref, wc_ref, d_ref, wg_ref, o_ref):
    y = (_dot(hr_ref[...], wc_ref[:SSM_N]) + _dot(hi_ref[...], wc_ref[SSM_N:])
         + d_ref[...] * u_ref[...])
    z = jax.nn.gelu(y)
    o_ref[...] = z * _sigmoid(_dot(z, wg_ref[...]))


def _block_diag(blocks):
    g, a, b = blocks.shape
    eye = jnp.eye(g, dtype=blocks.dtype)
    return (eye[:, None, :, None] * blocks[:, :, None, :]).reshape(g * a, g * b)


def _s5_weights(a_re, a_im, log_dt, b_re, b_im, c_re, c_im):
    a_re = a_re.astype(F32)
    a_im = a_im.astype(F32)
    dt = jnp.exp(log_dt.astype(F32))[:, None]
    mag = jnp.exp(a_re * dt)
    ab_re = mag * jnp.cos(a_im * dt)
    ab_im = mag * jnp.sin(a_im * dt)
    den = a_re * a_re + a_im * a_im
    nr = ab_re - 1.0
    cf_re = (nr * a_re + ab_im * a_im) / den
    cf_im = (ab_im * a_re - nr * a_im) / den
    b_re = b_re.astype(F32)
    b_im = b_im.astype(F32)
    bb_re = cf_re[..., None] * b_re - cf_im[..., None] * b_im
    bb_im = cf_re[..., None] * b_im + cf_im[..., None] * b_re
    w_b = jnp.concatenate([_block_diag(jnp.swapaxes(bb_re, 1, 2)),
                           _block_diag(jnp.swapaxes(bb_im, 1, 2))], axis=1)
    w_c = jnp.concatenate([_block_diag(jnp.swapaxes(c_re.astype(F32), 1, 2)),
                           -_block_diag(jnp.swapaxes(c_im.astype(F32), 1, 2))], axis=0)
    return ab_re.reshape(1, SSM_N), ab_im.reshape(1, SSM_N), w_b.astype(BF16), w_c.astype(BF16)


def _seg_ones():
    r = lax.broadcasted_iota(jnp.int32, (RWKV_WIDTH, RWKV_WIDTH), 0) // RWKV_HEAD
    c = lax.broadcasted_iota(jnp.int32, (RWKV_WIDTH, RWKV_WIDTH), 1) // RWKV_HEAD
    return (r == c).astype(BF16)


def _rwkv_prep_kernel(pc_ref, prev_ref, mu_ref, vec_ref, w2_ref, a2_ref, g2_ref, *out_refs):
    outs = _rwkv_prep(pc_ref[...], prev_ref[...], mu_ref[...], vec_ref[...], w2_ref[...], a2_ref[...], g2_ref[...])
    for o_ref, val in zip(out_refs, outs):
        o_ref[...] = val


def _rwkv_prep(pc, prev, mu, vec, w2, a2, g2):
    xm = pc + (prev - pc) * mu
    rw = RWKV_WIDTH
    xr, xk, xv, xl = xm[:, :rw], xm[:, rw:2 * rw], xm[:, 2 * rw:3 * rw], xm[:, 3 * rw:]
    wpre = -(vec[RV_W0:RV_W0 + 1] + _dot(jnp.tanh(xl), w2))
    softplus = jnp.maximum(wpre, 0.0) + jnp.log(1.0 + jnp.exp(-jnp.abs(wpre)))
    lw = -jnp.exp(-softplus - 0.5)
    a = _sigmoid(vec[RV_A0:RV_A0 + 1] + _dot(xl, a2))
    g = _dot(_sigmoid(xl), g2)
    kk = xk * vec[RV_KK:RV_KK + 1]
    norm = jnp.sqrt(_dot_exact_rhs(kk * kk, _seg_ones()))
    kk = kk / jnp.maximum(norm, 1e-12)
    return xr, xk * (1.0 + (a - 1.0) * vec[RV_KA:RV_KA + 1]), xv, lw, kk, kk * a, g


def _rwkv_post_kernel(r_ref, k_ref, v_ref, g_ref, y_ref, vec_ref, o_ref):
    o_ref[...] = _rwkv_post(y_ref[...], r_ref[...], k_ref[...], v_ref[...], g_ref[...], vec_ref[...])


def _rwkv_post(y, r, k, v, g, vec):
    ones = _seg_ones()
    yc = y - _dot_exact_rhs(y, ones) * (1.0 / RWKV_HEAD)
    yv = _dot_exact_rhs(yc * yc, ones) * (1.0 / RWKV_HEAD)
    yn = yc * lax.rsqrt(yv + RWKV_GN_EPS) * vec[RV_GNW:RV_GNW + 1] + vec[RV_GNB:RV_GNB + 1]
    bonus = _dot_exact_rhs(r * k * vec[RV_RK:RV_RK + 1], ones) * v
    return (yn + bonus) * g


def _rwkv_seq_kernel(*refs, chunk, nb, valid_len):
    pc_refs = refs[:nb]
    first_ref, s0_ref, mu_ref, vec_ref, w2_ref, a2_ref, g2_ref, o_ref, so_ref, s_ref, above_ref = refs[nb:]
    t = pl.program_id(0)

    @pl.when(t == 0)
    def _():
        s_ref[...] = s0_ref[...]
        above_ref[...] = first_ref[...]

    tb = pc_refs[0].shape[0]
    vec = vec_ref[...]
    row_id = lax.broadcasted_iota(jnp.int32, (tb, 1), 0)
    r_v, k_v, v_v, lw_v, kk_v, bb_v, g_v = [], [], [], [], [], [], []
    for b in range(nb):
        pc = pc_refs[b][...]
        prev = jnp.where(row_id == 0, above_ref[b:b + 1, :], pltpu.roll(pc, 1, axis=0))
        above_ref[b:b + 1, :] = pc[tb - 1:tb, :]
        vals = _rwkv_prep(pc, prev, mu_ref[...], vec, w2_ref[...], a2_ref[...], g2_ref[...])
        if valid_len < tb:
            keep = (row_id < valid_len).astype(F32)
            vals = tuple(val * keep for val in vals)
        for dst, val in zip((r_v, k_v, v_v, lw_v, kk_v, bb_v, g_v), vals):
            dst.append(val)

    c = chunk
    row = lax.broadcasted_iota(jnp.int32, (c, c), 0)
    col = lax.broadcasted_iota(jnp.int32, (c, c), 1)
    incl = (row >= col).astype(F32)
    strict = (row > col).astype(F32)
    eye = (row == col).astype(F32)
    nsub = tb // c
    heads = range(RWKV_HEADS)
    hsl = [slice(h * RWKV_HEAD, (h + 1) * RWKV_HEAD) for h in heads]
    tsl = [slice(sub * c, (sub + 1) * c) for sub in range(nsub)]
    seqs = [(b, h) for b in range(nb) for h in heads]
    units = [(b, sub, h) for b in range(nb) for sub in range(nsub) for h in heads]

    scaled = {}
    for b in range(nb):
        for sub in range(nsub):
            lw = lw_v[b][tsl[sub], :]
            cum = _dot_exact_lhs(incl.astype(BF16), lw)
            ecum = jnp.exp(cum)
            einv = jnp.exp(-cum)
            scaled[b, sub] = (kk_v[b][tsl[sub], :] * jnp.exp(cum - lw), bb_v[b][tsl[sub], :] * einv,
                              k_v[b][tsl[sub], :] * einv, r_v[b][tsl[sub], :] * ecum, ecum[c - 1:c, :])
    kt = {u: scaled[u[0], u[1]][0][:, hsl[u[2]]] for u in units}
    bt = {u: scaled[u[0], u[1]][1][:, hsl[u[2]]] for u in units}
    kkt = {u: scaled[u[0], u[1]][2][:, hsl[u[2]]] for u in units}
    rt = {u: scaled[u[0], u[1]][3][:, hsl[u[2]]] for u in units}
    v_t = [v_v[b].T for b in range(nb)]
    vt = {u: v_t[u[0]][hsl[u[2]], tsl[u[1]]] for u in units}
    a_b = {u: strict * _dot_nt(kt[u], bt[u]) for u in units}
    a_k = {u: strict * _dot_nt(kt[u], kkt[u]) for u in units}
    r_b = {u: incl * _dot_nt(rt[u], bt[u]) for u in units}
    r_k = {u: incl * _dot_nt(rt[u], kkt[u]) for u in units}
    pw = {u: -a_b[u] for u in units}
    tinv = {u: eye + pw[u] for u in units}
    n = 1
    while 2 * n < c:
        pw = {u: _dot(pw[u], pw[u]) for u in units}
        tinv = {u: tinv[u] + _dot(tinv[u], pw[u]) for u in units}
        n *= 2
    x = {u: _dot_nt(vt[u], a_k[u]) for u in units}
    w1t = {u: _dot_nt(x[u], tinv[u]) for u in units}
    w2 = {u: _dot(tinv[u], kt[u]) for u in units}
    yt_local = {u: _dot_nt(vt[u], r_k[u]) for u in units}
    s_local = {u: _dot(vt[u], kkt[u]) for u in units}

    s = {q: s_ref[q[0], q[1]] for q in seqs}
    yt = {}
    for sub in range(nsub):
        ut = {(b, h): -(_dot_nt(s[b, h], w2[b, sub, h]) + w1t[b, sub, h]) for b, h in seqs}
        for b, h in seqs:
            yt[b, sub, h] = (_dot_nt(s[b, h], rt[b, sub, h]) + _dot_nt(ut[b, h], r_b[b, sub, h])
                             + yt_local[b, sub, h])
        s = {(b, h): (s[b, h] + _dot(ut[b, h], bt[b, sub, h]) + s_local[b, sub, h]) * scaled[b, sub][4][:, hsl[h]]
             for b, h in seqs}
    for b, h in seqs:
        s_ref[b, h] = s[b, h]
    for b in range(nb):
        y_t = jnp.concatenate([jnp.concatenate([yt[b, sub, h] for sub in range(nsub)], axis=1) for h in heads],
                              axis=0)
        o_ref[:, b * RWKV_WIDTH:(b + 1) * RWKV_WIDTH] = _rwkv_post(y_t.T, r_v[b], k_v[b], v_v[b], g_v[b], vec)

    @pl.when(t == pl.num_programs(0) - 1)
    def _():
        so_ref[...] = s_ref[...]


def _rwkv_seq(pc, first, s0, consts, bsz, seq, valid_len):
    tb = 2 * RWKV_CHUNK
    assert seq % tb == 0 and (valid_len == seq or seq == tb)
    nt = seq // tb
    tiles = [pl.BlockSpec((tb, RWKV_COLS), lambda t, b=b: (b * nt + t, 0)) for b in range(bsz)]
    whole = lambda a: pl.BlockSpec(a.shape, lambda t, nd=a.ndim: (0,) * nd)
    return pl.pallas_call(
        functools.partial(_rwkv_seq_kernel, chunk=RWKV_CHUNK, nb=bsz, valid_len=min(valid_len, tb)), grid=(nt,),
        in_specs=tiles + [whole(first), whole(s0)] + [whole(c) for c in consts],
        out_specs=[pl.BlockSpec((tb, bsz * RWKV_WIDTH), lambda t: (t, 0)), whole(s0)],
        out_shape=[jax.ShapeDtypeStruct((seq, bsz * RWKV_WIDTH), F32), jax.ShapeDtypeStruct(s0.shape, F32)],
        scratch_shapes=[pltpu.VMEM(s0.shape, F32), pltpu.VMEM((bsz, RWKV_COLS), F32)],
        compiler_params=_params(("arbitrary",)), name="rwkv_seq",
    )(*([pc] * bsz), first, s0, *consts)


def _rwkv_step_kernel(s_ref, r_ref, k_ref, lw_ref, kk_ref, bb_ref, v_ref, so_ref, y_ref):
    s = s_ref[...]
    sa = jnp.sum(s * (-kk_ref[...]), axis=-1, keepdims=True)
    s = s * jnp.exp(lw_ref[...]) + sa * bb_ref[...] + v_ref[...] * k_ref[...]
    so_ref[...] = s
    y_ref[...] = jnp.sum(s * r_ref[...], axis=-1, keepdims=True)


def _rwkv_step(r, k, v, lw, kk, bb, s0):
    bsz = r.shape[0]
    p = bsz * RWKV_HEADS
    nb = min(RWKV_STEP_PAIRS, p)
    rowv = lambda a: a.reshape(p, 1, RWKV_HEAD)
    rs = pl.BlockSpec((nb, 1, RWKV_HEAD), lambda i: (i, 0, 0))
    cs = pl.BlockSpec((nb, RWKV_HEAD, 1), lambda i: (i, 0, 0))
    ss = pl.BlockSpec((nb, RWKV_HEAD, RWKV_HEAD), lambda i: (i, 0, 0))
    s_out, y = pl.pallas_call(
        _rwkv_step_kernel, grid=(p // nb,), in_specs=[ss, rs, rs, rs, rs, rs, cs], out_specs=[ss, cs],
        out_shape=[jax.ShapeDtypeStruct((p, RWKV_HEAD, RWKV_HEAD), F32),
                   jax.ShapeDtypeStruct((p, RWKV_HEAD, 1), F32)],
        compiler_params=_params(("parallel",)), name="rwkv_step",
    )(s0.reshape(p, RWKV_HEAD, RWKV_HEAD), rowv(r), rowv(k), rowv(lw), rowv(kk), rowv(bb),
      v.reshape(p, RWKV_HEAD, 1))
    return y.reshape(bsz, RWKV_WIDTH), s_out.reshape(s0.shape)


def _merge_kernel(h_ref, oa_ref, ob_ref, oc_ref, wg_ref, wb_ref, wo_ref, g_ref, b_ref, o_ref, *tile_refs):
    h = h_ref[...]
    gates = _sigmoid(jnp.dot(h.astype(BF16), wg_ref[...], preferred_element_type=F32))
    d = D_MODEL
    merged = (gates[:, :d] * _dot(oa_ref[...], wb_ref[:ATTN_W])
              + gates[:, d:2 * d] * _dot(ob_ref[...], wb_ref[ATTN_W:ATTN_W + SSM_WIDTH])
              + gates[:, 2 * d:] * _dot(oc_ref[...], wb_ref[ATTN_W + SSM_WIDTH:]))
    mix = _dot(merged, wo_ref[...])
    out = _layer_norm(DEEPNORM_ALPHA * h + mix, g_ref[...], b_ref[...])
    o_ref[...] = out
    for t_ref in tile_refs:
        t_ref[...] = _rows_to_tiles(out)


def _ffn_kernel(be_ref, x_ref, wg_ref, wu_ref, wd_ref, g_ref, b_ref, o_ref, xb_ref, acc_ref, *, post_ln):
    del be_ref
    j = pl.program_id(1)

    @pl.when(j == 0)
    def _():
        xb_ref[...] = x_ref[...].astype(BF16)
        acc_ref[...] = jnp.zeros_like(acc_ref)

    xb = xb_ref[...]
    gate = jnp.dot(xb, wg_ref[0], preferred_element_type=F32)
    up = jnp.dot(xb, wu_ref[0], preferred_element_type=F32)
    act = gate * _sigmoid(gate) * up
    acc_ref[...] += jnp.dot(act.astype(BF16), wd_ref[0], preferred_element_type=F32)

    @pl.when(j == pl.num_programs(1) - 1)
    def _():
        if post_ln:
            o_ref[...] = _layer_norm(DEEPNORM_ALPHA * x_ref[...] + acc_ref[...], g_ref[...], b_ref[...])
        else:
            o_ref[...] = acc_ref[...]


def _ffn(x, block_e, w_in, w_down, ln_g, ln_b, blk, tf, post_ln):
    rows = x.shape[0]
    assert rows % blk == 0
    f = w_down.shape[1]
    nf = f // tf
    grid_spec = pltpu.PrefetchScalarGridSpec(
        num_scalar_prefetch=1, grid=(rows // blk, nf),
        in_specs=[
            pl.BlockSpec((blk, D_MODEL), lambda i, j, be: (i, 0)),
            pl.BlockSpec((1, D_MODEL, tf), lambda i, j, be: (be[i], 0, j)),
            pl.BlockSpec((1, D_MODEL, tf), lambda i, j, be: (be[i], 0, nf + j)),
            pl.BlockSpec((1, tf, D_MODEL), lambda i, j, be: (be[i], j, 0)),
            pl.BlockSpec((1, D_MODEL), lambda i, j, be: (0, 0)),
            pl.BlockSpec((1, D_MODEL), lambda i, j, be: (0, 0)),
        ],
        out_specs=pl.BlockSpec((blk, D_MODEL), lambda i, j, be: (i, 0)),
        scratch_shapes=[pltpu.VMEM((blk, D_MODEL), BF16), pltpu.VMEM((blk, D_MODEL), F32)],
    )
    return pl.pallas_call(
        functools.partial(_ffn_kernel, post_ln=post_ln), grid_spec=grid_spec,
        out_shape=jax.ShapeDtypeStruct((rows, D_MODEL), F32),
        compiler_params=_params(("arbitrary", "arbitrary")), name="ffn",
    )(block_e, x, w_in, w_in, w_down, ln_g, ln_b)


def _router_kernel(h_ref, w_ref, e_ref, g_ref):
    logits = jnp.dot(h_ref[...], w_ref[...], preferred_element_type=F32, precision=lax.Precision.HIGHEST)
    lane = lax.broadcasted_iota(jnp.int32, logits.shape, 1)
    lg = jnp.where(lane < N_EXPERTS, logits, -jnp.inf)
    m1 = jnp.max(lg, axis=-1, keepdims=True)
    i1 = jnp.min(jnp.where(lg == m1, lane, LANES), axis=-1, keepdims=True)
    lg2 = jnp.where(lane == i1, -jnp.inf, lg)
    m2 = jnp.max(lg2, axis=-1, keepdims=True)
    i2 = jnp.min(jnp.where(lg2 == m2, lane, LANES), axis=-1, keepdims=True)
    e2 = jnp.exp(m2 - m1)
    den = 1.0 + e2
    e_ref[...] = jnp.where(lane == 0, i1, jnp.where(lane == 1, i2, 0))
    g_ref[...] = jnp.where(lane == 0, 1.0 / den, jnp.where(lane == 1, e2 / den, 0.0))


def _moe_ffn_kernel(be_ref, nv_ref, first_ref, nxt_ref, prev_dst_ref, last_dst_ref, h_hbm, wg_ref, wu_ref, wd_ref,
                    out_hbm, xbuf, xb_ref, acc_ref, stage, gsem, ssem):
    del be_ref
    i, j = pl.program_id(0), pl.program_id(1)
    nblk, nf = pl.num_programs(0), pl.num_programs(1)
    blk = xb_ref.shape[0]
    n_valid = nv_ref[0]
    valid = i < n_valid
    slot = lax.rem(i, 2)
    spare_base = out_hbm.shape[0] - blk

    def gather_row(idx_ref, r, s):
        pltpu.make_async_copy(h_hbm.at[idx_ref[0, 0, r]], xbuf.at[s, r], gsem.at[s]).start(priority=0)

    def scatter_row(r, d):
        pltpu.make_async_copy(stage.at[r], out_hbm.at[d], ssem).start(priority=1)

    def wait_gather(s):
        pltpu.make_async_copy(h_hbm.at[pl.ds(0, blk)], xbuf.at[s], gsem.at[s]).wait()

    def wait_scatter():
        pltpu.make_async_copy(stage, out_hbm.at[pl.ds(0, blk)], ssem).wait()

    def for_rows(fn):
        def body(r, c):
            fn(r)
            return c
        lax.fori_loop(0, blk, body, 0, unroll=8)

    def compute():
        xb = xb_ref[...]
        gate = jnp.dot(xb, wg_ref[0], preferred_element_type=F32)
        up = jnp.dot(xb, wu_ref[0], preferred_element_type=F32)
        act = gate * _sigmoid(gate) * up
        acc_ref[...] += jnp.dot(act.astype(BF16), wd_ref[0], preferred_element_type=F32)

    @pl.when((i == 0) & (j == 0))
    def _():
        for_rows(lambda r: gather_row(first_ref, r, 0))
        stage[...] = jnp.zeros_like(stage)

    @pl.when(valid & (j == 0))
    def _():
        wait_gather(slot)
        for t, cols in enumerate(_tiles_to_cols(xbuf[slot])):
            xb_ref[:, t * LANES:(t + 1) * LANES] = cols.astype(BF16)
        acc_ref[...] = jnp.zeros_like(acc_ref)

    @pl.when(valid & (j < nf - 1))
    def _():
        compute()
        per_step = blk // (nf - 1)
        for rr in range(per_step):
            r = j * per_step + rr
            gather_row(nxt_ref, r, 1 - slot)
            scatter_row(r, jnp.where(i > 0, prev_dst_ref[0, 0, r], spare_base + r))

    @pl.when(valid & (j == nf - 1))
    def _():
        compute()
        wait_scatter()
        stage[...] = _rows_to_tiles(acc_ref[...])

    @pl.when((i == nblk - 1) & (j == nf - 1))
    def _():
        for_rows(lambda r: scatter_row(r, last_dst_ref[0, 0, r]))
        wait_scatter()
        wait_gather(lax.rem(n_valid, 2))


def _moe_ffn(h_tiles, rows_tok, rows_dst, block_e, n_valid, w_in, w_down, blk, tf):
    n = h_tiles.shape[0]
    rows = rows_tok.shape[0]
    nblk = rows // blk
    nf = w_down.shape[1] // tf
    assert blk % (nf - 1) == 0 and n >= blk
    idx3 = lambda a: a.reshape(nblk, 1, blk)
    smem = lambda fn: pl.BlockSpec((1, 1, blk), fn, memory_space=pltpu.SMEM)
    ftile = lambda i, j, nv: jnp.where(i < nv[0], j, nf - 1)
    grid_spec = pltpu.PrefetchScalarGridSpec(
        num_scalar_prefetch=2, grid=(nblk, nf),
        in_specs=[
            smem(lambda i, j, be, nv: (0, 0, 0)),
            smem(lambda i, j, be, nv: (jnp.minimum(i + 1, nblk - 1), 0, 0)),
            smem(lambda i, j, be, nv: (jnp.maximum(i - 1, 0), 0, 0)),
            smem(lambda i, j, be, nv: (nv[0] - 1, 0, 0)),
            pl.BlockSpec(memory_space=pl.ANY),
            pl.BlockSpec((1, D_MODEL, tf), lambda i, j, be, nv: (be[i], 0, ftile(i, j, nv))),
            pl.BlockSpec((1, D_MODEL, tf), lambda i, j, be, nv: (be[i], 0, nf + ftile(i, j, nv))),
            pl.BlockSpec((1, tf, D_MODEL), lambda i, j, be, nv: (be[i], ftile(i, j, nv), 0)),
        ],
        out_specs=pl.BlockSpec(memory_space=pl.ANY),
        scratch_shapes=[pltpu.VMEM((2, blk, SUBLANES, LANES), F32), pltpu.VMEM((blk, D_MODEL), BF16),
                        pltpu.VMEM((blk, D_MODEL), F32), pltpu.VMEM((blk, SUBLANES, LANES), F32),
                        pltpu.SemaphoreType.DMA((2,)), pltpu.SemaphoreType.DMA(())],
    )
    return pl.pallas_call(
        _moe_ffn_kernel, grid_spec=grid_spec,
        out_shape=jax.ShapeDtypeStruct((TOP_K * n + blk, SUBLANES, LANES), F32),
        compiler_params=_params(("arbitrary", "arbitrary")), name="moe_ffn",
    )(block_e, n_valid, idx3(rows_tok), idx3(rows_tok), idx3(rows_dst), idx3(rows_dst), h_tiles,
      w_in, w_in, w_down)


def _combine_kernel(h_ref, y0_ref, y1_ref, gate_ref, g_ref, bias_ref, o_ref):
    gate = gate_ref[...]
    rows = lambda y_ref: jnp.concatenate(_tiles_to_cols(y_ref[...]), axis=-1)
    f = rows(y0_ref) * gate[:, 0:1] + rows(y1_ref) * gate[:, 1:2]
    o_ref[...] = _layer_norm(DEEPNORM_ALPHA * h_ref[...] + f, g_ref[...], bias_ref[...])


def _moe(h, h_tiles, router_pad, w_in, w_down, ln_g, ln_b, tm, blk, tf):
    n = h.shape[0]
    e_pad, gate = _rows_call(_router_kernel, [h], [router_pad], (LANES, LANES), tm,
                             out_dtypes=[jnp.int32, F32], name="router")
    flat_e = jnp.concatenate([e_pad[:, s] for s in range(TOP_K)])
    n_assign = n * TOP_K
    n_blocks = -(-(n_assign + N_EXPERTS * (blk - 1)) // blk)
    experts = jnp.arange(N_EXPERTS, dtype=jnp.int32)
    onehot = (flat_e[:, None] == experts[None, :]).astype(jnp.int32)
    csum = jnp.cumsum(onehot, axis=0)
    counts = csum[-1]
    padded = (counts + blk - 1) // blk * blk
    pad_end = jnp.cumsum(padded)
    dest = jnp.sum((csum - 1 + (pad_end - padded)[None, :]) * onehot, axis=1)
    assign = jnp.arange(n_assign, dtype=jnp.int32)
    rows_dst = jnp.full((n_blocks * blk,), -1, jnp.int32).at[dest].set(assign, unique_indices=True)
    rows_tok = jnp.maximum(rows_dst, 0) % n
    spare = n_assign + jnp.arange(n_blocks * blk, dtype=jnp.int32) % blk
    rows_dst = jnp.where(rows_dst < 0, spare, rows_dst)
    block_start = jnp.arange(n_blocks, dtype=jnp.int32) * blk
    block_e = jnp.minimum(jnp.sum((block_start[:, None] >= pad_end[None, :]).astype(jnp.int32), axis=1),
                          N_EXPERTS - 1)
    n_valid = (pad_end[-1:] // blk).astype(jnp.int32)
    y = _moe_ffn(h_tiles, rows_tok, rows_dst, block_e, n_valid, w_in, w_down, blk, tf)
    tm = min(tm, n)
    nt = n // tm
    cst = pl.BlockSpec((1, D_MODEL), lambda i: (0, 0))
    slot = lambda s: pl.BlockSpec((tm, SUBLANES, LANES), lambda i: (i + s * nt, 0, 0))
    return pl.pallas_call(
        _combine_kernel, grid=(nt,),
        in_specs=[pl.BlockSpec((tm, D_MODEL), lambda i: (i, 0)), slot(0), slot(1),
                  pl.BlockSpec((tm, LANES), lambda i: (i, 0)), cst, cst],
        out_specs=pl.BlockSpec((tm, D_MODEL), lambda i: (i, 0)),
        out_shape=jax.ShapeDtypeStruct((n, D_MODEL), F32),
        compiler_params=_params(("parallel",)), name="moe_combine",
    )(h, y, y, gate, ln_g, ln_b)


def _layer(l, hb, hs, pre_ln, bsz, seq, dec, cache, states, prm):
    (cache_meta_k, cache_meta_v, cache_win_k, cache_win_v) = cache
    (state_ssm_re, state_ssm_im, state_wkv, state_shift) = states
    n_meta = bsz * N_META
    tm_b, tm_s = ROW_TILE, hs.shape[0]
    w_in = prm['w_in'][l]
    w_mix = w_in[:, :MIX_COLS].astype(BF16)
    w_gate = w_in[:, MIX_COLS:].astype(BF16)
    ln_in_g, ln_in_b = prm['ln_in_g'].reshape(1, -1), prm['ln_in_b'].reshape(1, -1)

    hb, (q_b, k_b, v_b, u_b, pc_b) = _proj(hb, ln_in_g, ln_in_b, w_mix, pre_ln, WIDE_ROW_TILE)
    hs, (q_s, k_s, v_s, u_s, pc_s) = _proj(hs, ln_in_g, ln_in_b, w_mix, pre_ln, tm_s)

    rel_bias, sinks = prm['rel_bias'], prm['attn_sinks'][l]
    k_meta, v_meta = k_s[:n_meta], v_s[:n_meta]
    oa_b = _body_attention(q_b, k_b, v_b, k_meta, v_meta, rel_bias, sinks, bsz, seq)
    oa_m = _meta_attention(q_s[:n_meta], k_meta, v_meta, rel_bias, sinks, bsz)
    kd = lambda t: t[n_meta:].reshape(dec, 1, N_KV_HEADS, HEAD_DIM)
    k_all = jnp.concatenate([cache_win_k[l].astype(F32), kd(k_s)], axis=1)
    v_all = jnp.concatenate([cache_win_v[l].astype(F32), kd(v_s)], axis=1)
    oa_d = _sample_attention(q_s[n_meta:], k_all, v_all, cache_meta_k[l], cache_meta_v[l], rel_bias, sinks)
    oa_s = jnp.concatenate([oa_m, oa_d], axis=0)
    kv4 = lambda t, b: t.reshape(b, -1, N_KV_HEADS, HEAD_DIM)
    tail = lambda t: kv4(jnp.concatenate([t[(b + 1) * seq - WINDOW:(b + 1) * seq] for b in range(bsz)], axis=0), bsz)
    attn_out = (kv4(k_meta, bsz), kv4(v_meta, bsz), tail(k_b), tail(v_b), k_all[:, 1:], v_all[:, 1:])

    ar, ai, w_b, w_c = _s5_weights(prm['ssm_a_re'][l], prm['ssm_a_im'][l], prm['ssm_log_dt'][l],
                                   prm['ssm_b_re'][l], prm['ssm_b_im'][l], prm['ssm_c_re'][l], prm['ssm_c_im'][l])
    zero_h = jnp.zeros((bsz, 1, SSM_N), F32)
    hr_m, hi_m, fr_m, fi_m = _s5_scan(u_s[:n_meta], w_b, zero_h, zero_h, ar, ai, bsz, N_META, N_META)
    hr_b, hi_b, fr_b, fi_b = _s5_scan(u_b, w_b, fr_m, fi_m, ar, ai, bsz, seq, WIDE_ROW_TILE)
    bur_d, bui_d = _rows_call(_s5_bu_kernel, [u_s[n_meta:]], [w_b], (SSM_N, SSM_N), dec, name="s5_bu")
    hr_d, hi_d = _rows_call(_s5_step_kernel,
                            [bur_d, bui_d, state_ssm_re[l].reshape(dec, SSM_N).astype(F32),
                             state_ssm_im[l].reshape(dec, SSM_N).astype(F32)], [ar, ai], (SSM_N, SSM_N), dec,
                            name="s5_step")
    hr_s = jnp.concatenate([hr_m, hr_d.astype(BF16)], axis=0)
    hi_s = jnp.concatenate([hi_m, hi_d.astype(BF16)], axis=0)
    s5_consts = [w_c, prm['ssm_d'][l].reshape(1, -1).astype(F32), prm['ssm_w_glu'][l].astype(BF16)]
    (ob_b,) = _rows_call(_s5_out_kernel, [u_b, hr_b, hi_b], s5_consts, (SSM_WIDTH,), tm_b, name="s5_out")
    (ob_s,) = _rows_call(_s5_out_kernel, [u_s, hr_s, hi_s], s5_consts, (SSM_WIDTH,), tm_s, name="s5_out")
    st4 = lambda t, b: t.reshape(b, SSM_GROUPS, SSM_STATE)
    ssm_out = (st4(fr_b, bsz), st4(fi_b, bsz), st4(hr_d, dec), st4(hi_d, dec))

    pad_rows = lambda w, lo: jnp.pad(w.astype(F32), ((lo, RWKV_LORA - lo - w.shape[0]), (0, 0))).astype(BF16)
    vec = jnp.pad(prm['rwkv_vec'][l].astype(F32), ((0, 1), (0, 0)))
    prep_consts = [prm['rwkv_mu'][l].reshape(1, -1).astype(F32), vec,
                   pad_rows(prm['rwkv_w2'][l], 0), pad_rows(prm['rwkv_a2'][l], RWKV_W_LORA),
                   pad_rows(prm['rwkv_g2'][l], RWKV_W_LORA + RWKV_A_LORA)]
    pc_m = pc_s[:n_meta].reshape(bsz, N_META, RWKV_COLS)
    meta_len = 2 * RWKV_CHUNK
    pc_m_pad = jnp.pad(pc_m, ((0, 0), (0, meta_len - N_META), (0, 0))).reshape(bsz * meta_len, RWKV_COLS)
    zero_s = jnp.zeros((bsz, RWKV_HEADS, RWKV_HEAD, RWKV_HEAD), F32)
    oc_m, s_m = _rwkv_seq(pc_m_pad, jnp.zeros((bsz, RWKV_COLS), F32), zero_s, prep_consts, bsz, meta_len, N_META)
    oc_m = jnp.swapaxes(oc_m[:N_META].reshape(N_META, bsz, RWKV_WIDTH), 0, 1).reshape(n_meta, RWKV_WIDTH)
    oc_b, s_b = _rwkv_seq(pc_b, pc_m[:, -1], s_m, prep_consts, bsz, seq, seq)
    pc_d = pc_s[n_meta:]
    r_d, k_d, v_d, lw_d, kk_d, bb_d, g_d = _rows_call(_rwkv_prep_kernel, [pc_d, state_shift[l].astype(F32)],
                                                      prep_consts, (RWKV_WIDTH,) * 7, dec, name="rwkv_prep")
    y_d, s_d = _rwkv_step(r_d, k_d, v_d, lw_d, kk_d, bb_d, state_wkv[l].astype(F32))
    (oc_d,) = _rows_call(_rwkv_post_kernel, [r_d, k_d, v_d, g_d, y_d], [vec], (RWKV_WIDTH,), dec, name="rwkv_post")
    oc_s = jnp.concatenate([oc_m, oc_d], axis=0)
    rwkv_out = (s_b, s_d, pc_b.reshape(bsz, seq, RWKV_COLS)[:, -1], pc_d)

    ln_g, ln_b = prm['ln_g'][l].astype(F32), prm['ln_b'][l].astype(F32)
    merge_consts = [w_gate, prm['w_branch'][l].astype(BF16), prm['w_out'][l].astype(BF16), ln_g[0:1], ln_b[0:1]]
    moe_layer = l % 2 == 1
    merge_outs = (D_MODEL,) + (((SUBLANES, LANES),) if moe_layer else ())
    tm_merge = min(tm_b, seq)
    oc_tile = (oc_b, (tm_merge, RWKV_WIDTH), lambda i: (i % (seq // tm_merge), i // (seq // tm_merge)))
    hb, *hb_tiles = _rows_call(_merge_kernel, [hb, oa_b, ob_b], merge_consts, merge_outs, tm_merge, name="merge",
                               extra_inputs=[oc_tile])
    hs, *hs_tiles = _rows_call(_merge_kernel, [hs, oa_s, ob_s, oc_s], merge_consts, merge_outs, tm_s, name="merge")

    if not moe_layer:
        w_ffn_in = prm['ffn_w_in'][l // 2].astype(BF16)[None]
        w_ffn_down = prm['ffn_w_down'][l // 2].astype(BF16)[None]
        blk_b = min(FFN_ROWS, hb.shape[0])
        hb = _ffn(hb, jnp.zeros((hb.shape[0] // blk_b,), jnp.int32), w_ffn_in, w_ffn_down,
                  ln_g[1:2], ln_b[1:2], blk_b, FFN_F_TILE, post_ln=True)
        hs = _ffn(hs, jnp.zeros((1,), jnp.int32), w_ffn_in, w_ffn_down, ln_g[1:2], ln_b[1:2], tm_s, FFN_F_TILE,
                  post_ln=True)
    else:
        router_pad = jnp.pad(prm['moe_router'][l // 2].astype(F32), ((0, 0), (0, LANES - N_EXPERTS)))
        w_moe_in = prm['moe_w_in'][l // 2].astype(BF16)
        w_moe_down = prm['moe_w_down'][l // 2].astype(BF16)
        hb = _moe(hb, hb_tiles[0], router_pad, w_moe_in, w_moe_down, ln_g[1:2], ln_b[1:2], tm_b, MOE_ROWS,
                  MOE_F_TILE)
        hs = _moe(hs, hs_tiles[0], router_pad, w_moe_in, w_moe_down, ln_g[1:2], ln_b[1:2], tm_s, MOE_ROWS_SMALL,
                  MOE_F_TILE)
    return hb, hs, attn_out, ssm_out, rwkv_out


def kernel(x_prompt, x_sample, cache_meta_k, cache_meta_v, cache_win_k, cache_win_v, state_ssm_re, state_ssm_im, state_wkv, state_shift, meta_tokens, ln_in_g, ln_in_b, w_in, rel_bias, attn_sinks, ssm_a_re, ssm_a_im, ssm_log_dt, ssm_b_re, ssm_b_im, ssm_c_re, ssm_c_im, ssm_d, ssm_w_glu, rwkv_mu, rwkv_vec, rwkv_w2, rwkv_a2, rwkv_g2, w_branch, w_out, ln_g, ln_b, ffn_w_in, ffn_w_down, moe_router, moe_w_in, moe_w_down):
    bsz, seq, _ = x_prompt.shape
    dec = x_sample.shape[0]
    assert x_sample.shape[1] == 1 and seq % (2 * RWKV_CHUNK) == 0
    prm = dict(ln_in_g=ln_in_g.astype(F32), ln_in_b=ln_in_b.astype(F32), w_in=w_in, rel_bias=rel_bias,
               attn_sinks=attn_sinks, ssm_a_re=ssm_a_re, ssm_a_im=ssm_a_im, ssm_log_dt=ssm_log_dt,
               ssm_b_re=ssm_b_re, ssm_b_im=ssm_b_im, ssm_c_re=ssm_c_re, ssm_c_im=ssm_c_im, ssm_d=ssm_d,
               ssm_w_glu=ssm_w_glu, rwkv_mu=rwkv_mu, rwkv_vec=rwkv_vec, rwkv_w2=rwkv_w2, rwkv_a2=rwkv_a2,
               rwkv_g2=rwkv_g2, w_branch=w_branch, w_out=w_out, ln_g=ln_g, ln_b=ln_b, ffn_w_in=ffn_w_in,
               ffn_w_down=ffn_w_down, moe_router=moe_router, moe_w_in=moe_w_in, moe_w_down=moe_w_down)
    hb = x_prompt.reshape(bsz * seq, D_MODEL).astype(F32)
    meta = jnp.broadcast_to(meta_tokens.astype(F32)[None], (bsz, N_META, D_MODEL)).reshape(bsz * N_META, D_MODEL)
    hs = jnp.concatenate([meta, x_sample.reshape(dec, D_MODEL).astype(F32)], axis=0)
    cache = (cache_meta_k, cache_meta_v, cache_win_k, cache_win_v)
    states = (state_ssm_re, state_ssm_im, state_wkv, state_shift)
    attn_outs, ssm_outs, rwkv_outs = [], [], []
    for l in range(DEPTH):
        hb, hs, a_o, s_o, r_o = _layer(l, hb, hs, l == 0, bsz, seq, dec, cache, states, prm)
        attn_outs.append(a_o)
        ssm_outs.append(s_o)
        rwkv_outs.append(r_o)
    stack = lambda outs, i: jnp.stack([o[i] for o in outs])
    y_prompt = hb.reshape(bsz, seq, D_MODEL)
    y_sample = hs[bsz * N_META:].reshape(dec, 1, D_MODEL)
    return (y_prompt, y_sample,
            stack(attn_outs, 0), stack(attn_outs, 1), stack(attn_outs, 2), stack(attn_outs, 3),
            stack(attn_outs, 4), stack(attn_outs, 5),
            stack(ssm_outs, 0), stack(ssm_outs, 1), stack(ssm_outs, 2), stack(ssm_outs, 3),
            stack(rwkv_outs, 0), stack(rwkv_outs, 1), stack(rwkv_outs, 2), stack(rwkv_outs, 3))
```
